```python
import math
import jax, jax.numpy as jnp
from jax import lax
import numpy as np

D_MODEL = 2048
BATCH = 4
SEQ = 2048
DEPTH = 4

N_META = 16
EPS = 1e-6
ATT_HEADS = 8
QK_NOPE = 128
QK_ROPE = 64
V_DIM = 128
Q_LORA = 512
KV_LORA = 256
ROPE_THETA = 10000.0
Q_BLOCK = 128
ATT_WIDTH = ATT_HEADS * V_DIM
SSM_GROUP = 16
SSM_WIDTH = D_MODEL - ATT_WIDTH
SSM_GROUPS = SSM_WIDTH // SSM_GROUP
SSM_STATE = 64
DT_MIN = 1e-3
DT_MAX = 1e-1
MIX_WIDTH = ATT_WIDTH + SSM_WIDTH
IN_COLS = Q_LORA + KV_LORA + QK_ROPE + SSM_WIDTH
D_FF = 5632
N_EXPERTS = 8
TOP_K = 2
D_FF_EXPERT = 1408
N_DENSE = (DEPTH + 1) // 2
N_MOE = DEPTH // 2

kernel_name = "hybrid_mla_s5_moe_encoder"

F32 = jnp.float32


def rms_norm(x, g):
    xf = x.astype(F32)
    y = xf * lax.rsqrt(jnp.mean(xf * xf, axis=-1, keepdims=True) + EPS)
    return (y * g.astype(F32)).astype(x.dtype)


def rope_tables(n):
    inv = ROPE_THETA ** (-jnp.arange(0, QK_ROPE, 2, dtype=F32) / QK_ROPE)
    ang = jnp.arange(n, dtype=F32)[:, None] * inv[None, :]
    return jnp.cos(ang), jnp.sin(ang)


def apply_rope(x, cos, sin):
    xf = x.astype(F32)
    x1, x2 = jnp.split(xf, 2, axis=-1)
    return jnp.concatenate([x1 * cos - x2 * sin, x1 * sin + x2 * cos], axis=-1).astype(x.dtype)


def mla_attention(c_q, c_kv, k_rope, q_norm, w_uq, kv_norm, w_ukv, cos, sin):
    bsz, L, _ = c_q.shape
    q = (rms_norm(c_q, q_norm) @ w_uq).reshape(bsz, L, ATT_HEADS, QK_NOPE + QK_ROPE)
    q = jnp.concatenate([q[..., :QK_NOPE], apply_rope(q[..., QK_NOPE:], cos[:, None, :], sin[:, None, :])], axis=-1)
    kv = (rms_norm(c_kv, kv_norm) @ w_ukv).reshape(bsz, L, ATT_HEADS, QK_NOPE + V_DIM)
    k_nope, v = kv[..., :QK_NOPE], kv[..., QK_NOPE:]
    k_pe = apply_rope(k_rope, cos, sin)
    k = jnp.concatenate([k_nope, jnp.broadcast_to(k_pe[:, :, None, :], (bsz, L, ATT_HEADS, QK_ROPE))], axis=-1)
    n_blk = -(-L // Q_BLOCK)
    lp = n_blk * Q_BLOCK
    q = jnp.pad(q, ((0, 0), (0, lp - L), (0, 0), (0, 0)))
    qb = q.reshape(bsz, n_blk, Q_BLOCK, ATT_HEADS, QK_NOPE + QK_ROPE).transpose(1, 0, 2, 3, 4)
    scale = (QK_NOPE + QK_ROPE) ** -0.5

    def block(q_blk):
        s = jnp.einsum('bqhd,bkhd->bhqk', q_blk, k, preferred_element_type=F32) * scale
        p = jax.nn.softmax(s, axis=-1)
        return jnp.einsum('bhqk,bkhd->bqhd', p.astype(v.dtype), v)

    o = lax.map(block, qb)
    return o.transpose(1, 0, 2, 3, 4).reshape(bsz, lp, ATT_WIDTH)[:, :L]


def s5_direction(u_c, lam_re, lam_im, log_step, b_re, b_im, c_re, c_im, reverse):
    lam = lax.complex(lam_re.astype(F32), lam_im.astype(F32))
    dt = jnp.exp(log_step.astype(F32))[:, None]
    lam_bar = jnp.exp(lam * dt)
    b = lax.complex(b_re.astype(F32), b_im.astype(F32))
    b_bar = ((lam_bar - 1.0) / lam)[..., None] * b
    bu = jnp.einsum('gpc,blgc->blgp', b_bar, u_c)
    a = jnp.broadcast_to(lam_bar, bu.shape)

    def combine(left, right):
        a_l, b_l = left
        a_r, b_r = right
        return a_r * a_l, a_r * b_l + b_r

    _, h = lax.associative_scan(combine, (a, bu), axis=1, reverse=reverse)
    c = lax.complex(c_re.astype(F32), c_im.astype(F32))
    return jnp.real(jnp.einsum('gcp,blgp->blgc', c, h))


def s5_mixer(u, lam_re, lam_im, log_step, b_re, b_im, c_re, c_im, d_skip, w_glu):
    bsz, L, _ = u.shape
    uf = u.astype(F32)
    ug = uf.reshape(bsz, L, SSM_GROUPS, SSM_GROUP)
    uc = lax.complex(ug, jnp.zeros_like(ug))
    y_fwd = s5_direction(uc, lam_re[0], lam_im[0], log_step[0], b_re[0], b_im[0], c_re[0], c_im[0], False)
    y_bwd = s5_direction(uc, lam_re[1], lam_im[1], log_step[1], b_re[1], b_im[1], c_re[1], c_im[1], True)
    y = (y_fwd + y_bwd).reshape(bsz, L, SSM_WIDTH) + d_skip.astype(F32) * uf
    g = jax.nn.gelu(y).astype(u.dtype)
    return g * jax.nn.sigmoid(g @ w_glu)


def swiglu(x, w_gate, w_up, w_down):
    return (jax.nn.silu(x @ w_gate) * (x @ w_up)) @ w_down


def moe_swiglu(x, w_router, w_gate, w_up, w_down):
    bsz, L, d = x.shape
    t = x.reshape(bsz * L, d)
    logits = (t @ w_router).astype(F32)
    top_val, top_idx = lax.top_k(logits, TOP_K)
    top_w = jax.nn.softmax(top_val, axis=-1)
    gates = jnp.sum(jax.nn.one_hot(top_idx, N_EXPERTS, dtype=F32) * top_w[..., None], axis=1)
    hid = jax.nn.silu(jnp.einsum('td,edf->etf', t, w_gate)) * jnp.einsum('td,edf->etf', t, w_up)
    hid = hid * gates.T[:, :, None].astype(hid.dtype)
    out = jnp.einsum('etf,efd->td', hid, w_down)
    return out.reshape(bsz, L, d)


def setup_inputs(seed: int = 0) -> dict:
    key = jax.random.key(seed)
    ks = iter(jax.random.split(key, 40))
    nrm = lambda shape, s: jax.random.normal(next(ks), shape, F32) * s
    gain = lambda shape: 1.0 + 0.01 * jax.random.normal(next(ks), shape, F32)
    res_scale = (2 * DEPTH) ** -0.5
    n_idx = jnp.arange(SSM_STATE, dtype=F32)
    lam_re = -0.5 + 0.01 * jax.random.normal(next(ks), (DEPTH, 2, SSM_GROUPS, SSM_STATE), F32)
    lam_im = math.pi * n_idx + 0.01 * jax.random.normal(next(ks), (DEPTH, 2, SSM_GROUPS, SSM_STATE), F32)
    log_step = jax.random.uniform(next(ks), (DEPTH, 2, SSM_GROUPS), F32, math.log(DT_MIN), math.log(DT_MAX))
    return {
        "x": nrm((BATCH, SEQ, D_MODEL), 1.0),
        "meta_tokens": nrm((N_META, D_MODEL), 1.0),
        "mix_norm": gain((DEPTH, D_MODEL)),
        "w_in": nrm((DEPTH, D_MODEL, IN_COLS), D_MODEL ** -0.5),
        "q_norm": gain((DEPTH, Q_LORA)),
        "w_uq": nrm((DEPTH, Q_LORA, ATT_HEADS * (QK_NOPE + QK_ROPE)), Q_LORA ** -0.5),
        "kv_norm": gain((DEPTH, KV_LORA)),
        "w_ukv": nrm((DEPTH, KV_LORA, ATT_HEADS * (QK_NOPE + V_DIM)), KV_LORA ** -0.5),
        "ssm_lambda_re": lam_re,
        "ssm_lambda_im": lam_im,
        "ssm_log_step": log_step,
        "ssm_b_re": nrm((DEPTH, 2, SSM_GROUPS, SSM_STATE, SSM_GROUP), (2 * SSM_GROUP) ** -0.5),
        "ssm_b_im": nrm((DEPTH, 2, SSM_GROUPS, SSM_STATE, SSM_GROUP), (2 * SSM_GROUP) ** -0.5),
        "ssm_c_re": nrm((DEPTH, 2, SSM_GROUPS, SSM_GROUP, SSM_STATE), 0.5),
        "ssm_c_im": nrm((DEPTH, 2, SSM_GROUPS, SSM_GROUP, SSM_STATE), 0.5),
        "ssm_d": nrm((DEPTH, SSM_WIDTH), 0.5),
        "ssm_w_glu": nrm((DEPTH, SSM_WIDTH, SSM_WIDTH), SSM_WIDTH ** -0.5),
        "attn_out_norm": gain((DEPTH, ATT_WIDTH)),
        "ssm_out_norm": gain((DEPTH, SSM_WIDTH)),
        "w_out": nrm((DEPTH, MIX_WIDTH, D_MODEL), MIX_WIDTH ** -0.5 * res_scale),
        "ffn_norm": gain((DEPTH, D_MODEL)),
        "dense_w_gate": nrm((N_DENSE, D_MODEL, D_FF), D_MODEL ** -0.5),
        "dense_w_up": nrm((N_DENSE, D_MODEL, D_FF), D_MODEL ** -0.5),
        "dense_w_down": nrm((N_DENSE, D_FF, D_MODEL), D_FF ** -0.5 * res_scale),
        "moe_router": nrm((N_MOE, D_MODEL, N_EXPERTS), D_MODEL ** -0.5),
        "moe_w_gate": nrm((N_MOE, N_EXPERTS, D_MODEL, D_FF_EXPERT), D_MODEL ** -0.5),
        "moe_w_up": nrm((N_MOE, N_EXPERTS, D_MODEL, D_FF_EXPERT), D_MODEL ** -0.5),
        "moe_w_down": nrm((N_MOE, N_EXPERTS, D_FF_EXPERT, D_MODEL), D_FF_EXPERT ** -0.5 * res_scale),
        "final_norm": gain((D_MODEL,)),
    }


def reference(x, meta_tokens, mix_norm, w_in, q_norm, w_uq, kv_norm, w_ukv,
              ssm_lambda_re, ssm_lambda_im, ssm_log_step, ssm_b_re, ssm_b_im, ssm_c_re, ssm_c_im,
              ssm_d, ssm_w_glu, attn_out_norm, ssm_out_norm, w_out, ffn_norm,
              dense_w_gate, dense_w_up, dense_w_down,
              moe_router, moe_w_gate, moe_w_up, moe_w_down, final_norm):
    bsz, seq, d = x.shape
    L = N_META + seq
    meta = jnp.broadcast_to(meta_tokens[None].astype(x.dtype), (bsz, N_META, d))
    h = jnp.concatenate([meta, x], axis=1)
    cos, sin = rope_tables(L)
    splits = [Q_LORA, Q_LORA + KV_LORA, Q_LORA + KV_LORA + QK_ROPE]
    for layer in range(DEPTH):
        hn = rms_norm(h, mix_norm[layer])
        proj = hn @ w_in[layer]
        c_q, c_kv, k_rope, u = jnp.split(proj, splits, axis=-1)
        att = mla_attention(c_q, c_kv, k_rope, q_norm[layer], w_uq[layer],
                            kv_norm[layer], w_ukv[layer], cos, sin)
        ssm = s5_mixer(u, ssm_lambda_re[layer], ssm_lambda_im[layer], ssm_log_step[layer],
                       ssm_b_re[layer], ssm_b_im[layer], ssm_c_re[layer], ssm_c_im[layer],
                       ssm_d[layer], ssm_w_glu[layer])
        mixed = jnp.concatenate([rms_norm(att, attn_out_norm[layer]),
                                 rms_norm(ssm.astype(h.dtype), ssm_out_norm[layer])], axis=-1)
        h = h + mixed @ w_out[layer]
        hn = rms_norm(h, ffn_norm[layer])
        if layer % 2 == 0:
            i = layer // 2
            h = h + swiglu(hn, dense_w_gate[i], dense_w_up[i], dense_w_down[i])
        else:
            i = layer // 2
            h = h + moe_swiglu(hn, moe_router[i], moe_w_gate[i], moe_w_up[i], moe_w_down[i])
    out = rms_norm(h, final_norm)[:, N_META:]
    return out
```

```python
import functools
import math

import jax
import jax.numpy as jnp
from jax import lax
from jax.experimental import pallas as pl
from jax.experimental.pallas import tpu as pltpu

F32 = jnp.float32
BF16 = jnp.bfloat16

D_MODEL = 2048
BATCH = 4
SEQ = 2048
DEPTH = 4
N_META = 16
EPS = 1e-6
ATT_HEADS = 8
QK_NOPE = 128
QK_ROPE = 64
V_DIM = 128
Q_LORA = 512
KV_LORA = 256
ROPE_THETA = 10000.0
ATT_WIDTH = ATT_HEADS * V_DIM
SSM_GROUP = 16
SSM_WIDTH = D_MODEL - ATT_WIDTH
SSM_GROUPS = SSM_WIDTH // SSM_GROUP
SSM_STATE = 64
D_FF = 5632
N_EXPERTS = 8
D_FF_EXPERT = 1408

L_TOK = N_META + SEQ
T_TOK = L_TOK * BATCH
LANE = 128
SUBLANE = 8
HEAD_PAD = 256
L_PAD = 2176
PROJ_COLS = 2048
U_COL0 = 1024
VMEM_LIMIT = 58 * 1024 * 1024

SSM_BLK_GROUPS = 8
SSM_BLKS = SSM_GROUPS // SSM_BLK_GROUPS
SSM_BLK_CH = SSM_BLK_GROUPS * SSM_GROUP
SSM_BLK_ST = SSM_BLK_GROUPS * SSM_STATE
S5_CHUNKS = 4
S5_ROWS = T_TOK // S5_CHUNKS


def _cparams(sem):
    return pltpu.CompilerParams(dimension_semantics=sem, vmem_limit_bytes=VMEM_LIMIT)


def _rms(x, g):
    ms = jnp.mean(x * x, axis=-1, keepdims=True)
    return x * lax.rsqrt(ms + EPS) * g


def _bdot(a, b):
    return jnp.dot(a, b, preferred_element_type=F32)


def _norm_matmul_kernel(x_ref, g_ref, w_ref, o_ref, hn_ref):
    @pl.when(pl.program_id(1) == 0)
    def _():
        hn_ref[...] = _rms(x_ref[...], g_ref[...]).astype(BF16)

    o_ref[...] = _bdot(hn_ref[...], w_ref[...].astype(BF16))


def _norm_matmul(x, g, w, tm, tn):
    t, d = x.shape
    n = w.shape[1]
    return pl.pallas_call(
        _norm_matmul_kernel,
        grid=(t // tm, n // tn),
        in_specs=[
            pl.BlockSpec((tm, d), lambda i, j: (i, 0)),
            pl.BlockSpec((1, d), lambda i, j: (0, 0)),
            pl.BlockSpec((d, tn), lambda i, j: (0, j)),
        ],
        out_specs=pl.BlockSpec((tm, tn), lambda i, j: (i, j)),
        out_shape=jax.ShapeDtypeStruct((t, n), F32),
        scratch_shapes=[pltpu.VMEM((tm, d), BF16)],
        compiler_params=_cparams(("parallel", "arbitrary")),
        name="norm_matmul",
    )(x, g.reshape(1, d), w)


def _upproj_kernel(cq_ref, ckv_ref, kr_ref, krot_ref, qn_ref, kvn_ref, wq_ref, wqr_ref,
                   wkv_ref, cos_ref, sin_ref, q_ref, kv_ref, kpe_ref):
    scale = (QK_NOPE + QK_ROPE) ** -0.5
    cqn = _rms(cq_ref[...], qn_ref[...]).astype(BF16)
    a = _bdot(cqn, wq_ref[...].astype(BF16))
    r = _bdot(cqn, wqr_ref[...].astype(BF16))
    c = cos_ref[...]
    s = sin_ref[...]
    for h in range(ATT_HEADS):
        lo = h * HEAD_PAD
        q_ref[:, lo:lo + LANE] = (a[:, lo:lo + LANE] * scale).astype(BF16)
        rope = a[:, lo + LANE:lo + HEAD_PAD] * c + r[:, h * LANE:(h + 1) * LANE] * s
        q_ref[:, lo + LANE:lo + HEAD_PAD] = (rope * scale).astype(BF16)
    ckvn = _rms(ckv_ref[...], kvn_ref[...]).astype(BF16)
    kv_ref[...] = _bdot(ckvn, wkv_ref[...].astype(BF16)).astype(BF16)
    kpe_ref[...] = (kr_ref[...] * c + krot_ref[...] * s).astype(BF16)


def _upproj(proj, qn, kvn, wq, wqr, wkv, cos_t, sin_t, tm):
    t = proj.shape[0]
    full = lambda shape: pl.BlockSpec(shape, lambda i: (0, 0))
    return pl.pallas_call(
        _upproj_kernel,
        grid=(t // tm,),
        in_specs=[
            pl.BlockSpec((tm, Q_LORA), lambda i: (i, 0)),
            pl.BlockSpec((tm, KV_LORA), lambda i: (i, Q_LORA // KV_LORA)),
            pl.BlockSpec((tm, LANE), lambda i: (i, (Q_LORA + KV_LORA) // LANE)),
            pl.BlockSpec((tm, LANE), lambda i: (i, (Q_LORA + KV_LORA) // LANE + 1)),
            full((1, Q_LORA)),
            full((1, KV_LORA)),
            full(wq.shape),
            full(wqr.shape),
            full(wkv.shape),
            pl.BlockSpec((tm, LANE), lambda i: (i, 0)),
            pl.BlockSpec((tm, LANE), lambda i: (i, 0)),
        ],
        out_specs=[
            pl.BlockSpec((tm, ATT_HEADS * HEAD_PAD), lambda i: (i, 0)),
            pl.BlockSpec((tm, ATT_HEADS * (QK_NOPE + V_DIM)), lambda i: (i, 0)),
            pl.BlockSpec((tm, LANE), lambda i: (i, 0)),
        ],
        out_shape=[
            jax.ShapeDtypeStruct((t, ATT_HEADS * HEAD_PAD), BF16),
            jax.ShapeDtypeStruct((t, ATT_HEADS * (QK_NOPE + V_DIM)), BF16),
            jax.ShapeDtypeStruct((t, LANE), BF16),
        ],
        compiler_params=_cparams(("parallel",)),
        name="upproj",
    )(proj, proj, proj, proj, qn.reshape(1, -1), kvn.reshape(1, -1), wq, wqr, wkv, cos_t, sin_t)


def _attn_kernel(q_ref, kv_ref, kpe_ref, o_ref, k_sc, v_sc):
    @pl.when(pl.program_id(2) == 0)
    def _():
        k_sc[0:L_TOK, 0:LANE] = kv_ref[:, 0:LANE]
        k_sc[0:L_TOK, LANE:HEAD_PAD] = kpe_ref[...]
        k_sc[L_TOK:L_PAD, :] = jnp.zeros((L_PAD - L_TOK, HEAD_PAD), BF16)
        v_sc[0:L_TOK, :] = kv_ref[:, LANE:HEAD_PAD]
        v_sc[L_TOK:L_PAD, :] = jnp.zeros((L_PAD - L_TOK, V_DIM), BF16)

    s = lax.dot_general(q_ref[...], k_sc[...], (((1,), (1,)), ((), ())),
                        preferred_element_type=F32)
    col = lax.broadcasted_iota(jnp.int32, s.shape, 1)
    s = jnp.where(col < L_TOK, s, -1e30)
    m = jnp.max(s, axis=-1, keepdims=True)
    p = jnp.exp(s - m)
    l = jnp.sum(p, axis=-1, keepdims=True)
    o = _bdot(p.astype(BF16), v_sc[...])
    o_ref[...] = o / l


def _attention(q, kv, kpe, tq):
    q2 = q.reshape(L_TOK, BATCH * ATT_HEADS * HEAD_PAD)
    kv2 = kv.reshape(L_TOK, BATCH * ATT_HEADS * HEAD_PAD)
    kpe2 = kpe.reshape(L_TOK, BATCH * LANE)
    out = pl.pallas_call(
        _attn_kernel,
        grid=(BATCH, ATT_HEADS, L_TOK // tq),
        in_specs=[
            pl.BlockSpec((tq, HEAD_PAD), lambda b, h, i: (i, b * ATT_HEADS + h)),
            pl.BlockSpec((L_TOK, HEAD_PAD), lambda b, h, i: (0, b * ATT_HEADS + h)),
            pl.BlockSpec((L_TOK, LANE), lambda b, h, i: (0, b)),
        ],
        out_specs=pl.BlockSpec((tq, V_DIM), lambda b, h, i: (i, b * ATT_HEADS + h)),
        out_shape=jax.ShapeDtypeStruct((L_TOK, BATCH * ATT_WIDTH), F32),
        scratch_shapes=[pltpu.VMEM((L_PAD, HEAD_PAD), BF16), pltpu.VMEM((L_PAD, V_DIM), BF16)],
        compiler_params=_cparams(("parallel", "parallel", "arbitrary")),
        name="attention",
    )(q2, kv2, kpe2)
    return out.reshape(T_TOK, ATT_WIDTH)


def _s5_param_kernel(lre_ref, lim_ref, ls_ref, bre_ref, bim_ref,
                     lbr_o, lbi_o, l2r_o, l2i_o, bbr_o, bbi_o, lbbr_o, lbbi_o):
    lre = lre_ref[...]
    lim = lim_ref[...]
    dt = jnp.exp(ls_ref[...])
    mag = jnp.exp(lre * dt)
    ang = lim * dt
    br = mag * jnp.cos(ang)
    bi = mag * jnp.sin(ang)
    nr = br - 1.0
    den = lre * lre + lim * lim
    cr = (nr * lre + bi * lim) / den
    ci = (bi * lre - nr * lim) / den
    b_r = bre_ref[...]
    b_i = bim_ref[...]
    bbr = cr * b_r - ci * b_i
    bbi = cr * b_i + ci * b_r
    lbr_o[...] = br
    lbi_o[...] = bi
    l2r_o[...] = br * br - bi * bi
    l2i_o[...] = 2.0 * br * bi
    bbr_o[...] = bbr
    bbi_o[...] = bbi
    lbbr_o[...] = br * bbr - bi * bbi
    lbbi_o[...] = br * bbi + bi * bbr


def _s5_params(lam_re, lam_im, log_step, b_re, b_im):
    n = DEPTH * 2 * SSM_GROUPS
    lre = lam_re.reshape(n, 1, SSM_STATE)
    lim = lam_im.reshape(n, 1, SSM_STATE)
    ls = log_step.reshape(n, 1, 1)
    btr = b_re.transpose(0, 1, 2, 4, 3).reshape(n, SSM_GROUP, SSM_STATE)
    bti = b_im.transpose(0, 1, 2, 4, 3).reshape(n, SSM_GROUP, SSM_STATE)
    small = jax.ShapeDtypeStruct((n, 1, SSM_STATE), F32)
    big = jax.ShapeDtypeStruct((n, SSM_GROUP, SSM_STATE), F32)
    return pl.pallas_call(
        _s5_param_kernel,
        out_shape=[small, small, small, small, big, big, big, big],
        compiler_params=pltpu.CompilerParams(vmem_limit_bytes=VMEM_LIMIT),
        name="s5_params",
    )(lre, lim, ls, btr, bti)


def _s5_layouts(lam_re, lam_im, log_step, b_re, b_im, c_re, c_im):
    lbr, lbi, l2r, l2i, bbr, bbi, lbbr, lbbi = _s5_params(lam_re, lam_im, log_step, b_re, b_im)
    eye = jnp.eye(SSM_BLK_GROUPS, dtype=F32)
    lead = (DEPTH, 2, SSM_BLKS, SSM_BLK_GROUPS)

    def in_block(x):
        x = x.reshape(lead + (SSM_GROUP, SSM_STATE))
        return jnp.einsum('ldbgcp,gh->ldbgchp', x, eye).reshape(
            DEPTH, 2, SSM_BLKS, SSM_BLK_CH, SSM_BLK_ST)

    def out_block(x):
        x = x.reshape(lead + (SSM_GROUP, SSM_STATE))
        return jnp.einsum('ldbgcp,gh->ldbgphc', x, eye).reshape(
            DEPTH, 2, SSM_BLKS, SSM_BLK_ST, SSM_BLK_CH)

    wb = jnp.concatenate([
        jnp.concatenate([in_block(bbr), in_block(bbi)], axis=-1),
        jnp.concatenate([in_block(lbbr), in_block(lbbi)], axis=-1)], axis=-2).astype(BF16)
    wc = jnp.concatenate([out_block(c_re), out_block(-c_im)], axis=-2).astype(BF16)

    def table(one, two):
        one = one.reshape(DEPTH, 2, SSM_BLKS, 1, SSM_BLK_ST)
        two = two.reshape(DEPTH, 2, SSM_BLKS, 1, SSM_BLK_ST)
        half = SUBLANE // 2
        fwd = jnp.concatenate([jnp.broadcast_to(one[:, 0:1], (DEPTH, 1, SSM_BLKS, half, SSM_BLK_ST)),
                               jnp.broadcast_to(two[:, 0:1], (DEPTH, 1, SSM_BLKS, half, SSM_BLK_ST))], axis=3)
        bwd = jnp.concatenate([jnp.broadcast_to(two[:, 1:2], (DEPTH, 1, SSM_BLKS, half, SSM_BLK_ST)),
                               jnp.broadcast_to(one[:, 1:2], (DEPTH, 1, SSM_BLKS, half, SSM_BLK_ST))], axis=3)
        return jnp.concatenate([fwd, bwd], axis=1)

    return wb, wc, table(lbr, l2r), table(lbi, l2i)


def _s5_scan_kernel(uf_ref, ub_ref, wb_ref, wc_ref, tre_ref, tim_ref, yf_ref, yb_ref,
                    xf, xb, carry):
    @pl.when(pl.program_id(1) == 0)
    def _():
        carry[...] = jnp.zeros(carry.shape, F32)

    n_tiles = S5_ROWS // SUBLANE
    half = SUBLANE // 2

    def paired_lhs(u, take_upper):
        u3 = u.reshape(n_tiles, SUBLANE, SSM_BLK_CH)
        swapped = pltpu.roll(u3, half, axis=1)
        sub = lax.broadcasted_iota(jnp.int32, u3.shape, 1)
        keep = (sub >= half) if take_upper else (sub < half)
        nb = jnp.where(keep, swapped, 0.0).reshape(S5_ROWS, SSM_BLK_CH)
        return jnp.concatenate([u, nb], axis=1).astype(BF16)

    xf[...] = _bdot(paired_lhs(uf_ref[...], True), wb_ref[0, 0, 0])
    xb[...] = _bdot(paired_lhs(ub_ref[...], False), wb_ref[0, 1, 0])

    lower = lax.broadcasted_iota(jnp.int32, (SUBLANE, 2 * LANE), 0) < half
    for hh in range(SSM_BLK_ST // (2 * LANE)):
        lo = hh * 2 * LANE
        re = slice(lo, lo + 2 * LANE)
        im = slice(SSM_BLK_ST + lo, SSM_BLK_ST + lo + 2 * LANE)
        afr, afi = tre_ref[0, 0, 0, :, re], tim_ref[0, 0, 0, :, re]
        abr, abi = tre_ref[0, 1, 0, :, re], tim_ref[0, 1, 0, :, re]

        def body(k, st):
            hfr, hfi, hbr, hbi = st
            rf = pl.multiple_of(k * SUBLANE, SUBLANE)
            pr = jnp.where(lower, pltpu.roll(hfr, half, axis=0), hfr)
            pi = jnp.where(lower, pltpu.roll(hfi, half, axis=0), hfi)
            nfr = xf[pl.ds(rf, SUBLANE), re] + (afr * pr - afi * pi)
            nfi = xf[pl.ds(rf, SUBLANE), im] + (afr * pi + afi * pr)
            xf[pl.ds(rf, SUBLANE), re] = nfr
            xf[pl.ds(rf, SUBLANE), im] = nfi
            rb = pl.multiple_of((n_tiles - 1 - k) * SUBLANE, SUBLANE)
            qr = jnp.where(lower, hbr, pltpu.roll(hbr, half, axis=0))
            qi = jnp.where(lower, hbi, pltpu.roll(hbi, half, axis=0))
            nbr = xb[pl.ds(rb, SUBLANE), re] + (abr * qr - abi * qi)
            nbi = xb[pl.ds(rb, SUBLANE), im] + (abr * qi + abi * qr)
            xb[pl.ds(rb, SUBLANE), re] = nbr
            xb[pl.ds(rb, SUBLANE), im] = nbi
            return nfr, nfi, nbr, nbi

        init = (carry[0, :, re], carry[1, :, re], carry[2, :, re], carry[3, :, re])
        fin = lax.fori_loop(0, n_tiles, body, init, unroll=2)
        for idx in range(4):
            carry[idx, :, re] = fin[idx]

    yf_ref[...] = _bdot(xf[...].astype(BF16), wc_ref[0, 0, 0])
    yb_ref[...] = _bdot(xb[...].astype(BF16), wc_ref[0, 1, 0])


def _s5_scan(proj, wb, wc, tre, tim, layer):
    last = S5_CHUNKS - 1
    u_blk0 = U_COL0 // SSM_BLK_CH
    y_shape = jax.ShapeDtypeStruct((T_TOK, SSM_WIDTH), F32)
    return pl.pallas_call(
        _s5_scan_kernel,
        grid=(SSM_BLKS, S5_CHUNKS),
        in_specs=[
            pl.BlockSpec((S5_ROWS, SSM_BLK_CH), lambda j, c: (c, u_blk0 + j)),
            pl.BlockSpec((S5_ROWS, SSM_BLK_CH), lambda j, c: (last - c, u_blk0 + j)),
            pl.BlockSpec((1, 2, 1, 2 * SSM_BLK_CH, 2 * SSM_BLK_ST), lambda j, c: (layer, 0, j, 0, 0)),
            pl.BlockSpec((1, 2, 1, 2 * SSM_BLK_ST, SSM_BLK_CH), lambda j, c: (layer, 0, j, 0, 0)),
            pl.BlockSpec((1, 2, 1, SUBLANE, SSM_BLK_ST), lambda j, c: (layer, 0, j, 0, 0)),
            pl.BlockSpec((1, 2, 1, SUBLANE, SSM_BLK_ST), lambda j, c: (layer, 0, j, 0, 0)),
        ],
        out_specs=[
            pl.BlockSpec((S5_ROWS, SSM_BLK_CH), lambda j, c: (c, j)),
            pl.BlockSpec((S5_ROWS, SSM_BLK_CH), lambda j, c: (last - c, j)),
        ],
        out_shape=[y_shape, y_shape],
        scratch_shapes=[
            pltpu.VMEM((S5_ROWS, 2 * SSM_BLK_ST), F32),
            pltpu.VMEM((S5_ROWS, 2 * SSM_BLK_ST), F32),
            pltpu.VMEM((4, SUBLANE, SSM_BLK_ST), F32),
        ],
        compiler_params=_cparams(("parallel", "arbitrary")),
        name="s5_scan",
    )(proj, proj, wb, wc, tre, tim)


def _gelu_tanh(x):
    return 0.5 * x * (1.0 + jnp.tanh(math.sqrt(2.0 / math.pi) * (x + 0.044715 * (x * x * x))))


def _glu_kernel(yf_ref, yb_ref, u_ref, d_ref, w_ref, o_ref):
    y = (yf_ref[...] + yb_ref[...]) + d_ref[...] * u_ref[...]
    g = _gelu_tanh(y)
    z = _bdot(g.astype(BF16), w_ref[...].astype(BF16))
    o_ref[...] = g * jax.nn.sigmoid(z)


def _glu(yf, yb, proj, d, w, tm):
    t = yf.shape[0]
    row = pl.BlockSpec((tm, SSM_WIDTH), lambda i: (i, 0))
    return pl.pallas_call(
        _glu_kernel,
        grid=(t // tm,),
        in_specs=[row, row,
                  pl.BlockSpec((tm, SSM_WIDTH), lambda i: (i, U_COL0 // SSM_WIDTH)),
                  pl.BlockSpec((1, SSM_WIDTH), lambda i: (0, 0)),
                  pl.BlockSpec((SSM_WIDTH, SSM_WIDTH), lambda i: (0, 0))],
        out_specs=row,
        out_shape=jax.ShapeDtypeStruct((t, SSM_WIDTH), F32),
        compiler_params=_cparams(("parallel",)),
        name="s5_glu",
    )(yf, yb, proj, d.reshape(1, -1), w)


def _outproj_kernel(h_ref, a_ref, s_ref, ga_ref, gs_ref, wa_ref, ws_ref, o_ref, an_ref, sn_ref):
    @pl.when(pl.program_id(1) == 0)
    def _():
        an_ref[...] = _rms(a_ref[...], ga_ref[...]).astype(BF16)
        sn_ref[...] = _rms(s_ref[...], gs_ref[...]).astype(BF16)

    o_ref[...] = (h_ref[...] + _bdot(an_ref[...], wa_ref[...].astype(BF16))
                  + _bdot(sn_ref[...], ws_ref[...].astype(BF16)))


def _outproj(h, att, ssm, ga, gs, w, tm, tn):
    t, d = h.shape
    return pl.pallas_call(
        _outproj_kernel,
        grid=(t // tm, d // tn),
        in_specs=[
            pl.BlockSpec((tm, tn), lambda i, j: (i, j)),
            pl.BlockSpec((tm, ATT_WIDTH), lambda i, j: (i, 0)),
            pl.BlockSpec((tm, SSM_WIDTH), lambda i, j: (i, 0)),
            pl.BlockSpec((1, ATT_WIDTH), lambda i, j: (0, 0)),
            pl.BlockSpec((1, SSM_WIDTH), lambda i, j: (0, 0)),
            pl.BlockSpec((ATT_WIDTH, tn), lambda i, j: (0, j)),
            pl.BlockSpec((SSM_WIDTH, tn), lambda i, j: (1, j)),
        ],
        out_specs=pl.BlockSpec((tm, tn), lambda i, j: (i, j)),
        out_shape=jax.ShapeDtypeStruct((t, d), F32),
        scratch_shapes=[pltpu.VMEM((tm, ATT_WIDTH), BF16), pltpu.VMEM((tm, SSM_WIDTH), BF16)],
        input_output_aliases={0: 0},
        compiler_params=_cparams(("parallel", "arbitrary")),
        name="out_proj",
    )(h, att, ssm, ga.reshape(1, -1), gs.reshape(1, -1), w, w)


def _ffn_kernel(x_ref, g_ref, wg_ref, wu_ref, wd_ref, o_ref, hn_ref):
    @pl.when(pl.program_id(1) == 0)
    def _():
        x = x_ref[...]
        hn_ref[...] = _rms(x, g_ref[...]).astype(BF16)
        o_ref[...] = x

    hn = hn_ref[...]
    a = _bdot(hn, wg_ref[...].astype(BF16))
    b = _bdot(hn, wu_ref[...].astype(BF16))
    hid = (a * jax.nn.sigmoid(a)) * b
    o_ref[...] += _bdot(hid.astype(BF16), wd_ref[...].astype(BF16))


def _ffn(h, g, wg, wu, wd, tm, tf):
    t, d = h.shape
    f = wg.shape[1]
    once = pl.Buffered(1)
    return pl.pallas_call(
        _ffn_kernel,
        grid=(t // tm, f // tf),
        in_specs=[
            pl.BlockSpec((tm, d), lambda i, j: (i, 0), pipeline_mode=once),
            pl.BlockSpec((1, d), lambda i, j: (0, 0)),
            pl.BlockSpec((d, tf), lambda i, j: (0, j)),
            pl.BlockSpec((d, tf), lambda i, j: (0, j)),
            pl.BlockSpec((tf, d), lambda i, j: (j, 0)),
        ],
        out_specs=pl.BlockSpec((tm, d), lambda i, j: (i, 0), pipeline_mode=once),
        out_shape=jax.ShapeDtypeStruct((t, d), F32),
        scratch_shapes=[pltpu.VMEM((tm, d), BF16)],
        input_output_aliases={0: 0},
        compiler_params=_cparams(("parallel", "arbitrary")),
        name="dense_ffn",
    )(h, g.reshape(1, -1), wg, wu, wd)


MOE_KC = 4
MOE_CW = D_MODEL // MOE_KC


def _split_bf16(x):
    hi = x.astype(BF16)
    lo = (x - hi.astype(F32)).astype(BF16)
    return hi, lo


def _router_gates(hn, wr):
    xh, xl = _split_bf16(hn)
    wh, wl = _split_bf16(wr)
    logits = _bdot(xh, wh) + (_bdot(xh, wl) + _bdot(xl, wh)) + _bdot(xl, wl)
    lane = lax.broadcasted_iota(jnp.int32, logits.shape, 1)
    neg = jnp.float32(-jnp.inf)
    logits = jnp.where(lane < N_EXPERTS, logits, neg)
    v1 = jnp.max(logits, axis=-1, keepdims=True)
    i1 = jnp.min(jnp.where(logits == v1, lane, LANE), axis=-1, keepdims=True)
    rest = jnp.where(lane == i1, neg, logits)
    v2 = jnp.max(rest, axis=-1, keepdims=True)
    i2 = jnp.min(jnp.where(rest == v2, lane, LANE), axis=-1, keepdims=True)
    e2 = jnp.exp(v2 - v1)
    w1 = 1.0 / (1.0 + e2)
    w2 = e2 / (1.0 + e2)
    return jnp.where(lane == i1, w1, 0.0) + jnp.where(lane == i2, w2, 0.0)


def _moe_kernel(x_ref, g_ref, wr_ref, wg_ref, wu_ref, wd_ref, o_ref,
                hn_ref, acc_ref, ga_ref, ua_ref, hid_ref, gate_ref):
    e = pl.program_id(1)
    s = pl.program_id(2)

    @pl.when((e == 0) & (s == 0))
    def _():
        x = x_ref[...]
        hn = _rms(x, g_ref[...])
        gate_ref[...] = _router_gates(hn, wr_ref[...])
        for k in range(MOE_KC):
            hn_ref[k] = hn[:, k * MOE_CW:(k + 1) * MOE_CW].astype(BF16)
            acc_ref[k] = x[:, k * MOE_CW:(k + 1) * MOE_CW]

    @pl.when(s < MOE_KC)
    def _():
        hk = hn_ref[jnp.minimum(s, MOE_KC - 1)]
        pa = _bdot(hk, wg_ref[0].astype(BF16))
        pu = _bdot(hk, wu_ref[0].astype(BF16))

        @pl.when(s == 0)
        def _():
            ga_ref[...] = pa
            ua_ref[...] = pu

        @pl.when(s > 0)
        def _():
            ga_ref[...] += pa
            ua_ref[...] += pu

    @pl.when(s == MOE_KC - 1)
    def _():
        gates = gate_ref[...]
        lane = lax.broadcasted_iota(jnp.int32, gates.shape, 1)
        ge = jnp.sum(jnp.where(lane == e, gates, 0.0), axis=-1, keepdims=True)
        a = ga_ref[...]
        hid_ref[...] = ((a * jax.nn.sigmoid(a)) * ua_ref[...] * ge).astype(BF16)

    @pl.when(s >= MOE_KC)
    def _():
        n = jnp.maximum(s - MOE_KC, 0)
        acc_ref[n] += _bdot(hid_ref[...], wd_ref[0].astype(BF16))

    @pl.when((e == N_EXPERTS - 1) & (s == 2 * MOE_KC - 1))
    def _():
        for k in range(MOE_KC):
            o_ref[:, k * MOE_CW:(k + 1) * MOE_CW] = acc_ref[k]


def _moe(h, g, wr, wg, wu, wd, tm):
    t, d = h.shape
    wr_pad = jnp.pad(wr, ((0, 0), (0, LANE - N_EXPERTS)))
    once = pl.Buffered(1)
    kc = MOE_KC
    return pl.pallas_call(
        _moe_kernel,
        grid=(t // tm, N_EXPERTS, 2 * kc),
        in_specs=[
            pl.BlockSpec((tm, d), lambda i, e, s: (i, 0), pipeline_mode=once),
            pl.BlockSpec((1, d), lambda i, e, s: (0, 0)),
            pl.BlockSpec((d, LANE), lambda i, e, s: (0, 0)),
            pl.BlockSpec((1, MOE_CW, D_FF_EXPERT), lambda i, e, s: (e, jnp.minimum(s, kc - 1), 0)),
            pl.BlockSpec((1, MOE_CW, D_FF_EXPERT), lambda i, e, s: (e, jnp.minimum(s, kc - 1), 0)),
            pl.BlockSpec((1, D_FF_EXPERT, MOE_CW), lambda i, e, s: (e, 0, jnp.maximum(s - kc, 0))),
        ],
        out_specs=pl.BlockSpec((tm, d), lambda i, e, s: (i, 0), pipeline_mode=once),
        out_shape=jax.ShapeDtypeStruct((t, d), F32),
        scratch_shapes=[
            pltpu.VMEM((kc, tm, MOE_CW), BF16),
            pltpu.VMEM((kc, tm, MOE_CW), F32),
            pltpu.VMEM((tm, D_FF_EXPERT), F32),
            pltpu.VMEM((tm, D_FF_EXPERT), F32),
            pltpu.VMEM((tm, D_FF_EXPERT), BF16),
            pltpu.VMEM((tm, LANE), F32),
        ],
        input_output_aliases={0: 0},
        compiler_params=_cparams(("parallel", "arbitrary", "arbitrary")),
        name="moe_dense",
    )(h, g.reshape(1, -1), wr_pad, wg, wu, wd)


def _final_norm_kernel(x_ref, g_ref, o_ref):
    o_ref[...] = _rms(x_ref[...], g_ref[...])


def _final_norm(h, g, tm):
    t, d = h.shape
    return pl.pallas_call(
        _final_norm_kernel,
        grid=(t // tm,),
        in_specs=[pl.BlockSpec((tm, d), lambda i: (i, 0)), pl.BlockSpec((1, d), lambda i: (0, 0))],
        out_specs=pl.BlockSpec((tm, d), lambda i: (i, 0)),
        out_shape=jax.ShapeDtypeStruct((t, d), F32),
        compiler_params=_cparams(("parallel",)),
        name="final_norm",
    )(h, g.reshape(1, -1))


def _rope_partner(w):
    half = QK_ROPE // 2
    return jnp.concatenate([-w[..., half:], w[..., :half]], axis=-1)


def _pad_cols(w, width):
    return jnp.pad(w, [(0, 0)] * (w.ndim - 1) + [(0, width - w.shape[-1])])


def _in_proj_layout(w_in):
    kr = w_in[..., Q_LORA + KV_LORA:Q_LORA + KV_LORA + QK_ROPE]
    return jnp.concatenate([
        w_in[..., :Q_LORA + KV_LORA],
        _pad_cols(kr, LANE),
        _pad_cols(_rope_partner(kr), LANE),
        w_in[..., Q_LORA + KV_LORA + QK_ROPE:]], axis=-1)


def _q_layouts(w_uq):
    w = w_uq.reshape(DEPTH, Q_LORA, ATT_HEADS, QK_NOPE + QK_ROPE)
    main = _pad_cols(w, HEAD_PAD).reshape(DEPTH, Q_LORA, ATT_HEADS * HEAD_PAD)
    rot = _pad_cols(_rope_partner(w[..., QK_NOPE:]), LANE).reshape(DEPTH, Q_LORA, ATT_HEADS * LANE)
    return main, rot


def _rope_tables():
    inv = ROPE_THETA ** (-jnp.arange(0, QK_ROPE, 2, dtype=F32) / QK_ROPE)
    ang = jnp.arange(L_TOK, dtype=F32)[:, None] * inv[None, :]
    cos = jnp.cos(ang)
    sin = jnp.sin(ang)
    cos_t = _pad_cols(jnp.concatenate([cos, cos], axis=-1), LANE)
    sin_t = _pad_cols(jnp.concatenate([sin, sin], axis=-1), LANE)
    return jnp.repeat(cos_t, BATCH, axis=0), jnp.repeat(sin_t, BATCH, axis=0)


def kernel(x, meta_tokens, mix_norm, w_in, q_norm, w_uq, kv_norm, w_ukv, ssm_lambda_re, ssm_lambda_im, ssm_log_step, ssm_b_re, ssm_b_im, ssm_c_re, ssm_c_im, ssm_d, ssm_w_glu, attn_out_norm, ssm_out_norm, w_out, ffn_norm, dense_w_gate, dense_w_up, dense_w_down, moe_router, moe_w_gate, moe_w_up, moe_w_down, final_norm):
    meta = jnp.broadcast_to(meta_tokens[None].astype(x.dtype), (BATCH, N_META, D_MODEL))
    h = jnp.concatenate([meta, x], axis=1).transpose(1, 0, 2).reshape(T_TOK, D_MODEL)
    cos_t, sin_t = _rope_tables()
    w_in_l = _in_proj_layout(w_in)
    wq_main, wq_rot = _q_layouts(w_uq)
    wb, wc, tre, tim = _s5_layouts(ssm_lambda_re, ssm_lambda_im, ssm_log_step,
                                   ssm_b_re, ssm_b_im, ssm_c_re, ssm_c_im)
    for layer in range(DEPTH):
        proj = _norm_matmul(h, mix_norm[layer], w_in_l[layer], tm=1376, tn=512)
        q, kv, kpe = _upproj(proj, q_norm[layer], kv_norm[layer], wq_main[layer], wq_rot[layer],
                             w_ukv[layer], cos_t, sin_t, tm=688)
        att = _attention(q, kv, kpe, tq=688)
        yf, yb = _s5_scan(proj, wb, wc, tre, tim, layer)
        ssm = _glu(yf, yb, proj, ssm_d[layer], ssm_w_glu[layer], tm=688)
        h = _outproj(h, att, ssm, attn_out_norm[layer], ssm_out_norm[layer], w_out[layer],
                     tm=1376, tn=512)
        i = layer // 2
        if layer % 2 == 0:
            h = _ffn(h, ffn_norm[layer], dense_w_gate[i], dense_w_up[i], dense_w_down[i],
                     tm=1376, tf=256)
        else:
            h = _moe(h, ffn_norm[layer], moe_router[i], moe_w_gate[i], moe_w_up[i], moe_w_down[i],
                     tm=688)
    out = _final_norm(h, final_norm, tm=688)
    return out.reshape(L_TOK, BATCH, D_MODEL)[N_META:].transpose(1, 0, 2)
```

```python
import functools
import math

import jax
import jax.numpy as jnp
from jax import lax
from jax.experimental import pallas as pl
from jax.experimental.pallas import tpu as pltpu

F32 = jnp.float32
BF16 = jnp.bfloat16

D_MODEL = 2048
BATCH = 4
SEQ = 2048
DEPTH = 4
N_META = 16
EPS = 1e-6
ATT_HEADS = 8
QK_NOPE = 128
QK_ROPE = 64
V_DIM = 128
Q_LORA = 512
KV_LORA = 256
ROPE_THETA = 10000.0
ATT_WIDTH = ATT_HEADS * V_DIM
SSM_GROUP = 16
SSM_WIDTH = D_MODEL - ATT_WIDTH
SSM_GROUPS = SSM_WIDTH // SSM_GROUP
SSM_STATE = 64
D_FF = 5632
N_EXPERTS = 8
D_FF_EXPERT = 1408

L_TOK = N_META + SEQ
T_TOK = L_TOK * BATCH
LANE = 128
SUBLANE = 8
HEAD_PAD = 256
L_PAD = 2176
PROJ_COLS = 2048
U_COL0 = 1024
VMEM_LIMIT = 58 * 1024 * 1024

SSM_BLK_GROUPS = 8
SSM_BLKS = SSM_GROUPS // SSM_BLK_GROUPS
SSM_BLK_CH = SSM_BLK_GROUPS * SSM_GROUP
SSM_BLK_ST = SSM_BLK_GROUPS * SSM_STATE
S5_CHUNKS = 3
S5_STEPS = L_TOK // S5_CHUNKS
S5_ROWS = S5_STEPS * BATCH


def _cparams(sem):
    return pltpu.CompilerParams(dimension_semantics=sem, vmem_limit_bytes=VMEM_LIMIT)


def _rms(x, g):
    ms = jnp.mean(x * x, axis=-1, keepdims=True)
    return x * lax.rsqrt(ms + EPS) * g


def _bdot(a, b):
    return jnp.dot(a, b, preferred_element_type=F32)


def _norm_matmul_kernel(x_ref, g_ref, w_ref, o_ref, hn_ref):
    @pl.when(pl.program_id(1) == 0)
    def _():
        hn_ref[...] = _rms(x_ref[...], g_ref[0]).astype(BF16)

    o_ref[...] = _bdot(hn_ref[...], w_ref[0].astype(BF16))


def _norm_matmul(x, g, w, layer, tm, tn):
    t, d = x.shape
    n = w.shape[2]
    return pl.pallas_call(
        _norm_matmul_kernel,
        grid=(t // tm, n // tn),
        in_specs=[
            pl.BlockSpec((tm, d), lambda i, j: (i, 0)),
            pl.BlockSpec((1, 1, d), lambda i, j: (layer, 0, 0)),
            pl.BlockSpec((1, d, tn), lambda i, j: (layer, 0, j)),
        ],
        out_specs=pl.BlockSpec((tm, tn), lambda i, j: (i, j)),
        out_shape=jax.ShapeDtypeStruct((t, n), F32),
        scratch_shapes=[pltpu.VMEM((tm, d), BF16)],
        compiler_params=_cparams(("parallel", "arbitrary")),
        name="norm_matmul",
    )(x, g.reshape(DEPTH, 1, d), w)


def _upproj_kernel(cq_ref, ckv_ref, kr_ref, krot_ref, qn_ref, kvn_ref, wq_ref, wqr_ref,
                   wkv_ref, cos_ref, sin_ref, q_ref, kv_ref, kpe_ref):
    scale = (QK_NOPE + QK_ROPE) ** -0.5
    cqn = _rms(cq_ref[...], qn_ref[0]).astype(BF16)
    a = _bdot(cqn, wq_ref[0].astype(BF16))
    r = _bdot(cqn, wqr_ref[0].astype(BF16))
    c = cos_ref[...]
    s = sin_ref[...]
    for h in range(ATT_HEADS):
        lo = h * HEAD_PAD
        q_ref[:, lo:lo + LANE] = (a[:, lo:lo + LANE] * scale).astype(BF16)
        rope = a[:, lo + LANE:lo + HEAD_PAD] * c + r[:, h * LANE:(h + 1) * LANE] * s
        q_ref[:, lo + LANE:lo + HEAD_PAD] = (rope * scale).astype(BF16)
    ckvn = _rms(ckv_ref[...], kvn_ref[0]).astype(BF16)
    kv_ref[...] = _bdot(ckvn, wkv_ref[0].astype(BF16)).astype(BF16)
    kpe_ref[...] = (kr_ref[...] * c + krot_ref[...] * s).astype(BF16)


def _upproj(proj, qn, kvn, wq, wqr, wkv, cos_t, sin_t, layer, tm):
    t = proj.shape[0]
    full = lambda a: pl.BlockSpec((1,) + a.shape[1:], lambda i: (layer, 0, 0))
    qn = qn.reshape(DEPTH, 1, Q_LORA)
    kvn = kvn.reshape(DEPTH, 1, KV_LORA)
    pos_blocks = L_TOK // tm
    return pl.pallas_call(
        _upproj_kernel,
        grid=(t // tm,),
        in_specs=[
            pl.BlockSpec((tm, Q_LORA), lambda i: (i, 0)),
            pl.BlockSpec((tm, KV_LORA), lambda i: (i, Q_LORA // KV_LORA)),
            pl.BlockSpec((tm, LANE), lambda i: (i, (Q_LORA + KV_LORA) // LANE)),
            pl.BlockSpec((tm, LANE), lambda i: (i, (Q_LORA + KV_LORA) // LANE + 1)),
            full(qn),
            full(kvn),
            full(wq),
            full(wqr),
            full(wkv),
            pl.BlockSpec((tm, LANE), lambda i: (i % pos_blocks, 0)),
            pl.BlockSpec((tm, LANE), lambda i: (i % pos_blocks, 0)),
        ],
        out_specs=[
            pl.BlockSpec((tm, ATT_HEADS * HEAD_PAD), lambda i: (i, 0)),
            pl.BlockSpec((tm, ATT_HEADS * (QK_NOPE + V_DIM)), lambda i: (i, 0)),
            pl.BlockSpec((tm, LANE), lambda i: (i, 0)),
        ],
        out_shape=[
            jax.ShapeDtypeStruct((t, ATT_HEADS * HEAD_PAD), BF16),
            jax.ShapeDtypeStruct((t, ATT_HEADS * (QK_NOPE + V_DIM)), BF16),
            jax.ShapeDtypeStruct((t, LANE), BF16),
        ],
        compiler_params=_cparams(("parallel",)),
        name="upproj",
    )(proj, proj, proj, proj, qn, kvn, wq, wqr, wkv, cos_t, sin_t)


def _attn_kernel(q_ref, kv_ref, kpe_ref, o_ref, k_sc, v_sc):
    @pl.when(pl.program_id(2) == 0)
    def _():
        k_sc[0:L_TOK, 0:LANE] = kv_ref[:, 0:LANE]
        k_sc[0:L_TOK, LANE:HEAD_PAD] = kpe_ref[...]
        k_sc[L_TOK:L_PAD, :] = jnp.zeros((L_PAD - L_TOK, HEAD_PAD), BF16)
        v_sc[0:L_TOK, :] = kv_ref[:, LANE:HEAD_PAD]
        v_sc[L_TOK:L_PAD, :] = jnp.zeros((L_PAD - L_TOK, V_DIM), BF16)

    s = lax.dot_general(q_ref[...], k_sc[...], (((1,), (1,)), ((), ())),
                        preferred_element_type=F32)
    col = lax.broadcasted_iota(jnp.int32, s.shape, 1)
    s = jnp.where(col < L_TOK, s, -1e30)
    m = jnp.max(s, axis=-1, keepdims=True)
    p = jnp.exp(s - m)
    l = jnp.sum(p, axis=-1, keepdims=True)
    o = _bdot(p.astype(BF16), v_sc[...])
    o_ref[...] = o / l


def _attention(q, kv, kpe, tq):
    nq = L_TOK // tq
    return pl.pallas_call(
        _attn_kernel,
        grid=(BATCH, ATT_HEADS, nq),
        in_specs=[
            pl.BlockSpec((tq, HEAD_PAD), lambda b, h, i: (b * nq + i, h)),
            pl.BlockSpec((L_TOK, HEAD_PAD), lambda b, h, i: (b, h)),
            pl.BlockSpec((L_TOK, LANE), lambda b, h, i: (b, 0)),
        ],
        out_specs=pl.BlockSpec((tq, V_DIM), lambda b, h, i: (b * nq + i, h)),
        out_shape=jax.ShapeDtypeStruct((T_TOK, ATT_WIDTH), F32),
        scratch_shapes=[pltpu.VMEM((L_PAD, HEAD_PAD), BF16), pltpu.VMEM((L_PAD, V_DIM), BF16)],
        compiler_params=_cparams(("parallel", "parallel", "arbitrary")),
        name="attention",
    )(q, kv, kpe)


def _s5_param_kernel(lre_ref, lim_ref, ls_ref, bre_ref, bim_ref,
                     lbr_o, lbi_o, l2r_o, l2i_o, bbr_o, bbi_o, lbbr_o, lbbi_o):
    lre = lre_ref[...]
    lim = lim_ref[...]
    dt = jnp.exp(ls_ref[...])
    mag = jnp.exp(lre * dt)
    ang = lim * dt
    br = mag * jnp.cos(ang)
    bi = mag * jnp.sin(ang)
    nr = br - 1.0
    den = lre * lre + lim * lim
    cr = (nr * lre + bi * lim) / den
    ci = (bi * lre - nr * lim) / den
    b_r = bre_ref[...]
    b_i = bim_ref[...]
    bbr = cr * b_r - ci * b_i
    bbi = cr * b_i + ci * b_r
    lbr_o[...] = br
    lbi_o[...] = bi
    l2r_o[...] = br * br - bi * bi
    l2i_o[...] = 2.0 * br * bi
    bbr_o[...] = bbr
    bbi_o[...] = bbi
    lbbr_o[...] = br * bbr - bi * bbi
    lbbi_o[...] = br * bbi + bi * bbr


def _s5_params(lam_re, lam_im, log_step, b_re, b_im):
    n = DEPTH * 2 * SSM_GROUPS
    lre = lam_re.reshape(n, 1, SSM_STATE)
    lim = lam_im.reshape(n, 1, SSM_STATE)
    ls = log_step.reshape(n, 1, 1)
    btr = b_re.transpose(0, 1, 2, 4, 3).reshape(n, SSM_GROUP, SSM_STATE)
    bti = b_im.transpose(0, 1, 2, 4, 3).reshape(n, SSM_GROUP, SSM_STATE)
    small = jax.ShapeDtypeStruct((n, 1, SSM_STATE), F32)
    big = jax.ShapeDtypeStruct((n, SSM_GROUP, SSM_STATE), F32)
    return pl.pallas_call(
        _s5_param_kernel,
        out_shape=[small, small, small, small, big, big, big, big],
        compiler_params=pltpu.CompilerParams(vmem_limit_bytes=VMEM_LIMIT),
        name="s5_params",
    )(lre, lim, ls, btr, bti)


def _s5_layouts(lam_re, lam_im, log_step, b_re, b_im, c_re, c_im):
    lbr, lbi, l2r, l2i, bbr, bbi, lbbr, lbbi = _s5_params(lam_re, lam_im, log_step, b_re, b_im)
    eye = jnp.eye(SSM_BLK_GROUPS, dtype=F32)
    lead = (DEPTH, 2, SSM_BLKS, SSM_BLK_GROUPS)

    def in_block(x):
        x = x.reshape(lead + (SSM_GROUP, SSM_STATE))
        return jnp.einsum('ldbgcp,gh->ldbgchp', x, eye).reshape(
            DEPTH, 2, SSM_BLKS, SSM_BLK_CH, SSM_BLK_ST)

    def out_block(x):
        x = x.reshape(lead + (SSM_GROUP, SSM_STATE))
        return jnp.einsum('ldbgcp,gh->ldbgphc', x, eye).reshape(
            DEPTH, 2, SSM_BLKS, SSM_BLK_ST, SSM_BLK_CH)

    wb = jnp.concatenate([
        jnp.concatenate([in_block(bbr), in_block(bbi)], axis=-1),
        jnp.concatenate([in_block(lbbr), in_block(lbbi)], axis=-1)], axis=-2).astype(BF16)
    wc = jnp.concatenate([out_block(c_re), out_block(-c_im)], axis=-2).astype(BF16)

    def table(one, two):
        one = one.reshape(DEPTH, 2, SSM_BLKS, 1, SSM_BLK_ST)
        two = two.reshape(DEPTH, 2, SSM_BLKS, 1, SSM_BLK_ST)
        half = SUBLANE // 2
        fwd = jnp.concatenate([jnp.broadcast_to(one[:, 0:1], (DEPTH, 1, SSM_BLKS, half, SSM_BLK_ST)),
                               jnp.broadcast_to(two[:, 0:1], (DEPTH, 1, SSM_BLKS, half, SSM_BLK_ST))], axis=3)
        bwd = jnp.concatenate([jnp.broadcast_to(two[:, 1:2], (DEPTH, 1, SSM_BLKS, half, SSM_BLK_ST)),
                               jnp.broadcast_to(one[:, 1:2], (DEPTH, 1, SSM_BLKS, half, SSM_BLK_ST))], axis=3)
        return jnp.concatenate([fwd, bwd], axis=1)

    return wb, wc, table(lbr, l2r), table(lbi, l2i)


def _s5_scan_kernel(uf_ref, ub_ref, wb_ref, wc_ref, tre_ref, tim_ref, yf_ref, yb_ref,
                    tmaj, xf, xb, carry):
    @pl.when(pl.program_id(1) == 0)
    def _():
        carry[...] = jnp.zeros(carry.shape, F32)

    n_tiles = S5_ROWS // SUBLANE
    half = SUBLANE // 2

    def time_major(u_ref):
        for b in range(BATCH):
            tmaj[pl.ds(b, S5_STEPS, stride=BATCH), :] = u_ref[b]
        return tmaj[...]

    def batch_major(y, y_ref):
        tmaj[...] = y
        for b in range(BATCH):
            y_ref[b] = tmaj[pl.ds(b, S5_STEPS, stride=BATCH), :]

    def paired_lhs(u, take_upper):
        u3 = u.reshape(n_tiles, SUBLANE, SSM_BLK_CH)
        swapped = pltpu.roll(u3, half, axis=1)
        sub = lax.broadcasted_iota(jnp.int32, u3.shape, 1)
        keep = (sub >= half) if take_upper else (sub < half)
        nb = jnp.where(keep, swapped, 0.0).reshape(S5_ROWS, SSM_BLK_CH)
        return jnp.concatenate([u, nb], axis=1).astype(BF16)

    xf[...] = _bdot(paired_lhs(time_major(uf_ref), True), wb_ref[0, 0, 0])
    xb[...] = _bdot(paired_lhs(time_major(ub_ref), False), wb_ref[0, 1, 0])

    lower = lax.broadcasted_iota(jnp.int32, (SUBLANE, 2 * LANE), 0) < half
    for hh in range(SSM_BLK_ST // (2 * LANE)):
        lo = hh * 2 * LANE
        re = slice(lo, lo + 2 * LANE)
        im = slice(SSM_BLK_ST + lo, SSM_BLK_ST + lo + 2 * LANE)
        afr, afi = tre_ref[0, 0, 0, :, re], tim_ref[0, 0, 0, :, re]
        abr, abi = tre_ref[0, 1, 0, :, re], tim_ref[0, 1, 0, :, re]

        def body(k, st):
            hfr, hfi, hbr, hbi = st
            rf = pl.multiple_of(k * SUBLANE, SUBLANE)
            pr = jnp.where(lower, pltpu.roll(hfr, half, axis=0), hfr)
            pi = jnp.where(lower, pltpu.roll(hfi, half, axis=0), hfi)
            nfr = xf[pl.ds(rf, SUBLANE), re] + (afr * pr - afi * pi)
            nfi = xf[pl.ds(rf, SUBLANE), im] + (afr * pi + afi * pr)
            xf[pl.ds(rf, SUBLANE), re] = nfr
            xf[pl.ds(rf, SUBLANE), im] = nfi
            rb = pl.multiple_of((n_tiles - 1 - k) * SUBLANE, SUBLANE)
            qr = jnp.where(lower, hbr, pltpu.roll(hbr, half, axis=0))
            qi = jnp.where(lower, hbi, pltpu.roll(hbi, half, axis=0))
            nbr = xb[pl.ds(rb, SUBLANE), re] + (abr * qr - abi * qi)
            nbi = xb[pl.ds(rb, SUBLANE), im] + (abr * qi + abi * qr)
            xb[pl.ds(rb, SUBLANE), re] = nbr
            xb[pl.ds(rb, SUBLANE), im] = nbi
            return nfr, nfi, nbr, nbi

        init = (carry[0, :, re], carry[1, :, re], carry[2, :, re], carry[3, :, re])
        fin = lax.fori_loop(0, n_tiles, body, init, unroll=2)
        for idx in range(4):
            carry[idx, :, re] = fin[idx]

    batch_major(_bdot(xf[...].astype(BF16), wc_ref[0, 0, 0]), yf_ref)
    batch_major(_bdot(xb[...].astype(BF16), wc_ref[0, 1, 0]), yb_ref)


def _s5_scan(proj, wb, wc, tre, tim, layer):
    last = S5_CHUNKS - 1
    u_blk0 = U_COL0 // SSM_BLK_CH
    proj3 = proj.reshape(BATCH, L_TOK, PROJ_COLS)
    y_shape = jax.ShapeDtypeStruct((BATCH, L_TOK, SSM_WIDTH), F32)
    yf, yb = pl.pallas_call(
        _s5_scan_kernel,
        grid=(SSM_BLKS, S5_CHUNKS),
        in_specs=[
            pl.BlockSpec((BATCH, S5_STEPS, SSM_BLK_CH), lambda j, c: (0, c, u_blk0 + j)),
            pl.BlockSpec((BATCH, S5_STEPS, SSM_BLK_CH), lambda j, c: (0, last - c, u_blk0 + j)),
            pl.BlockSpec((1, 2, 1, 2 * SSM_BLK_CH, 2 * SSM_BLK_ST), lambda j, c: (layer, 0, j, 0, 0)),
            pl.BlockSpec((1, 2, 1, 2 * SSM_BLK_ST, SSM_BLK_CH), lambda j, c: (layer, 0, j, 0, 0)),
            pl.BlockSpec((1, 2, 1, SUBLANE, SSM_BLK_ST), lambda j, c: (layer, 0, j, 0, 0)),
            pl.BlockSpec((1, 2, 1, SUBLANE, SSM_BLK_ST), lambda j, c: (layer, 0, j, 0, 0)),
        ],
        out_specs=[
            pl.BlockSpec((BATCH, S5_STEPS, SSM_BLK_CH), lambda j, c: (0, c, j)),
            pl.BlockSpec((BATCH, S5_STEPS, SSM_BLK_CH), lambda j, c: (0, last - c, j)),
        ],
        out_shape=[y_shape, y_shape],
        scratch_shapes=[
            pltpu.VMEM((S5_ROWS, SSM_BLK_CH), F32),
            pltpu.VMEM((S5_ROWS, 2 * SSM_BLK_ST), F32),
            pltpu.VMEM((S5_ROWS, 2 * SSM_BLK_ST), F32),
            pltpu.VMEM((4, SUBLANE, SSM_BLK_ST), F32),
        ],
        compiler_params=_cparams(("parallel", "arbitrary")),
        name="s5_scan",
    )(proj3, proj3, wb, wc, tre, tim)
    return yf.reshape(T_TOK, SSM_WIDTH), yb.reshape(T_TOK, SSM_WIDTH)


def _gelu_tanh(x):
    return 0.5 * x * (1.0 + jnp.tanh(math.sqrt(2.0 / math.pi) * (x + 0.044715 * (x * x * x))))


def _glu_kernel(yf_ref, yb_ref, u_ref, d_ref, w_ref, o_ref):
    y = (yf_ref[...] + yb_ref[...]) + d_ref[0] * u_ref[...]
    g = _gelu_tanh(y)
    z = _bdot(g.astype(BF16), w_ref[0].astype(BF16))
    o_ref[...] = g * jax.nn.sigmoid(z)


def _glu(yf, yb, proj, d, w, layer, tm):
    t = yf.shape[0]
    row = pl.BlockSpec((tm, SSM_WIDTH), lambda i: (i, 0))
    return pl.pallas_call(
        _glu_kernel,
        grid=(t // tm,),
        in_specs=[row, row,
                  pl.BlockSpec((tm, SSM_WIDTH), lambda i: (i, U_COL0 // SSM_WIDTH)),
                  pl.BlockSpec((1, 1, SSM_WIDTH), lambda i: (layer, 0, 0)),
                  pl.BlockSpec((1, SSM_WIDTH, SSM_WIDTH), lambda i: (layer, 0, 0))],
        out_specs=row,
        out_shape=jax.ShapeDtypeStruct((t, SSM_WIDTH), F32),
        compiler_params=_cparams(("parallel",)),
        name="s5_glu",
    )(yf, yb, proj, d.reshape(DEPTH, 1, SSM_WIDTH), w)


def _outproj_kernel(h_ref, a_ref, s_ref, ga_ref, gs_ref, wa_ref, ws_ref, o_ref, an_ref, sn_ref):
    @pl.when(pl.program_id(1) == 0)
    def _():
        an_ref[...] = _rms(a_ref[...], ga_ref[0]).astype(BF16)
        sn_ref[...] = _rms(s_ref[...], gs_ref[0]).astype(BF16)

    o_ref[...] = (h_ref[...] + _bdot(an_ref[...], wa_ref[0].astype(BF16))
                  + _bdot(sn_ref[...], ws_ref[0].astype(BF16)))


def _outproj(h, att, ssm, ga, gs, w, layer, tm, tn):
    t, d = h.shape
    return pl.pallas_call(
        _outproj_kernel,
        grid=(t // tm, d // tn),
        in_specs=[
            pl.BlockSpec((tm, tn), lambda i, j: (i, j)),
            pl.BlockSpec((tm, ATT_WIDTH), lambda i, j: (i, 0)),
            pl.BlockSpec((tm, SSM_WIDTH), lambda i, j: (i, 0)),
            pl.BlockSpec((1, 1, ATT_WIDTH), lambda i, j: (layer, 0, 0)),
            pl.BlockSpec((1, 1, SSM_WIDTH), lambda i, j: (layer, 0, 0)),
            pl.BlockSpec((1, ATT_WIDTH, tn), lambda i, j: (layer, 0, j)),
            pl.BlockSpec((1, SSM_WIDTH, tn), lambda i, j: (layer, 1, j)),
        ],
        out_specs=pl.BlockSpec((tm, tn), lambda i, j: (i, j)),
        out_shape=jax.ShapeDtypeStruct((t, d), F32),
        scratch_shapes=[pltpu.VMEM((tm, ATT_WIDTH), BF16), pltpu.VMEM((tm, SSM_WIDTH), BF16)],
        input_output_aliases={0: 0},
        compiler_params=_cparams(("parallel", "arbitrary")),
        name="out_proj",
    )(h, att, ssm, ga.reshape(DEPTH, 1, ATT_WIDTH), gs.reshape(DEPTH, 1, SSM_WIDTH), w, w)


def _ffn_kernel(x_ref, g_ref, wg_ref, wu_ref, wd_ref, o_ref, hn_ref):
    @pl.when(pl.program_id(1) == 0)
    def _():
        x = x_ref[...]
        hn_ref[...] = _rms(x, g_ref[0]).astype(BF16)
        o_ref[...] = x

    hn = hn_ref[...]
    a = _bdot(hn, wg_ref[0].astype(BF16))
    b = _bdot(hn, wu_ref[0].astype(BF16))
    hid = (a * jax.nn.sigmoid(a)) * b
    o_ref[...] += _bdot(hid.astype(BF16), wd_ref[0].astype(BF16))


def _ffn(h, g, wg, wu, wd, layer, tm, tf):
    t, d = h.shape
    f = wg.shape[2]
    idx = layer // 2
    once = pl.Buffered(1)
    return pl.pallas_call(
        _ffn_kernel,
        grid=(t // tm, f // tf),
        in_specs=[
            pl.BlockSpec((tm, d), lambda i, j: (i, 0), pipeline_mode=once),
            pl.BlockSpec((1, 1, d), lambda i, j: (layer, 0, 0)),
            pl.BlockSpec((1, d, tf), lambda i, j: (idx, 0, j)),
            pl.BlockSpec((1, d, tf), lambda i, j: (idx, 0, j)),
            pl.BlockSpec((1, tf, d), lambda i, j: (idx, j, 0)),
        ],
        out_specs=pl.BlockSpec((tm, d), lambda i, j: (i, 0), pipeline_mode=once),
        out_shape=jax.ShapeDtypeStruct((t, d), F32),
        scratch_shapes=[pltpu.VMEM((tm, d), BF16)],
        input_output_aliases={0: 0},
        compiler_params=_cparams(("parallel", "arbitrary")),
        name="dense_ffn",
    )(h, g.reshape(DEPTH, 1, d), wg, wu, wd)


MOE_KC = 4
MOE_CW = D_MODEL // MOE_KC


def _split_bf16(x):
    hi = x.astype(BF16)
    lo = (x - hi.astype(F32)).astype(BF16)
    return hi, lo


def _router_gates(hn, wr):
    xh, xl = _split_bf16(hn)
    wh, wl = _split_bf16(wr)
    logits = _bdot(xh, wh) + (_bdot(xh, wl) + _bdot(xl, wh)) + _bdot(xl, wl)
    lane = lax.broadcasted_iota(jnp.int32, logits.shape, 1)
    neg = jnp.float32(-jnp.inf)
    logits = jnp.where(lane < N_EXPERTS, logits, neg)
    v1 = jnp.max(logits, axis=-1, keepdims=True)
    i1 = jnp.min(jnp.where(logits == v1, lane, LANE), axis=-1, keepdims=True)
    rest = jnp.where(lane == i1, neg, logits)
    v2 = jnp.max(rest, axis=-1, keepdims=True)
    i2 = jnp.min(jnp.where(rest == v2, lane, LANE), axis=-1, keepdims=True)
    e2 = jnp.exp(v2 - v1)
    w1 = 1.0 / (1.0 + e2)
    w2 = e2 / (1.0 + e2)
    return jnp.where(lane == i1, w1, 0.0) + jnp.where(lane == i2, w2, 0.0)


def _moe_kernel(x_ref, g_ref, wr_ref, wg_ref, wu_ref, wd_ref, o_ref,
                hn_ref, acc_ref, ga_ref, ua_ref, hid_ref, gate_ref):
    e = pl.program_id(1)
    s = pl.program_id(2)

    @pl.when((e == 0) & (s == 0))
    def _():
        x = x_ref[...]
        hn = _rms(x, g_ref[0])
        gate_ref[...] = _router_gates(hn, wr_ref[0])
        for k in range(MOE_KC):
            hn_ref[k] = hn[:, k * MOE_CW:(k + 1) * MOE_CW].astype(BF16)
            acc_ref[k] = x[:, k * MOE_CW:(k + 1) * MOE_CW]

    @pl.when(s < MOE_KC)
    def _():
        hk = hn_ref[jnp.minimum(s, MOE_KC - 1)]
        pa = _bdot(hk, wg_ref[0, 0].astype(BF16))
        pu = _bdot(hk, wu_ref[0, 0].astype(BF16))

        @pl.when(s == 0)
        def _():
            ga_ref[...] = pa
            ua_ref[...] = pu

        @pl.when(s > 0)
        def _():
            ga_ref[...] += pa
            ua_ref[...] += pu

    @pl.when(s == MOE_KC - 1)
    def _():
        gates = gate_ref[...]
        lane = lax.broadcasted_iota(jnp.int32, gates.shape, 1)
        ge = jnp.sum(jnp.where(lane == e, gates, 0.0), axis=-1, keepdims=True)
        a = ga_ref[...]
        hid_ref[...] = ((a * jax.nn.sigmoid(a)) * ua_ref[...] * ge).astype(BF16)

    @pl.when(s >= MOE_KC)
    def _():
        n = jnp.maximum(s - MOE_KC, 0)
        acc_ref[n] += _bdot(hid_ref[...], wd_ref[0, 0].astype(BF16))

    @pl.when((e == N_EXPERTS - 1) & (s == 2 * MOE_KC - 1))
    def _():
        for k in range(MOE_KC):
            o_ref[:, k * MOE_CW:(k + 1) * MOE_CW] = acc_ref[k]


def _moe(h, g, wr_pad, wg, wu, wd, layer, tm):
    t, d = h.shape
    idx = layer // 2
    once = pl.Buffered(1)
    kc = MOE_KC
    return pl.pallas_call(
        _moe_kernel,
        grid=(t // tm, N_EXPERTS, 2 * kc),
        in_specs=[
            pl.BlockSpec((tm, d), lambda i, e, s: (i, 0), pipeline_mode=once),
            pl.BlockSpec((1, 1, d), lambda i, e, s: (layer, 0, 0)),
            pl.BlockSpec((1, d, LANE), lambda i, e, s: (idx, 0, 0)),
            pl.BlockSpec((1, 1, MOE_CW, D_FF_EXPERT),
                         lambda i, e, s: (idx, e, jnp.minimum(s, kc - 1), 0)),
            pl.BlockSpec((1, 1, MOE_CW, D_FF_EXPERT),
                         lambda i, e, s: (idx, e, jnp.minimum(s, kc - 1), 0)),
            pl.BlockSpec((1, 1, D_FF_EXPERT, MOE_CW),
                         lambda i, e, s: (idx, e, 0, jnp.maximum(s - kc, 0))),
        ],
        out_specs=pl.BlockSpec((tm, d), lambda i, e, s: (i, 0), pipeline_mode=once),
        out_shape=jax.ShapeDtypeStruct((t, d), F32),
        scratch_shapes=[
            pltpu.VMEM((kc, tm, MOE_CW), BF16),
            pltpu.VMEM((kc, tm, MOE_CW), F32),
            pltpu.VMEM((tm, D_FF_EXPERT), F32),
            pltpu.VMEM((tm, D_FF_EXPERT), F32),
            pltpu.VMEM((tm, D_FF_EXPERT), BF16),
            pltpu.VMEM((tm, LANE), F32),
        ],
        input_output_aliases={0: 0},
        compiler_params=_cparams(("parallel", "arbitrary", "arbitrary")),
        name="moe_dense",
    )(h, g.reshape(DEPTH, 1, d), wr_pad, wg, wu, wd)


def _final_norm_kernel(x_ref, g_ref, o_ref):
    o_ref[...] = _rms(x_ref[...], g_ref[...])


def _final_norm(h, g, tm):
    t, d = h.shape
    return pl.pallas_call(
        _final_norm_kernel,
        grid=(t // tm,),
        in_specs=[pl.BlockSpec((tm, d), lambda i: (i, 0)), pl.BlockSpec((1, d), lambda i: (0, 0))],
        out_specs=pl.BlockSpec((tm, d), lambda i: (i, 0)),
        out_shape=jax.ShapeDtypeStruct((t, d), F32),
        compiler_params=_cparams(("parallel",)),
        name="final_norm",
    )(h, g.reshape(1, -1))


def _rope_partner(w):
    half = QK_ROPE // 2
    return jnp.concatenate([-w[..., half:], w[..., :half]], axis=-1)


def _pad_cols(w, width):
    return jnp.pad(w, [(0, 0)] * (w.ndim - 1) + [(0, width - w.shape[-1])])


def _in_proj_layout(w_in):
    kr = w_in[..., Q_LORA + KV_LORA:Q_LORA + KV_LORA + QK_ROPE]
    return jnp.concatenate([
        w_in[..., :Q_LORA + KV_LORA],
        _pad_cols(kr, LANE),
        _pad_cols(_rope_partner(kr), LANE),
        w_in[..., Q_LORA + KV_LORA + QK_ROPE:]], axis=-1)


def _q_layouts(w_uq):
    w = w_uq.reshape(DEPTH, Q_LORA, ATT_HEADS, QK_NOPE + QK_ROPE)
    main = _pad_cols(w, HEAD_PAD).reshape(DEPTH, Q_LORA, ATT_HEADS * HEAD_PAD)
    rot = _pad_cols(_rope_partner(w[..., QK_NOPE:]), LANE).reshape(DEPTH, Q_LORA, ATT_HEADS * LANE)
    return main, rot


def _rope_tables():
    inv = ROPE_THETA ** (-jnp.arange(0, QK_ROPE, 2, dtype=F32) / QK_ROPE)
    ang = jnp.arange(L_TOK, dtype=F32)[:, None] * inv[None, :]
    cos = jnp.cos(ang)
    sin = jnp.sin(ang)
    cos_t = _pad_cols(jnp.concatenate([cos, cos], axis=-1), LANE)
    sin_t = _pad_cols(jnp.concatenate([sin, sin], axis=-1), LANE)
    return cos_t, sin_t


def kernel(x, meta_tokens, mix_norm, w_in, q_norm, w_uq, kv_norm, w_ukv, ssm_lambda_re, ssm_lambda_im, ssm_log_step, ssm_b_re, ssm_b_im, ssm_c_re, ssm_c_im, ssm_d, ssm_w_glu, attn_out_norm, ssm_out_norm, w_out, ffn_norm, dense_w_gate, dense_w_up, dense_w_down, moe_router, moe_w_gate, moe_w_up, moe_w_down, final_norm):
    meta = jnp.broadcast_to(meta_tokens[None].astype(x.dtype), (BATCH, N_META, D_MODEL))
    h = jnp.concatenate([meta, x], axis=1).reshape(T_TOK, D_MODEL)
    cos_t, sin_t = _rope_tables()
    w_in_l = _in_proj_layout(w_in)
    wq_main, wq_rot = _q_layouts(w_uq)
    wr_pad = _pad_cols(moe_router, LANE)
    wb, wc, tre, tim = _s5_layouts(ssm_lambda_re, ssm_lambda_im, ssm_log_step,
                                   ssm_b_re, ssm_b_im, ssm_c_re, ssm_c_im)
    for layer in range(DEPTH):
        proj = _norm_matmul(h, mix_norm, w_in_l, layer, tm=1376, tn=512)
        q, kv, kpe = _upproj(proj, q_norm, kv_norm, wq_main, wq_rot, w_ukv, cos_t, sin_t, layer,
                             tm=688)
        att = _attention(q, kv, kpe, tq=688)
        yf, yb = _s5_scan(proj, wb, wc, tre, tim, layer)
        ssm = _glu(yf, yb, proj, ssm_d, ssm_w_glu, layer, tm=688)
        h = _outproj(h, att, ssm, attn_out_norm, ssm_out_norm, w_out, layer, tm=1376, tn=512)
        if layer % 2 == 0:
            h = _ffn(h, ffn_norm, dense_w_gate, dense_w_up, dense_w_down, layer, tm=1376, tf=256)
        else:
            h = _moe(h, ffn_norm, wr_pad, moe_w_gate, moe_w_up, moe_w_down, layer, tm=688)
    out = _final_norm(h, final_norm, tm=688)
    return out.reshape(BATCH, L_TOK, D_MODEL)[:, N_META:]
```

```python
import functools
import math

import jax
import jax.numpy as jnp
from jax import lax
from jax.experimental import pallas as pl
from jax.experimental.pallas import tpu as pltpu

F32 = jnp.float32
BF16 = jnp.bfloat16

D_MODEL = 2048
BATCH = 4
SEQ = 2048
DEPTH = 4
N_META = 16
EPS = 1e-6
ATT_HEADS = 8
QK_NOPE = 128
QK_ROPE = 64
V_DIM = 128
Q_LORA = 512
KV_LORA = 256
ROPE_THETA = 10000.0
ATT_WIDTH = ATT_HEADS * V_DIM
SSM_GROUP = 16
SSM_WIDTH = D_MODEL - ATT_WIDTH
SSM_GROUPS = SSM_WIDTH // SSM_GROUP
SSM_STATE = 64
D_FF = 5632
N_EXPERTS = 8
D_FF_EXPERT = 1408

L_TOK = N_META + SEQ
T_TOK = L_TOK * BATCH
LANE = 128
SUBLANE = 8
HEAD_PAD = 256
L_PAD = 2176
PROJ_COLS = 2048
U_COL0 = 1024
VMEM_LIMIT = 58 * 1024 * 1024

SSM_BLK_GROUPS = 8
SSM_BLKS = SSM_GROUPS // SSM_BLK_GROUPS
SSM_BLK_CH = SSM_BLK_GROUPS * SSM_GROUP
SSM_BLK_ST = SSM_BLK_GROUPS * SSM_STATE
S5_CHUNKS = 3
S5_STEPS = L_TOK // S5_CHUNKS
S5_ROWS = S5_STEPS * BATCH


def _cparams(sem):
    return pltpu.CompilerParams(dimension_semantics=sem, vmem_limit_bytes=VMEM_LIMIT)


def _rms(x, g):
    ms = jnp.mean(x * x, axis=-1, keepdims=True)
    return x * lax.rsqrt(ms + EPS) * g


def _bdot(a, b):
    return jnp.dot(a, b, preferred_element_type=F32)


def _norm_matmul_kernel(x_ref, g_ref, w_ref, o_ref, hn_ref):
    @pl.when(pl.program_id(1) == 0)
    def _():
        hn_ref[...] = _rms(x_ref[...], g_ref[0]).astype(BF16)

    o_ref[...] = _bdot(hn_ref[...], w_ref[0].astype(BF16))


def _norm_matmul(x, g, w, layer, tm, tn):
    t, d = x.shape
    n = w.shape[2]
    return pl.pallas_call(
        _norm_matmul_kernel,
        grid=(t // tm, n // tn),
        in_specs=[
            pl.BlockSpec((tm, d), lambda i, j: (i, 0)),
            pl.BlockSpec((1, 1, d), lambda i, j: (layer, 0, 0)),
            pl.BlockSpec((1, d, tn), lambda i, j: (layer, 0, j)),
        ],
        out_specs=pl.BlockSpec((tm, tn), lambda i, j: (i, j)),
        out_shape=jax.ShapeDtypeStruct((t, n), F32),
        scratch_shapes=[pltpu.VMEM((tm, d), BF16)],
        compiler_params=_cparams(("parallel", "arbitrary")),
        name="norm_matmul",
    )(x, g.reshape(DEPTH, 1, d), w)


def _upproj_kernel(cq_ref, ckv_ref, kr_ref, krot_ref, qn_ref, kvn_ref, wq_ref, wqr_ref,
                   wkv_ref, cos_ref, sin_ref, q_ref, kv_ref, kpe_ref):
    scale = (QK_NOPE + QK_ROPE) ** -0.5
    cqn = _rms(cq_ref[...], qn_ref[0]).astype(BF16)
    a = _bdot(cqn, wq_ref[0].astype(BF16))
    r = _bdot(cqn, wqr_ref[0].astype(BF16))
    c = cos_ref[...]
    s = sin_ref[...]
    for h in range(ATT_HEADS):
        lo = h * HEAD_PAD
        q_ref[:, lo:lo + LANE] = (a[:, lo:lo + LANE] * scale).astype(BF16)
        rope = a[:, lo + LANE:lo + HEAD_PAD] * c + r[:, h * LANE:(h + 1) * LANE] * s
        q_ref[:, lo + LANE:lo + HEAD_PAD] = (rope * scale).astype(BF16)
    ckvn = _rms(ckv_ref[...], kvn_ref[0]).astype(BF16)
    kv_ref[...] = _bdot(ckvn, wkv_ref[0].astype(BF16)).astype(BF16)
    kpe_ref[...] = (kr_ref[...] * c + krot_ref[...] * s).astype(BF16)


def _upproj(proj, qn, kvn, wq, wqr, wkv, cos_t, sin_t, layer, tm):
    t = proj.shape[0]
    full = lambda a: pl.BlockSpec((1,) + a.shape[1:], lambda i: (layer, 0, 0))
    qn = qn.reshape(DEPTH, 1, Q_LORA)
    kvn = kvn.reshape(DEPTH, 1, KV_LORA)
    pos_blocks = L_TOK // tm
    return pl.pallas_call(
        _upproj_kernel,
        grid=(t // tm,),
        in_specs=[
            pl.BlockSpec((tm, Q_LORA), lambda i: (i, 0)),
            pl.BlockSpec((tm, KV_LORA), lambda i: (i, Q_LORA // KV_LORA)),
            pl.BlockSpec((tm, LANE), lambda i: (i, (Q_LORA + KV_LORA) // LANE)),
            pl.BlockSpec((tm, LANE), lambda i: (i, (Q_LORA + KV_LORA) // LANE + 1)),
            full(qn),
            full(kvn),
            full(wq),
            full(wqr),
            full(wkv),
            pl.BlockSpec((tm, LANE), lambda i: (i % pos_blocks, 0)),
            pl.BlockSpec((tm, LANE), lambda i: (i % pos_blocks, 0)),
        ],
        out_specs=[
            pl.BlockSpec((tm, ATT_HEADS * HEAD_PAD), lambda i: (i, 0)),
            pl.BlockSpec((tm, ATT_HEADS * (QK_NOPE + V_DIM)), lambda i: (i, 0)),
            pl.BlockSpec((tm, LANE), lambda i: (i, 0)),
        ],
        out_shape=[
            jax.ShapeDtypeStruct((t, ATT_HEADS * HEAD_PAD), BF16),
            jax.ShapeDtypeStruct((t, ATT_HEADS * (QK_NOPE + V_DIM)), BF16),
            jax.ShapeDtypeStruct((t, LANE), BF16),
        ],
        compiler_params=_cparams(("parallel",)),
        name="upproj",
    )(proj, proj, proj, proj, qn, kvn, wq, wqr, wkv, cos_t, sin_t)


def _attn_kernel(q_ref, kv_ref, kpe_ref, o_ref, k_sc, v_sc):
    @pl.when(pl.program_id(2) == 0)
    def _():
        k_sc[0:L_TOK, 0:LANE] = kv_ref[:, 0:LANE]
        k_sc[0:L_TOK, LANE:HEAD_PAD] = kpe_ref[...]
        k_sc[L_TOK:L_PAD, :] = jnp.zeros((L_PAD - L_TOK, HEAD_PAD), BF16)
        v_sc[0:L_TOK, :] = kv_ref[:, LANE:HEAD_PAD]
        v_sc[L_TOK:L_PAD, :] = jnp.zeros((L_PAD - L_TOK, V_DIM), BF16)

    s = lax.dot_general(q_ref[...], k_sc[...], (((1,), (1,)), ((), ())),
                        preferred_element_type=F32)
    col = lax.broadcasted_iota(jnp.int32, s.shape, 1)
    s = jnp.where(col < L_TOK, s, -1e30)
    m = jnp.max(s, axis=-1, keepdims=True)
    p = jnp.exp(s - m)
    l = jnp.sum(p, axis=-1, keepdims=True)
    o = _bdot(p.astype(BF16), v_sc[...])
    o_ref[...] = o / l


def _attention(q, kv, kpe, tq):
    nq = L_TOK // tq
    return pl.pallas_call(
        _attn_kernel,
        grid=(BATCH, ATT_HEADS, nq),
        in_specs=[
            pl.BlockSpec((tq, HEAD_PAD), lambda b, h, i: (b * nq + i, h)),
            pl.BlockSpec((L_TOK, HEAD_PAD), lambda b, h, i: (b, h)),
            pl.BlockSpec((L_TOK, LANE), lambda b, h, i: (b, 0)),
        ],
        out_specs=pl.BlockSpec((tq, V_DIM), lambda b, h, i: (b * nq + i, h)),
        out_shape=jax.ShapeDtypeStruct((T_TOK, ATT_WIDTH), F32),
        scratch_shapes=[pltpu.VMEM((L_PAD, HEAD_PAD), BF16), pltpu.VMEM((L_PAD, V_DIM), BF16)],
        compiler_params=_cparams(("parallel", "parallel", "arbitrary")),
        name="attention",
    )(q, kv, kpe)


def _s5_param_kernel(lre_ref, lim_ref, ls_ref, bre_ref, bim_ref,
                     lbr_o, lbi_o, l2r_o, l2i_o, bbr_o, bbi_o, lbbr_o, lbbi_o):
    lre = lre_ref[...]
    lim = lim_ref[...]
    dt = jnp.exp(ls_ref[...])
    mag = jnp.exp(lre * dt)
    ang = lim * dt
    br = mag * jnp.cos(ang)
    bi = mag * jnp.sin(ang)
    nr = br - 1.0
    den = lre * lre + lim * lim
    cr = (nr * lre + bi * lim) / den
    ci = (bi * lre - nr * lim) / den
    b_r = bre_ref[...]
    b_i = bim_ref[...]
    bbr = cr * b_r - ci * b_i
    bbi = cr * b_i + ci * b_r
    lbr_o[...] = br
    lbi_o[...] = bi
    l2r_o[...] = br * br - bi * bi
    l2i_o[...] = 2.0 * br * bi
    bbr_o[...] = bbr
    bbi_o[...] = bbi
    lbbr_o[...] = br * bbr - bi * bbi
    lbbi_o[...] = br * bbi + bi * bbr


def _s5_params(lam_re, lam_im, log_step, b_re, b_im):
    n = DEPTH * 2 * SSM_GROUPS
    lre = lam_re.reshape(n, 1, SSM_STATE)
    lim = lam_im.reshape(n, 1, SSM_STATE)
    ls = log_step.reshape(n, 1, 1)
    btr = b_re.transpose(0, 1, 2, 4, 3).reshape(n, SSM_GROUP, SSM_STATE)
    bti = b_im.transpose(0, 1, 2, 4, 3).reshape(n, SSM_GROUP, SSM_STATE)
    small = jax.ShapeDtypeStruct((n, 1, SSM_STATE), F32)
    big = jax.ShapeDtypeStruct((n, SSM_GROUP, SSM_STATE), F32)
    return pl.pallas_call(
        _s5_param_kernel,
        out_shape=[small, small, small, small, big, big, big, big],
        compiler_params=pltpu.CompilerParams(vmem_limit_bytes=VMEM_LIMIT),
        name="s5_params",
    )(lre, lim, ls, btr, bti)


def _s5_layouts(lam_re, lam_im, log_step, b_re, b_im, c_re, c_im):
    lbr, lbi, l2r, l2i, bbr, bbi, lbbr, lbbi = _s5_params(lam_re, lam_im, log_step, b_re, b_im)
    eye = jnp.eye(SSM_BLK_GROUPS, dtype=F32)
    lead = (DEPTH, 2, SSM_BLKS, SSM_BLK_GROUPS)

    def in_block(x):
        x = x.reshape(lead + (SSM_GROUP, SSM_STATE))
        return jnp.einsum('ldbgcp,gh->ldbgchp', x, eye).reshape(
            DEPTH, 2, SSM_BLKS, SSM_BLK_CH, SSM_BLK_ST)

    def out_block(x):
        x = x.reshape(lead + (SSM_GROUP, SSM_STATE))
        return jnp.einsum('ldbgcp,gh->ldbgphc', x, eye).reshape(
            DEPTH, 2, SSM_BLKS, SSM_BLK_ST, SSM_BLK_CH)

    wb = jnp.concatenate([
        jnp.concatenate([in_block(bbr), in_block(bbi)], axis=-1),
        jnp.concatenate([in_block(lbbr), in_block(lbbi)], axis=-1)], axis=-2).astype(BF16)
    wc = jnp.concatenate([out_block(c_re), out_block(-c_im)], axis=-2).astype(BF16)

    def table(one, two):
        one = one.reshape(DEPTH, 2, SSM_BLKS, 1, SSM_BLK_ST)
        two = two.reshape(DEPTH, 2, SSM_BLKS, 1, SSM_BLK_ST)
        half = SUBLANE // 2
        fwd = jnp.concatenate([jnp.broadcast_to(one[:, 0:1], (DEPTH, 1, SSM_BLKS, half, SSM_BLK_ST)),
                               jnp.broadcast_to(two[:, 0:1], (DEPTH, 1, SSM_BLKS, half, SSM_BLK_ST))], axis=3)
        bwd = jnp.concatenate([jnp.broadcast_to(two[:, 1:2], (DEPTH, 1, SSM_BLKS, half, SSM_BLK_ST)),
                               jnp.broadcast_to(one[:, 1:2], (DEPTH, 1, SSM_BLKS, half, SSM_BLK_ST))], axis=3)
        return jnp.concatenate([fwd, bwd], axis=1)

    return wb, wc, table(lbr, l2r), table(lbi, l2i)


def _s5_scan_kernel(uf_ref, ub_ref, wb_ref, wc_ref, tre_ref, tim_ref, yf_ref, yb_ref,
                    tmaj, xf, xb, carry):
    @pl.when(pl.program_id(1) == 0)
    def _():
        carry[...] = jnp.zeros(carry.shape, F32)

    n_tiles = S5_ROWS // SUBLANE
    half = SUBLANE // 2

    def time_major(u_ref):
        for b in range(BATCH):
            tmaj[pl.ds(b, S5_STEPS, stride=BATCH), :] = u_ref[b]
        return tmaj[...]

    def batch_major(y, y_ref):
        tmaj[...] = y
        for b in range(BATCH):
            y_ref[b] = tmaj[pl.ds(b, S5_STEPS, stride=BATCH), :]

    def paired_lhs(u, take_upper):
        u3 = u.reshape(n_tiles, SUBLANE, SSM_BLK_CH)
        swapped = pltpu.roll(u3, half, axis=1)
        sub = lax.broadcasted_iota(jnp.int32, u3.shape, 1)
        keep = (sub >= half) if take_upper else (sub < half)
        nb = jnp.where(keep, swapped, 0.0).reshape(S5_ROWS, SSM_BLK_CH)
        return jnp.concatenate([u, nb], axis=1).astype(BF16)

    xf[...] = _bdot(paired_lhs(time_major(uf_ref), True), wb_ref[0, 0, 0])
    xb[...] = _bdot(paired_lhs(time_major(ub_ref), False), wb_ref[0, 1, 0])

    lower = lax.broadcasted_iota(jnp.int32, (SUBLANE, 2 * LANE), 0) < half
    for hh in range(SSM_BLK_ST // (2 * LANE)):
        lo = hh * 2 * LANE
        re = slice(lo, lo + 2 * LANE)
        im = slice(SSM_BLK_ST + lo, SSM_BLK_ST + lo + 2 * LANE)
        afr, afi = tre_ref[0, 0, 0, :, re], tim_ref[0, 0, 0, :, re]
        abr, abi = tre_ref[0, 1, 0, :, re], tim_ref[0, 1, 0, :, re]

        def body(k, st):
            hfr, hfi, hbr, hbi = st
            rf = pl.multiple_of(k * SUBLANE, SUBLANE)
            pr = jnp.where(lower, pltpu.roll(hfr, half, axis=0), hfr)
            pi = jnp.where(lower, pltpu.roll(hfi, half, axis=0), hfi)
            nfr = xf[pl.ds(rf, SUBLANE), re] + (afr * pr - afi * pi)
            nfi = xf[pl.ds(rf, SUBLANE), im] + (afr * pi + afi * pr)
            xf[pl.ds(rf, SUBLANE), re] = nfr
            xf[pl.ds(rf, SUBLANE), im] = nfi
            rb = pl.multiple_of((n_tiles - 1 - k) * SUBLANE, SUBLANE)
            qr = jnp.where(lower, hbr, pltpu.roll(hbr, half, axis=0))
            qi = jnp.where(lower, hbi, pltpu.roll(hbi, half, axis=0))
            nbr = xb[pl.ds(rb, SUBLANE), re] + (abr * qr - abi * qi)
            nbi = xb[pl.ds(rb, SUBLANE), im] + (abr * qi + abi * qr)
            xb[pl.ds(rb, SUBLANE), re] = nbr
            xb[pl.ds(rb, SUBLANE), im] = nbi
            return nfr, nfi, nbr, nbi

        init = (carry[0, :, re], carry[1, :, re], carry[2, :, re], carry[3, :, re])
        fin = lax.fori_loop(0, n_tiles, body, init, unroll=2)
        for idx in range(4):
            carry[idx, :, re] = fin[idx]

    batch_major(_bdot(xf[...].astype(BF16), wc_ref[0, 0, 0]), yf_ref)
    batch_major(_bdot(xb[...].astype(BF16), wc_ref[0, 1, 0]), yb_ref)


def _s5_scan(proj, wb, wc, tre, tim, layer):
    last = S5_CHUNKS - 1
    u_blk0 = U_COL0 // SSM_BLK_CH
    proj3 = proj.reshape(BATCH, L_TOK, PROJ_COLS)
    y_shape = jax.ShapeDtypeStruct((BATCH, L_TOK, SSM_WIDTH), F32)
    yf, yb = pl.pallas_call(
        _s5_scan_kernel,
        grid=(SSM_BLKS, S5_CHUNKS),
        in_specs=[
            pl.BlockSpec((BATCH, S5_STEPS, SSM_BLK_CH), lambda j, c: (0, c, u_blk0 + j)),
            pl.BlockSpec((BATCH, S5_STEPS, SSM_BLK_CH), lambda j, c: (0, last - c, u_blk0 + j)),
            pl.BlockSpec((1, 2, 1, 2 * SSM_BLK_CH, 2 * SSM_BLK_ST), lambda j, c: (layer, 0, j, 0, 0)),
            pl.BlockSpec((1, 2, 1, 2 * SSM_BLK_ST, SSM_BLK_CH), lambda j, c: (layer, 0, j, 0, 0)),
            pl.BlockSpec((1, 2, 1, SUBLANE, SSM_BLK_ST), lambda j, c: (layer, 0, j, 0, 0)),
            pl.BlockSpec((1, 2, 1, SUBLANE, SSM_BLK_ST), lambda j, c: (layer, 0, j, 0, 0)),
        ],
        out_specs=[
            pl.BlockSpec((BATCH, S5_STEPS, SSM_BLK_CH), lambda j, c: (0, c, j)),
            pl.BlockSpec((BATCH, S5_STEPS, SSM_BLK_CH), lambda j, c: (0, last - c, j)),
        ],
        out_shape=[y_shape, y_shape],
        scratch_shapes=[
            pltpu.VMEM((S5_ROWS, SSM_BLK_CH), F32),
            pltpu.VMEM((S5_ROWS, 2 * SSM_BLK_ST), F32),
            pltpu.VMEM((S5_ROWS, 2 * SSM_BLK_ST), F32),
            pltpu.VMEM((4, SUBLANE, SSM_BLK_ST), F32),
        ],
        compiler_params=_cparams(("parallel", "arbitrary")),
        name="s5_scan",
    )(proj3, proj3, wb, wc, tre, tim)
    return yf.reshape(T_TOK, SSM_WIDTH), yb.reshape(T_TOK, SSM_WIDTH)


def _gelu_tanh(x):
    return 0.5 * x * (1.0 + jnp.tanh(math.sqrt(2.0 / math.pi) * (x + 0.044715 * (x * x * x))))


def _glu_kernel(yf_ref, yb_ref, u_ref, d_ref, w_ref, o_ref):
    y = (yf_ref[...] + yb_ref[...]) + d_ref[0] * u_ref[...]
    g = _gelu_tanh(y)
    z = _bdot(g.astype(BF16), w_ref[0].astype(BF16))
    o_ref[...] = g * jax.nn.sigmoid(z)


def _glu(yf, yb, proj, d, w, layer, tm):
    t = yf.shape[0]
    row = pl.BlockSpec((tm, SSM_WIDTH), lambda i: (i, 0))
    return pl.pallas_call(
        _glu_kernel,
        grid=(t // tm,),
        in_specs=[row, row,
                  pl.BlockSpec((tm, SSM_WIDTH), lambda i: (i, U_COL0 // SSM_WIDTH)),
                  pl.BlockSpec((1, 1, SSM_WIDTH), lambda i: (layer, 0, 0)),
                  pl.BlockSpec((1, SSM_WIDTH, SSM_WIDTH), lambda i: (layer, 0, 0))],
        out_specs=row,
        out_shape=jax.ShapeDtypeStruct((t, SSM_WIDTH), F32),
        compiler_params=_cparams(("parallel",)),
        name="s5_glu",
    )(yf, yb, proj, d.reshape(DEPTH, 1, SSM_WIDTH), w)


def _outproj_kernel(h_ref, a_ref, s_ref, ga_ref, gs_ref, wa_ref, ws_ref, o_ref, an_ref, sn_ref):
    @pl.when(pl.program_id(1) == 0)
    def _():
        an_ref[...] = _rms(a_ref[...], ga_ref[0]).astype(BF16)
        sn_ref[...] = _rms(s_ref[...], gs_ref[0]).astype(BF16)

    o_ref[...] = (h_ref[...] + _bdot(an_ref[...], wa_ref[0].astype(BF16))
                  + _bdot(sn_ref[...], ws_ref[0].astype(BF16)))


def _outproj(h, att, ssm, ga, gs, w, layer, tm, tn):
    t, d = h.shape
    return pl.pallas_call(
        _outproj_kernel,
        grid=(t // tm, d // tn),
        in_specs=[
            pl.BlockSpec((tm, tn), lambda i, j: (i, j)),
            pl.BlockSpec((tm, ATT_WIDTH), lambda i, j: (i, 0)),
            pl.BlockSpec((tm, SSM_WIDTH), lambda i, j: (i, 0)),
            pl.BlockSpec((1, 1, ATT_WIDTH), lambda i, j: (layer, 0, 0)),
            pl.BlockSpec((1, 1, SSM_WIDTH), lambda i, j: (layer, 0, 0)),
            pl.BlockSpec((1, ATT_WIDTH, tn), lambda i, j: (layer, 0, j)),
            pl.BlockSpec((1, SSM_WIDTH, tn), lambda i, j: (layer, 1, j)),
        ],
        out_specs=pl.BlockSpec((tm, tn), lambda i, j: (i, j)),
        out_shape=jax.ShapeDtypeStruct((t, d), F32),
        scratch_shapes=[pltpu.VMEM((tm, ATT_WIDTH), BF16), pltpu.VMEM((tm, SSM_WIDTH), BF16)],
        input_output_aliases={0: 0},
        compiler_params=_cparams(("parallel", "arbitrary")),
        name="out_proj",
    )(h, att, ssm, ga.reshape(DEPTH, 1, ATT_WIDTH), gs.reshape(DEPTH, 1, SSM_WIDTH), w, w)


def _ffn_kernel(x_ref, g_ref, wg_ref, wu_ref, wd_ref, o_ref, hn_ref):
    @pl.when(pl.program_id(1) == 0)
    def _():
        x = x_ref[...]
        hn_ref[...] = _rms(x, g_ref[0]).astype(BF16)
        o_ref[...] = x

    hn = hn_ref[...]
    a = _bdot(hn, wg_ref[0].astype(BF16))
    b = _bdot(hn, wu_ref[0].astype(BF16))
    hid = (a * jax.nn.sigmoid(a)) * b
    o_ref[...] += _bdot(hid.astype(BF16), wd_ref[0].astype(BF16))


def _ffn(h, g, wg, wu, wd, layer, tm, tf):
    t, d = h.shape
    f = wg.shape[2]
    idx = layer // 2
    once = pl.Buffered(1)
    return pl.pallas_call(
        _ffn_kernel,
        grid=(t // tm, f // tf),
        in_specs=[
            pl.BlockSpec((tm, d), lambda i, j: (i, 0), pipeline_mode=once),
            pl.BlockSpec((1, 1, d), lambda i, j: (layer, 0, 0)),
            pl.BlockSpec((1, d, tf), lambda i, j: (idx, 0, j)),
            pl.BlockSpec((1, d, tf), lambda i, j: (idx, 0, j)),
            pl.BlockSpec((1, tf, d), lambda i, j: (idx, j, 0)),
        ],
        out_specs=pl.BlockSpec((tm, d), lambda i, j: (i, 0), pipeline_mode=once),
        out_shape=jax.ShapeDtypeStruct((t, d), F32),
        scratch_shapes=[pltpu.VMEM((tm, d), BF16)],
        input_output_aliases={0: 0},
        compiler_params=_cparams(("parallel", "arbitrary")),
        name="dense_ffn",
    )(h, g.reshape(DEPTH, 1, d), wg, wu, wd)


MOE_KC = 4
MOE_CW = D_MODEL // MOE_KC
MOE_TM = 688
MOE_NT = (2 * T_TOK) // MOE_TM + N_EXPERTS
MOE_NP = MOE_NT * MOE_TM
GATHER_WINDOW = 344


def _split_bf16(x):
    hi = x.astype(BF16)
    lo = (x - hi.astype(F32)).astype(BF16)
    return hi, lo


def _lane_pick(x, lane, k):
    return jnp.sum(jnp.where(lane == k, x, 0.0), axis=-1, keepdims=True)


def _router_kernel(x_ref, g_ref, wr_ref, sel_ref, wts_ref):
    hn = _rms(x_ref[...], g_ref[0])
    xh, xl = _split_bf16(hn)
    wh, wl = _split_bf16(wr_ref[0])
    logits = _bdot(xh, wh) + (_bdot(xh, wl) + _bdot(xl, wh)) + _bdot(xl, wl)
    lane = lax.broadcasted_iota(jnp.int32, logits.shape, 1).astype(F32)
    neg = jnp.float32(-jnp.inf)
    logits = jnp.where(lane < N_EXPERTS, logits, neg)
    v1 = jnp.max(logits, axis=-1, keepdims=True)
    i1 = jnp.min(jnp.where(logits == v1, lane, float(LANE)), axis=-1, keepdims=True)
    rest = jnp.where(lane == i1, neg, logits)
    v2 = jnp.max(rest, axis=-1, keepdims=True)
    i2 = jnp.min(jnp.where(rest == v2, lane, float(LANE)), axis=-1, keepdims=True)
    e2 = jnp.exp(v2 - v1)
    w1 = 1.0 / (1.0 + e2)
    w2 = e2 / (1.0 + e2)
    sel_ref[...] = jnp.where((lane == i1) | (lane == i2), 1.0, 0.0)
    wts_ref[...] = jnp.where(lane == 0, w1, jnp.where(lane == 1, w2,
                             jnp.where(lane == 2, i1, jnp.where(lane == 3, i2, 0.0))))


def _router(h, g, wr_pad, layer, tm):
    t, d = h.shape
    idx = layer // 2
    row = pl.BlockSpec((tm, LANE), lambda i: (i, 0))
    tab = jax.ShapeDtypeStruct((t, LANE), F32)
    return pl.pallas_call(
        _router_kernel,
        grid=(t // tm,),
        in_specs=[
            pl.BlockSpec((tm, d), lambda i: (i, 0)),
            pl.BlockSpec((1, 1, d), lambda i: (layer, 0, 0)),
            pl.BlockSpec((1, d, LANE), lambda i: (idx, 0, 0)),
        ],
        out_specs=[row, row],
        out_shape=[tab, tab],
        compiler_params=_cparams(("parallel",)),
        name="moe_router",
    )(h, g.reshape(DEPTH, 1, d), wr_pad)


def _positions_kernel(sel_ref, wts_ref, pos_ref, meta_ref, cnt_ref, off_ref):
    p = pl.program_id(0)
    i = pl.program_id(1)
    tm = sel_ref.shape[0]
    lane = lax.broadcasted_iota(jnp.int32, (1, LANE), 1).astype(F32)
    sel = sel_ref[...]

    @pl.when((p == 0) & (i == 0))
    def _():
        cnt_ref[...] = jnp.zeros((1, LANE), F32)

    @pl.when(p == 0)
    def _():
        cnt_ref[...] += jnp.sum(sel, axis=0, keepdims=True)

    @pl.when((p == 1) & (i == 0))
    def _():
        cnt = cnt_ref[...]
        tiles = jnp.zeros((1, LANE), F32)
        for k in range(MOE_NT):
            tiles = tiles + jnp.where(cnt > float(k * MOE_TM), 1.0, 0.0)
        padded = tiles * float(MOE_TM)
        off = jnp.zeros((1, LANE), F32)
        for e in range(N_EXPERTS):
            off = off + jnp.where(lane > e, _lane_pick(padded, lane, e), 0.0)
        end = off + padded
        tile_start = lane * float(MOE_TM)
        owner = jnp.zeros((1, LANE), F32)
        for e in range(N_EXPERTS):
            owner = owner + jnp.where(tile_start >= _lane_pick(end, lane, e), 1.0, 0.0)
        owner = jnp.minimum(owner, float(N_EXPERTS - 1))
        used = jnp.sum(tiles, axis=-1, keepdims=True)
        off_ref[...] = off
        cnt_ref[...] = jnp.zeros((1, LANE), F32)
        row = lax.broadcasted_iota(jnp.int32, (SUBLANE, LANE), 0)
        meta = jnp.where(row == 0, owner, jnp.where(row == 1, used, jnp.where(row == 2, cnt, off)))
        meta_ref[...] = meta.astype(jnp.int32)

    @pl.when(p == 1)
    def _():
        r = lax.broadcasted_iota(jnp.int32, (tm, tm), 0)
        c = lax.broadcasted_iota(jnp.int32, (tm, tm), 1)
        earlier = jnp.where(c < r, 1.0, 0.0).astype(BF16)
        rank = _bdot(earlier, sel.astype(BF16)) + cnt_ref[...]
        slot = off_ref[...] + rank
        wts = wts_ref[...]
        lane_t = lax.broadcasted_iota(jnp.int32, (tm, LANE), 1).astype(F32)
        p1 = _lane_pick(slot, lane_t, _lane_pick(wts, lane_t, 2))
        p2 = _lane_pick(slot, lane_t, _lane_pick(wts, lane_t, 3))
        pos_ref[...] = jnp.where(lane_t == 0, p1, jnp.where(lane_t == 1, p2, 0.0)).astype(jnp.int32)
        cnt_ref[...] += jnp.sum(sel, axis=0, keepdims=True)


def _positions(sel, wts, tm):
    t = sel.shape[0]
    row = pl.BlockSpec((tm, LANE), lambda p, i: (i, 0))
    return pl.pallas_call(
        _positions_kernel,
        grid=(2, t // tm),
        in_specs=[row, row],
        out_specs=[
            pl.BlockSpec((tm, LANE), lambda p, i: (i * p, 0)),
            pl.BlockSpec((SUBLANE, LANE), lambda p, i: (0, 0)),
        ],
        out_shape=[jax.ShapeDtypeStruct((t, LANE), jnp.int32),
                   jax.ShapeDtypeStruct((SUBLANE, LANE), jnp.int32)],
        scratch_shapes=[pltpu.VMEM((1, LANE), F32), pltpu.VMEM((1, LANE), F32)],
        compiler_params=_cparams(("arbitrary", "arbitrary")),
        name="moe_positions",
    )(sel, wts)


def _gather_kernel(pos1_ref, pos2_ref, h_ref, xs_ref, src_ref, sem):
    def clear(p, c):
        src_ref[p] = 0
        return c

    lax.fori_loop(0, MOE_NP, clear, 0)

    def fill(t, c):
        src_ref[pos1_ref[t]] = t
        src_ref[pos2_ref[t]] = t
        return c

    lax.fori_loop(0, T_TOK, fill, 0)

    def row_copy(p):
        return pltpu.make_async_copy(h_ref.at[pl.ds(src_ref[p], 1), :],
                                     xs_ref.at[pl.ds(p, 1), :], sem)

    def drain():
        def wait_one(r, c):
            row_copy(0).wait()
            return c

        lax.fori_loop(0, GATHER_WINDOW, wait_one, 0)

    def window(w, c):
        def issue(r, cc):
            row_copy(w * GATHER_WINDOW + r).start()
            return cc

        lax.fori_loop(0, GATHER_WINDOW, issue, 0)

        @pl.when(w > 0)
        def _():
            drain()

        return c

    lax.fori_loop(0, MOE_NP // GATHER_WINDOW, window, 0)
    drain()


def _gather(h, pos1, pos2):
    d = h.shape[1]
    return pl.pallas_call(
        _gather_kernel,
        grid_spec=pltpu.PrefetchScalarGridSpec(
            num_scalar_prefetch=2,
            grid=(1,),
            in_specs=[pl.BlockSpec(memory_space=pl.ANY)],
            out_specs=pl.BlockSpec(memory_space=pl.ANY),
            scratch_shapes=[pltpu.SMEM((MOE_NP,), jnp.int32), pltpu.SemaphoreType.DMA(())],
        ),
        out_shape=jax.ShapeDtypeStruct((MOE_NP, d), F32),
        compiler_params=_cparams(("arbitrary",)),
        name="moe_gather",
    )(pos1, pos2, h)


def _experts_kernel(owner_ref, used_ref, x_ref, g_ref, wg_ref, wu_ref, wd_ref, o_ref,
                    hn_ref, ga_ref, ua_ref, hid_ref):
    i = pl.program_id(0)
    s = pl.program_id(1)
    live = i < used_ref[0]

    @pl.when(live & (s == 0))
    def _():
        hn = _rms(x_ref[...], g_ref[0])
        for k in range(MOE_KC):
            hn_ref[k] = hn[:, k * MOE_CW:(k + 1) * MOE_CW].astype(BF16)

    @pl.when(live & (s < MOE_KC))
    def _():
        hk = hn_ref[jnp.minimum(s, MOE_KC - 1)]
        pa = _bdot(hk, wg_ref[0, 0].astype(BF16))
        pu = _bdot(hk, wu_ref[0, 0].astype(BF16))

        @pl.when(s == 0)
        def _():
            ga_ref[...] = pa
            ua_ref[...] = pu

        @pl.when(s > 0)
        def _():
            ga_ref[...] += pa
            ua_ref[...] += pu

    @pl.when(live & (s == MOE_KC - 1))
    def _():
        a = ga_ref[...]
        hid_ref[...] = ((a * jax.nn.sigmoid(a)) * ua_ref[...]).astype(BF16)

    @pl.when(live & (s >= MOE_KC))
    def _():
        o_ref[...] = _bdot(hid_ref[...], wd_ref[0, 0].astype(BF16))

    @pl.when(jnp.logical_not(live) & (s >= MOE_KC))
    def _():
        o_ref[...] = jnp.zeros(o_ref.shape, F32)


def _experts(xs, owner, used, g, wg, wu, wd, layer):
    d = xs.shape[1]
    idx = layer // 2
    kc = MOE_KC

    def tile(i, used_ref):
        return jnp.minimum(i, used_ref[0] - 1)

    def k_chunk(i, s, used_ref):
        return jnp.where(i < used_ref[0], jnp.minimum(s, kc - 1), kc - 1)

    def n_chunk(i, s, used_ref):
        return jnp.where(i < used_ref[0], jnp.maximum(s - kc, 0), kc - 1)

    def out_chunk(s):
        return jnp.maximum(s - kc, 0)

    w_in = lambda i, s, o, u: (idx, o[tile(i, u)], k_chunk(i, s, u), 0)
    return pl.pallas_call(
        _experts_kernel,
        grid_spec=pltpu.PrefetchScalarGridSpec(
            num_scalar_prefetch=2,
            grid=(MOE_NT, 2 * kc),
            in_specs=[
                pl.BlockSpec((MOE_TM, d), lambda i, s, o, u: (tile(i, u), 0)),
                pl.BlockSpec((1, 1, d), lambda i, s, o, u: (layer, 0, 0)),
                pl.BlockSpec((1, 1, MOE_CW, D_FF_EXPERT), w_in),
                pl.BlockSpec((1, 1, MOE_CW, D_FF_EXPERT), w_in),
                pl.BlockSpec((1, 1, D_FF_EXPERT, MOE_CW),
                             lambda i, s, o, u: (idx, o[tile(i, u)], 0, n_chunk(i, s, u))),
            ],
            out_specs=pl.BlockSpec((MOE_TM, MOE_CW), lambda i, s, o, u: (i, out_chunk(s))),
            scratch_shapes=[
                pltpu.VMEM((kc, MOE_TM, MOE_CW), BF16),
                pltpu.VMEM((MOE_TM, D_FF_EXPERT), F32),
                pltpu.VMEM((MOE_TM, D_FF_EXPERT), F32),
                pltpu.VMEM((MOE_TM, D_FF_EXPERT), BF16),
            ],
        ),
        out_shape=jax.ShapeDtypeStruct((MOE_NP, d), F32),
        compiler_params=_cparams(("arbitrary", "arbitrary")),
        name="moe_experts",
    )(owner, used, xs, g.reshape(DEPTH, 1, d), wg, wu, wd)


def _combine_kernel(pos1_ref, pos2_ref, h_ref, wts_ref, ys_ref, o_ref, y1, y2, sem1, sem2):
    tm = h_ref.shape[0]
    base = pl.program_id(0) * tm

    def fetch(r, c):
        pltpu.make_async_copy(ys_ref.at[pl.ds(pos1_ref[base + r], 1), :],
                              y1.at[pl.ds(r, 1), :], sem1).start()
        pltpu.make_async_copy(ys_ref.at[pl.ds(pos2_ref[base + r], 1), :],
                              y2.at[pl.ds(r, 1), :], sem2).start()
        return c

    lax.fori_loop(0, tm, fetch, 0)

    def land(r, c):
        pltpu.make_async_copy(ys_ref.at[pl.ds(0, 1), :], y1.at[pl.ds(0, 1), :], sem1).wait()
        pltpu.make_async_copy(ys_ref.at[pl.ds(0, 1), :], y2.at[pl.ds(0, 1), :], sem2).wait()
        return c

    lax.fori_loop(0, tm, land, 0)
    wts = wts_ref[...]
    lane = lax.broadcasted_iota(jnp.int32, wts.shape, 1).astype(F32)
    w1 = _lane_pick(wts, lane, 0)
    w2 = _lane_pick(wts, lane, 1)
    o_ref[...] = h_ref[...] + (w1 * y1[...] + w2 * y2[...])


def _combine(h, wts, ys, pos1, pos2, tm):
    t, d = h.shape
    return pl.pallas_call(
        _combine_kernel,
        grid_spec=pltpu.PrefetchScalarGridSpec(
            num_scalar_prefetch=2,
            grid=(t // tm,),
            in_specs=[
                pl.BlockSpec((tm, d), lambda i, a, b: (i, 0)),
                pl.BlockSpec((tm, LANE), lambda i, a, b: (i, 0)),
                pl.BlockSpec(memory_space=pl.ANY),
            ],
            out_specs=pl.BlockSpec((tm, d), lambda i, a, b: (i, 0)),
            scratch_shapes=[
                pltpu.VMEM((tm, d), F32),
                pltpu.VMEM((tm, d), F32),
                pltpu.SemaphoreType.DMA(()),
                pltpu.SemaphoreType.DMA(()),
            ],
        ),
        out_shape=jax.ShapeDtypeStruct((t, d), F32),
        input_output_aliases={2: 0},
        compiler_params=_cparams(("arbitrary",)),
        name="moe_combine",
    )(pos1, pos2, h, wts, ys)


def _moe(h, g, wr_pad, wg, wu, wd, layer):
    sel, wts = _router(h, g, wr_pad, layer, tm=688)
    pos, meta = _positions(sel, wts, tm=688)
    pos1 = pos[:, 0]
    pos2 = pos[:, 1]
    owner = meta[0, :MOE_NT]
    used = meta[1, :1]
    xs = _gather(h, pos1, pos2)
    ys = _experts(xs, owner, used, g, wg, wu, wd, layer)
    return _combine(h, wts, ys, pos1, pos2, tm=688)


def _final_norm_kernel(x_ref, g_ref, o_ref):
    o_ref[...] = _rms(x_ref[...], g_ref[...])


def _final_norm(h, g, tm):
    t, d = h.shape
    return pl.pallas_call(
        _final_norm_kernel,
        grid=(t // tm,),
        in_specs=[pl.BlockSpec((tm, d), lambda i: (i, 0)), pl.BlockSpec((1, d), lambda i: (0, 0))],
        out_specs=pl.BlockSpec((tm, d), lambda i: (i, 0)),
        out_shape=jax.ShapeDtypeStruct((t, d), F32),
        compiler_params=_cparams(("parallel",)),
        name="final_norm",
    )(h, g.reshape(1, -1))


def _rope_partner(w):
    half = QK_ROPE // 2
    return jnp.concatenate([-w[..., half:], w[..., :half]], axis=-1)


def _pad_cols(w, width):
    return jnp.pad(w, [(0, 0)] * (w.ndim - 1) + [(0, width - w.shape[-1])])


def _in_proj_layout(w_in):
    kr = w_in[..., Q_LORA + KV_LORA:Q_LORA + KV_LORA + QK_ROPE]
    return jnp.concatenate([
        w_in[..., :Q_LORA + KV_LORA],
        _pad_cols(kr, LANE),
        _pad_cols(_rope_partner(kr), LANE),
        w_in[..., Q_LORA + KV_LORA + QK_ROPE:]], axis=-1)


def _q_layouts(w_uq):
    w = w_uq.reshape(DEPTH, Q_LORA, ATT_HEADS, QK_NOPE + QK_ROPE)
    main = _pad_cols(w, HEAD_PAD).reshape(DEPTH, Q_LORA, ATT_HEADS * HEAD_PAD)
    rot = _pad_cols(_rope_partner(w[..., QK_NOPE:]), LANE).reshape(DEPTH, Q_LORA, ATT_HEADS * LANE)
    return main, rot


def _rope_tables():
    inv = ROPE_THETA ** (-jnp.arange(0, QK_ROPE, 2, dtype=F32) / QK_ROPE)
    ang = jnp.arange(L_TOK, dtype=F32)[:, None] * inv[None, :]
    cos = jnp.cos(ang)
    sin = jnp.sin(ang)
    cos_t = _pad_cols(jnp.concatenate([cos, cos], axis=-1), LANE)
    sin_t = _pad_cols(jnp.concatenate([sin, sin], axis=-1), LANE)
    return cos_t, sin_t


def kernel(x, meta_tokens, mix_norm, w_in, q_norm, w_uq, kv_norm, w_ukv, ssm_lambda_re, ssm_lambda_im, ssm_log_step, ssm_b_re, ssm_b_im, ssm_c_re, ssm_c_im, ssm_d, ssm_w_glu, attn_out_norm, ssm_out_norm, w_out, ffn_norm, dense_w_gate, dense_w_up, dense_w_down, moe_router, moe_w_gate, moe_w_up, moe_w_down, final_norm):
    meta = jnp.broadcast_to(meta_tokens[None].astype(x.dtype), (BATCH, N_META, D_MODEL))
    h = jnp.concatenate([meta, x], axis=1).reshape(T_TOK, D_MODEL)
    cos_t, sin_t = _rope_tables()
    w_in_l = _in_proj_layout(w_in)
    wq_main, wq_rot = _q_layouts(w_uq)
    wr_pad = _pad_cols(moe_router, LANE)
    wb, wc, tre, tim = _s5_layouts(ssm_lambda_re, ssm_lambda_im, ssm_log_step,
                                   ssm_b_re, ssm_b_im, ssm_c_re, ssm_c_im)
    for layer in range(DEPTH):
        proj = _norm_matmul(h, mix_norm, w_in_l, layer, tm=1376, tn=512)
        q, kv, kpe = _upproj(proj, q_norm, kv_norm, wq_main, wq_rot, w_ukv, cos_t, sin_t, layer,
                             tm=688)
        att = _attention(q, kv, kpe, tq=688)
        yf, yb = _s5_scan(proj, wb, wc, tre, tim, layer)
        ssm = _glu(yf, yb, proj, ssm_d, ssm_w_glu, layer, tm=688)
        h = _outproj(h, att, ssm, attn_out_norm, ssm_out_norm, w_out, layer, tm=1376, tn=512)
        if layer % 2 == 0:
            h = _ffn(h, ffn_norm, dense_w_gate, dense_w_up, dense_w_down, layer, tm=1376, tf=256)
        else:
            h = _moe(h, ffn_norm, wr_pad, moe_w_gate, moe_w_up, moe_w_down, layer)
    out = _final_norm(h, final_norm, tm=688)
    return out.reshape(BATCH, L_TOK, D_MODEL)[:, N_META:]
```

```python
import functools
import math

import jax
import jax.numpy as jnp
from jax import lax
from jax.experimental import pallas as pl
from jax.experimental.pallas import tpu as pltpu

F32 = jnp.float32
BF16 = jnp.bfloat16

D_MODEL = 2048
BATCH = 4
SEQ = 2048
DEPTH = 4
N_META = 16
EPS = 1e-6
ATT_HEADS = 8
QK_NOPE = 128
QK_ROPE = 64
V_DIM = 128
Q_LORA = 512
KV_LORA = 256
ROPE_THETA = 10000.0
ATT_WIDTH = ATT_HEADS * V_DIM
SSM_GROUP = 16
SSM_WIDTH = D_MODEL - ATT_WIDTH
SSM_GROUPS = SSM_WIDTH // SSM_GROUP
SSM_STATE = 64
D_FF = 5632
N_EXPERTS = 8
D_FF_EXPERT = 1408

L_TOK = N_META + SEQ
T_TOK = L_TOK * BATCH
LANE = 128
SUBLANE = 8
HEAD_PAD = 256
L_PAD = 2176
PROJ_COLS = 2048
U_COL0 = 1024
VMEM_LIMIT = 58 * 1024 * 1024

SSM_BLK_GROUPS = 8
SSM_BLKS = SSM_GROUPS // SSM_BLK_GROUPS
SSM_BLK_CH = SSM_BLK_GROUPS * SSM_GROUP
SSM_BLK_ST = SSM_BLK_GROUPS * SSM_STATE
S5_CHUNKS = 3
S5_STEPS = L_TOK // S5_CHUNKS
S5_ROWS = S5_STEPS * BATCH


def _cparams(sem):
    return pltpu.CompilerParams(dimension_semantics=sem, vmem_limit_bytes=VMEM_LIMIT)


def _rms(x, g):
    ms = jnp.mean(x * x, axis=-1, keepdims=True)
    return x * lax.rsqrt(ms + EPS) * g


def _bdot(a, b):
    return jnp.dot(a, b, preferred_element_type=F32)


def _norm_matmul_kernel(x_ref, g_ref, w_ref, o_ref, hn_ref):
    @pl.when(pl.program_id(1) == 0)
    def _():
        hn_ref[...] = _rms(x_ref[...], g_ref[0]).astype(BF16)

    o_ref[...] = _bdot(hn_ref[...], w_ref[0].astype(BF16))


def _norm_matmul(x, g, w, layer, tm, tn):
    t, d = x.shape
    n = w.shape[2]
    return pl.pallas_call(
        _norm_matmul_kernel,
        grid=(t // tm, n // tn),
        in_specs=[
            pl.BlockSpec((tm, d), lambda i, j: (i, 0)),
            pl.BlockSpec((1, 1, d), lambda i, j: (layer, 0, 0)),
            pl.BlockSpec((1, d, tn), lambda i, j: (layer, 0, j)),
        ],
        out_specs=pl.BlockSpec((tm, tn), lambda i, j: (i, j)),
        out_shape=jax.ShapeDtypeStruct((t, n), F32),
        scratch_shapes=[pltpu.VMEM((tm, d), BF16)],
        compiler_params=_cparams(("parallel", "arbitrary")),
        name="norm_matmul",
    )(x, g.reshape(DEPTH, 1, d), w)


def _upproj_kernel(cq_ref, ckv_ref, kr_ref, krot_ref, qn_ref, kvn_ref, wq_ref, wqr_ref,
                   wkv_ref, cos_ref, sin_ref, q_ref, kv_ref, kpe_ref):
    scale = (QK_NOPE + QK_ROPE) ** -0.5
    cqn = _rms(cq_ref[...], qn_ref[0]).astype(BF16)
    a = _bdot(cqn, wq_ref[0].astype(BF16))
    r = _bdot(cqn, wqr_ref[0].astype(BF16))
    c = cos_ref[...]
    s = sin_ref[...]
    for h in range(ATT_HEADS):
        lo = h * HEAD_PAD
        q_ref[:, lo:lo + LANE] = (a[:, lo:lo + LANE] * scale).astype(BF16)
        rope = a[:, lo + LANE:lo + HEAD_PAD] * c + r[:, h * LANE:(h + 1) * LANE] * s
        q_ref[:, lo + LANE:lo + HEAD_PAD] = (rope * scale).astype(BF16)
    ckvn = _rms(ckv_ref[...], kvn_ref[0]).astype(BF16)
    kv_ref[...] = _bdot(ckvn, wkv_ref[0].astype(BF16)).astype(BF16)
    kpe_ref[...] = (kr_ref[...] * c + krot_ref[...] * s).astype(BF16)


def _upproj(proj, qn, kvn, wq, wqr, wkv, cos_t, sin_t, layer, tm):
    t = proj.shape[0]
    full = lambda a: pl.BlockSpec((1,) + a.shape[1:], lambda i: (layer, 0, 0))
    qn = qn.reshape(DEPTH, 1, Q_LORA)
    kvn = kvn.reshape(DEPTH, 1, KV_LORA)
    pos_blocks = L_TOK // tm
    return pl.pallas_call(
        _upproj_kernel,
        grid=(t // tm,),
        in_specs=[
            pl.BlockSpec((tm, Q_LORA), lambda i: (i, 0)),
            pl.BlockSpec((tm, KV_LORA), lambda i: (i, Q_LORA // KV_LORA)),
            pl.BlockSpec((tm, LANE), lambda i: (i, (Q_LORA + KV_LORA) // LANE)),
            pl.BlockSpec((tm, LANE), lambda i: (i, (Q_LORA + KV_LORA) // LANE + 1)),
            full(qn),
            full(kvn),
            full(wq),
            full(wqr),
            full(wkv),
            pl.BlockSpec((tm, LANE), lambda i: (i % pos_blocks, 0)),
            pl.BlockSpec((tm, LANE), lambda i: (i % pos_blocks, 0)),
        ],
        out_specs=[
            pl.BlockSpec((tm, ATT_HEADS * HEAD_PAD), lambda i: (i, 0)),
            pl.BlockSpec((tm, ATT_HEADS * (QK_NOPE + V_DIM)), lambda i: (i, 0)),
            pl.BlockSpec((tm, LANE), lambda i: (i, 0)),
        ],
        out_shape=[
            jax.ShapeDtypeStruct((t, ATT_HEADS * HEAD_PAD), BF16),
            jax.ShapeDtypeStruct((t, ATT_HEADS * (QK_NOPE + V_DIM)), BF16),
            jax.ShapeDtypeStruct((t, LANE), BF16),
        ],
        compiler_params=_cparams(("parallel",)),
        name="upproj",
    )(proj, proj, proj, proj, qn, kvn, wq, wqr, wkv, cos_t, sin_t)


def _attn_kernel(q_ref, kv_ref, kpe_ref, o_ref, k_sc, v_sc):
    @pl.when(pl.program_id(2) == 0)
    def _():
        k_sc[0:L_TOK, 0:LANE] = kv_ref[:, 0:LANE]
        k_sc[0:L_TOK, LANE:HEAD_PAD] = kpe_ref[...]
        k_sc[L_TOK:L_PAD, :] = jnp.zeros((L_PAD - L_TOK, HEAD_PAD), BF16)
        v_sc[0:L_TOK, :] = kv_ref[:, LANE:HEAD_PAD]
        v_sc[L_TOK:L_PAD, :] = jnp.zeros((L_PAD - L_TOK, V_DIM), BF16)

    s = lax.dot_general(q_ref[...], k_sc[...], (((1,), (1,)), ((), ())),
                        preferred_element_type=F32)
    col = lax.broadcasted_iota(jnp.int32, s.shape, 1)
    s = jnp.where(col < L_TOK, s, -1e30)
    m = jnp.max(s, axis=-1, keepdims=True)
    p = jnp.exp(s - m)
    l = jnp.sum(p, axis=-1, keepdims=True)
    o = _bdot(p.astype(BF16), v_sc[...])
    o_ref[...] = o / l


def _attention(q, kv, kpe, tq):
    nq = L_TOK // tq
    return pl.pallas_call(
        _attn_kernel,
        grid=(BATCH, ATT_HEADS, nq),
        in_specs=[
            pl.BlockSpec((tq, HEAD_PAD), lambda b, h, i: (b * nq + i, h)),
            pl.BlockSpec((L_TOK, HEAD_PAD), lambda b, h, i: (b, h)),
            pl.BlockSpec((L_TOK, LANE), lambda b, h, i: (b, 0)),
        ],
        out_specs=pl.BlockSpec((tq, V_DIM), lambda b, h, i: (b * nq + i, h)),
        out_shape=jax.ShapeDtypeStruct((T_TOK, ATT_WIDTH), F32),
        scratch_shapes=[pltpu.VMEM((L_PAD, HEAD_PAD), BF16), pltpu.VMEM((L_PAD, V_DIM), BF16)],
        compiler_params=_cparams(("parallel", "parallel", "arbitrary")),
        name="attention",
    )(q, kv, kpe)


def _s5_param_kernel(lre_ref, lim_ref, ls_ref, bre_ref, bim_ref,
                     lbr_o, lbi_o, l2r_o, l2i_o, bbr_o, bbi_o, lbbr_o, lbbi_o):
    lre = lre_ref[...]
    lim = lim_ref[...]
    dt = jnp.exp(ls_ref[...])
    mag = jnp.exp(lre * dt)
    ang = lim * dt
    br = mag * jnp.cos(ang)
    bi = mag * jnp.sin(ang)
    nr = br - 1.0
    den = lre * lre + lim * lim
    cr = (nr * lre + bi * lim) / den
    ci = (bi * lre - nr * lim) / den
    b_r = bre_ref[...]
    b_i = bim_ref[...]
    bbr = cr * b_r - ci * b_i
    bbi = cr * b_i + ci * b_r
    lbr_o[...] = br
    lbi_o[...] = bi
    l2r_o[...] = br * br - bi * bi
    l2i_o[...] = 2.0 * br * bi
    bbr_o[...] = bbr
    bbi_o[...] = bbi
    lbbr_o[...] = br * bbr - bi * bbi
    lbbi_o[...] = br * bbi + bi * bbr


def _s5_params(lam_re, lam_im, log_step, b_re, b_im):
    n = DEPTH * 2 * SSM_GROUPS
    lre = lam_re.reshape(n, 1, SSM_STATE)
    lim = lam_im.reshape(n, 1, SSM_STATE)
    ls = log_step.reshape(n, 1, 1)
    btr = b_re.transpose(0, 1, 2, 4, 3).reshape(n, SSM_GROUP, SSM_STATE)
    bti = b_im.transpose(0, 1, 2, 4, 3).reshape(n, SSM_GROUP, SSM_STATE)
    small = jax.ShapeDtypeStruct((n, 1, SSM_STATE), F32)
    big = jax.ShapeDtypeStruct((n, SSM_GROUP, SSM_STATE), F32)
    return pl.pallas_call(
        _s5_param_kernel,
        out_shape=[small, small, small, small, big, big, big, big],
        compiler_params=pltpu.CompilerParams(vmem_limit_bytes=VMEM_LIMIT),
        name="s5_params",
    )(lre, lim, ls, btr, bti)


def _s5_layouts(lam_re, lam_im, log_step, b_re, b_im, c_re, c_im):
    lbr, lbi, l2r, l2i, bbr, bbi, lbbr, lbbi = _s5_params(lam_re, lam_im, log_step, b_re, b_im)
    eye = jnp.eye(SSM_BLK_GROUPS, dtype=F32)
    lead = (DEPTH, 2, SSM_BLKS, SSM_BLK_GROUPS)

    def in_block(x):
        x = x.reshape(lead + (SSM_GROUP, SSM_STATE))
        return jnp.einsum('ldbgcp,gh->ldbgchp', x, eye).reshape(
            DEPTH, 2, SSM_BLKS, SSM_BLK_CH, SSM_BLK_ST)

    def out_block(x):
        x = x.reshape(lead + (SSM_GROUP, SSM_STATE))
        return jnp.einsum('ldbgcp,gh->ldbgphc', x, eye).reshape(
            DEPTH, 2, SSM_BLKS, SSM_BLK_ST, SSM_BLK_CH)

    wb = jnp.concatenate([
        jnp.concatenate([in_block(bbr), in_block(bbi)], axis=-1),
        jnp.concatenate([in_block(lbbr), in_block(lbbi)], axis=-1)], axis=-2).astype(BF16)
    wc = jnp.concatenate([out_block(c_re), out_block(-c_im)], axis=-2).astype(BF16)

    def table(one, two):
        one = one.reshape(DEPTH, 2, SSM_BLKS, 1, SSM_BLK_ST)
        two = two.reshape(DEPTH, 2, SSM_BLKS, 1, SSM_BLK_ST)
        half = SUBLANE // 2
        fwd = jnp.concatenate([jnp.broadcast_to(one[:, 0:1], (DEPTH, 1, SSM_BLKS, half, SSM_BLK_ST)),
                               jnp.broadcast_to(two[:, 0:1], (DEPTH, 1, SSM_BLKS, half, SSM_BLK_ST))], axis=3)
        bwd = jnp.concatenate([jnp.broadcast_to(two[:, 1:2], (DEPTH, 1, SSM_BLKS, half, SSM_BLK_ST)),
                               jnp.broadcast_to(one[:, 1:2], (DEPTH, 1, SSM_BLKS, half, SSM_BLK_ST))], axis=3)
        return jnp.concatenate([fwd, bwd], axis=1)

    return wb, wc, table(lbr, l2r), table(lbi, l2i)


def _s5_scan_kernel(uf_ref, ub_ref, wb_ref, wc_ref, tre_ref, tim_ref, yf_ref, yb_ref,
                    tmaj, xf, xb, carry):
    @pl.when(pl.program_id(1) == 0)
    def _():
        carry[...] = jnp.zeros(carry.shape, F32)

    n_tiles = S5_ROWS // SUBLANE
    half = SUBLANE // 2

    def time_major(u_ref):
        for b in range(BATCH):
            tmaj[pl.ds(b, S5_STEPS, stride=BATCH), :] = u_ref[b]
        return tmaj[...]

    def batch_major(y, y_ref):
        tmaj[...] = y
        for b in range(BATCH):
            y_ref[b] = tmaj[pl.ds(b, S5_STEPS, stride=BATCH), :]

    def paired_lhs(u, take_upper):
        u3 = u.reshape(n_tiles, SUBLANE, SSM_BLK_CH)
        swapped = pltpu.roll(u3, half, axis=1)
        sub = lax.broadcasted_iota(jnp.int32, u3.shape, 1)
        keep = (sub >= half) if take_upper else (sub < half)
        nb = jnp.where(keep, swapped, 0.0).reshape(S5_ROWS, SSM_BLK_CH)
        return jnp.concatenate([u, nb], axis=1).astype(BF16)

    xf[...] = _bdot(paired_lhs(time_major(uf_ref), True), wb_ref[0, 0, 0])
    xb[...] = _bdot(paired_lhs(time_major(ub_ref), False), wb_ref[0, 1, 0])

    lower = lax.broadcasted_iota(jnp.int32, (SUBLANE, 2 * LANE), 0) < half
    for hh in range(SSM_BLK_ST // (2 * LANE)):
        lo = hh * 2 * LANE
        re = slice(lo, lo + 2 * LANE)
        im = slice(SSM_BLK_ST + lo, SSM_BLK_ST + lo + 2 * LANE)
        afr, afi = tre_ref[0, 0, 0, :, re], tim_ref[0, 0, 0, :, re]
        abr, abi = tre_ref[0, 1, 0, :, re], tim_ref[0, 1, 0, :, re]

        def body(k, st):
            hfr, hfi, hbr, hbi = st
            rf = pl.multiple_of(k * SUBLANE, SUBLANE)
            pr = jnp.where(lower, pltpu.roll(hfr, half, axis=0), hfr)
            pi = jnp.where(lower, pltpu.roll(hfi, half, axis=0), hfi)
            nfr = xf[pl.ds(rf, SUBLANE), re] + (afr * pr - afi * pi)
            nfi = xf[pl.ds(rf, SUBLANE), im] + (afr * pi + afi * pr)
            xf[pl.ds(rf, SUBLANE), re] = nfr
            xf[pl.ds(rf, SUBLANE), im] = nfi
            rb = pl.multiple_of((n_tiles - 1 - k) * SUBLANE, SUBLANE)
            qr = jnp.where(lower, hbr, pltpu.roll(hbr, half, axis=0))
            qi = jnp.where(lower, hbi, pltpu.roll(hbi, half, axis=0))
            nbr = xb[pl.ds(rb, SUBLANE), re] + (abr * qr - abi * qi)
            nbi = xb[pl.ds(rb, SUBLANE), im] + (abr * qi + abi * qr)
            xb[pl.ds(rb, SUBLANE), re] = nbr
            xb[pl.ds(rb, SUBLANE), im] = nbi
            return nfr, nfi, nbr, nbi

        init = (carry[0, :, re], carry[1, :, re], carry[2, :, re], carry[3, :, re])
        fin = lax.fori_loop(0, n_tiles, body, init, unroll=2)
        for idx in range(4):
            carry[idx, :, re] = fin[idx]

    batch_major(_bdot(xf[...].astype(BF16), wc_ref[0, 0, 0]), yf_ref)
    batch_major(_bdot(xb[...].astype(BF16), wc_ref[0, 1, 0]), yb_ref)


def _s5_scan(proj, wb, wc, tre, tim, layer):
    last = S5_CHUNKS - 1
    u_blk0 = U_COL0 // SSM_BLK_CH
    proj3 = proj.reshape(BATCH, L_TOK, PROJ_COLS)
    y_shape = jax.ShapeDtypeStruct((BATCH, L_TOK, SSM_WIDTH), F32)
    yf, yb = pl.pallas_call(
        _s5_scan_kernel,
        grid=(SSM_BLKS, S5_CHUNKS),
        in_specs=[
            pl.BlockSpec((BATCH, S5_STEPS, SSM_BLK_CH), lambda j, c: (0, c, u_blk0 + j)),
            pl.BlockSpec((BATCH, S5_STEPS, SSM_BLK_CH), lambda j, c: (0, last - c, u_blk0 + j)),
            pl.BlockSpec((1, 2, 1, 2 * SSM_BLK_CH, 2 * SSM_BLK_ST), lambda j, c: (layer, 0, j, 0, 0)),
            pl.BlockSpec((1, 2, 1, 2 * SSM_BLK_ST, SSM_BLK_CH), lambda j, c: (layer, 0, j, 0, 0)),
            pl.BlockSpec((1, 2, 1, SUBLANE, SSM_BLK_ST), lambda j, c: (layer, 0, j, 0, 0)),
            pl.BlockSpec((1, 2, 1, SUBLANE, SSM_BLK_ST), lambda j, c: (layer, 0, j, 0, 0)),
        ],
        out_specs=[
            pl.BlockSpec((BATCH, S5_STEPS, SSM_BLK_CH), lambda j, c: (0, c, j)),
            pl.BlockSpec((BATCH, S5_STEPS, SSM_BLK_CH), lambda j, c: (0, last - c, j)),
        ],
        out_shape=[y_shape, y_shape],
        scratch_shapes=[
            pltpu.VMEM((S5_ROWS, SSM_BLK_CH), F32),
            pltpu.VMEM((S5_ROWS, 2 * SSM_BLK_ST), F32),
            pltpu.VMEM((S5_ROWS, 2 * SSM_BLK_ST), F32),
            pltpu.VMEM((4, SUBLANE, SSM_BLK_ST), F32),
        ],
        compiler_params=_cparams(("parallel", "arbitrary")),
        name="s5_scan",
    )(proj3, proj3, wb, wc, tre, tim)
    return yf.reshape(T_TOK, SSM_WIDTH), yb.reshape(T_TOK, SSM_WIDTH)


def _gelu_tanh(x):
    return 0.5 * x * (1.0 + jnp.tanh(math.sqrt(2.0 / math.pi) * (x + 0.044715 * (x * x * x))))


def _glu_kernel(yf_ref, yb_ref, u_ref, d_ref, w_ref, o_ref):
    y = (yf_ref[...] + yb_ref[...]) + d_ref[0] * u_ref[...]
    g = _gelu_tanh(y)
    z = _bdot(g.astype(BF16), w_ref[0].astype(BF16))
    o_ref[...] = g * jax.nn.sigmoid(z)


def _glu(yf, yb, proj, d, w, layer, tm):
    t = yf.shape[0]
    row = pl.BlockSpec((tm, SSM_WIDTH), lambda i: (i, 0))
    return pl.pallas_call(
        _glu_kernel,
        grid=(t // tm,),
        in_specs=[row, row,
                  pl.BlockSpec((tm, SSM_WIDTH), lambda i: (i, U_COL0 // SSM_WIDTH)),
                  pl.BlockSpec((1, 1, SSM_WIDTH), lambda i: (layer, 0, 0)),
                  pl.BlockSpec((1, SSM_WIDTH, SSM_WIDTH), lambda i: (layer, 0, 0))],
        out_specs=row,
        out_shape=jax.ShapeDtypeStruct((t, SSM_WIDTH), F32),
        compiler_params=_cparams(("parallel",)),
        name="s5_glu",
    )(yf, yb, proj, d.reshape(DEPTH, 1, SSM_WIDTH), w)


def _outproj_kernel(h_ref, a_ref, s_ref, ga_ref, gs_ref, wa_ref, ws_ref, o_ref, an_ref, sn_ref):
    @pl.when(pl.program_id(1) == 0)
    def _():
        an_ref[...] = _rms(a_ref[...], ga_ref[0]).astype(BF16)
        sn_ref[...] = _rms(s_ref[...], gs_ref[0]).astype(BF16)

    o_ref[...] = (h_ref[...] + _bdot(an_ref[...], wa_ref[0].astype(BF16))
                  + _bdot(sn_ref[...], ws_ref[0].astype(BF16)))


def _outproj(h, att, ssm, ga, gs, w, layer, tm, tn):
    t, d = h.shape
    return pl.pallas_call(
        _outproj_kernel,
        grid=(t // tm, d // tn),
        in_specs=[
            pl.BlockSpec((tm, tn), lambda i, j: (i, j)),
            pl.BlockSpec((tm, ATT_WIDTH), lambda i, j: (i, 0)),
            pl.BlockSpec((tm, SSM_WIDTH), lambda i, j: (i, 0)),
            pl.BlockSpec((1, 1, ATT_WIDTH), lambda i, j: (layer, 0, 0)),
            pl.BlockSpec((1, 1, SSM_WIDTH), lambda i, j: (layer, 0, 0)),
            pl.BlockSpec((1, ATT_WIDTH, tn), lambda i, j: (layer, 0, j)),
            pl.BlockSpec((1, SSM_WIDTH, tn), lambda i, j: (layer, 1, j)),
        ],
        out_specs=pl.BlockSpec((tm, tn), lambda i, j: (i, j)),
        out_shape=jax.ShapeDtypeStruct((t, d), F32),
        scratch_shapes=[pltpu.VMEM((tm, ATT_WIDTH), BF16), pltpu.VMEM((tm, SSM_WIDTH), BF16)],
        input_output_aliases={0: 0},
        compiler_params=_cparams(("parallel", "arbitrary")),
        name="out_proj",
    )(h, att, ssm, ga.reshape(DEPTH, 1, ATT_WIDTH), gs.reshape(DEPTH, 1, SSM_WIDTH), w, w)


def _ffn_kernel(x_ref, g_ref, wg_ref, wu_ref, wd_ref, o_ref, hn_ref):
    @pl.when(pl.program_id(1) == 0)
    def _():
        x = x_ref[...]
        hn_ref[...] = _rms(x, g_ref[0]).astype(BF16)
        o_ref[...] = x

    hn = hn_ref[...]
    a = _bdot(hn, wg_ref[0].astype(BF16))
    b = _bdot(hn, wu_ref[0].astype(BF16))
    hid = (a * jax.nn.sigmoid(a)) * b
    o_ref[...] += _bdot(hid.astype(BF16), wd_ref[0].astype(BF16))


def _ffn(h, g, wg, wu, wd, layer, tm, tf):
    t, d = h.shape
    f = wg.shape[2]
    idx = layer // 2
    once = pl.Buffered(1)
    return pl.pallas_call(
        _ffn_kernel,
        grid=(t // tm, f // tf),
        in_specs=[
            pl.BlockSpec((tm, d), lambda i, j: (i, 0), pipeline_mode=once),
            pl.BlockSpec((1, 1, d), lambda i, j: (layer, 0, 0)),
            pl.BlockSpec((1, d, tf), lambda i, j: (idx, 0, j)),
            pl.BlockSpec((1, d, tf), lambda i, j: (idx, 0, j)),
            pl.BlockSpec((1, tf, d), lambda i, j: (idx, j, 0)),
        ],
        out_specs=pl.BlockSpec((tm, d), lambda i, j: (i, 0), pipeline_mode=once),
        out_shape=jax.ShapeDtypeStruct((t, d), F32),
        scratch_shapes=[pltpu.VMEM((tm, d), BF16)],
        input_output_aliases={0: 0},
        compiler_params=_cparams(("parallel", "arbitrary")),
        name="dense_ffn",
    )(h, g.reshape(DEPTH, 1, d), wg, wu, wd)


MOE_KC = 4
MOE_CW = D_MODEL // MOE_KC
MOE_TM = 688
MOE_NT = (2 * T_TOK) // MOE_TM + N_EXPERTS
MOE_NP = MOE_NT * MOE_TM


def _split_bf16(x):
    hi = x.astype(BF16)
    lo = (x - hi.astype(F32)).astype(BF16)
    return hi, lo


def _lane_pick(x, lane, k):
    return jnp.sum(jnp.where(lane == k, x, 0.0), axis=-1, keepdims=True)


ROW_SLABS = D_MODEL // LANE


def _row_slab(ref, c, rows):
    return ref.at[pl.ds(c, rows, stride=ROW_SLABS), :]


def _router_kernel(x_ref, g_ref, wr_ref, sel_ref, wts_ref, rows_ref):
    x = x_ref[...]
    for c in range(ROW_SLABS):
        _row_slab(rows_ref, c, x.shape[0])[...] = x[:, c * LANE:(c + 1) * LANE]
    hn = _rms(x, g_ref[0])
    xh, xl = _split_bf16(hn)
    wh, wl = _split_bf16(wr_ref[0])
    logits = _bdot(xh, wh) + (_bdot(xh, wl) + _bdot(xl, wh)) + _bdot(xl, wl)
    lane = lax.broadcasted_iota(jnp.int32, logits.shape, 1).astype(F32)
    neg = jnp.float32(-jnp.inf)
    logits = jnp.where(lane < N_EXPERTS, logits, neg)
    v1 = jnp.max(logits, axis=-1, keepdims=True)
    i1 = jnp.min(jnp.where(logits == v1, lane, float(LANE)), axis=-1, keepdims=True)
    rest = jnp.where(lane == i1, neg, logits)
    v2 = jnp.max(rest, axis=-1, keepdims=True)
    i2 = jnp.min(jnp.where(rest == v2, lane, float(LANE)), axis=-1, keepdims=True)
    e2 = jnp.exp(v2 - v1)
    w1 = 1.0 / (1.0 + e2)
    w2 = e2 / (1.0 + e2)
    sel_ref[...] = jnp.where((lane == i1) | (lane == i2), 1.0, 0.0)
    wts_ref[...] = jnp.where(lane == 0, w1, jnp.where(lane == 1, w2,
                             jnp.where(lane == 2, i1, jnp.where(lane == 3, i2, 0.0))))


def _router(h, g, wr_pad, layer, tm):
    t, d = h.shape
    idx = layer // 2
    row = pl.BlockSpec((tm, LANE), lambda i: (i, 0))
    tab = jax.ShapeDtypeStruct((t, LANE), F32)
    return pl.pallas_call(
        _router_kernel,
        grid=(t // tm,),
        in_specs=[
            pl.BlockSpec((tm, d), lambda i: (i, 0)),
            pl.BlockSpec((1, 1, d), lambda i: (layer, 0, 0)),
            pl.BlockSpec((1, d, LANE), lambda i: (idx, 0, 0)),
        ],
        out_specs=[row, row, pl.BlockSpec((tm * ROW_SLABS, LANE), lambda i: (i, 0))],
        out_shape=[tab, tab, jax.ShapeDtypeStruct((t * ROW_SLABS, LANE), F32)],
        compiler_params=_cparams(("parallel",)),
        name="moe_router",
    )(h, g.reshape(DEPTH, 1, d), wr_pad)


def _positions_kernel(sel_ref, wts_ref, pos_ref, meta_ref, cnt_ref, off_ref):
    p = pl.program_id(0)
    i = pl.program_id(1)
    tm = sel_ref.shape[0]
    lane = lax.broadcasted_iota(jnp.int32, (1, LANE), 1).astype(F32)
    sel = sel_ref[...]

    @pl.when((p == 0) & (i == 0))
    def _():
        cnt_ref[...] = jnp.zeros((1, LANE), F32)

    @pl.when(p == 0)
    def _():
        cnt_ref[...] += jnp.sum(sel, axis=0, keepdims=True)

    @pl.when((p == 1) & (i == 0))
    def _():
        cnt = cnt_ref[...]
        tiles = jnp.zeros((1, LANE), F32)
        for k in range(MOE_NT):
            tiles = tiles + jnp.where(cnt > float(k * MOE_TM), 1.0, 0.0)
        padded = tiles * float(MOE_TM)
        off = jnp.zeros((1, LANE), F32)
        for e in range(N_EXPERTS):
            off = off + jnp.where(lane > e, _lane_pick(padded, lane, e), 0.0)
        end = off + padded
        tile_start = lane * float(MOE_TM)
        owner = jnp.zeros((1, LANE), F32)
        for e in range(N_EXPERTS):
            owner = owner + jnp.where(tile_start >= _lane_pick(end, lane, e), 1.0, 0.0)
        owner = jnp.minimum(owner, float(N_EXPERTS - 1))
        used = jnp.sum(tiles, axis=-1, keepdims=True)
        off_ref[...] = off
        cnt_ref[...] = jnp.zeros((1, LANE), F32)
        row = lax.broadcasted_iota(jnp.int32, (SUBLANE, LANE), 0)
        meta = jnp.where(row == 0, owner, jnp.where(row == 1, used, jnp.where(row == 2, cnt, off)))
        meta_ref[...] = meta.astype(jnp.int32)

    @pl.when(p == 1)
    def _():
        r = lax.broadcasted_iota(jnp.int32, (tm, tm), 0)
        c = lax.broadcasted_iota(jnp.int32, (tm, tm), 1)
        earlier = jnp.where(c < r, 1.0, 0.0).astype(BF16)
        rank = _bdot(earlier, sel.astype(BF16)) + cnt_ref[...]
        slot = off_ref[...] + rank
        wts = wts_ref[...]
        lane_t = lax.broadcasted_iota(jnp.int32, (tm, LANE), 1).astype(F32)
        p1 = _lane_pick(slot, lane_t, _lane_pick(wts, lane_t, 2))
        p2 = _lane_pick(slot, lane_t, _lane_pick(wts, lane_t, 3))
        pos_ref[...] = jnp.where(lane_t == 0, p1, jnp.where(lane_t == 1, p2, 0.0)).astype(jnp.int32)
        cnt_ref[...] += jnp.sum(sel, axis=0, keepdims=True)


def _positions(sel, wts, tm):
    t = sel.shape[0]
    row = pl.BlockSpec((tm, LANE), lambda p, i: (i, 0))
    return pl.pallas_call(
        _positions_kernel,
        grid=(2, t // tm),
        in_specs=[row, row],
        out_specs=[
            pl.BlockSpec((tm, LANE), lambda p, i: (i * p, 0)),
            pl.BlockSpec((SUBLANE, LANE), lambda p, i: (0, 0)),
        ],
        out_shape=[jax.ShapeDtypeStruct((t, LANE), jnp.int32),
                   jax.ShapeDtypeStruct((SUBLANE, LANE), jnp.int32)],
        scratch_shapes=[pltpu.VMEM((1, LANE), F32), pltpu.VMEM((1, LANE), F32)],
        compiler_params=_cparams(("arbitrary", "arbitrary")),
        name="moe_positions",
    )(sel, wts)


def _gather_kernel(pos1_ref, pos2_ref, used_ref, rows_ref, xs_ref, src_ref, sem):
    i = pl.program_id(0)

    @pl.when(i == 0)
    def _():
        def clear(p, c):
            src_ref[p] = 0
            return c

        lax.fori_loop(0, MOE_NP, clear, 0, unroll=8)

        def fill(t, c):
            src_ref[pos1_ref[t]] = t
            src_ref[pos2_ref[t]] = t
            return c

        lax.fori_loop(0, T_TOK, fill, 0, unroll=4)

    def row_copy(tok, r):
        return pltpu.make_async_copy(rows_ref.at[pl.ds(tok * ROW_SLABS, ROW_SLABS), :],
                                     xs_ref.at[pl.ds(r * ROW_SLABS, ROW_SLABS), :], sem)

    @pl.when(i < used_ref[0])
    def _():
        def issue(r, c):
            row_copy(src_ref[i * MOE_TM + r], r).start()
            return c

        lax.fori_loop(0, MOE_TM, issue, 0, unroll=4)

        def land(r, c):
            row_copy(0, 0).wait()
            return c

        lax.fori_loop(0, MOE_TM, land, 0, unroll=4)

    @pl.when(i >= used_ref[0])
    def _():
        xs_ref[...] = jnp.zeros(xs_ref.shape, F32)


def _gather(rows, pos1, pos2, used):
    blk = MOE_TM * ROW_SLABS
    return pl.pallas_call(
        _gather_kernel,
        grid_spec=pltpu.PrefetchScalarGridSpec(
            num_scalar_prefetch=3,
            grid=(MOE_NT,),
            in_specs=[pl.BlockSpec(memory_space=pl.ANY)],
            out_specs=pl.BlockSpec((blk, LANE), lambda i, a, b, u: (i, 0)),
            scratch_shapes=[pltpu.SMEM((MOE_NP,), jnp.int32), pltpu.SemaphoreType.DMA(())],
        ),
        out_shape=jax.ShapeDtypeStruct((MOE_NP * ROW_SLABS, LANE), F32),
        compiler_params=_cparams(("arbitrary",)),
        name="moe_gather",
    )(pos1, pos2, used, rows)


def _experts_kernel(owner_ref, used_ref, x_ref, g_ref, wg_ref, wu_ref, wd_ref, o_ref,
                    hn_ref, ga_ref, ua_ref, hid_ref):
    i = pl.program_id(0)
    s = pl.program_id(1)
    live = i < used_ref[0]

    per_chunk = MOE_CW // LANE

    @pl.when(live & (s == 0))
    def _():
        slabs = [_row_slab(x_ref, c, MOE_TM)[...] for c in range(ROW_SLABS)]
        ssq = slabs[0] * slabs[0]
        for c in range(1, ROW_SLABS):
            ssq = ssq + slabs[c] * slabs[c]
        inv = lax.rsqrt(jnp.sum(ssq, axis=-1, keepdims=True) / D_MODEL + EPS)
        g = g_ref[0]
        for k in range(MOE_KC):
            hn_ref[k] = jnp.concatenate(
                [slabs[c] * inv * g[:, c * LANE:(c + 1) * LANE]
                 for c in range(k * per_chunk, (k + 1) * per_chunk)], axis=1).astype(BF16)

    @pl.when(live & (s < MOE_KC))
    def _():
        hk = hn_ref[jnp.minimum(s, MOE_KC - 1)]
        pa = _bdot(hk, wg_ref[0, 0].astype(BF16))
        pu = _bdot(hk, wu_ref[0, 0].astype(BF16))

        @pl.when(s == 0)
        def _():
            ga_ref[...] = pa
            ua_ref[...] = pu

        @pl.when(s > 0)
        def _():
            ga_ref[...] += pa
            ua_ref[...] += pu

    @pl.when(live & (s == MOE_KC - 1))
    def _():
        a = ga_ref[...]
        hid_ref[...] = ((a * jax.nn.sigmoid(a)) * ua_ref[...]).astype(BF16)

    @pl.when(live & (s >= MOE_KC))
    def _():
        res = _bdot(hid_ref[...], wd_ref[0, 0].astype(BF16))
        first = jnp.maximum(s - MOE_KC, 0) * per_chunk
        for j in range(per_chunk):
            o_ref[pl.ds(first + j, MOE_TM, stride=ROW_SLABS), :] = res[:, j * LANE:(j + 1) * LANE]

    @pl.when(jnp.logical_not(live) & (s == MOE_KC))
    def _():
        o_ref[...] = jnp.zeros(o_ref.shape, F32)


def _experts(xs, owner, used, g, wg, wu, wd, layer):
    d = D_MODEL
    idx = layer // 2
    kc = MOE_KC
    blk = MOE_TM * ROW_SLABS

    def tile(i, used_ref):
        return jnp.minimum(i, used_ref[0] - 1)

    def k_chunk(i, s, used_ref):
        return jnp.where(i < used_ref[0], jnp.minimum(s, kc - 1), kc - 1)

    def n_chunk(i, s, used_ref):
        return jnp.where(i < used_ref[0], jnp.maximum(s - kc, 0), kc - 1)

    w_in = lambda i, s, o, u: (idx, o[tile(i, u)], k_chunk(i, s, u), 0)
    return pl.pallas_call(
        _experts_kernel,
        grid_spec=pltpu.PrefetchScalarGridSpec(
            num_scalar_prefetch=2,
            grid=(MOE_NT, 2 * kc),
            in_specs=[
                pl.BlockSpec((blk, LANE), lambda i, s, o, u: (tile(i, u), 0)),
                pl.BlockSpec((1, 1, d), lambda i, s, o, u: (layer, 0, 0)),
                pl.BlockSpec((1, 1, MOE_CW, D_FF_EXPERT), w_in),
                pl.BlockSpec((1, 1, MOE_CW, D_FF_EXPERT), w_in),
                pl.BlockSpec((1, 1, D_FF_EXPERT, MOE_CW),
                             lambda i, s, o, u: (idx, o[tile(i, u)], 0, n_chunk(i, s, u))),
            ],
            out_specs=pl.BlockSpec((blk, LANE), lambda i, s, o, u: (i, 0)),
            scratch_shapes=[
                pltpu.VMEM((kc, MOE_TM, MOE_CW), BF16),
                pltpu.VMEM((MOE_TM, D_FF_EXPERT), F32),
                pltpu.VMEM((MOE_TM, D_FF_EXPERT), F32),
                pltpu.VMEM((MOE_TM, D_FF_EXPERT), BF16),
            ],
        ),
        out_shape=jax.ShapeDtypeStruct((MOE_NP * ROW_SLABS, LANE), F32),
        compiler_params=_cparams(("arbitrary", "arbitrary")),
        name="moe_experts",
    )(owner, used, xs, g.reshape(DEPTH, 1, d), wg, wu, wd)


def _combine_kernel(pos1_ref, pos2_ref, h_ref, wts_ref, ys_ref, o_ref, y1, y2, sem1, sem2):
    tm = h_ref.shape[0]
    base = pl.program_id(0) * tm

    def row_copy(slot, r, buf, sem):
        return pltpu.make_async_copy(ys_ref.at[pl.ds(slot * ROW_SLABS, ROW_SLABS), :],
                                     buf.at[pl.ds(r * ROW_SLABS, ROW_SLABS), :], sem)

    def fetch(r, c):
        row_copy(pos1_ref[base + r], r, y1, sem1).start()
        row_copy(pos2_ref[base + r], r, y2, sem2).start()
        return c

    lax.fori_loop(0, tm, fetch, 0, unroll=4)

    def land(r, c):
        row_copy(0, 0, y1, sem1).wait()
        row_copy(0, 0, y2, sem2).wait()
        return c

    lax.fori_loop(0, tm, land, 0, unroll=4)
    wts = wts_ref[...]
    lane = lax.broadcasted_iota(jnp.int32, wts.shape, 1).astype(F32)
    w1 = _lane_pick(wts, lane, 0)
    w2 = _lane_pick(wts, lane, 1)
    for c in range(ROW_SLABS):
        cols = slice(c * LANE, (c + 1) * LANE)
        o_ref[:, cols] = h_ref[:, cols] + (w1 * _row_slab(y1, c, tm)[...] + w2 * _row_slab(y2, c, tm)[...])


def _combine(h, wts, ys, pos1, pos2, tm):
    t, d = h.shape
    return pl.pallas_call(
        _combine_kernel,
        grid_spec=pltpu.PrefetchScalarGridSpec(
            num_scalar_prefetch=2,
            grid=(t // tm,),
            in_specs=[
                pl.BlockSpec((tm, d), lambda i, a, b: (i, 0)),
                pl.BlockSpec((tm, LANE), lambda i, a, b: (i, 0)),
                pl.BlockSpec(memory_space=pl.ANY),
            ],
            out_specs=pl.BlockSpec((tm, d), lambda i, a, b: (i, 0)),
            scratch_shapes=[
                pltpu.VMEM((tm * ROW_SLABS, LANE), F32),
                pltpu.VMEM((tm * ROW_SLABS, LANE), F32),
                pltpu.SemaphoreType.DMA(()),
                pltpu.SemaphoreType.DMA(()),
            ],
        ),
        out_shape=jax.ShapeDtypeStruct((t, d), F32),
        input_output_aliases={2: 0},
        compiler_params=_cparams(("arbitrary",)),
        name="moe_combine",
    )(pos1, pos2, h, wts, ys)


def _moe(h, g, wr_pad, wg, wu, wd, layer):
    sel, wts, rows = _router(h, g, wr_pad, layer, tm=688)
    pos, meta = _positions(sel, wts, tm=688)
    pos1 = pos[:, 0]
    pos2 = pos[:, 1]
    owner = meta[0, :MOE_NT]
    used = meta[1, :1]
    xs = _gather(rows, pos1, pos2, used)
    ys = _experts(xs, owner, used, g, wg, wu, wd, layer)
    return _combine(h, wts, ys, pos1, pos2, tm=688)


def _final_norm_kernel(x_ref, g_ref, o_ref):
    o_ref[...] = _rms(x_ref[...], g_ref[...])


def _final_norm(h, g, tm):
    t, d = h.shape
    return pl.pallas_call(
        _final_norm_kernel,
        grid=(t // tm,),
        in_specs=[pl.BlockSpec((tm, d), lambda i: (i, 0)), pl.BlockSpec((1, d), lambda i: (0, 0))],
        out_specs=pl.BlockSpec((tm, d), lambda i: (i, 0)),
        out_shape=jax.ShapeDtypeStruct((t, d), F32),
        compiler_params=_cparams(("parallel",)),
        name="final_norm",
    )(h, g.reshape(1, -1))


def _rope_partner(w):
    half = QK_ROPE // 2
    return jnp.concatenate([-w[..., half:], w[..., :half]], axis=-1)


def _pad_cols(w, width):
    return jnp.pad(w, [(0, 0)] * (w.ndim - 1) + [(0, width - w.shape[-1])])


def _in_proj_layout(w_in):
    kr = w_in[..., Q_LORA + KV_LORA:Q_LORA + KV_LORA + QK_ROPE]
    return jnp.concatenate([
        w_in[..., :Q_LORA + KV_LORA],
        _pad_cols(kr, LANE),
        _pad_cols(_rope_partner(kr), LANE),
        w_in[..., Q_LORA + KV_LORA + QK_ROPE:]], axis=-1)


def _q_layouts(w_uq):
    w = w_uq.reshape(DEPTH, Q_LORA, ATT_HEADS, QK_NOPE + QK_ROPE)
    main = _pad_cols(w, HEAD_PAD).reshape(DEPTH, Q_LORA, ATT_HEADS * HEAD_PAD)
    rot = _pad_cols(_rope_partner(w[..., QK_NOPE:]), LANE).reshape(DEPTH, Q_LORA, ATT_HEADS * LANE)
    return main, rot


def _rope_tables():
    inv = ROPE_THETA ** (-jnp.arange(0, QK_ROPE, 2, dtype=F32) / QK_ROPE)
    ang = jnp.arange(L_TOK, dtype=F32)[:, None] * inv[None, :]
    cos = jnp.cos(ang)
    sin = jnp.sin(ang)
    cos_t = _pad_cols(jnp.concatenate([cos, cos], axis=-1), LANE)
    sin_t = _pad_cols(jnp.concatenate([sin, sin], axis=-1), LANE)
    return cos_t, sin_t


def kernel(x, meta_tokens, mix_norm, w_in, q_norm, w_uq, kv_norm, w_ukv, ssm_lambda_re, ssm_lambda_im, ssm_log_step, ssm_b_re, ssm_b_im, ssm_c_re, ssm_c_im, ssm_d, ssm_w_glu, attn_out_norm, ssm_out_norm, w_out, ffn_norm, dense_w_gate, dense_w_up, dense_w_down, moe_router, moe_w_gate, moe_w_up, moe_w_down, final_norm):
    meta = jnp.broadcast_to(meta_tokens[None].astype(x.dtype), (BATCH, N_META, D_MODEL))
    h = jnp.concatenate([meta, x], axis=1).reshape(T_TOK, D_MODEL)
    cos_t, sin_t = _rope_tables()
    w_in_l = _in_proj_layout(w_in)
    wq_main, wq_rot = _q_layouts(w_uq)
    wr_pad = _pad_cols(moe_router, LANE)
    wb, wc, tre, tim = _s5_layouts(ssm_lambda_re, ssm_lambda_im, ssm_log_step,
                                   ssm_b_re, ssm_b_im, ssm_c_re, ssm_c_im)
    for layer in range(DEPTH):
        proj = _norm_matmul(h, mix_norm, w_in_l, layer, tm=1376, tn=512)
        q, kv, kpe = _upproj(proj, q_norm, kv_norm, wq_main, wq_rot, w_ukv, cos_t, sin_t, layer,
                             tm=688)
        att = _attention(q, kv, kpe, tq=688)
        yf, yb = _s5_scan(proj, wb, wc, tre, tim, layer)
        ssm = _glu(yf, yb, proj, ssm_d, ssm_w_glu, layer, tm=688)
        h = _outproj(h, att, ssm, attn_out_norm, ssm_out_norm, w_out, layer, tm=1376, tn=512)
        if layer % 2 == 0:
            h = _ffn(h, ffn_norm, dense_w_gate, dense_w_up, dense_w_down, layer, tm=1376, tf=256)
        else:
            h = _moe(h, ffn_norm, wr_pad, moe_w_gate, moe_w_up, moe_w_down, layer)
    out = _final_norm(h, final_norm, tm=688)
    return out.reshape(BATCH, L_TOK, D_MODEL)[:, N_META:]
```

```python
import functools
import math

import jax
import jax.numpy as jnp
from jax import lax
from jax.experimental import pallas as pl
from jax.experimental.pallas import tpu as pltpu

F32 = jnp.float32
BF16 = jnp.bfloat16

D_MODEL = 2048
BATCH = 4
SEQ = 2048
DEPTH = 4
N_META = 16
EPS = 1e-6
ATT_HEADS = 8
QK_NOPE = 128
QK_ROPE = 64
V_DIM = 128
Q_LORA = 512
KV_LORA = 256
ROPE_THETA = 10000.0
ATT_WIDTH = ATT_HEADS * V_DIM
SSM_GROUP = 16
SSM_WIDTH = D_MODEL - ATT_WIDTH
SSM_GROUPS = SSM_WIDTH // SSM_GROUP
SSM_STATE = 64
D_FF = 5632
N_EXPERTS = 8
D_FF_EXPERT = 1408

L_TOK = N_META + SEQ
T_TOK = L_TOK * BATCH
LANE = 128
SUBLANE = 8
HEAD_PAD = 256
L_PAD = 2176
PROJ_COLS = 2048
U_COL0 = 1024
VMEM_LIMIT = 58 * 1024 * 1024

SSM_BLK_GROUPS = 8
SSM_BLKS = SSM_GROUPS // SSM_BLK_GROUPS
SSM_BLK_CH = SSM_BLK_GROUPS * SSM_GROUP
SSM_BLK_ST = SSM_BLK_GROUPS * SSM_STATE
S5_CHUNKS = 3
S5_STEPS = L_TOK // S5_CHUNKS
S5_ROWS = S5_STEPS * BATCH


def _cparams(sem):
    return pltpu.CompilerParams(dimension_semantics=sem, vmem_limit_bytes=VMEM_LIMIT)


def _rms(x, g):
    ms = jnp.mean(x * x, axis=-1, keepdims=True)
    return x * lax.rsqrt(ms + EPS) * g


def _bdot(a, b):
    return jnp.dot(a, b, preferred_element_type=F32)


def _norm_matmul_kernel(x_ref, g_ref, w_ref, o_ref, hn_ref):
    @pl.when(pl.program_id(1) == 0)
    def _():
        hn_ref[...] = _rms(x_ref[...], g_ref[0]).astype(BF16)

    o_ref[...] = _bdot(hn_ref[...], w_ref[0].astype(BF16))


def _norm_matmul(x, g, w, layer, tm, tn):
    t, d = x.shape
    n = w.shape[2]
    return pl.pallas_call(
        _norm_matmul_kernel,
        grid=(t // tm, n // tn),
        in_specs=[
            pl.BlockSpec((tm, d), lambda i, j: (i, 0)),
            pl.BlockSpec((1, 1, d), lambda i, j: (layer, 0, 0)),
            pl.BlockSpec((1, d, tn), lambda i, j: (layer, 0, j)),
        ],
        out_specs=pl.BlockSpec((tm, tn), lambda i, j: (i, j)),
        out_shape=jax.ShapeDtypeStruct((t, n), F32),
        scratch_shapes=[pltpu.VMEM((tm, d), BF16)],
        compiler_params=_cparams(("parallel", "arbitrary")),
        name="norm_matmul",
    )(x, g.reshape(DEPTH, 1, d), w)


def _upproj_kernel(cq_ref, ckv_ref, kr_ref, krot_ref, qn_ref, kvn_ref, wq_ref, wqr_ref,
                   wkv_ref, cos_ref, sin_ref, q_ref, kv_ref, kpe_ref):
    scale = (QK_NOPE + QK_ROPE) ** -0.5 * math.log2(math.e)
    cqn = _rms(cq_ref[...], qn_ref[0]).astype(BF16)
    a = _bdot(cqn, wq_ref[0].astype(BF16))
    r = _bdot(cqn, wqr_ref[0].astype(BF16))
    c = cos_ref[...]
    s = sin_ref[...]
    last_lane = lax.broadcasted_iota(jnp.int32, c.shape, 1) == LANE - 1
    for h in range(ATT_HEADS):
        lo = h * HEAD_PAD
        q_ref[:, lo:lo + LANE] = (a[:, lo:lo + LANE] * scale).astype(BF16)
        rope = a[:, lo + LANE:lo + HEAD_PAD] * c + r[:, h * LANE:(h + 1) * LANE] * s
        q_ref[:, lo + LANE:lo + HEAD_PAD] = jnp.where(last_lane, 1.0, rope * scale).astype(BF16)
    ckvn = _rms(ckv_ref[...], kvn_ref[0]).astype(BF16)
    kv_ref[...] = _bdot(ckvn, wkv_ref[0].astype(BF16)).astype(BF16)
    kpe_ref[...] = (kr_ref[...] * c + krot_ref[...] * s).astype(BF16)


def _upproj(proj, qn, kvn, wq, wqr, wkv, cos_t, sin_t, layer, tm):
    t = proj.shape[0]
    full = lambda a: pl.BlockSpec((1,) + a.shape[1:], lambda i: (layer, 0, 0))
    qn = qn.reshape(DEPTH, 1, Q_LORA)
    kvn = kvn.reshape(DEPTH, 1, KV_LORA)
    pos_blocks = L_TOK // tm
    return pl.pallas_call(
        _upproj_kernel,
        grid=(t // tm,),
        in_specs=[
            pl.BlockSpec((tm, Q_LORA), lambda i: (i, 0)),
            pl.BlockSpec((tm, KV_LORA), lambda i: (i, Q_LORA // KV_LORA)),
            pl.BlockSpec((tm, LANE), lambda i: (i, (Q_LORA + KV_LORA) // LANE)),
            pl.BlockSpec((tm, LANE), lambda i: (i, (Q_LORA + KV_LORA) // LANE + 1)),
            full(qn),
            full(kvn),
            full(wq),
            full(wqr),
            full(wkv),
            pl.BlockSpec((tm, LANE), lambda i: (i % pos_blocks, 0)),
            pl.BlockSpec((tm, LANE), lambda i: (i % pos_blocks, 0)),
        ],
        out_specs=[
            pl.BlockSpec((tm, ATT_HEADS * HEAD_PAD), lambda i: (i, 0)),
            pl.BlockSpec((tm, ATT_HEADS * (QK_NOPE + V_DIM)), lambda i: (i, 0)),
            pl.BlockSpec((tm, LANE), lambda i: (i, 0)),
        ],
        out_shape=[
            jax.ShapeDtypeStruct((t, ATT_HEADS * HEAD_PAD), BF16),
            jax.ShapeDtypeStruct((t, ATT_HEADS * (QK_NOPE + V_DIM)), BF16),
            jax.ShapeDtypeStruct((t, LANE), BF16),
        ],
        compiler_params=_cparams(("parallel",)),
        name="upproj",
    )(proj, proj, proj, proj, qn, kvn, wq, wqr, wkv, cos_t, sin_t)


ATT_HPS = 2
PAD_BIAS = -1e30


def _attn_kernel(q_ref, kv_ref, kpe_ref, o_ref, k_sc, v_sc):
    @pl.when(pl.program_id(2) == 0)
    def _():
        n_pad = L_PAD - L_TOK
        pad_lane = lax.broadcasted_iota(jnp.int32, (n_pad, LANE), 1)
        k_pad = jnp.where(pad_lane == LANE - 1, PAD_BIAS, 0.0).astype(BF16)
        row_lane = lax.broadcasted_iota(jnp.int32, (L_TOK, LANE), 1)
        ones_col = jnp.where(row_lane == 0, 1.0, 0.0).astype(BF16)
        for hh in range(ATT_HPS):
            lo = hh * HEAD_PAD
            k_sc[hh, 0:L_TOK, 0:LANE] = kv_ref[:, lo:lo + LANE]
            k_sc[hh, 0:L_TOK, LANE:HEAD_PAD] = kpe_ref[...]
            k_sc[hh, L_TOK:L_PAD, 0:LANE] = jnp.zeros((n_pad, LANE), BF16)
            k_sc[hh, L_TOK:L_PAD, LANE:HEAD_PAD] = k_pad
            v_sc[hh, 0:L_TOK, 0:V_DIM] = kv_ref[:, lo + LANE:lo + HEAD_PAD]
            v_sc[hh, 0:L_TOK, V_DIM:HEAD_PAD] = ones_col
            v_sc[hh, L_TOK:L_PAD, :] = jnp.zeros((n_pad, HEAD_PAD), BF16)

    for hh in range(ATT_HPS):
        q = q_ref[:, hh * HEAD_PAD:(hh + 1) * HEAD_PAD]
        s = lax.dot_general(q, k_sc[hh], (((1,), (1,)), ((), ())),
                            preferred_element_type=F32)
        m = jnp.max(s, axis=-1, keepdims=True)
        p = jnp.exp2(s - m).astype(BF16)
        o = _bdot(p, v_sc[hh])
        o_ref[:, hh * V_DIM:(hh + 1) * V_DIM] = o[:, :V_DIM] / o[:, V_DIM:V_DIM + 1]


def _attention(q, kv, kpe, tq):
    nq = L_TOK // tq
    hps = ATT_HPS
    return pl.pallas_call(
        _attn_kernel,
        grid=(BATCH, ATT_HEADS // hps, nq),
        in_specs=[
            pl.BlockSpec((tq, hps * HEAD_PAD), lambda b, h, i: (b * nq + i, h)),
            pl.BlockSpec((L_TOK, hps * HEAD_PAD), lambda b, h, i: (b, h)),
            pl.BlockSpec((L_TOK, LANE), lambda b, h, i: (b, 0)),
        ],
        out_specs=pl.BlockSpec((tq, hps * V_DIM), lambda b, h, i: (b * nq + i, h)),
        out_shape=jax.ShapeDtypeStruct((T_TOK, ATT_WIDTH), F32),
        scratch_shapes=[pltpu.VMEM((hps, L_PAD, HEAD_PAD), BF16),
                        pltpu.VMEM((hps, L_PAD, HEAD_PAD), BF16)],
        compiler_params=_cparams(("parallel", "parallel", "arbitrary")),
        name="attention",
    )(q, kv, kpe)


def _s5_param_kernel(lre_ref, lim_ref, ls_ref, bre_ref, bim_ref,
                     lbr_o, lbi_o, l2r_o, l2i_o, bbr_o, bbi_o, lbbr_o, lbbi_o):
    lre = lre_ref[...]
    lim = lim_ref[...]
    dt = jnp.exp(ls_ref[...])
    mag = jnp.exp(lre * dt)
    ang = lim * dt
    br = mag * jnp.cos(ang)
    bi = mag * jnp.sin(ang)
    nr = br - 1.0
    den = lre * lre + lim * lim
    cr = (nr * lre + bi * lim) / den
    ci = (bi * lre - nr * lim) / den
    b_r = bre_ref[...]
    b_i = bim_ref[...]
    bbr = cr * b_r - ci * b_i
    bbi = cr * b_i + ci * b_r
    lbr_o[...] = br
    lbi_o[...] = bi
    l2r_o[...] = br * br - bi * bi
    l2i_o[...] = 2.0 * br * bi
    bbr_o[...] = bbr
    bbi_o[...] = bbi
    lbbr_o[...] = br * bbr - bi * bbi
    lbbi_o[...] = br * bbi + bi * bbr


def _s5_params(lam_re, lam_im, log_step, b_re, b_im):
    n = DEPTH * 2 * SSM_GROUPS
    lre = lam_re.reshape(n, 1, SSM_STATE)
    lim = lam_im.reshape(n, 1, SSM_STATE)
    ls = log_step.reshape(n, 1, 1)
    btr = b_re.transpose(0, 1, 2, 4, 3).reshape(n, SSM_GROUP, SSM_STATE)
    bti = b_im.transpose(0, 1, 2, 4, 3).reshape(n, SSM_GROUP, SSM_STATE)
    small = jax.ShapeDtypeStruct((n, 1, SSM_STATE), F32)
    big = jax.ShapeDtypeStruct((n, SSM_GROUP, SSM_STATE), F32)
    return pl.pallas_call(
        _s5_param_kernel,
        out_shape=[small, small, small, small, big, big, big, big],
        compiler_params=pltpu.CompilerParams(vmem_limit_bytes=VMEM_LIMIT),
        name="s5_params",
    )(lre, lim, ls, btr, bti)


def _s5_layouts(lam_re, lam_im, log_step, b_re, b_im, c_re, c_im):
    lbr, lbi, l2r, l2i, bbr, bbi, lbbr, lbbi = _s5_params(lam_re, lam_im, log_step, b_re, b_im)
    eye = jnp.eye(SSM_BLK_GROUPS, dtype=F32)
    lead = (DEPTH, 2, SSM_BLKS, SSM_BLK_GROUPS)

    def in_block(x):
        x = x.reshape(lead + (SSM_GROUP, SSM_STATE))
        return jnp.einsum('ldbgcp,gh->ldbgchp', x, eye).reshape(
            DEPTH, 2, SSM_BLKS, SSM_BLK_CH, SSM_BLK_ST)

    def out_block(x):
        x = x.reshape(lead + (SSM_GROUP, SSM_STATE))
        return jnp.einsum('ldbgcp,gh->ldbgphc', x, eye).reshape(
            DEPTH, 2, SSM_BLKS, SSM_BLK_ST, SSM_BLK_CH)

    wb = jnp.concatenate([
        jnp.concatenate([in_block(bbr), in_block(bbi)], axis=-1),
        jnp.concatenate([in_block(lbbr), in_block(lbbi)], axis=-1)], axis=-2).astype(BF16)
    wc = jnp.concatenate([out_block(c_re), out_block(-c_im)], axis=-2).astype(BF16)

    def table(one, two):
        one = one.reshape(DEPTH, 2, SSM_BLKS, 1, SSM_BLK_ST)
        two = two.reshape(DEPTH, 2, SSM_BLKS, 1, SSM_BLK_ST)
        half = SUBLANE // 2
        fwd = jnp.concatenate([jnp.broadcast_to(one[:, 0:1], (DEPTH, 1, SSM_BLKS, half, SSM_BLK_ST)),
                               jnp.broadcast_to(two[:, 0:1], (DEPTH, 1, SSM_BLKS, half, SSM_BLK_ST))], axis=3)
        bwd = jnp.concatenate([jnp.broadcast_to(two[:, 1:2], (DEPTH, 1, SSM_BLKS, half, SSM_BLK_ST)),
                               jnp.broadcast_to(one[:, 1:2], (DEPTH, 1, SSM_BLKS, half, SSM_BLK_ST))], axis=3)
        return jnp.concatenate([fwd, bwd], axis=1)

    return wb, wc, table(lbr, l2r), table(lbi, l2i)


def _s5_scan_kernel(uf_ref, ub_ref, wb_ref, wc_ref, tre_ref, tim_ref, yf_ref, yb_ref,
                    tmaj, xf, xb, carry):
    @pl.when(pl.program_id(1) == 0)
    def _():
        carry[...] = jnp.zeros(carry.shape, F32)

    n_tiles = S5_ROWS // SUBLANE
    half = SUBLANE // 2

    def time_major(u_ref):
        for b in range(BATCH):
            tmaj[pl.ds(b, S5_STEPS, stride=BATCH), :] = u_ref[b]
        return tmaj[...]

    def batch_major(y, y_ref):
        tmaj[...] = y
        for b in range(BATCH):
            y_ref[b] = tmaj[pl.ds(b, S5_STEPS, stride=BATCH), :]

    def paired_lhs(u, take_upper):
        u3 = u.reshape(n_tiles, SUBLANE, SSM_BLK_CH)
        swapped = pltpu.roll(u3, half, axis=1)
        sub = lax.broadcasted_iota(jnp.int32, u3.shape, 1)
        keep = (sub >= half) if take_upper else (sub < half)
        nb = jnp.where(keep, swapped, 0.0).reshape(S5_ROWS, SSM_BLK_CH)
        return jnp.concatenate([u, nb], axis=1).astype(BF16)

    xf[...] = _bdot(paired_lhs(time_major(uf_ref), True), wb_ref[0, 0, 0])
    xb[...] = _bdot(paired_lhs(time_major(ub_ref), False), wb_ref[0, 1, 0])

    lower = lax.broadcasted_iota(jnp.int32, (SUBLANE, 2 * LANE), 0) < half
    for hh in range(SSM_BLK_ST // (2 * LANE)):
        lo = hh * 2 * LANE
        re = slice(lo, lo + 2 * LANE)
        im = slice(SSM_BLK_ST + lo, SSM_BLK_ST + lo + 2 * LANE)
        afr, afi = tre_ref[0, 0, 0, :, re], tim_ref[0, 0, 0, :, re]
        abr, abi = tre_ref[0, 1, 0, :, re], tim_ref[0, 1, 0, :, re]

        def body(k, st):
            hfr, hfi, hbr, hbi = st
            rf = pl.multiple_of(k * SUBLANE, SUBLANE)
            pr = jnp.where(lower, pltpu.roll(hfr, half, axis=0), hfr)
            pi = jnp.where(lower, pltpu.roll(hfi, half, axis=0), hfi)
            nfr = xf[pl.ds(rf, SUBLANE), re] + (afr * pr - afi * pi)
            nfi = xf[pl.ds(rf, SUBLANE), im] + (afr * pi + afi * pr)
            xf[pl.ds(rf, SUBLANE), re] = nfr
            xf[pl.ds(rf, SUBLANE), im] = nfi
            rb = pl.multiple_of((n_tiles - 1 - k) * SUBLANE, SUBLANE)
            qr = jnp.where(lower, hbr, pltpu.roll(hbr, half, axis=0))
            qi = jnp.where(lower, hbi, pltpu.roll(hbi, half, axis=0))
            nbr = xb[pl.ds(rb, SUBLANE), re] + (abr * qr - abi * qi)
            nbi = xb[pl.ds(rb, SUBLANE), im] + (abr * qi + abi * qr)
            xb[pl.ds(rb, SUBLANE), re] = nbr
            xb[pl.ds(rb, SUBLANE), im] = nbi
            return nfr, nfi, nbr, nbi

        init = (carry[0, :, re], carry[1, :, re], carry[2, :, re], carry[3, :, re])
        fin = lax.fori_loop(0, n_tiles, body, init, unroll=2)
        for idx in range(4):
            carry[idx, :, re] = fin[idx]

    batch_major(_bdot(xf[...].astype(BF16), wc_ref[0, 0, 0]), yf_ref)
    batch_major(_bdot(xb[...].astype(BF16), wc_ref[0, 1, 0]), yb_ref)


def _s5_scan(proj, wb, wc, tre, tim, layer):
    last = S5_CHUNKS - 1
    u_blk0 = U_COL0 // SSM_BLK_CH
    proj3 = proj.reshape(BATCH, L_TOK, PROJ_COLS)
    y_shape = jax.ShapeDtypeStruct((BATCH, L_TOK, SSM_WIDTH), F32)
    yf, yb = pl.pallas_call(
        _s5_scan_kernel,
        grid=(SSM_BLKS, S5_CHUNKS),
        in_specs=[
            pl.BlockSpec((BATCH, S5_STEPS, SSM_BLK_CH), lambda j, c: (0, c, u_blk0 + j)),
            pl.BlockSpec((BATCH, S5_STEPS, SSM_BLK_CH), lambda j, c: (0, last - c, u_blk0 + j)),
            pl.BlockSpec((1, 2, 1, 2 * SSM_BLK_CH, 2 * SSM_BLK_ST), lambda j, c: (layer, 0, j, 0, 0)),
            pl.BlockSpec((1, 2, 1, 2 * SSM_BLK_ST, SSM_BLK_CH), lambda j, c: (layer, 0, j, 0, 0)),
            pl.BlockSpec((1, 2, 1, SUBLANE, SSM_BLK_ST), lambda j, c: (layer, 0, j, 0, 0)),
            pl.BlockSpec((1, 2, 1, SUBLANE, SSM_BLK_ST), lambda j, c: (layer, 0, j, 0, 0)),
        ],
        out_specs=[
            pl.BlockSpec((BATCH, S5_STEPS, SSM_BLK_CH), lambda j, c: (0, c, j)),
            pl.BlockSpec((BATCH, S5_STEPS, SSM_BLK_CH), lambda j, c: (0, last - c, j)),
        ],
        out_shape=[y_shape, y_shape],
        scratch_shapes=[
            pltpu.VMEM((S5_ROWS, SSM_BLK_CH), F32),
            pltpu.VMEM((S5_ROWS, 2 * SSM_BLK_ST), F32),
            pltpu.VMEM((S5_ROWS, 2 * SSM_BLK_ST), F32),
            pltpu.VMEM((4, SUBLANE, SSM_BLK_ST), F32),
        ],
        compiler_params=_cparams(("parallel", "arbitrary")),
        name="s5_scan",
    )(proj3, proj3, wb, wc, tre, tim)
    return yf.reshape(T_TOK, SSM_WIDTH), yb.reshape(T_TOK, SSM_WIDTH)


def _gelu_tanh(x):
    return 0.5 * x * (1.0 + jnp.tanh(math.sqrt(2.0 / math.pi) * (x + 0.044715 * (x * x * x))))


def _glu_kernel(yf_ref, yb_ref, u_ref, d_ref, w_ref, o_ref):
    y = (yf_ref[...] + yb_ref[...]) + d_ref[0] * u_ref[...]
    g = _gelu_tanh(y)
    z = _bdot(g.astype(BF16), w_ref[0].astype(BF16))
    o_ref[...] = g * jax.nn.sigmoid(z)


def _glu(yf, yb, proj, d, w, layer, tm):
    t = yf.shape[0]
    row = pl.BlockSpec((tm, SSM_WIDTH), lambda i: (i, 0))
    return pl.pallas_call(
        _glu_kernel,
        grid=(t // tm,),
        in_specs=[row, row,
                  pl.BlockSpec((tm, SSM_WIDTH), lambda i: (i, U_COL0 // SSM_WIDTH)),
                  pl.BlockSpec((1, 1, SSM_WIDTH), lambda i: (layer, 0, 0)),
                  pl.BlockSpec((1, SSM_WIDTH, SSM_WIDTH), lambda i: (layer, 0, 0))],
        out_specs=row,
        out_shape=jax.ShapeDtypeStruct((t, SSM_WIDTH), F32),
        compiler_params=_cparams(("parallel",)),
        name="s5_glu",
    )(yf, yb, proj, d.reshape(DEPTH, 1, SSM_WIDTH), w)


def _outproj_kernel(h_ref, a_ref, s_ref, ga_ref, gs_ref, wa_ref, ws_ref, o_ref, an_ref, sn_ref):
    @pl.when(pl.program_id(1) == 0)
    def _():
        an_ref[...] = _rms(a_ref[...], ga_ref[0]).astype(BF16)
        sn_ref[...] = _rms(s_ref[...], gs_ref[0]).astype(BF16)

    o_ref[...] = (h_ref[...] + _bdot(an_ref[...], wa_ref[0].astype(BF16))
                  + _bdot(sn_ref[...], ws_ref[0].astype(BF16)))


def _outproj(h, att, ssm, ga, gs, w, layer, tm, tn):
    t, d = h.shape
    return pl.pallas_call(
        _outproj_kernel,
        grid=(t // tm, d // tn),
        in_specs=[
            pl.BlockSpec((tm, tn), lambda i, j: (i, j)),
            pl.BlockSpec((tm, ATT_WIDTH), lambda i, j: (i, 0)),
            pl.BlockSpec((tm, SSM_WIDTH), lambda i, j: (i, 0)),
            pl.BlockSpec((1, 1, ATT_WIDTH), lambda i, j: (layer, 0, 0)),
            pl.BlockSpec((1, 1, SSM_WIDTH), lambda i, j: (layer, 0, 0)),
            pl.BlockSpec((1, ATT_WIDTH, tn), lambda i, j: (layer, 0, j)),
            pl.BlockSpec((1, SSM_WIDTH, tn), lambda i, j: (layer, 1, j)),
        ],
        out_specs=pl.BlockSpec((tm, tn), lambda i, j: (i, j)),
        out_shape=jax.ShapeDtypeStruct((t, d), F32),
        scratch_shapes=[pltpu.VMEM((tm, ATT_WIDTH), BF16), pltpu.VMEM((tm, SSM_WIDTH), BF16)],
        input_output_aliases={0: 0},
        compiler_params=_cparams(("parallel", "arbitrary")),
        name="out_proj",
    )(h, att, ssm, ga.reshape(DEPTH, 1, ATT_WIDTH), gs.reshape(DEPTH, 1, SSM_WIDTH), w, w)


def _ffn_kernel(x_ref, g_ref, wg_ref, wu_ref, wd_ref, o_ref, hn_ref):
    @pl.when(pl.program_id(1) == 0)
    def _():
        x = x_ref[...]
        hn_ref[...] = _rms(x, g_ref[0]).astype(BF16)
        o_ref[...] = x

    hn = hn_ref[...]
    a = _bdot(hn, wg_ref[0].astype(BF16))
    b = _bdot(hn, wu_ref[0].astype(BF16))
    hid = (a * jax.nn.sigmoid(a)) * b
    o_ref[...] += _bdot(hid.astype(BF16), wd_ref[0].astype(BF16))


def _ffn(h, g, wg, wu, wd, layer, tm, tf):
    t, d = h.shape
    f = wg.shape[2]
    idx = layer // 2
    once = pl.Buffered(1)
    return pl.pallas_call(
        _ffn_kernel,
        grid=(t // tm, f // tf),
        in_specs=[
            pl.BlockSpec((tm, d), lambda i, j: (i, 0), pipeline_mode=once),
            pl.BlockSpec((1, 1, d), lambda i, j: (layer, 0, 0)),
            pl.BlockSpec((1, d, tf), lambda i, j: (idx, 0, j)),
            pl.BlockSpec((1, d, tf), lambda i, j: (idx, 0, j)),
            pl.BlockSpec((1, tf, d), lambda i, j: (idx, j, 0)),
        ],
        out_specs=pl.BlockSpec((tm, d), lambda i, j: (i, 0), pipeline_mode=once),
        out_shape=jax.ShapeDtypeStruct((t, d), F32),
        scratch_shapes=[pltpu.VMEM((tm, d), BF16)],
        input_output_aliases={0: 0},
        compiler_params=_cparams(("parallel", "arbitrary")),
        name="dense_ffn",
    )(h, g.reshape(DEPTH, 1, d), wg, wu, wd)


MOE_TM = 344
MOE_NT = (2 * T_TOK) // MOE_TM + N_EXPERTS
MOE_NP = MOE_NT * MOE_TM


def _split_bf16(x):
    hi = x.astype(BF16)
    lo = (x - hi.astype(F32)).astype(BF16)
    return hi, lo


def _lane_pick(x, lane, k):
    return jnp.sum(jnp.where(lane == k, x, 0.0), axis=-1, keepdims=True)


ROW_SLABS = D_MODEL // LANE


def _row_slab(ref, c, rows):
    return ref.at[pl.ds(c, rows, stride=ROW_SLABS), :]


def _router_kernel(x_ref, g_ref, wr_ref, sel_ref, wts_ref, rows_ref):
    x = x_ref[...]
    for c in range(ROW_SLABS):
        _row_slab(rows_ref, c, x.shape[0])[...] = x[:, c * LANE:(c + 1) * LANE]
    hn = _rms(x, g_ref[0])
    xh, xl = _split_bf16(hn)
    wh, wl = _split_bf16(wr_ref[0])
    logits = _bdot(xh, wh) + (_bdot(xh, wl) + _bdot(xl, wh)) + _bdot(xl, wl)
    lane = lax.broadcasted_iota(jnp.int32, logits.shape, 1).astype(F32)
    neg = jnp.float32(-jnp.inf)
    logits = jnp.where(lane < N_EXPERTS, logits, neg)
    v1 = jnp.max(logits, axis=-1, keepdims=True)
    i1 = jnp.min(jnp.where(logits == v1, lane, float(LANE)), axis=-1, keepdims=True)
    rest = jnp.where(lane == i1, neg, logits)
    v2 = jnp.max(rest, axis=-1, keepdims=True)
    i2 = jnp.min(jnp.where(rest == v2, lane, float(LANE)), axis=-1, keepdims=True)
    e2 = jnp.exp(v2 - v1)
    w1 = 1.0 / (1.0 + e2)
    w2 = e2 / (1.0 + e2)
    sel_ref[...] = jnp.where((lane == i1) | (lane == i2), 1.0, 0.0)
    wts_ref[...] = jnp.where(lane == 0, w1, jnp.where(lane == 1, w2,
                             jnp.where(lane == 2, i1, jnp.where(lane == 3, i2, 0.0))))


def _router(h, g, wr_pad, layer, tm):
    t, d = h.shape
    idx = layer // 2
    row = pl.BlockSpec((tm, LANE), lambda i: (i, 0))
    tab = jax.ShapeDtypeStruct((t, LANE), F32)
    return pl.pallas_call(
        _router_kernel,
        grid=(t // tm,),
        in_specs=[
            pl.BlockSpec((tm, d), lambda i: (i, 0)),
            pl.BlockSpec((1, 1, d), lambda i: (layer, 0, 0)),
            pl.BlockSpec((1, d, LANE), lambda i: (idx, 0, 0)),
        ],
        out_specs=[row, row, pl.BlockSpec((tm * ROW_SLABS, LANE), lambda i: (i, 0))],
        out_shape=[tab, tab, jax.ShapeDtypeStruct((t * ROW_SLABS, LANE), F32)],
        compiler_params=_cparams(("parallel",)),
        name="moe_router",
    )(h, g.reshape(DEPTH, 1, d), wr_pad)


def _positions_kernel(sel_ref, wts_ref, pos_ref, meta_ref, cnt_ref, off_ref):
    p = pl.program_id(0)
    i = pl.program_id(1)
    tm = sel_ref.shape[0]
    lane = lax.broadcasted_iota(jnp.int32, (1, LANE), 1).astype(F32)
    sel = sel_ref[...]

    @pl.when((p == 0) & (i == 0))
    def _():
        cnt_ref[...] = jnp.zeros((1, LANE), F32)

    @pl.when(p == 0)
    def _():
        cnt_ref[...] += jnp.sum(sel, axis=0, keepdims=True)

    @pl.when((p == 1) & (i == 0))
    def _():
        cnt = cnt_ref[...]
        tiles = jnp.zeros((1, LANE), F32)
        for k in range(MOE_NT):
            tiles = tiles + jnp.where(cnt > float(k * MOE_TM), 1.0, 0.0)
        padded = tiles * float(MOE_TM)
        off = jnp.zeros((1, LANE), F32)
        for e in range(N_EXPERTS):
            off = off + jnp.where(lane > e, _lane_pick(padded, lane, e), 0.0)
        end = off + padded
        tile_start = lane * float(MOE_TM)
        owner = jnp.zeros((1, LANE), F32)
        for e in range(N_EXPERTS):
            owner = owner + jnp.where(tile_start >= _lane_pick(end, lane, e), 1.0, 0.0)
        owner = jnp.minimum(owner, float(N_EXPERTS - 1))
        used = jnp.sum(tiles, axis=-1, keepdims=True)
        off_ref[...] = off
        cnt_ref[...] = jnp.zeros((1, LANE), F32)
        row = lax.broadcasted_iota(jnp.int32, (SUBLANE, LANE), 0)
        meta = jnp.where(row == 0, owner, jnp.where(row == 1, used, jnp.where(row == 2, cnt, off)))
        meta_ref[...] = meta.astype(jnp.int32)

    @pl.when(p == 1)
    def _():
        r = lax.broadcasted_iota(jnp.int32, (tm, tm), 0)
        c = lax.broadcasted_iota(jnp.int32, (tm, tm), 1)
        earlier = jnp.where(c < r, 1.0, 0.0).astype(BF16)
        rank = _bdot(earlier, sel.astype(BF16)) + cnt_ref[...]
        slot = off_ref[...] + rank
        wts = wts_ref[...]
        lane_t = lax.broadcasted_iota(jnp.int32, (tm, LANE), 1).astype(F32)
        p1 = _lane_pick(slot, lane_t, _lane_pick(wts, lane_t, 2))
        p2 = _lane_pick(slot, lane_t, _lane_pick(wts, lane_t, 3))
        pos_ref[...] = jnp.where(lane_t == 0, p1, jnp.where(lane_t == 1, p2, 0.0)).astype(jnp.int32)
        cnt_ref[...] += jnp.sum(sel, axis=0, keepdims=True)


def _positions(sel, wts, tm):
    t = sel.shape[0]
    row = pl.BlockSpec((tm, LANE), lambda p, i: (i, 0))
    return pl.pallas_call(
        _positions_kernel,
        grid=(2, t // tm),
        in_specs=[row, row],
        out_specs=[
            pl.BlockSpec((tm, LANE), lambda p, i: (i * p, 0)),
            pl.BlockSpec((SUBLANE, LANE), lambda p, i: (0, 0)),
        ],
        out_shape=[jax.ShapeDtypeStruct((t, LANE), jnp.int32),
                   jax.ShapeDtypeStruct((SUBLANE, LANE), jnp.int32)],
        scratch_shapes=[pltpu.VMEM((1, LANE), F32), pltpu.VMEM((1, LANE), F32)],
        compiler_params=_cparams(("arbitrary", "arbitrary")),
        name="moe_positions",
    )(sel, wts)


def _gather_kernel(pos1_ref, pos2_ref, used_ref, rows_ref, xs_ref, src_ref, sem):
    i = pl.program_id(0)

    @pl.when(i == 0)
    def _():
        def clear(p, c):
            src_ref[p] = 0
            return c

        lax.fori_loop(0, MOE_NP, clear, 0, unroll=8)

        def fill(t, c):
            src_ref[pos1_ref[t]] = t
            src_ref[pos2_ref[t]] = t
            return c

        lax.fori_loop(0, T_TOK, fill, 0, unroll=4)

    def row_copy(tok, r):
        return pltpu.make_async_copy(rows_ref.at[pl.ds(tok * ROW_SLABS, ROW_SLABS), :],
                                     xs_ref.at[pl.ds(r * ROW_SLABS, ROW_SLABS), :], sem)

    @pl.when(i < used_ref[0])
    def _():
        def issue(r, c):
            row_copy(src_ref[i * MOE_TM + r], r).start()
            return c

        lax.fori_loop(0, MOE_TM, issue, 0, unroll=4)

        def land(r, c):
            row_copy(0, 0).wait()
            return c

        lax.fori_loop(0, MOE_TM, land, 0, unroll=4)

    @pl.when(i >= used_ref[0])
    def _():
        xs_ref[...] = jnp.zeros(xs_ref.shape, F32)


def _gather(rows, pos1, pos2, used):
    blk = MOE_TM * ROW_SLABS
    return pl.pallas_call(
        _gather_kernel,
        grid_spec=pltpu.PrefetchScalarGridSpec(
            num_scalar_prefetch=3,
            grid=(MOE_NT,),
            in_specs=[pl.BlockSpec(memory_space=pl.ANY)],
            out_specs=pl.BlockSpec((blk, LANE), lambda i, a, b, u: (i, 0)),
            scratch_shapes=[pltpu.SMEM((MOE_NP,), jnp.int32), pltpu.SemaphoreType.DMA(())],
        ),
        out_shape=jax.ShapeDtypeStruct((MOE_NP * ROW_SLABS, LANE), F32),
        compiler_params=_cparams(("arbitrary",)),
        name="moe_gather",
    )(pos1, pos2, used, rows)


def _experts_kernel(owner_ref, used_ref, x_ref, g_ref, wg_ref, wu_ref, wd_ref, o_ref):
    live = pl.program_id(0) < used_ref[0]

    @pl.when(live)
    def _():
        slabs = [_row_slab(x_ref, c, MOE_TM)[...] for c in range(ROW_SLABS)]
        ssq = slabs[0] * slabs[0]
        for c in range(1, ROW_SLABS):
            ssq = ssq + slabs[c] * slabs[c]
        inv = lax.rsqrt(jnp.sum(ssq, axis=-1, keepdims=True) / D_MODEL + EPS)
        g = g_ref[0]
        hn = jnp.concatenate([slabs[c] * inv * g[:, c * LANE:(c + 1) * LANE]
                              for c in range(ROW_SLABS)], axis=1).astype(BF16)
        a = _bdot(hn, wg_ref[0, 0])
        b = _bdot(hn, wu_ref[0, 0])
        hid = ((a * jax.nn.sigmoid(a)) * b).astype(BF16)
        res = _bdot(hid, wd_ref[0, 0])
        for c in range(ROW_SLABS):
            _row_slab(o_ref, c, MOE_TM)[...] = res[:, c * LANE:(c + 1) * LANE]

    @pl.when(jnp.logical_not(live))
    def _():
        o_ref[...] = jnp.zeros(o_ref.shape, F32)


def _experts(xs, owner, used, g, wg, wu, wd, layer):
    d = D_MODEL
    idx = layer // 2
    blk = MOE_TM * ROW_SLABS

    def tile(i, used_ref):
        return jnp.minimum(i, used_ref[0] - 1)

    w_map = lambda i, o, u: (idx, o[tile(i, u)], 0, 0)
    return pl.pallas_call(
        _experts_kernel,
        grid_spec=pltpu.PrefetchScalarGridSpec(
            num_scalar_prefetch=2,
            grid=(MOE_NT,),
            in_specs=[
                pl.BlockSpec((blk, LANE), lambda i, o, u: (tile(i, u), 0)),
                pl.BlockSpec((1, 1, d), lambda i, o, u: (layer, 0, 0)),
                pl.BlockSpec((1, 1, d, D_FF_EXPERT), w_map),
                pl.BlockSpec((1, 1, d, D_FF_EXPERT), w_map),
                pl.BlockSpec((1, 1, D_FF_EXPERT, d), w_map),
            ],
            out_specs=pl.BlockSpec((blk, LANE), lambda i, o, u: (i, 0)),
        ),
        out_shape=jax.ShapeDtypeStruct((MOE_NP * ROW_SLABS, LANE), F32),
        compiler_params=_cparams(("arbitrary",)),
        name="moe_experts",
    )(owner, used, xs, g.reshape(DEPTH, 1, d), wg, wu, wd)


def _combine_kernel(pos1_ref, pos2_ref, h_ref, wts_ref, ys_ref, o_ref, y1, y2, sem1, sem2):
    tm = h_ref.shape[0]
    base = pl.program_id(0) * tm

    def row_copy(slot, r, buf, sem):
        return pltpu.make_async_copy(ys_ref.at[pl.ds(slot * ROW_SLABS, ROW_SLABS), :],
                                     buf.at[pl.ds(r * ROW_SLABS, ROW_SLABS), :], sem)

    def fetch(r, c):
        row_copy(pos1_ref[base + r], r, y1, sem1).start()
        row_copy(pos2_ref[base + r], r, y2, sem2).start()
        return c

    lax.fori_loop(0, tm, fetch, 0, unroll=4)

    def land(r, c):
        row_copy(0, 0, y1, sem1).wait()
        row_copy(0, 0, y2, sem2).wait()
        return c

    lax.fori_loop(0, tm, land, 0, unroll=4)
    wts = wts_ref[...]
    lane = lax.broadcasted_iota(jnp.int32, wts.shape, 1).astype(F32)
    w1 = _lane_pick(wts, lane, 0)
    w2 = _lane_pick(wts, lane, 1)
    for c in range(ROW_SLABS):
        cols = slice(c * LANE, (c + 1) * LANE)
        o_ref[:, cols] = h_ref[:, cols] + (w1 * _row_slab(y1, c, tm)[...] + w2 * _row_slab(y2, c, tm)[...])


def _combine(h, wts, ys, pos1, pos2, tm):
    t, d = h.shape
    return pl.pallas_call(
        _combine_kernel,
        grid_spec=pltpu.PrefetchScalarGridSpec(
            num_scalar_prefetch=2,
            grid=(t // tm,),
            in_specs=[
                pl.BlockSpec((tm, d), lambda i, a, b: (i, 0)),
                pl.BlockSpec((tm, LANE), lambda i, a, b: (i, 0)),
                pl.BlockSpec(memory_space=pl.ANY),
            ],
            out_specs=pl.BlockSpec((tm, d), lambda i, a, b: (i, 0)),
            scratch_shapes=[
                pltpu.VMEM((tm * ROW_SLABS, LANE), F32),
                pltpu.VMEM((tm * ROW_SLABS, LANE), F32),
                pltpu.SemaphoreType.DMA(()),
                pltpu.SemaphoreType.DMA(()),
            ],
        ),
        out_shape=jax.ShapeDtypeStruct((t, d), F32),
        input_output_aliases={2: 0},
        compiler_params=_cparams(("arbitrary",)),
        name="moe_combine",
    )(pos1, pos2, h, wts, ys)


def _moe(h, g, wr_pad, wg, wu, wd, layer):
    sel, wts, rows = _router(h, g, wr_pad, layer, tm=688)
    pos, meta = _positions(sel, wts, tm=688)
    pos1 = pos[:, 0]
    pos2 = pos[:, 1]
    owner = meta[0, :MOE_NT]
    used = meta[1, :1]
    xs = _gather(rows, pos1, pos2, used)
    ys = _experts(xs, owner, used, g, wg, wu, wd, layer)
    return _combine(h, wts, ys, pos1, pos2, tm=688)


def _final_norm_kernel(x_ref, g_ref, o_ref):
    o_ref[...] = _rms(x_ref[...], g_ref[...])


def _final_norm(h, g, tm):
    t, d = h.shape
    return pl.pallas_call(
        _final_norm_kernel,
        grid=(t // tm,),
        in_specs=[pl.BlockSpec((tm, d), lambda i: (i, 0)), pl.BlockSpec((1, d), lambda i: (0, 0))],
        out_specs=pl.BlockSpec((tm, d), lambda i: (i, 0)),
        out_shape=jax.ShapeDtypeStruct((t, d), F32),
        compiler_params=_cparams(("parallel",)),
        name="final_norm",
    )(h, g.reshape(1, -1))


def _rope_partner(w):
    half = QK_ROPE // 2
    return jnp.concatenate([-w[..., half:], w[..., :half]], axis=-1)


def _pad_cols(w, width):
    return jnp.pad(w, [(0, 0)] * (w.ndim - 1) + [(0, width - w.shape[-1])])


def _in_proj_layout(w_in):
    kr = w_in[..., Q_LORA + KV_LORA:Q_LORA + KV_LORA + QK_ROPE]
    return jnp.concatenate([
        w_in[..., :Q_LORA + KV_LORA],
        _pad_cols(kr, LANE),
        _pad_cols(_rope_partner(kr), LANE),
        w_in[..., Q_LORA + KV_LORA + QK_ROPE:]], axis=-1)


def _q_layouts(w_uq):
    w = w_uq.reshape(DEPTH, Q_LORA, ATT_HEADS, QK_NOPE + QK_ROPE)
    main = _pad_cols(w, HEAD_PAD).reshape(DEPTH, Q_LORA, ATT_HEADS * HEAD_PAD)
    rot = _pad_cols(_rope_partner(w[..., QK_NOPE:]), LANE).reshape(DEPTH, Q_LORA, ATT_HEADS * LANE)
    return main, rot


def _rope_tables():
    inv = ROPE_THETA ** (-jnp.arange(0, QK_ROPE, 2, dtype=F32) / QK_ROPE)
    ang = jnp.arange(L_TOK, dtype=F32)[:, None] * inv[None, :]
    cos = jnp.cos(ang)
    sin = jnp.sin(ang)
    cos_t = _pad_cols(jnp.concatenate([cos, cos], axis=-1), LANE)
    sin_t = _pad_cols(jnp.concatenate([sin, sin], axis=-1), LANE)
    return cos_t, sin_t


def kernel(x, meta_tokens, mix_norm, w_in, q_norm, w_uq, kv_norm, w_ukv, ssm_lambda_re, ssm_lambda_im, ssm_log_step, ssm_b_re, ssm_b_im, ssm_c_re, ssm_c_im, ssm_d, ssm_w_glu, attn_out_norm, ssm_out_norm, w_out, ffn_norm, dense_w_gate, dense_w_up, dense_w_down, moe_router, moe_w_gate, moe_w_up, moe_w_down, final_norm):
    meta = jnp.broadcast_to(meta_tokens[None].astype(x.dtype), (BATCH, N_META, D_MODEL))
    h = jnp.concatenate([meta, x], axis=1).reshape(T_TOK, D_MODEL)
    cos_t, sin_t = _rope_tables()
    w_in_l = _in_proj_layout(w_in)
    wq_main, wq_rot = _q_layouts(w_uq)
    wr_pad = _pad_cols(moe_router, LANE)
    moe_w_gate, moe_w_up, moe_w_down = (w.astype(BF16) for w in (moe_w_gate, moe_w_up, moe_w_down))
    wb, wc, tre, tim = _s5_layouts(ssm_lambda_re, ssm_lambda_im, ssm_log_step,
                                   ssm_b_re, ssm_b_im, ssm_c_re, ssm_c_im)
    for layer in range(DEPTH):
        proj = _norm_matmul(h, mix_norm, w_in_l, layer, tm=1376, tn=512)
        q, kv, kpe = _upproj(proj, q_norm, kv_norm, wq_main, wq_rot, w_ukv, cos_t, sin_t, layer,
                             tm=688)
        att = _attention(q, kv, kpe, tq=688)
        yf, yb = _s5_scan(proj, wb, wc, tre, tim, layer)
        ssm = _glu(yf, yb, proj, ssm_d, ssm_w_glu, layer, tm=688)
        h = _outproj(h, att, ssm, attn_out_norm, ssm_out_norm, w_out, layer, tm=1376, tn=512)
        if layer % 2 == 0:
            h = _ffn(h, ffn_norm, dense_w_gate, dense_w_up, dense_w_down, layer, tm=1376, tf=256)
        else:
            h = _moe(h, ffn_norm, wr_pad, moe_w_gate, moe_w_up, moe_w_down, layer)
    out = _final_norm(h, final_norm, tm=688)
    return out.reshape(BATCH, L_TOK, D_MODEL)[:, N_META:]
```

```python
import functools
import math

import jax
import jax.numpy as jnp
from jax import lax
from jax.experimental import pallas as pl
from jax.experimental.pallas import tpu as pltpu

F32 = jnp.float32
BF16 = jnp.bfloat16

D_MODEL = 2048
BATCH = 4
SEQ = 2048
DEPTH = 4
N_META = 16
EPS = 1e-6
ATT_HEADS = 8
QK_NOPE = 128
QK_ROPE = 64
V_DIM = 128
Q_LORA = 512
KV_LORA = 256
ROPE_THETA = 10000.0
ATT_WIDTH = ATT_HEADS * V_DIM
SSM_GROUP = 16
SSM_WIDTH = D_MODEL - ATT_WIDTH
SSM_GROUPS = SSM_WIDTH // SSM_GROUP
SSM_STATE = 64
D_FF = 5632
N_EXPERTS = 8
D_FF_EXPERT = 1408

L_TOK = N_META + SEQ
T_TOK = L_TOK * BATCH
LANE = 128
SUBLANE = 8
HEAD_PAD = 256
L_PAD = 2176
PROJ_COLS = 2048
U_COL0 = 1024
VMEM_LIMIT = 58 * 1024 * 1024

SSM_BLK_GROUPS = 8
SSM_BLKS = SSM_GROUPS // SSM_BLK_GROUPS
SSM_BLK_CH = SSM_BLK_GROUPS * SSM_GROUP
SSM_BLK_ST = SSM_BLK_GROUPS * SSM_STATE
S5_CHUNKS = 3
S5_STEPS = L_TOK // S5_CHUNKS
S5_ROWS = S5_STEPS * BATCH


def _cparams(sem):
    return pltpu.CompilerParams(dimension_semantics=sem, vmem_limit_bytes=VMEM_LIMIT)


def _rms(x, g):
    ms = jnp.mean(x * x, axis=-1, keepdims=True)
    return x * lax.rsqrt(ms + EPS) * g


def _bdot(a, b):
    return jnp.dot(a, b, preferred_element_type=F32)


def _rms_cast_kernel(x_ref, g_ref, o_ref):
    o_ref[...] = _rms(x_ref[...], g_ref[0]).astype(BF16)


def _rms_cast(x, g, layer, tm):
    t, d = x.shape
    return pl.pallas_call(
        _rms_cast_kernel,
        grid=(t // tm,),
        in_specs=[pl.BlockSpec((tm, d), lambda i: (i, 0)),
                  pl.BlockSpec((1, 1, d), lambda i: (layer, 0, 0))],
        out_specs=pl.BlockSpec((tm, d), lambda i: (i, 0)),
        out_shape=jax.ShapeDtypeStruct((t, d), BF16),
        compiler_params=_cparams(("parallel",)),
        name="rms_cast",
    )(x, g.reshape(DEPTH, 1, d))


def _in_proj_kernel(x_ref, w_ref, o_ref):
    o_ref[...] = _bdot(x_ref[...], w_ref[0].astype(BF16))


def _in_proj(hn, w, layer, tm, tn):
    t, d = hn.shape
    n = w.shape[2]
    return pl.pallas_call(
        _in_proj_kernel,
        grid=(t // tm, n // tn),
        in_specs=[
            pl.BlockSpec((tm, d), lambda i, j: (i, 0)),
            pl.BlockSpec((1, d, tn), lambda i, j: (layer, 0, j)),
        ],
        out_specs=pl.BlockSpec((tm, tn), lambda i, j: (i, j)),
        out_shape=jax.ShapeDtypeStruct((t, n), F32),
        compiler_params=_cparams(("parallel", "arbitrary")),
        name="in_proj",
    )(hn, w)


def _upproj_kernel(cq_ref, ckv_ref, kr_ref, krot_ref, qn_ref, kvn_ref, wq_ref, wqr_ref,
                   wkv_ref, cos_ref, sin_ref, q_ref, kv_ref, kpe_ref):
    scale = (QK_NOPE + QK_ROPE) ** -0.5 * math.log2(math.e)
    cqn = _rms(cq_ref[...], qn_ref[0]).astype(BF16)
    a = _bdot(cqn, wq_ref[0].astype(BF16))
    r = _bdot(cqn, wqr_ref[0].astype(BF16))
    c = cos_ref[...]
    s = sin_ref[...]
    last_lane = lax.broadcasted_iota(jnp.int32, c.shape, 1) == LANE - 1
    for h in range(ATT_HEADS):
        lo = h * HEAD_PAD
        q_ref[:, lo:lo + LANE] = (a[:, lo:lo + LANE] * scale).astype(BF16)
        rope = a[:, lo + LANE:lo + HEAD_PAD] * c + r[:, h * LANE:(h + 1) * LANE] * s
        q_ref[:, lo + LANE:lo + HEAD_PAD] = jnp.where(last_lane, 1.0, rope * scale).astype(BF16)
    ckvn = _rms(ckv_ref[...], kvn_ref[0]).astype(BF16)
    kv_ref[...] = _bdot(ckvn, wkv_ref[0].astype(BF16)).astype(BF16)
    kpe_ref[...] = (kr_ref[...] * c + krot_ref[...] * s).astype(BF16)


def _upproj(proj, qn, kvn, wq, wqr, wkv, cos_t, sin_t, layer, tm):
    t = proj.shape[0]
    full = lambda a: pl.BlockSpec((1,) + a.shape[1:], lambda i: (layer, 0, 0))
    qn = qn.reshape(DEPTH, 1, Q_LORA)
    kvn = kvn.reshape(DEPTH, 1, KV_LORA)
    pos_blocks = L_TOK // tm
    return pl.pallas_call(
        _upproj_kernel,
        grid=(t // tm,),
        in_specs=[
            pl.BlockSpec((tm, Q_LORA), lambda i: (i, 0)),
            pl.BlockSpec((tm, KV_LORA), lambda i: (i, Q_LORA // KV_LORA)),
            pl.BlockSpec((tm, LANE), lambda i: (i, (Q_LORA + KV_LORA) // LANE)),
            pl.BlockSpec((tm, LANE), lambda i: (i, (Q_LORA + KV_LORA) // LANE + 1)),
            full(qn),
            full(kvn),
            full(wq),
            full(wqr),
            full(wkv),
            pl.BlockSpec((tm, LANE), lambda i: (i % pos_blocks, 0)),
            pl.BlockSpec((tm, LANE), lambda i: (i % pos_blocks, 0)),
        ],
        out_specs=[
            pl.BlockSpec((tm, ATT_HEADS * HEAD_PAD), lambda i: (i, 0)),
            pl.BlockSpec((tm, ATT_HEADS * (QK_NOPE + V_DIM)), lambda i: (i, 0)),
            pl.BlockSpec((tm, LANE), lambda i: (i, 0)),
        ],
        out_shape=[
            jax.ShapeDtypeStruct((t, ATT_HEADS * HEAD_PAD), BF16),
            jax.ShapeDtypeStruct((t, ATT_HEADS * (QK_NOPE + V_DIM)), BF16),
            jax.ShapeDtypeStruct((t, LANE), BF16),
        ],
        compiler_params=_cparams(("parallel",)),
        name="upproj",
    )(proj, proj, proj, proj, qn, kvn, wq, wqr, wkv, cos_t, sin_t)


ATT_HPS = 2
PAD_BIAS = -1e30


def _attn_kernel(q_ref, kv_ref, kpe_ref, o_ref, k_sc, v_sc):
    @pl.when(pl.program_id(2) == 0)
    def _():
        n_pad = L_PAD - L_TOK
        pad_lane = lax.broadcasted_iota(jnp.int32, (n_pad, LANE), 1)
        k_pad = jnp.where(pad_lane == LANE - 1, PAD_BIAS, 0.0).astype(BF16)
        row_lane = lax.broadcasted_iota(jnp.int32, (L_TOK, LANE), 1)
        ones_col = jnp.where(row_lane == 0, 1.0, 0.0).astype(BF16)
        for hh in range(ATT_HPS):
            lo = hh * HEAD_PAD
            k_sc[hh, 0:L_TOK, 0:LANE] = kv_ref[:, lo:lo + LANE]
            k_sc[hh, 0:L_TOK, LANE:HEAD_PAD] = kpe_ref[...]
            k_sc[hh, L_TOK:L_PAD, 0:LANE] = jnp.zeros((n_pad, LANE), BF16)
            k_sc[hh, L_TOK:L_PAD, LANE:HEAD_PAD] = k_pad
            v_sc[hh, 0:L_TOK, 0:V_DIM] = kv_ref[:, lo + LANE:lo + HEAD_PAD]
            v_sc[hh, 0:L_TOK, V_DIM:HEAD_PAD] = ones_col
            v_sc[hh, L_TOK:L_PAD, :] = jnp.zeros((n_pad, HEAD_PAD), BF16)

    for hh in range(ATT_HPS):
        q = q_ref[:, hh * HEAD_PAD:(hh + 1) * HEAD_PAD]
        s = lax.dot_general(q, k_sc[hh], (((1,), (1,)), ((), ())),
                            preferred_element_type=F32)
        m = jnp.max(s, axis=-1, keepdims=True)
        p = jnp.exp2(s - m).astype(BF16)
        o = _bdot(p, v_sc[hh])
        o_ref[:, hh * V_DIM:(hh + 1) * V_DIM] = o[:, :V_DIM] / o[:, V_DIM:V_DIM + 1]


def _attention(q, kv, kpe, tq):
    nq = L_TOK // tq
    hps = ATT_HPS
    return pl.pallas_call(
        _attn_kernel,
        grid=(BATCH, ATT_HEADS // hps, nq),
        in_specs=[
            pl.BlockSpec((tq, hps * HEAD_PAD), lambda b, h, i: (b * nq + i, h)),
            pl.BlockSpec((L_TOK, hps * HEAD_PAD), lambda b, h, i: (b, h)),
            pl.BlockSpec((L_TOK, LANE), lambda b, h, i: (b, 0)),
        ],
        out_specs=pl.BlockSpec((tq, hps * V_DIM), lambda b, h, i: (b * nq + i, h)),
        out_shape=jax.ShapeDtypeStruct((T_TOK, ATT_WIDTH), F32),
        scratch_shapes=[pltpu.VMEM((hps, L_PAD, HEAD_PAD), BF16),
                        pltpu.VMEM((hps, L_PAD, HEAD_PAD), BF16)],
        compiler_params=_cparams(("parallel", "parallel", "arbitrary")),
        name="attention",
    )(q, kv, kpe)


def _s5_param_kernel(lre_ref, lim_ref, ls_ref, bre_ref, bim_ref,
                     lbr_o, lbi_o, l2r_o, l2i_o, bbr_o, bbi_o, lbbr_o, lbbi_o):
    lre = lre_ref[...]
    lim = lim_ref[...]
    dt = jnp.exp(ls_ref[...])
    mag = jnp.exp(lre * dt)
    ang = lim * dt
    br = mag * jnp.cos(ang)
    bi = mag * jnp.sin(ang)
    nr = br - 1.0
    den = lre * lre + lim * lim
    cr = (nr * lre + bi * lim) / den
    ci = (bi * lre - nr * lim) / den
    b_r = bre_ref[...]
    b_i = bim_ref[...]
    bbr = cr * b_r - ci * b_i
    bbi = cr * b_i + ci * b_r
    lbr_o[...] = br
    lbi_o[...] = bi
    l2r_o[...] = br * br - bi * bi
    l2i_o[...] = 2.0 * br * bi
    bbr_o[...] = bbr
    bbi_o[...] = bbi
    lbbr_o[...] = br * bbr - bi * bbi
    lbbi_o[...] = br * bbi + bi * bbr


def _s5_params(lam_re, lam_im, log_step, b_re, b_im):
    n = DEPTH * 2 * SSM_GROUPS
    lre = lam_re.reshape(n, 1, SSM_STATE)
    lim = lam_im.reshape(n, 1, SSM_STATE)
    ls = log_step.reshape(n, 1, 1)
    btr = b_re.transpose(0, 1, 2, 4, 3).reshape(n, SSM_GROUP, SSM_STATE)
    bti = b_im.transpose(0, 1, 2, 4, 3).reshape(n, SSM_GROUP, SSM_STATE)
    small = jax.ShapeDtypeStruct((n, 1, SSM_STATE), F32)
    big = jax.ShapeDtypeStruct((n, SSM_GROUP, SSM_STATE), F32)
    return pl.pallas_call(
        _s5_param_kernel,
        out_shape=[small, small, small, small, big, big, big, big],
        compiler_params=pltpu.CompilerParams(vmem_limit_bytes=VMEM_LIMIT),
        name="s5_params",
    )(lre, lim, ls, btr, bti)


def _s5_layouts(lam_re, lam_im, log_step, b_re, b_im, c_re, c_im):
    lbr, lbi, l2r, l2i, bbr, bbi, lbbr, lbbi = _s5_params(lam_re, lam_im, log_step, b_re, b_im)
    eye = jnp.eye(SSM_BLK_GROUPS, dtype=F32)
    lead = (DEPTH, 2, SSM_BLKS, SSM_BLK_GROUPS)

    def in_block(x):
        x = x.reshape(lead + (SSM_GROUP, SSM_STATE))
        return jnp.einsum('ldbgcp,gh->ldbgchp', x, eye).reshape(
            DEPTH, 2, SSM_BLKS, SSM_BLK_CH, SSM_BLK_ST)

    def out_block(x):
        x = x.reshape(lead + (SSM_GROUP, SSM_STATE))
        return jnp.einsum('ldbgcp,gh->ldbgphc', x, eye).reshape(
            DEPTH, 2, SSM_BLKS, SSM_BLK_ST, SSM_BLK_CH)

    wb = jnp.concatenate([
        jnp.concatenate([in_block(bbr), in_block(bbi)], axis=-1),
        jnp.concatenate([in_block(lbbr), in_block(lbbi)], axis=-1)], axis=-2).astype(BF16)
    wc = jnp.concatenate([out_block(c_re), out_block(-c_im)], axis=-2).astype(BF16)

    def table(one, two):
        one = one.reshape(DEPTH, 2, SSM_BLKS, 1, SSM_BLK_ST)
        two = two.reshape(DEPTH, 2, SSM_BLKS, 1, SSM_BLK_ST)
        half = SUBLANE // 2
        fwd = jnp.concatenate([jnp.broadcast_to(one[:, 0:1], (DEPTH, 1, SSM_BLKS, half, SSM_BLK_ST)),
                               jnp.broadcast_to(two[:, 0:1], (DEPTH, 1, SSM_BLKS, half, SSM_BLK_ST))], axis=3)
        bwd = jnp.concatenate([jnp.broadcast_to(two[:, 1:2], (DEPTH, 1, SSM_BLKS, half, SSM_BLK_ST)),
                               jnp.broadcast_to(one[:, 1:2], (DEPTH, 1, SSM_BLKS, half, SSM_BLK_ST))], axis=3)
        return jnp.concatenate([fwd, bwd], axis=1)

    return wb, wc, table(lbr, l2r), table(lbi, l2i)


def _s5_scan_kernel(uf_ref, ub_ref, wb_ref, wc_ref, tre_ref, tim_ref, yf_ref, yb_ref,
                    tmaj, xf, xb, carry):
    @pl.when(pl.program_id(1) == 0)
    def _():
        carry[...] = jnp.zeros(carry.shape, F32)

    n_tiles = S5_ROWS // SUBLANE
    half = SUBLANE // 2

    def time_major(u_ref):
        for b in range(BATCH):
            tmaj[pl.ds(b, S5_STEPS, stride=BATCH), :] = u_ref[b]
        return tmaj[...]

    def batch_major(y, y_ref):
        tmaj[...] = y
        for b in range(BATCH):
            y_ref[b] = tmaj[pl.ds(b, S5_STEPS, stride=BATCH), :]

    def paired_lhs(u, take_upper):
        u3 = u.reshape(n_tiles, SUBLANE, SSM_BLK_CH)
        swapped = pltpu.roll(u3, half, axis=1)
        sub = lax.broadcasted_iota(jnp.int32, u3.shape, 1)
        keep = (sub >= half) if take_upper else (sub < half)
        nb = jnp.where(keep, swapped, 0.0).reshape(S5_ROWS, SSM_BLK_CH)
        return jnp.concatenate([u, nb], axis=1).astype(BF16)

    xf[...] = _bdot(paired_lhs(time_major(uf_ref), True), wb_ref[0, 0, 0])
    xb[...] = _bdot(paired_lhs(time_major(ub_ref), False), wb_ref[0, 1, 0])

    lower = lax.broadcasted_iota(jnp.int32, (SUBLANE, 2 * LANE), 0) < half
    for hh in range(SSM_BLK_ST // (2 * LANE)):
        lo = hh * 2 * LANE
        re = slice(lo, lo + 2 * LANE)
        im = slice(SSM_BLK_ST + lo, SSM_BLK_ST + lo + 2 * LANE)
        afr, afi = tre_ref[0, 0, 0, :, re], tim_ref[0, 0, 0, :, re]
        abr, abi = tre_ref[0, 1, 0, :, re], tim_ref[0, 1, 0, :, re]

        def body(k, st):
            hfr, hfi, hbr, hbi = st
            rf = pl.multiple_of(k * SUBLANE, SUBLANE)
            pr = jnp.where(lower, pltpu.roll(hfr, half, axis=0), hfr)
            pi = jnp.where(lower, pltpu.roll(hfi, half, axis=0), hfi)
            nfr = xf[pl.ds(rf, SUBLANE), re] + (afr * pr - afi * pi)
            nfi = xf[pl.ds(rf, SUBLANE), im] + (afr * pi + afi * pr)
            xf[pl.ds(rf, SUBLANE), re] = nfr
            xf[pl.ds(rf, SUBLANE), im] = nfi
            rb = pl.multiple_of((n_tiles - 1 - k) * SUBLANE, SUBLANE)
            qr = jnp.where(lower, hbr, pltpu.roll(hbr, half, axis=0))
            qi = jnp.where(lower, hbi, pltpu.roll(hbi, half, axis=0))
            nbr = xb[pl.ds(rb, SUBLANE), re] + (abr * qr - abi * qi)
            nbi = xb[pl.ds(rb, SUBLANE), im] + (abr * qi + abi * qr)
            xb[pl.ds(rb, SUBLANE), re] = nbr
            xb[pl.ds(rb, SUBLANE), im] = nbi
            return nfr, nfi, nbr, nbi

        init = (carry[0, :, re], carry[1, :, re], carry[2, :, re], carry[3, :, re])
        fin = lax.fori_loop(0, n_tiles, body, init, unroll=2)
        for idx in range(4):
            carry[idx, :, re] = fin[idx]

    batch_major(_bdot(xf[...].astype(BF16), wc_ref[0, 0, 0]), yf_ref)
    batch_major(_bdot(xb[...].astype(BF16), wc_ref[0, 1, 0]), yb_ref)


def _s5_scan(proj, wb, wc, tre, tim, layer):
    last = S5_CHUNKS - 1
    u_blk0 = U_COL0 // SSM_BLK_CH
    proj3 = proj.reshape(BATCH, L_TOK, PROJ_COLS)
    y_shape = jax.ShapeDtypeStruct((BATCH, L_TOK, SSM_WIDTH), F32)
    yf, yb = pl.pallas_call(
        _s5_scan_kernel,
        grid=(SSM_BLKS, S5_CHUNKS),
        in_specs=[
            pl.BlockSpec((BATCH, S5_STEPS, SSM_BLK_CH), lambda j, c: (0, c, u_blk0 + j)),
            pl.BlockSpec((BATCH, S5_STEPS, SSM_BLK_CH), lambda j, c: (0, last - c, u_blk0 + j)),
            pl.BlockSpec((1, 2, 1, 2 * SSM_BLK_CH, 2 * SSM_BLK_ST), lambda j, c: (layer, 0, j, 0, 0)),
            pl.BlockSpec((1, 2, 1, 2 * SSM_BLK_ST, SSM_BLK_CH), lambda j, c: (layer, 0, j, 0, 0)),
            pl.BlockSpec((1, 2, 1, SUBLANE, SSM_BLK_ST), lambda j, c: (layer, 0, j, 0, 0)),
            pl.BlockSpec((1, 2, 1, SUBLANE, SSM_BLK_ST), lambda j, c: (layer, 0, j, 0, 0)),
        ],
        out_specs=[
            pl.BlockSpec((BATCH, S5_STEPS, SSM_BLK_CH), lambda j, c: (0, c, j)),
            pl.BlockSpec((BATCH, S5_STEPS, SSM_BLK_CH), lambda j, c: (0, last - c, j)),
        ],
        out_shape=[y_shape, y_shape],
        scratch_shapes=[
            pltpu.VMEM((S5_ROWS, SSM_BLK_CH), F32),
            pltpu.VMEM((S5_ROWS, 2 * SSM_BLK_ST), F32),
            pltpu.VMEM((S5_ROWS, 2 * SSM_BLK_ST), F32),
            pltpu.VMEM((4, SUBLANE, SSM_BLK_ST), F32),
        ],
        compiler_params=_cparams(("parallel", "arbitrary")),
        name="s5_scan",
    )(proj3, proj3, wb, wc, tre, tim)
    return yf.reshape(T_TOK, SSM_WIDTH), yb.reshape(T_TOK, SSM_WIDTH)


def _gelu_tanh(x):
    return 0.5 * x * (1.0 + jnp.tanh(math.sqrt(2.0 / math.pi) * (x + 0.044715 * (x * x * x))))


def _glu_kernel(yf_ref, yb_ref, u_ref, d_ref, w_ref, gn_ref, o_ref):
    y = (yf_ref[...] + yb_ref[...]) + d_ref[0] * u_ref[...]
    g = _gelu_tanh(y)
    z = _bdot(g.astype(BF16), w_ref[0].astype(BF16))
    o_ref[...] = _rms(g * jax.nn.sigmoid(z), gn_ref[0]).astype(BF16)


def _glu(yf, yb, proj, d, w, gn, layer, tm):
    t = yf.shape[0]
    row = pl.BlockSpec((tm, SSM_WIDTH), lambda i: (i, 0))
    vec = pl.BlockSpec((1, 1, SSM_WIDTH), lambda i: (layer, 0, 0))
    return pl.pallas_call(
        _glu_kernel,
        grid=(t // tm,),
        in_specs=[row, row,
                  pl.BlockSpec((tm, SSM_WIDTH), lambda i: (i, U_COL0 // SSM_WIDTH)),
                  vec,
                  pl.BlockSpec((1, SSM_WIDTH, SSM_WIDTH), lambda i: (layer, 0, 0)),
                  vec],
        out_specs=row,
        out_shape=jax.ShapeDtypeStruct((t, SSM_WIDTH), BF16),
        compiler_params=_cparams(("parallel",)),
        name="s5_glu",
    )(yf, yb, proj, d.reshape(DEPTH, 1, SSM_WIDTH), w, gn.reshape(DEPTH, 1, SSM_WIDTH))


def _outproj_kernel(h_ref, a_ref, sn_ref, ga_ref, wa_ref, ws_ref, o_ref, an_ref):
    @pl.when(pl.program_id(1) == 0)
    def _():
        an_ref[...] = _rms(a_ref[...], ga_ref[0]).astype(BF16)

    o_ref[...] = (h_ref[...] + _bdot(an_ref[...], wa_ref[0].astype(BF16))
                  + _bdot(sn_ref[...], ws_ref[0].astype(BF16)))


def _outproj(h, att, ssm_n, ga, w, layer, tm, tn):
    t, d = h.shape
    return pl.pallas_call(
        _outproj_kernel,
        grid=(t // tm, d // tn),
        in_specs=[
            pl.BlockSpec((tm, tn), lambda i, j: (i, j)),
            pl.BlockSpec((tm, ATT_WIDTH), lambda i, j: (i, 0)),
            pl.BlockSpec((tm, SSM_WIDTH), lambda i, j: (i, 0)),
            pl.BlockSpec((1, 1, ATT_WIDTH), lambda i, j: (layer, 0, 0)),
            pl.BlockSpec((1, ATT_WIDTH, tn), lambda i, j: (layer, 0, j)),
            pl.BlockSpec((1, SSM_WIDTH, tn), lambda i, j: (layer, 1, j)),
        ],
        out_specs=pl.BlockSpec((tm, tn), lambda i, j: (i, j)),
        out_shape=jax.ShapeDtypeStruct((t, d), F32),
        scratch_shapes=[pltpu.VMEM((tm, ATT_WIDTH), BF16)],
        input_output_aliases={0: 0},
        compiler_params=_cparams(("parallel", "arbitrary")),
        name="out_proj",
    )(h, att, ssm_n, ga.reshape(DEPTH, 1, ATT_WIDTH), w, w)


def _ffn_kernel(x_ref, g_ref, wg_ref, wu_ref, wd_ref, gn_ref, o_ref, nxt_ref, hn_ref):
    j = pl.program_id(1)

    @pl.when(j == 0)
    def _():
        x = x_ref[...]
        hn_ref[...] = _rms(x, g_ref[0]).astype(BF16)
        o_ref[...] = x

    hn = hn_ref[...]
    a = _bdot(hn, wg_ref[0].astype(BF16))
    b = _bdot(hn, wu_ref[0].astype(BF16))
    hid = (a * jax.nn.sigmoid(a)) * b
    o_ref[...] += _bdot(hid.astype(BF16), wd_ref[0].astype(BF16))

    @pl.when(j == pl.num_programs(1) - 1)
    def _():
        nxt_ref[...] = _rms(o_ref[...], gn_ref[0]).astype(BF16)


def _ffn(h, g, gn, wg, wu, wd, layer, tm, tf):
    t, d = h.shape
    f = wg.shape[2]
    idx = layer // 2
    once = pl.Buffered(1)
    row = pl.BlockSpec((tm, d), lambda i, j: (i, 0), pipeline_mode=once)
    return pl.pallas_call(
        _ffn_kernel,
        grid=(t // tm, f // tf),
        in_specs=[
            row,
            pl.BlockSpec((1, 1, d), lambda i, j: (layer, 0, 0)),
            pl.BlockSpec((1, d, tf), lambda i, j: (idx, 0, j)),
            pl.BlockSpec((1, d, tf), lambda i, j: (idx, 0, j)),
            pl.BlockSpec((1, tf, d), lambda i, j: (idx, j, 0)),
            pl.BlockSpec((1, 1, d), lambda i, j: (layer + 1, 0, 0)),
        ],
        out_specs=[row, row],
        out_shape=[jax.ShapeDtypeStruct((t, d), F32), jax.ShapeDtypeStruct((t, d), BF16)],
        scratch_shapes=[pltpu.VMEM((tm, d), BF16)],
        input_output_aliases={0: 0},
        compiler_params=_cparams(("parallel", "arbitrary")),
        name="dense_ffn",
    )(h, g.reshape(DEPTH, 1, d), wg, wu, wd, gn.reshape(DEPTH, 1, d))


MOE_TM = 344
MOE_NT = (2 * T_TOK) // MOE_TM + N_EXPERTS
MOE_NP = MOE_NT * MOE_TM
DMA_FANOUT = 4


def _split_bf16(x):
    hi = x.astype(BF16)
    lo = (x - hi.astype(F32)).astype(BF16)
    return hi, lo


def _lane_pick(x, lane, k):
    return jnp.sum(jnp.where(lane == k, x, 0.0), axis=-1, keepdims=True)


ROW_SLABS = D_MODEL // LANE


def _row_slab(ref, c, rows):
    return ref.at[pl.ds(c, rows, stride=ROW_SLABS), :]


def _router_kernel(x_ref, g_ref, wr_ref, sel_ref, wts_ref, rows_ref):
    x = x_ref[...]
    for c in range(ROW_SLABS):
        _row_slab(rows_ref, c, x.shape[0])[...] = x[:, c * LANE:(c + 1) * LANE]
    hn = _rms(x, g_ref[0])
    xh, xl = _split_bf16(hn)
    wh, wl = _split_bf16(wr_ref[0])
    logits = _bdot(xh, wh) + (_bdot(xh, wl) + _bdot(xl, wh)) + _bdot(xl, wl)
    lane = lax.broadcasted_iota(jnp.int32, logits.shape, 1).astype(F32)
    neg = jnp.float32(-jnp.inf)
    logits = jnp.where(lane < N_EXPERTS, logits, neg)
    v1 = jnp.max(logits, axis=-1, keepdims=True)
    i1 = jnp.min(jnp.where(logits == v1, lane, float(LANE)), axis=-1, keepdims=True)
    rest = jnp.where(lane == i1, neg, logits)
    v2 = jnp.max(rest, axis=-1, keepdims=True)
    i2 = jnp.min(jnp.where(rest == v2, lane, float(LANE)), axis=-1, keepdims=True)
    e2 = jnp.exp(v2 - v1)
    w1 = 1.0 / (1.0 + e2)
    w2 = e2 / (1.0 + e2)
    sel_ref[...] = jnp.where((lane == i1) | (lane == i2), 1.0, 0.0)
    wts_ref[...] = jnp.where(lane == 0, w1, jnp.where(lane == 1, w2,
                             jnp.where(lane == 2, i1, jnp.where(lane == 3, i2, 0.0))))


def _router(h, g, wr_pad, layer, tm):
    t, d = h.shape
    idx = layer // 2
    row = pl.BlockSpec((tm, LANE), lambda i: (i, 0))
    tab = jax.ShapeDtypeStruct((t, LANE), F32)
    return pl.pallas_call(
        _router_kernel,
        grid=(t // tm,),
        in_specs=[
            pl.BlockSpec((tm, d), lambda i: (i, 0)),
            pl.BlockSpec((1, 1, d), lambda i: (layer, 0, 0)),
            pl.BlockSpec((1, d, LANE), lambda i: (idx, 0, 0)),
        ],
        out_specs=[row, row, pl.BlockSpec((tm * ROW_SLABS, LANE), lambda i: (i, 0))],
        out_shape=[tab, tab, jax.ShapeDtypeStruct((t * ROW_SLABS, LANE), F32)],
        compiler_params=_cparams(("parallel",)),
        name="moe_router",
    )(h, g.reshape(DEPTH, 1, d), wr_pad)


def _positions_kernel(sel_ref, wts_ref, pos_ref, meta_ref, cnt_ref, off_ref):
    p = pl.program_id(0)
    i = pl.program_id(1)
    tm = sel_ref.shape[0]
    lane = lax.broadcasted_iota(jnp.int32, (1, LANE), 1).astype(F32)
    sel = sel_ref[...]

    @pl.when((p == 0) & (i == 0))
    def _():
        cnt_ref[...] = jnp.zeros((1, LANE), F32)

    @pl.when(p == 0)
    def _():
        cnt_ref[...] += jnp.sum(sel, axis=0, keepdims=True)

    @pl.when((p == 1) & (i == 0))
    def _():
        cnt = cnt_ref[...]
        tiles = jnp.zeros((1, LANE), F32)
        for k in range(MOE_NT):
            tiles = tiles + jnp.where(cnt > float(k * MOE_TM), 1.0, 0.0)
        padded = tiles * float(MOE_TM)
        off = jnp.zeros((1, LANE), F32)
        for e in range(N_EXPERTS):
            off = off + jnp.where(lane > e, _lane_pick(padded, lane, e), 0.0)
        end = off + padded
        tile_start = lane * float(MOE_TM)
        owner = jnp.zeros((1, LANE), F32)
        for e in range(N_EXPERTS):
            owner = owner + jnp.where(tile_start >= _lane_pick(end, lane, e), 1.0, 0.0)
        owner = jnp.minimum(owner, float(N_EXPERTS - 1))
        used = jnp.sum(tiles, axis=-1, keepdims=True)
        off_ref[...] = off
        cnt_ref[...] = jnp.zeros((1, LANE), F32)
        row = lax.broadcasted_iota(jnp.int32, (SUBLANE, LANE), 0)
        meta = jnp.where(row == 0, owner, jnp.where(row == 1, used, jnp.where(row == 2, cnt, off)))
        meta_ref[...] = meta.astype(jnp.int32)

    @pl.when(p == 1)
    def _():
        r = lax.broadcasted_iota(jnp.int32, (tm, tm), 0)
        c = lax.broadcasted_iota(jnp.int32, (tm, tm), 1)
        earlier = jnp.where(c < r, 1.0, 0.0).astype(BF16)
        rank = _bdot(earlier, sel.astype(BF16)) + cnt_ref[...]
        slot = off_ref[...] + rank
        wts = wts_ref[...]
        lane_t = lax.broadcasted_iota(jnp.int32, (tm, LANE), 1).astype(F32)
        p1 = _lane_pick(slot, lane_t, _lane_pick(wts, lane_t, 2))
        p2 = _lane_pick(slot, lane_t, _lane_pick(wts, lane_t, 3))
        pos_ref[...] = jnp.where(lane_t == 0, p1, jnp.where(lane_t == 1, p2, 0.0)).astype(jnp.int32)
        cnt_ref[...] += jnp.sum(sel, axis=0, keepdims=True)


def _positions(sel, wts, tm):
    t = sel.shape[0]
    row = pl.BlockSpec((tm, LANE), lambda p, i: (i, 0))
    return pl.pallas_call(
        _positions_kernel,
        grid=(2, t // tm),
        in_specs=[row, row],
        out_specs=[
            pl.BlockSpec((tm, LANE), lambda p, i: (i * p, 0)),
            pl.BlockSpec((SUBLANE, LANE), lambda p, i: (0, 0)),
        ],
        out_shape=[jax.ShapeDtypeStruct((t, LANE), jnp.int32),
                   jax.ShapeDtypeStruct((SUBLANE, LANE), jnp.int32)],
        scratch_shapes=[pltpu.VMEM((1, LANE), F32), pltpu.VMEM((1, LANE), F32)],
        compiler_params=_cparams(("arbitrary", "arbitrary")),
        name="moe_positions",
    )(sel, wts)


def _gather_kernel(pos1_ref, pos2_ref, used_ref, rows_ref, xs_ref, src_ref, sem):
    i = pl.program_id(0)

    @pl.when(i == 0)
    def _():
        def clear(p, c):
            src_ref[p] = 0
            return c

        lax.fori_loop(0, MOE_NP, clear, 0, unroll=8)

        def fill(t, c):
            src_ref[pos1_ref[t]] = t
            src_ref[pos2_ref[t]] = t
            return c

        lax.fori_loop(0, T_TOK, fill, 0, unroll=4)

    def row_copy(tok, r):
        return pltpu.make_async_copy(rows_ref.at[pl.ds(tok * ROW_SLABS, ROW_SLABS), :],
                                     xs_ref.at[pl.ds(r * ROW_SLABS, ROW_SLABS), :], sem)

    @pl.when(i < used_ref[0])
    def _():
        def issue(q, c):
            for u in range(DMA_FANOUT):
                r = q * DMA_FANOUT + u
                row_copy(src_ref[i * MOE_TM + r], r).start(priority=u % 2)
            return c

        lax.fori_loop(0, MOE_TM // DMA_FANOUT, issue, 0)

        def land(r, c):
            row_copy(0, 0).wait()
            return c

        lax.fori_loop(0, MOE_TM, land, 0, unroll=4)

    @pl.when(i >= used_ref[0])
    def _():
        xs_ref[...] = jnp.zeros(xs_ref.shape, F32)


def _gather(rows, pos1, pos2, used):
    blk = MOE_TM * ROW_SLABS
    return pl.pallas_call(
        _gather_kernel,
        grid_spec=pltpu.PrefetchScalarGridSpec(
            num_scalar_prefetch=3,
            grid=(MOE_NT,),
            in_specs=[pl.BlockSpec(memory_space=pl.ANY)],
            out_specs=pl.BlockSpec((blk, LANE), lambda i, a, b, u: (i, 0)),
            scratch_shapes=[pltpu.SMEM((MOE_NP,), jnp.int32), pltpu.SemaphoreType.DMA(())],
        ),
        out_shape=jax.ShapeDtypeStruct((MOE_NP * ROW_SLABS, LANE), F32),
        compiler_params=_cparams(("arbitrary",)),
        name="moe_gather",
    )(pos1, pos2, used, rows)


def _experts_kernel(owner_ref, used_ref, x_ref, g_ref, wg_ref, wu_ref, wd_ref, o_ref):
    live = pl.program_id(0) < used_ref[0]

    @pl.when(live)
    def _():
        slabs = [_row_slab(x_ref, c, MOE_TM)[...] for c in range(ROW_SLABS)]
        ssq = slabs[0] * slabs[0]
        for c in range(1, ROW_SLABS):
            ssq = ssq + slabs[c] * slabs[c]
        inv = lax.rsqrt(jnp.sum(ssq, axis=-1, keepdims=True) / D_MODEL + EPS)
        g = g_ref[0]
        hn = jnp.concatenate([slabs[c] * inv * g[:, c * LANE:(c + 1) * LANE]
                              for c in range(ROW_SLABS)], axis=1).astype(BF16)
        a = _bdot(hn, wg_ref[0, 0])
        b = _bdot(hn, wu_ref[0, 0])
        hid = ((a * jax.nn.sigmoid(a)) * b).astype(BF16)
        res = _bdot(hid, wd_ref[0, 0])
        for c in range(ROW_SLABS):
            _row_slab(o_ref, c, MOE_TM)[...] = res[:, c * LANE:(c + 1) * LANE]

    @pl.when(jnp.logical_not(live))
    def _():
        o_ref[...] = jnp.zeros(o_ref.shape, F32)


def _experts(xs, owner, used, g, wg, wu, wd, layer):
    d = D_MODEL
    idx = layer // 2
    blk = MOE_TM * ROW_SLABS

    def tile(i, used_ref):
        return jnp.minimum(i, used_ref[0] - 1)

    w_map = lambda i, o, u: (idx, o[tile(i, u)], 0, 0)
    return pl.pallas_call(
        _experts_kernel,
        grid_spec=pltpu.PrefetchScalarGridSpec(
            num_scalar_prefetch=2,
            grid=(MOE_NT,),
            in_specs=[
                pl.BlockSpec((blk, LANE), lambda i, o, u: (tile(i, u), 0)),
                pl.BlockSpec((1, 1, d), lambda i, o, u: (layer, 0, 0)),
                pl.BlockSpec((1, 1, d, D_FF_EXPERT), w_map),
                pl.BlockSpec((1, 1, d, D_FF_EXPERT), w_map),
                pl.BlockSpec((1, 1, D_FF_EXPERT, d), w_map),
            ],
            out_specs=pl.BlockSpec((blk, LANE), lambda i, o, u: (i, 0)),
        ),
        out_shape=jax.ShapeDtypeStruct((MOE_NP * ROW_SLABS, LANE), F32),
        compiler_params=_cparams(("arbitrary",)),
        name="moe_experts",
    )(owner, used, xs, g.reshape(DEPTH, 1, d), wg, wu, wd)


def _combine_kernel(pos1_ref, pos2_ref, h_ref, wts_ref, ys_ref, gn_ref, o_ref, nxt_ref,
                    y1, y2, sem1, sem2):
    tm = h_ref.shape[0]
    base = pl.program_id(0) * tm

    def row_copy(slot, r, buf, sem):
        return pltpu.make_async_copy(ys_ref.at[pl.ds(slot * ROW_SLABS, ROW_SLABS), :],
                                     buf.at[pl.ds(r * ROW_SLABS, ROW_SLABS), :], sem)

    def fetch(q, c):
        for u in range(DMA_FANOUT):
            r = q * DMA_FANOUT + u
            row_copy(pos1_ref[base + r], r, y1, sem1).start(priority=0)
            row_copy(pos2_ref[base + r], r, y2, sem2).start(priority=1)
        return c

    lax.fori_loop(0, tm // DMA_FANOUT, fetch, 0)

    def land(r, c):
        row_copy(0, 0, y1, sem1).wait()
        row_copy(0, 0, y2, sem2).wait()
        return c

    lax.fori_loop(0, tm, land, 0, unroll=4)
    wts = wts_ref[...]
    lane = lax.broadcasted_iota(jnp.int32, wts.shape, 1).astype(F32)
    w1 = _lane_pick(wts, lane, 0)
    w2 = _lane_pick(wts, lane, 1)
    ssq = jnp.zeros((tm, 1), F32)
    for c in range(ROW_SLABS):
        cols = slice(c * LANE, (c + 1) * LANE)
        hc = h_ref[:, cols] + (w1 * _row_slab(y1, c, tm)[...] + w2 * _row_slab(y2, c, tm)[...])
        o_ref[:, cols] = hc
        ssq = ssq + jnp.sum(hc * hc, axis=-1, keepdims=True)
    normed = o_ref[...] * lax.rsqrt(ssq / D_MODEL + EPS) * gn_ref[...]
    if nxt_ref is None:
        o_ref[...] = normed
    else:
        nxt_ref[...] = normed.astype(BF16)


def _combine_mid_kernel(pos1_ref, pos2_ref, h_ref, wts_ref, ys_ref, gn_ref, o_ref, nxt_ref,
                        y1, y2, sem1, sem2):
    _combine_kernel(pos1_ref, pos2_ref, h_ref, wts_ref, ys_ref, gn_ref, o_ref, nxt_ref,
                    y1, y2, sem1, sem2)


def _combine_last_kernel(pos1_ref, pos2_ref, h_ref, wts_ref, ys_ref, gn_ref, o_ref,
                         y1, y2, sem1, sem2):
    _combine_kernel(pos1_ref, pos2_ref, h_ref, wts_ref, ys_ref, gn_ref, o_ref, None,
                    y1, y2, sem1, sem2)


def _combine(h, wts, ys, pos1, pos2, gn, last, tm):
    t, d = h.shape
    row = pl.BlockSpec((tm, d), lambda i, a, b: (i, 0))
    h_out = jax.ShapeDtypeStruct((t, d), F32)
    return pl.pallas_call(
        _combine_last_kernel if last else _combine_mid_kernel,
        grid_spec=pltpu.PrefetchScalarGridSpec(
            num_scalar_prefetch=2,
            grid=(t // tm,),
            in_specs=[
                row,
                pl.BlockSpec((tm, LANE), lambda i, a, b: (i, 0)),
                pl.BlockSpec(memory_space=pl.ANY),
                pl.BlockSpec((1, d), lambda i, a, b: (0, 0)),
            ],
            out_specs=row if last else [row, row],
            scratch_shapes=[
                pltpu.VMEM((tm * ROW_SLABS, LANE), F32),
                pltpu.VMEM((tm * ROW_SLABS, LANE), F32),
                pltpu.SemaphoreType.DMA(()),
                pltpu.SemaphoreType.DMA(()),
            ],
        ),
        out_shape=h_out if last else [h_out, jax.ShapeDtypeStruct((t, d), BF16)],
        input_output_aliases={2: 0},
        compiler_params=_cparams(("arbitrary",)),
        name="moe_combine",
    )(pos1, pos2, h, wts, ys, gn.reshape(1, d))


def _moe(h, g, gn, wr_pad, wg, wu, wd, layer, last):
    sel, wts, rows = _router(h, g, wr_pad, layer, tm=688)
    pos, meta = _positions(sel, wts, tm=688)
    pos1 = pos[:, 0]
    pos2 = pos[:, 1]
    owner = meta[0, :MOE_NT]
    used = meta[1, :1]
    xs = _gather(rows, pos1, pos2, used)
    ys = _experts(xs, owner, used, g, wg, wu, wd, layer)
    return _combine(h, wts, ys, pos1, pos2, gn, last, tm=688)


def _rope_partner(w):
    half = QK_ROPE // 2
    return jnp.concatenate([-w[..., half:], w[..., :half]], axis=-1)


def _pad_cols(w, width):
    return jnp.pad(w, [(0, 0)] * (w.ndim - 1) + [(0, width - w.shape[-1])])


def _in_proj_layout(w_in):
    kr = w_in[..., Q_LORA + KV_LORA:Q_LORA + KV_LORA + QK_ROPE]
    return jnp.concatenate([
        w_in[..., :Q_LORA + KV_LORA],
        _pad_cols(kr, LANE),
        _pad_cols(_rope_partner(kr), LANE),
        w_in[..., Q_LORA + KV_LORA + QK_ROPE:]], axis=-1)


def _q_layouts(w_uq):
    w = w_uq.reshape(DEPTH, Q_LORA, ATT_HEADS, QK_NOPE + QK_ROPE)
    main = _pad_cols(w, HEAD_PAD).reshape(DEPTH, Q_LORA, ATT_HEADS * HEAD_PAD)
    rot = _pad_cols(_rope_partner(w[..., QK_NOPE:]), LANE).reshape(DEPTH, Q_LORA, ATT_HEADS * LANE)
    return main, rot


def _rope_tables():
    inv = ROPE_THETA ** (-jnp.arange(0, QK_ROPE, 2, dtype=F32) / QK_ROPE)
    ang = jnp.arange(L_TOK, dtype=F32)[:, None] * inv[None, :]
    cos = jnp.cos(ang)
    sin = jnp.sin(ang)
    cos_t = _pad_cols(jnp.concatenate([cos, cos], axis=-1), LANE)
    sin_t = _pad_cols(jnp.concatenate([sin, sin], axis=-1), LANE)
    return cos_t, sin_t


def kernel(x, meta_tokens, mix_norm, w_in, q_norm, w_uq, kv_norm, w_ukv, ssm_lambda_re, ssm_lambda_im, ssm_log_step, ssm_b_re, ssm_b_im, ssm_c_re, ssm_c_im, ssm_d, ssm_w_glu, attn_out_norm, ssm_out_norm, w_out, ffn_norm, dense_w_gate, dense_w_up, dense_w_down, moe_router, moe_w_gate, moe_w_up, moe_w_down, final_norm):
    meta = jnp.broadcast_to(meta_tokens[None].astype(x.dtype), (BATCH, N_META, D_MODEL))
    h = jnp.concatenate([meta, x], axis=1).reshape(T_TOK, D_MODEL)
    cos_t, sin_t = _rope_tables()
    w_in_l = _in_proj_layout(w_in)
    wq_main, wq_rot = _q_layouts(w_uq)
    wr_pad = _pad_cols(moe_router, LANE)
    moe_w_gate, moe_w_up, moe_w_down = (w.astype(BF16) for w in (moe_w_gate, moe_w_up, moe_w_down))
    wb, wc, tre, tim = _s5_layouts(ssm_lambda_re, ssm_lambda_im, ssm_log_step,
                                   ssm_b_re, ssm_b_im, ssm_c_re, ssm_c_im)
    hn = _rms_cast(h, mix_norm, 0, tm=688)
    for layer in range(DEPTH):
        proj = _in_proj(hn, w_in_l, layer, tm=2064, tn=512)
        q, kv, kpe = _upproj(proj, q_norm, kv_norm, wq_main, wq_rot, w_ukv, cos_t, sin_t, layer,
                             tm=688)
        att = _attention(q, kv, kpe, tq=688)
        yf, yb = _s5_scan(proj, wb, wc, tre, tim, layer)
        ssm_n = _glu(yf, yb, proj, ssm_d, ssm_w_glu, ssm_out_norm, layer, tm=688)
        h = _outproj(h, att, ssm_n, attn_out_norm, w_out, layer, tm=1376, tn=512)
        last = layer == DEPTH - 1
        if layer % 2 == 0:
            h, hn = _ffn(h, ffn_norm, mix_norm, dense_w_gate, dense_w_up, dense_w_down, layer,
                         tm=1376, tf=256)
        elif last:
            h = _moe(h, ffn_norm, final_norm, wr_pad, moe_w_gate, moe_w_up, moe_w_down, layer, last)
        else:
            h, hn = _moe(h, ffn_norm, mix_norm[layer + 1], wr_pad, moe_w_gate, moe_w_up, moe_w_down,
                         layer, last)
    return h.reshape(BATCH, L_TOK, D_MODEL)[:, N_META:]
```

```python
import functools
import math

import jax
import jax.numpy as jnp
from jax import lax
from jax.experimental import pallas as pl
from jax.experimental.pallas import tpu as pltpu

F32 = jnp.float32
BF16 = jnp.bfloat16

D_MODEL = 2048
BATCH = 4
SEQ = 2048
DEPTH = 4
N_META = 16
EPS = 1e-6
ATT_HEADS = 8
QK_NOPE = 128
QK_ROPE = 64
V_DIM = 128
Q_LORA = 512
KV_LORA = 256
ROPE_THETA = 10000.0
ATT_WIDTH = ATT_HEADS * V_DIM
SSM_GROUP = 16
SSM_WIDTH = D_MODEL - ATT_WIDTH
SSM_GROUPS = SSM_WIDTH // SSM_GROUP
SSM_STATE = 64
D_FF = 5632
N_EXPERTS = 8
D_FF_EXPERT = 1408

L_TOK = N_META + SEQ
T_TOK = L_TOK * BATCH
LANE = 128
SUBLANE = 8
HEAD_PAD = 256
L_PAD = 2176
PROJ_COLS = 2048
U_COL0 = 1024
VMEM_LIMIT = 58 * 1024 * 1024

SSM_BLK_GROUPS = 8
SSM_BLKS = SSM_GROUPS // SSM_BLK_GROUPS
SSM_BLK_CH = SSM_BLK_GROUPS * SSM_GROUP
SSM_BLK_ST = SSM_BLK_GROUPS * SSM_STATE
S5_CHUNKS = 3
S5_STEPS = L_TOK // S5_CHUNKS
S5_ROWS = S5_STEPS * BATCH


def _cparams(sem):
    return pltpu.CompilerParams(dimension_semantics=sem, vmem_limit_bytes=VMEM_LIMIT)


def _rms(x, g):
    ms = jnp.mean(x * x, axis=-1, keepdims=True)
    return x * lax.rsqrt(ms + EPS) * g


def _bdot(a, b):
    return jnp.dot(a, b, preferred_element_type=F32)


def _rms_cast_kernel(x_ref, g_ref, o_ref):
    o_ref[...] = _rms(x_ref[...], g_ref[0]).astype(BF16)


def _rms_cast(x, g, layer, tm):
    t, d = x.shape
    return pl.pallas_call(
        _rms_cast_kernel,
        grid=(t // tm,),
        in_specs=[pl.BlockSpec((tm, d), lambda i: (i, 0)),
                  pl.BlockSpec((1, 1, d), lambda i: (layer, 0, 0))],
        out_specs=pl.BlockSpec((tm, d), lambda i: (i, 0)),
        out_shape=jax.ShapeDtypeStruct((t, d), BF16),
        compiler_params=_cparams(("parallel",)),
        name="rms_cast",
    )(x, g.reshape(DEPTH, 1, d))


def _in_proj_kernel(x_ref, w_ref, o_ref):
    o_ref[...] = _bdot(x_ref[...], w_ref[0].astype(BF16))


def _in_proj(hn, w, layer, tm, tn):
    t, d = hn.shape
    n = w.shape[2]
    return pl.pallas_call(
        _in_proj_kernel,
        grid=(t // tm, n // tn),
        in_specs=[
            pl.BlockSpec((tm, d), lambda i, j: (i, 0)),
            pl.BlockSpec((1, d, tn), lambda i, j: (layer, 0, j)),
        ],
        out_specs=pl.BlockSpec((tm, tn), lambda i, j: (i, j)),
        out_shape=jax.ShapeDtypeStruct((t, n), F32),
        compiler_params=_cparams(("parallel", "arbitrary")),
        name="in_proj",
    )(hn, w)


def _upproj_kernel(cq_ref, ckv_ref, kr_ref, krot_ref, qn_ref, kvn_ref, wq_ref, wqr_ref,
                   wkv_ref, cos_ref, sin_ref, q_ref, kv_ref, kpe_ref):
    scale = (QK_NOPE + QK_ROPE) ** -0.5 * math.log2(math.e)
    cqn = _rms(cq_ref[...], qn_ref[0]).astype(BF16)
    a = _bdot(cqn, wq_ref[0].astype(BF16))
    r = _bdot(cqn, wqr_ref[0].astype(BF16))
    c = cos_ref[...]
    s = sin_ref[...]
    last_lane = lax.broadcasted_iota(jnp.int32, c.shape, 1) == LANE - 1
    for h in range(ATT_HEADS):
        lo = h * HEAD_PAD
        q_ref[:, lo:lo + LANE] = (a[:, lo:lo + LANE] * scale).astype(BF16)
        rope = a[:, lo + LANE:lo + HEAD_PAD] * c + r[:, h * LANE:(h + 1) * LANE] * s
        q_ref[:, lo + LANE:lo + HEAD_PAD] = jnp.where(last_lane, 1.0, rope * scale).astype(BF16)
    ckvn = _rms(ckv_ref[...], kvn_ref[0]).astype(BF16)
    kv_ref[...] = _bdot(ckvn, wkv_ref[0].astype(BF16)).astype(BF16)
    kpe_ref[...] = (kr_ref[...] * c + krot_ref[...] * s).astype(BF16)


def _upproj(proj, qn, kvn, wq, wqr, wkv, cos_t, sin_t, layer, tm):
    t = proj.shape[0]
    full = lambda a: pl.BlockSpec((1,) + a.shape[1:], lambda i: (layer, 0, 0))
    qn = qn.reshape(DEPTH, 1, Q_LORA)
    kvn = kvn.reshape(DEPTH, 1, KV_LORA)
    pos_blocks = L_TOK // tm
    return pl.pallas_call(
        _upproj_kernel,
        grid=(t // tm,),
        in_specs=[
            pl.BlockSpec((tm, Q_LORA), lambda i: (i, 0)),
            pl.BlockSpec((tm, KV_LORA), lambda i: (i, Q_LORA // KV_LORA)),
            pl.BlockSpec((tm, LANE), lambda i: (i, (Q_LORA + KV_LORA) // LANE)),
            pl.BlockSpec((tm, LANE), lambda i: (i, (Q_LORA + KV_LORA) // LANE + 1)),
            full(qn),
            full(kvn),
            full(wq),
            full(wqr),
            full(wkv),
            pl.BlockSpec((tm, LANE), lambda i: (i % pos_blocks, 0)),
            pl.BlockSpec((tm, LANE), lambda i: (i % pos_blocks, 0)),
        ],
        out_specs=[
            pl.BlockSpec((tm, ATT_HEADS * HEAD_PAD), lambda i: (i, 0)),
            pl.BlockSpec((tm, ATT_HEADS * (QK_NOPE + V_DIM)), lambda i: (i, 0)),
            pl.BlockSpec((tm, LANE), lambda i: (i, 0)),
        ],
        out_shape=[
            jax.ShapeDtypeStruct((t, ATT_HEADS * HEAD_PAD), BF16),
            jax.ShapeDtypeStruct((t, ATT_HEADS * (QK_NOPE + V_DIM)), BF16),
            jax.ShapeDtypeStruct((t, LANE), BF16),
        ],
        compiler_params=_cparams(("parallel",)),
        name="upproj",
    )(proj, proj, proj, proj, qn, kvn, wq, wqr, wkv, cos_t, sin_t)


ATT_HPS = 2
PAD_BIAS = -1e30


def _attn_kernel(q_ref, kv_ref, kpe_ref, o_ref, k_sc, v_sc):
    @pl.when(pl.program_id(2) == 0)
    def _():
        n_pad = L_PAD - L_TOK
        pad_lane = lax.broadcasted_iota(jnp.int32, (n_pad, LANE), 1)
        k_pad = jnp.where(pad_lane == LANE - 1, PAD_BIAS, 0.0).astype(BF16)
        row_lane = lax.broadcasted_iota(jnp.int32, (L_TOK, LANE), 1)
        ones_col = jnp.where(row_lane == 0, 1.0, 0.0).astype(BF16)
        for hh in range(ATT_HPS):
            lo = hh * HEAD_PAD
            k_sc[hh, 0:L_TOK, 0:LANE] = kv_ref[:, lo:lo + LANE]
            k_sc[hh, 0:L_TOK, LANE:HEAD_PAD] = kpe_ref[...]
            k_sc[hh, L_TOK:L_PAD, 0:LANE] = jnp.zeros((n_pad, LANE), BF16)
            k_sc[hh, L_TOK:L_PAD, LANE:HEAD_PAD] = k_pad
            v_sc[hh, 0:L_TOK, 0:V_DIM] = kv_ref[:, lo + LANE:lo + HEAD_PAD]
            v_sc[hh, 0:L_TOK, V_DIM:HEAD_PAD] = ones_col
            v_sc[hh, L_TOK:L_PAD, :] = jnp.zeros((n_pad, HEAD_PAD), BF16)

    for hh in range(ATT_HPS):
        q = q_ref[:, hh * HEAD_PAD:(hh + 1) * HEAD_PAD]
        s = lax.dot_general(q, k_sc[hh], (((1,), (1,)), ((), ())),
                            preferred_element_type=F32)
        m = jnp.max(s, axis=-1, keepdims=True)
        p = jnp.exp2(s - m).astype(BF16)
        o = _bdot(p, v_sc[hh])
        o_ref[:, hh * V_DIM:(hh + 1) * V_DIM] = o[:, :V_DIM] / o[:, V_DIM:V_DIM + 1]


def _attention(q, kv, kpe, tq):
    nq = L_TOK // tq
    hps = ATT_HPS
    return pl.pallas_call(
        _attn_kernel,
        grid=(BATCH, ATT_HEADS // hps, nq),
        in_specs=[
            pl.BlockSpec((tq, hps * HEAD_PAD), lambda b, h, i: (b * nq + i, h)),
            pl.BlockSpec((L_TOK, hps * HEAD_PAD), lambda b, h, i: (b, h)),
            pl.BlockSpec((L_TOK, LANE), lambda b, h, i: (b, 0)),
        ],
        out_specs=pl.BlockSpec((tq, hps * V_DIM), lambda b, h, i: (b * nq + i, h)),
        out_shape=jax.ShapeDtypeStruct((T_TOK, ATT_WIDTH), F32),
        scratch_shapes=[pltpu.VMEM((hps, L_PAD, HEAD_PAD), BF16),
                        pltpu.VMEM((hps, L_PAD, HEAD_PAD), BF16)],
        compiler_params=_cparams(("parallel", "parallel", "arbitrary")),
        name="attention",
    )(q, kv, kpe)


def _s5_param_kernel(lre_ref, lim_ref, ls_ref, bre_ref, bim_ref,
                     lbr_o, lbi_o, l2r_o, l2i_o, bbr_o, bbi_o, lbbr_o, lbbi_o):
    lre = lre_ref[...]
    lim = lim_ref[...]
    dt = jnp.exp(ls_ref[...])
    mag = jnp.exp(lre * dt)
    ang = lim * dt
    br = mag * jnp.cos(ang)
    bi = mag * jnp.sin(ang)
    nr = br - 1.0
    den = lre * lre + lim * lim
    cr = (nr * lre + bi * lim) / den
    ci = (bi * lre - nr * lim) / den
    b_r = bre_ref[...]
    b_i = bim_ref[...]
    bbr = cr * b_r - ci * b_i
    bbi = cr * b_i + ci * b_r
    lbr_o[...] = br
    lbi_o[...] = bi
    l2r_o[...] = br * br - bi * bi
    l2i_o[...] = 2.0 * br * bi
    bbr_o[...] = bbr
    bbi_o[...] = bbi
    lbbr_o[...] = br * bbr - bi * bbi
    lbbi_o[...] = br * bbi + bi * bbr


def _s5_params(lam_re, lam_im, log_step, b_re, b_im):
    n = DEPTH * 2 * SSM_GROUPS
    lre = lam_re.reshape(n, 1, SSM_STATE)
    lim = lam_im.reshape(n, 1, SSM_STATE)
    ls = log_step.reshape(n, 1, 1)
    btr = b_re.transpose(0, 1, 2, 4, 3).reshape(n, SSM_GROUP, SSM_STATE)
    bti = b_im.transpose(0, 1, 2, 4, 3).reshape(n, SSM_GROUP, SSM_STATE)
    small = jax.ShapeDtypeStruct((n, 1, SSM_STATE), F32)
    big = jax.ShapeDtypeStruct((n, SSM_GROUP, SSM_STATE), F32)
    return pl.pallas_call(
        _s5_param_kernel,
        out_shape=[small, small, small, small, big, big, big, big],
        compiler_params=pltpu.CompilerParams(vmem_limit_bytes=VMEM_LIMIT),
        name="s5_params",
    )(lre, lim, ls, btr, bti)


def _s5_layouts(lam_re, lam_im, log_step, b_re, b_im, c_re, c_im):
    lbr, lbi, l2r, l2i, bbr, bbi, lbbr, lbbi = _s5_params(lam_re, lam_im, log_step, b_re, b_im)
    eye = jnp.eye(SSM_BLK_GROUPS, dtype=F32)
    lead = (DEPTH, 2, SSM_BLKS, SSM_BLK_GROUPS)

    def in_block(x):
        x = x.reshape(lead + (SSM_GROUP, SSM_STATE))
        return jnp.einsum('ldbgcp,gh->ldbgchp', x, eye).reshape(
            DEPTH, 2, SSM_BLKS, SSM_BLK_CH, SSM_BLK_ST)

    def out_block(x):
        x = x.reshape(lead + (SSM_GROUP, SSM_STATE))
        return jnp.einsum('ldbgcp,gh->ldbgphc', x, eye).reshape(
            DEPTH, 2, SSM_BLKS, SSM_BLK_ST, SSM_BLK_CH)

    wb = jnp.concatenate([
        jnp.concatenate([in_block(bbr), in_block(bbi)], axis=-1),
        jnp.concatenate([in_block(lbbr), in_block(lbbi)], axis=-1)], axis=-2).astype(BF16)
    wc = jnp.concatenate([out_block(c_re), out_block(-c_im)], axis=-2).astype(BF16)

    def table(one, two):
        one = one.reshape(DEPTH, 2, SSM_BLKS, 1, SSM_BLK_ST)
        two = two.reshape(DEPTH, 2, SSM_BLKS, 1, SSM_BLK_ST)
        half = SUBLANE // 2
        fwd = jnp.concatenate([jnp.broadcast_to(one[:, 0:1], (DEPTH, 1, SSM_BLKS, half, SSM_BLK_ST)),
                               jnp.broadcast_to(two[:, 0:1], (DEPTH, 1, SSM_BLKS, half, SSM_BLK_ST))], axis=3)
        bwd = jnp.concatenate([jnp.broadcast_to(two[:, 1:2], (DEPTH, 1, SSM_BLKS, half, SSM_BLK_ST)),
                               jnp.broadcast_to(one[:, 1:2], (DEPTH, 1, SSM_BLKS, half, SSM_BLK_ST))], axis=3)
        return jnp.concatenate([fwd, bwd], axis=1)

    return wb, wc, table(lbr, l2r), table(lbi, l2i)


def _s5_scan_kernel(uf_ref, ub_ref, wb_ref, wc_ref, tre_ref, tim_ref, yf_ref, yb_ref,
                    tmaj, xf, xb, carry):
    @pl.when(pl.program_id(1) == 0)
    def _():
        carry[...] = jnp.zeros(carry.shape, F32)

    n_tiles = S5_ROWS // SUBLANE
    half = SUBLANE // 2

    def time_major(u_ref):
        for b in range(BATCH):
            tmaj[pl.ds(b, S5_STEPS, stride=BATCH), :] = u_ref[b]
        return tmaj[...]

    def batch_major(y, y_ref):
        tmaj[...] = y
        for b in range(BATCH):
            y_ref[b] = tmaj[pl.ds(b, S5_STEPS, stride=BATCH), :]

    def paired_lhs(u, take_upper):
        u3 = u.reshape(n_tiles, SUBLANE, SSM_BLK_CH)
        swapped = pltpu.roll(u3, half, axis=1)
        sub = lax.broadcasted_iota(jnp.int32, u3.shape, 1)
        keep = (sub >= half) if take_upper else (sub < half)
        nb = jnp.where(keep, swapped, 0.0).reshape(S5_ROWS, SSM_BLK_CH)
        return jnp.concatenate([u, nb], axis=1).astype(BF16)

    xf[...] = _bdot(paired_lhs(time_major(uf_ref), True), wb_ref[0, 0, 0])
    xb[...] = _bdot(paired_lhs(time_major(ub_ref), False), wb_ref[0, 1, 0])

    lower = lax.broadcasted_iota(jnp.int32, (SUBLANE, 2 * LANE), 0) < half
    for hh in range(SSM_BLK_ST // (2 * LANE)):
        lo = hh * 2 * LANE
        re = slice(lo, lo + 2 * LANE)
        im = slice(SSM_BLK_ST + lo, SSM_BLK_ST + lo + 2 * LANE)
        afr, afi = tre_ref[0, 0, 0, :, re], tim_ref[0, 0, 0, :, re]
        abr, abi = tre_ref[0, 1, 0, :, re], tim_ref[0, 1, 0, :, re]

        def body(k, st):
            hfr, hfi, hbr, hbi = st
            rf = pl.multiple_of(k * SUBLANE, SUBLANE)
            pr = jnp.where(lower, pltpu.roll(hfr, half, axis=0), hfr)
            pi = jnp.where(lower, pltpu.roll(hfi, half, axis=0), hfi)
            nfr = xf[pl.ds(rf, SUBLANE), re] + (afr * pr - afi * pi)
            nfi = xf[pl.ds(rf, SUBLANE), im] + (afr * pi + afi * pr)
            xf[pl.ds(rf, SUBLANE), re] = nfr
            xf[pl.ds(rf, SUBLANE), im] = nfi
            rb = pl.multiple_of((n_tiles - 1 - k) * SUBLANE, SUBLANE)
            qr = jnp.where(lower, hbr, pltpu.roll(hbr, half, axis=0))
            qi = jnp.where(lower, hbi, pltpu.roll(hbi, half, axis=0))
            nbr = xb[pl.ds(rb, SUBLANE), re] + (abr * qr - abi * qi)
            nbi = xb[pl.ds(rb, SUBLANE), im] + (abr * qi + abi * qr)
            xb[pl.ds(rb, SUBLANE), re] = nbr
            xb[pl.ds(rb, SUBLANE), im] = nbi
            return nfr, nfi, nbr, nbi

        init = (carry[0, :, re], carry[1, :, re], carry[2, :, re], carry[3, :, re])
        fin = lax.fori_loop(0, n_tiles, body, init, unroll=2)
        for idx in range(4):
            carry[idx, :, re] = fin[idx]

    batch_major(_bdot(xf[...].astype(BF16), wc_ref[0, 0, 0]), yf_ref)
    batch_major(_bdot(xb[...].astype(BF16), wc_ref[0, 1, 0]), yb_ref)


def _s5_scan(proj, wb, wc, tre, tim, layer):
    last = S5_CHUNKS - 1
    u_blk0 = U_COL0 // SSM_BLK_CH
    proj3 = proj.reshape(BATCH, L_TOK, PROJ_COLS)
    y_shape = jax.ShapeDtypeStruct((BATCH, L_TOK, SSM_WIDTH), F32)
    yf, yb = pl.pallas_call(
        _s5_scan_kernel,
        grid=(SSM_BLKS, S5_CHUNKS),
        in_specs=[
            pl.BlockSpec((BATCH, S5_STEPS, SSM_BLK_CH), lambda j, c: (0, c, u_blk0 + j)),
            pl.BlockSpec((BATCH, S5_STEPS, SSM_BLK_CH), lambda j, c: (0, last - c, u_blk0 + j)),
            pl.BlockSpec((1, 2, 1, 2 * SSM_BLK_CH, 2 * SSM_BLK_ST), lambda j, c: (layer, 0, j, 0, 0)),
            pl.BlockSpec((1, 2, 1, 2 * SSM_BLK_ST, SSM_BLK_CH), lambda j, c: (layer, 0, j, 0, 0)),
            pl.BlockSpec((1, 2, 1, SUBLANE, SSM_BLK_ST), lambda j, c: (layer, 0, j, 0, 0)),
            pl.BlockSpec((1, 2, 1, SUBLANE, SSM_BLK_ST), lambda j, c: (layer, 0, j, 0, 0)),
        ],
        out_specs=[
            pl.BlockSpec((BATCH, S5_STEPS, SSM_BLK_CH), lambda j, c: (0, c, j)),
            pl.BlockSpec((BATCH, S5_STEPS, SSM_BLK_CH), lambda j, c: (0, last - c, j)),
        ],
        out_shape=[y_shape, y_shape],
        scratch_shapes=[
            pltpu.VMEM((S5_ROWS, SSM_BLK_CH), F32),
            pltpu.VMEM((S5_ROWS, 2 * SSM_BLK_ST), F32),
            pltpu.VMEM((S5_ROWS, 2 * SSM_BLK_ST), F32),
            pltpu.VMEM((4, SUBLANE, SSM_BLK_ST), F32),
        ],
        compiler_params=_cparams(("parallel", "arbitrary")),
        name="s5_scan",
    )(proj3, proj3, wb, wc, tre, tim)
    return yf.reshape(T_TOK, SSM_WIDTH), yb.reshape(T_TOK, SSM_WIDTH)


def _gelu_tanh(x):
    return 0.5 * x * (1.0 + jnp.tanh(math.sqrt(2.0 / math.pi) * (x + 0.044715 * (x * x * x))))


def _glu_kernel(yf_ref, yb_ref, u_ref, d_ref, w_ref, gn_ref, o_ref):
    y = (yf_ref[...] + yb_ref[...]) + d_ref[0] * u_ref[...]
    g = _gelu_tanh(y)
    z = _bdot(g.astype(BF16), w_ref[0].astype(BF16))
    o_ref[...] = _rms(g * jax.nn.sigmoid(z), gn_ref[0]).astype(BF16)


def _glu(yf, yb, proj, d, w, gn, layer, tm):
    t = yf.shape[0]
    row = pl.BlockSpec((tm, SSM_WIDTH), lambda i: (i, 0))
    vec = pl.BlockSpec((1, 1, SSM_WIDTH), lambda i: (layer, 0, 0))
    return pl.pallas_call(
        _glu_kernel,
        grid=(t // tm,),
        in_specs=[row, row,
                  pl.BlockSpec((tm, SSM_WIDTH), lambda i: (i, U_COL0 // SSM_WIDTH)),
                  vec,
                  pl.BlockSpec((1, SSM_WIDTH, SSM_WIDTH), lambda i: (layer, 0, 0)),
                  vec],
        out_specs=row,
        out_shape=jax.ShapeDtypeStruct((t, SSM_WIDTH), BF16),
        compiler_params=_cparams(("parallel",)),
        name="s5_glu",
    )(yf, yb, proj, d.reshape(DEPTH, 1, SSM_WIDTH), w, gn.reshape(DEPTH, 1, SSM_WIDTH))


def _outproj_kernel(h_ref, a_ref, sn_ref, ga_ref, wa_ref, ws_ref, o_ref, an_ref):
    @pl.when(pl.program_id(1) == 0)
    def _():
        an_ref[...] = _rms(a_ref[...], ga_ref[0]).astype(BF16)

    o_ref[...] = (h_ref[...] + _bdot(an_ref[...], wa_ref[0].astype(BF16))
                  + _bdot(sn_ref[...], ws_ref[0].astype(BF16)))


def _outproj(h, att, ssm_n, ga, w, layer, tm, tn):
    t, d = h.shape
    return pl.pallas_call(
        _outproj_kernel,
        grid=(t // tm, d // tn),
        in_specs=[
            pl.BlockSpec((tm, tn), lambda i, j: (i, j)),
            pl.BlockSpec((tm, ATT_WIDTH), lambda i, j: (i, 0)),
            pl.BlockSpec((tm, SSM_WIDTH), lambda i, j: (i, 0)),
            pl.BlockSpec((1, 1, ATT_WIDTH), lambda i, j: (layer, 0, 0)),
            pl.BlockSpec((1, ATT_WIDTH, tn), lambda i, j: (layer, 0, j)),
            pl.BlockSpec((1, SSM_WIDTH, tn), lambda i, j: (layer, 1, j)),
        ],
        out_specs=pl.BlockSpec((tm, tn), lambda i, j: (i, j)),
        out_shape=jax.ShapeDtypeStruct((t, d), F32),
        scratch_shapes=[pltpu.VMEM((tm, ATT_WIDTH), BF16)],
        input_output_aliases={0: 0},
        compiler_params=_cparams(("parallel", "arbitrary")),
        name="out_proj",
    )(h, att, ssm_n, ga.reshape(DEPTH, 1, ATT_WIDTH), w, w)


def _ffn_kernel(x_ref, g_ref, wg_ref, wu_ref, wd_ref, gn_ref, o_ref, nxt_ref, hn_ref):
    j = pl.program_id(1)

    @pl.when(j == 0)
    def _():
        x = x_ref[...]
        hn_ref[...] = _rms(x, g_ref[0]).astype(BF16)
        o_ref[...] = x

    hn = hn_ref[...]
    a = _bdot(hn, wg_ref[0].astype(BF16))
    b = _bdot(hn, wu_ref[0].astype(BF16))
    hid = (a * jax.nn.sigmoid(a)) * b
    o_ref[...] += _bdot(hid.astype(BF16), wd_ref[0].astype(BF16))

    @pl.when(j == pl.num_programs(1) - 1)
    def _():
        nxt_ref[...] = _rms(o_ref[...], gn_ref[0]).astype(BF16)


def _ffn(h, g, gn, wg, wu, wd, layer, tm, tf):
    t, d = h.shape
    f = wg.shape[2]
    idx = layer // 2
    once = pl.Buffered(1)
    row = pl.BlockSpec((tm, d), lambda i, j: (i, 0), pipeline_mode=once)
    return pl.pallas_call(
        _ffn_kernel,
        grid=(t // tm, f // tf),
        in_specs=[
            row,
            pl.BlockSpec((1, 1, d), lambda i, j: (layer, 0, 0)),
            pl.BlockSpec((1, d, tf), lambda i, j: (idx, 0, j)),
            pl.BlockSpec((1, d, tf), lambda i, j: (idx, 0, j)),
            pl.BlockSpec((1, tf, d), lambda i, j: (idx, j, 0)),
            pl.BlockSpec((1, 1, d), lambda i, j: (layer + 1, 0, 0)),
        ],
        out_specs=[row, row],
        out_shape=[jax.ShapeDtypeStruct((t, d), F32), jax.ShapeDtypeStruct((t, d), BF16)],
        scratch_shapes=[pltpu.VMEM((tm, d), BF16)],
        input_output_aliases={0: 0},
        compiler_params=_cparams(("parallel", "arbitrary")),
        name="dense_ffn",
    )(h, g.reshape(DEPTH, 1, d), wg, wu, wd, gn.reshape(DEPTH, 1, d))


MOE_TM = 344
MOE_NT = (2 * T_TOK) // MOE_TM + N_EXPERTS
MOE_NP = MOE_NT * MOE_TM
DMA_FANOUT = 4


def _split_bf16(x):
    hi = x.astype(BF16)
    lo = (x - hi.astype(F32)).astype(BF16)
    return hi, lo


def _lane_pick(x, lane, k):
    return jnp.sum(jnp.where(lane == k, x, 0.0), axis=-1, keepdims=True)


ROW_SLABS = D_MODEL // LANE


def _row_slab(ref, c, rows):
    return ref.at[pl.ds(c, rows, stride=ROW_SLABS), :]


def _router_kernel(x_ref, g_ref, wr_ref, sel_ref, wts_ref, rows_ref):
    x = x_ref[...]
    for c in range(ROW_SLABS):
        _row_slab(rows_ref, c, x.shape[0])[...] = x[:, c * LANE:(c + 1) * LANE]
    hn = _rms(x, g_ref[0])
    xh, xl = _split_bf16(hn)
    wh, wl = _split_bf16(wr_ref[0])
    logits = _bdot(xh, wh) + (_bdot(xh, wl) + _bdot(xl, wh)) + _bdot(xl, wl)
    lane = lax.broadcasted_iota(jnp.int32, logits.shape, 1).astype(F32)
    neg = jnp.float32(-jnp.inf)
    logits = jnp.where(lane < N_EXPERTS, logits, neg)
    v1 = jnp.max(logits, axis=-1, keepdims=True)
    i1 = jnp.min(jnp.where(logits == v1, lane, float(LANE)), axis=-1, keepdims=True)
    rest = jnp.where(lane == i1, neg, logits)
    v2 = jnp.max(rest, axis=-1, keepdims=True)
    i2 = jnp.min(jnp.where(rest == v2, lane, float(LANE)), axis=-1, keepdims=True)
    e2 = jnp.exp(v2 - v1)
    w1 = 1.0 / (1.0 + e2)
    w2 = e2 / (1.0 + e2)
    sel_ref[...] = jnp.where((lane == i1) | (lane == i2), 1.0, 0.0)
    wts_ref[...] = jnp.where(lane == 0, w1, jnp.where(lane == 1, w2,
                             jnp.where(lane == 2, i1, jnp.where(lane == 3, i2, 0.0))))


def _router(h, g, wr_pad, layer, tm):
    t, d = h.shape
    idx = layer // 2
    row = pl.BlockSpec((tm, LANE), lambda i: (i, 0))
    tab = jax.ShapeDtypeStruct((t, LANE), F32)
    return pl.pallas_call(
        _router_kernel,
        grid=(t // tm,),
        in_specs=[
            pl.BlockSpec((tm, d), lambda i: (i, 0)),
            pl.BlockSpec((1, 1, d), lambda i: (layer, 0, 0)),
            pl.BlockSpec((1, d, LANE), lambda i: (idx, 0, 0)),
        ],
        out_specs=[row, row, pl.BlockSpec((tm * ROW_SLABS, LANE), lambda i: (i, 0))],
        out_shape=[tab, tab, jax.ShapeDtypeStruct((t * ROW_SLABS, LANE), F32)],
        compiler_params=_cparams(("parallel",)),
        name="moe_router",
    )(h, g.reshape(DEPTH, 1, d), wr_pad)


def _positions_kernel(sel_ref, wts_ref, pos_ref, meta_ref, cnt_ref, off_ref):
    p = pl.program_id(0)
    i = pl.program_id(1)
    tm = sel_ref.shape[0]
    lane = lax.broadcasted_iota(jnp.int32, (1, LANE), 1).astype(F32)
    sel = sel_ref[...]

    @pl.when((p == 0) & (i == 0))
    def _():
        cnt_ref[...] = jnp.zeros((1, LANE), F32)

    @pl.when(p == 0)
    def _():
        cnt_ref[...] += jnp.sum(sel, axis=0, keepdims=True)

    @pl.when((p == 1) & (i == 0))
    def _():
        cnt = cnt_ref[...]
        tiles = jnp.zeros((1, LANE), F32)
        for k in range(MOE_NT):
            tiles = tiles + jnp.where(cnt > float(k * MOE_TM), 1.0, 0.0)
        padded = tiles * float(MOE_TM)
        off = jnp.zeros((1, LANE), F32)
        for e in range(N_EXPERTS):
            off = off + jnp.where(lane > e, _lane_pick(padded, lane, e), 0.0)
        end = off + padded
        tile_start = lane * float(MOE_TM)
        owner = jnp.zeros((1, LANE), F32)
        for e in range(N_EXPERTS):
            owner = owner + jnp.where(tile_start >= _lane_pick(end, lane, e), 1.0, 0.0)
        owner = jnp.minimum(owner, float(N_EXPERTS - 1))
        used = jnp.sum(tiles, axis=-1, keepdims=True)
        off_ref[...] = off
        cnt_ref[...] = jnp.zeros((1, LANE), F32)
        row = lax.broadcasted_iota(jnp.int32, (SUBLANE, LANE), 0)
        meta = jnp.where(row == 0, owner, jnp.where(row == 1, used, jnp.where(row == 2, cnt, off)))
        meta_ref[...] = meta.astype(jnp.int32)

    @pl.when(p == 1)
    def _():
        r = lax.broadcasted_iota(jnp.int32, (tm, tm), 0)
        c = lax.broadcasted_iota(jnp.int32, (tm, tm), 1)
        earlier = jnp.where(c < r, 1.0, 0.0).astype(BF16)
        rank = _bdot(earlier, sel.astype(BF16)) + cnt_ref[...]
        slot = off_ref[...] + rank
        wts = wts_ref[...]
        lane_t = lax.broadcasted_iota(jnp.int32, (tm, LANE), 1).astype(F32)
        p1 = _lane_pick(slot, lane_t, _lane_pick(wts, lane_t, 2))
        p2 = _lane_pick(slot, lane_t, _lane_pick(wts, lane_t, 3))
        pos_ref[...] = jnp.where(lane_t == 0, p1, jnp.where(lane_t == 1, p2, 0.0)).astype(jnp.int32)
        cnt_ref[...] += jnp.sum(sel, axis=0, keepdims=True)


def _positions(sel, wts, tm):
    t = sel.shape[0]
    row = pl.BlockSpec((tm, LANE), lambda p, i: (i, 0))
    return pl.pallas_call(
        _positions_kernel,
        grid=(2, t // tm),
        in_specs=[row, row],
        out_specs=[
            pl.BlockSpec((tm, LANE), lambda p, i: (i * p, 0)),
            pl.BlockSpec((SUBLANE, LANE), lambda p, i: (0, 0)),
        ],
        out_shape=[jax.ShapeDtypeStruct((t, LANE), jnp.int32),
                   jax.ShapeDtypeStruct((SUBLANE, LANE), jnp.int32)],
        scratch_shapes=[pltpu.VMEM((1, LANE), F32), pltpu.VMEM((1, LANE), F32)],
        compiler_params=_cparams(("arbitrary", "arbitrary")),
        name="moe_positions",
    )(sel, wts)


def _experts_kernel(pos1_ref, pos2_ref, owner_ref, used_ref, rows_ref, g_ref, wg_ref, wu_ref, wd_ref,
                    o_ref, src_ref, xbuf, sems):
    i = pl.program_id(0)
    used = used_ref[0]
    cur = lax.rem(i, 2)

    def row_copy(tok, r, buf):
        return pltpu.make_async_copy(rows_ref.at[pl.ds(tok * ROW_SLABS, ROW_SLABS), :],
                                     xbuf.at[buf, pl.ds(r * ROW_SLABS, ROW_SLABS), :], sems.at[buf])

    def fetch_tile(tile, buf):
        def issue(q, c):
            for u in range(DMA_FANOUT):
                r = q * DMA_FANOUT + u
                row_copy(src_ref[tile * MOE_TM + r], r, buf).start(priority=u % 2)
            return c

        lax.fori_loop(0, MOE_TM // DMA_FANOUT, issue, 0)

    @pl.when(i == 0)
    def _():
        def clear(p, c):
            src_ref[p] = 0
            return c

        lax.fori_loop(0, MOE_NP, clear, 0, unroll=8)

        def fill(t, c):
            src_ref[pos1_ref[t]] = t
            src_ref[pos2_ref[t]] = t
            return c

        lax.fori_loop(0, T_TOK, fill, 0, unroll=4)
        fetch_tile(0, 0)

    @pl.when(i + 1 < used)
    def _():
        fetch_tile(i + 1, 1 - cur)

    live = i < used

    @pl.when(live)
    def _():
        def land(r, c):
            row_copy(0, 0, cur).wait()
            return c

        lax.fori_loop(0, MOE_TM, land, 0, unroll=4)
        x_ref = xbuf.at[cur]
        slabs = [_row_slab(x_ref, c, MOE_TM)[...] for c in range(ROW_SLABS)]
        ssq = slabs[0] * slabs[0]
        for c in range(1, ROW_SLABS):
            ssq = ssq + slabs[c] * slabs[c]
        inv = lax.rsqrt(jnp.sum(ssq, axis=-1, keepdims=True) / D_MODEL + EPS)
        g = g_ref[0]
        hn = jnp.concatenate([slabs[c] * inv * g[:, c * LANE:(c + 1) * LANE]
                              for c in range(ROW_SLABS)], axis=1).astype(BF16)
        a = _bdot(hn, wg_ref[0, 0])
        b = _bdot(hn, wu_ref[0, 0])
        hid = ((a * jax.nn.sigmoid(a)) * b).astype(BF16)
        res = _bdot(hid, wd_ref[0, 0])
        for c in range(ROW_SLABS):
            _row_slab(o_ref, c, MOE_TM)[...] = res[:, c * LANE:(c + 1) * LANE]

    @pl.when(jnp.logical_not(live))
    def _():
        o_ref[...] = jnp.zeros(o_ref.shape, F32)


def _experts(rows, pos1, pos2, owner, used, g, wg, wu, wd, layer):
    d = D_MODEL
    idx = layer // 2
    blk = MOE_TM * ROW_SLABS

    w_map = lambda i, p1, p2, o, u: (idx, o[jnp.minimum(i, u[0] - 1)], 0, 0)
    return pl.pallas_call(
        _experts_kernel,
        grid_spec=pltpu.PrefetchScalarGridSpec(
            num_scalar_prefetch=4,
            grid=(MOE_NT,),
            in_specs=[
                pl.BlockSpec(memory_space=pl.ANY),
                pl.BlockSpec((1, 1, d), lambda i, p1, p2, o, u: (layer, 0, 0)),
                pl.BlockSpec((1, 1, d, D_FF_EXPERT), w_map),
                pl.BlockSpec((1, 1, d, D_FF_EXPERT), w_map),
                pl.BlockSpec((1, 1, D_FF_EXPERT, d), w_map),
            ],
            out_specs=pl.BlockSpec((blk, LANE), lambda i, p1, p2, o, u: (i, 0)),
            scratch_shapes=[
                pltpu.SMEM((MOE_NP,), jnp.int32),
                pltpu.VMEM((2, blk, LANE), F32),
                pltpu.SemaphoreType.DMA((2,)),
            ],
        ),
        out_shape=jax.ShapeDtypeStruct((MOE_NP * ROW_SLABS, LANE), F32),
        compiler_params=_cparams(("arbitrary",)),
        name="moe_experts",
    )(pos1, pos2, owner, used, rows, g.reshape(DEPTH, 1, d), wg, wu, wd)


def _combine_kernel(pos1_ref, pos2_ref, h_ref, wts_ref, ys_ref, gn_ref, o_ref, nxt_ref,
                    y1, y2, sem1, sem2):
    tm = h_ref.shape[0]
    base = pl.program_id(0) * tm

    def row_copy(slot, r, buf, sem):
        return pltpu.make_async_copy(ys_ref.at[pl.ds(slot * ROW_SLABS, ROW_SLABS), :],
                                     buf.at[pl.ds(r * ROW_SLABS, ROW_SLABS), :], sem)

    def fetch(q, c):
        for u in range(DMA_FANOUT):
            r = q * DMA_FANOUT + u
            row_copy(pos1_ref[base + r], r, y1, sem1).start(priority=0)
            row_copy(pos2_ref[base + r], r, y2, sem2).start(priority=1)
        return c

    lax.fori_loop(0, tm // DMA_FANOUT, fetch, 0)

    def land(r, c):
        row_copy(0, 0, y1, sem1).wait()
        row_copy(0, 0, y2, sem2).wait()
        return c

    lax.fori_loop(0, tm, land, 0, unroll=4)
    wts = wts_ref[...]
    lane = lax.broadcasted_iota(jnp.int32, wts.shape, 1).astype(F32)
    w1 = _lane_pick(wts, lane, 0)
    w2 = _lane_pick(wts, lane, 1)
    ssq = jnp.zeros((tm, 1), F32)
    for c in range(ROW_SLABS):
        cols = slice(c * LANE, (c + 1) * LANE)
        hc = h_ref[:, cols] + (w1 * _row_slab(y1, c, tm)[...] + w2 * _row_slab(y2, c, tm)[...])
        o_ref[:, cols] = hc
        ssq = ssq + jnp.sum(hc * hc, axis=-1, keepdims=True)
    normed = o_ref[...] * lax.rsqrt(ssq / D_MODEL + EPS) * gn_ref[...]
    if nxt_ref is None:
        o_ref[...] = normed
    else:
        nxt_ref[...] = normed.astype(BF16)


def _combine_mid_kernel(pos1_ref, pos2_ref, h_ref, wts_ref, ys_ref, gn_ref, o_ref, nxt_ref,
                        y1, y2, sem1, sem2):
    _combine_kernel(pos1_ref, pos2_ref, h_ref, wts_ref, ys_ref, gn_ref, o_ref, nxt_ref,
                    y1, y2, sem1, sem2)


def _combine_last_kernel(pos1_ref, pos2_ref, h_ref, wts_ref, ys_ref, gn_ref, o_ref,
                         y1, y2, sem1, sem2):
    _combine_kernel(pos1_ref, pos2_ref, h_ref, wts_ref, ys_ref, gn_ref, o_ref, None,
                    y1, y2, sem1, sem2)


def _combine(h, wts, ys, pos1, pos2, gn, last, tm):
    t, d = h.shape
    row = pl.BlockSpec((tm, d), lambda i, a, b: (i, 0))
    h_out = jax.ShapeDtypeStruct((t, d), F32)
    return pl.pallas_call(
        _combine_last_kernel if last else _combine_mid_kernel,
        grid_spec=pltpu.PrefetchScalarGridSpec(
            num_scalar_prefetch=2,
            grid=(t // tm,),
            in_specs=[
                row,
                pl.BlockSpec((tm, LANE), lambda i, a, b: (i, 0)),
                pl.BlockSpec(memory_space=pl.ANY),
                pl.BlockSpec((1, d), lambda i, a, b: (0, 0)),
            ],
            out_specs=row if last else [row, row],
            scratch_shapes=[
                pltpu.VMEM((tm * ROW_SLABS, LANE), F32),
                pltpu.VMEM((tm * ROW_SLABS, LANE), F32),
                pltpu.SemaphoreType.DMA(()),
                pltpu.SemaphoreType.DMA(()),
            ],
        ),
        out_shape=h_out if last else [h_out, jax.ShapeDtypeStruct((t, d), BF16)],
        input_output_aliases={2: 0},
        compiler_params=_cparams(("arbitrary",)),
        name="moe_combine",
    )(pos1, pos2, h, wts, ys, gn.reshape(1, d))


def _moe(h, g, gn, wr_pad, wg, wu, wd, layer, last):
    sel, wts, rows = _router(h, g, wr_pad, layer, tm=688)
    pos, meta = _positions(sel, wts, tm=688)
    pos1 = pos[:, 0]
    pos2 = pos[:, 1]
    owner = meta[0, :MOE_NT]
    used = meta[1, :1]
    ys = _experts(rows, pos1, pos2, owner, used, g, wg, wu, wd, layer)
    return _combine(h, wts, ys, pos1, pos2, gn, last, tm=688)


def _rope_partner(w):
    half = QK_ROPE // 2
    return jnp.concatenate([-w[..., half:], w[..., :half]], axis=-1)


def _pad_cols(w, width):
    return jnp.pad(w, [(0, 0)] * (w.ndim - 1) + [(0, width - w.shape[-1])])


def _in_proj_layout(w_in):
    kr = w_in[..., Q_LORA + KV_LORA:Q_LORA + KV_LORA + QK_ROPE]
    return jnp.concatenate([
        w_in[..., :Q_LORA + KV_LORA],
        _pad_cols(kr, LANE),
        _pad_cols(_rope_partner(kr), LANE),
        w_in[..., Q_LORA + KV_LORA + QK_ROPE:]], axis=-1)


def _q_layouts(w_uq):
    w = w_uq.reshape(DEPTH, Q_LORA, ATT_HEADS, QK_NOPE + QK_ROPE)
    main = _pad_cols(w, HEAD_PAD).reshape(DEPTH, Q_LORA, ATT_HEADS * HEAD_PAD)
    rot = _pad_cols(_rope_partner(w[..., QK_NOPE:]), LANE).reshape(DEPTH, Q_LORA, ATT_HEADS * LANE)
    return main, rot


def _rope_tables():
    inv = ROPE_THETA ** (-jnp.arange(0, QK_ROPE, 2, dtype=F32) / QK_ROPE)
    ang = jnp.arange(L_TOK, dtype=F32)[:, None] * inv[None, :]
    cos = jnp.cos(ang)
    sin = jnp.sin(ang)
    cos_t = _pad_cols(jnp.concatenate([cos, cos], axis=-1), LANE)
    sin_t = _pad_cols(jnp.concatenate([sin, sin], axis=-1), LANE)
    return cos_t, sin_t


def kernel(x, meta_tokens, mix_norm, w_in, q_norm, w_uq, kv_norm, w_ukv, ssm_lambda_re, ssm_lambda_im, ssm_log_step, ssm_b_re, ssm_b_im, ssm_c_re, ssm_c_im, ssm_d, ssm_w_glu, attn_out_norm, ssm_out_norm, w_out, ffn_norm, dense_w_gate, dense_w_up, dense_w_down, moe_router, moe_w_gate, moe_w_up, moe_w_down, final_norm):
    meta = jnp.broadcast_to(meta_tokens[None].astype(x.dtype), (BATCH, N_META, D_MODEL))
    h = jnp.concatenate([meta, x], axis=1).reshape(T_TOK, D_MODEL)
    cos_t, sin_t = _rope_tables()
    w_in_l = _in_proj_layout(w_in)
    wq_main, wq_rot = _q_layouts(w_uq)
    wr_pad = _pad_cols(moe_router, LANE)
    moe_w_gate, moe_w_up, moe_w_down = (w.astype(BF16) for w in (moe_w_gate, moe_w_up, moe_w_down))
    wb, wc, tre, tim = _s5_layouts(ssm_lambda_re, ssm_lambda_im, ssm_log_step,
                                   ssm_b_re, ssm_b_im, ssm_c_re, ssm_c_im)
    hn = _rms_cast(h, mix_norm, 0, tm=688)
    for layer in range(DEPTH):
        proj = _in_proj(hn, w_in_l, layer, tm=2064, tn=512)
        q, kv, kpe = _upproj(proj, q_norm, kv_norm, wq_main, wq_rot, w_ukv, cos_t, sin_t, layer,
                             tm=688)
        att = _attention(q, kv, kpe, tq=688)
        yf, yb = _s5_scan(proj, wb, wc, tre, tim, layer)
        ssm_n = _glu(yf, yb, proj, ssm_d, ssm_w_glu, ssm_out_norm, layer, tm=688)
        h = _outproj(h, att, ssm_n, attn_out_norm, w_out, layer, tm=1376, tn=512)
        last = layer == DEPTH - 1
        if layer % 2 == 0:
            h, hn = _ffn(h, ffn_norm, mix_norm, dense_w_gate, dense_w_up, dense_w_down, layer,
                         tm=1376, tf=256)
        elif last:
            h = _moe(h, ffn_norm, final_norm, wr_pad, moe_w_gate, moe_w_up, moe_w_down, layer, last)
        else:
            h, hn = _moe(h, ffn_norm, mix_norm[layer + 1], wr_pad, moe_w_gate, moe_w_up, moe_w_down,
                         layer, last)
    return h.reshape(BATCH, L_TOK, D_MODEL)[:, N_META:]
```

```python
import functools
import math

import jax
import jax.numpy as jnp
from jax import lax
from jax.experimental import pallas as pl
from jax.experimental.pallas import tpu as pltpu

F32 = jnp.float32
BF16 = jnp.bfloat16

D_MODEL = 2048
BATCH = 4
SEQ = 2048
DEPTH = 4
N_META = 16
EPS = 1e-6
ATT_HEADS = 8
QK_NOPE = 128
QK_ROPE = 64
V_DIM = 128
Q_LORA = 512
KV_LORA = 256
ROPE_THETA = 10000.0
ATT_WIDTH = ATT_HEADS * V_DIM
SSM_GROUP = 16
SSM_WIDTH = D_MODEL - ATT_WIDTH
SSM_GROUPS = SSM_WIDTH // SSM_GROUP
SSM_STATE = 64
D_FF = 5632
N_EXPERTS = 8
D_FF_EXPERT = 1408

L_TOK = N_META + SEQ
T_TOK = L_TOK * BATCH
LANE = 128
SUBLANE = 8
HEAD_PAD = 256
L_PAD = 2176
PROJ_COLS = 2048
U_COL0 = 1024
VMEM_LIMIT = 58 * 1024 * 1024

SSM_BLK_GROUPS = 8
SSM_BLKS = SSM_GROUPS // SSM_BLK_GROUPS
SSM_BLK_CH = SSM_BLK_GROUPS * SSM_GROUP
SSM_BLK_ST = SSM_BLK_GROUPS * SSM_STATE
S5_CHUNKS = 6
S5_STEPS = L_TOK // S5_CHUNKS
S5_ROWS = S5_STEPS * BATCH
S5_NB = 2


def _cparams(sem):
    return pltpu.CompilerParams(dimension_semantics=sem, vmem_limit_bytes=VMEM_LIMIT)


def _rms(x, g):
    ms = jnp.mean(x * x, axis=-1, keepdims=True)
    return x * lax.rsqrt(ms + EPS) * g


def _bdot(a, b):
    return jnp.dot(a, b, preferred_element_type=F32)


def _rms_cast_kernel(x_ref, g_ref, o_ref):
    o_ref[...] = _rms(x_ref[...], g_ref[0]).astype(BF16)


def _rms_cast(x, g, layer, tm):
    t, d = x.shape
    return pl.pallas_call(
        _rms_cast_kernel,
        grid=(t // tm,),
        in_specs=[pl.BlockSpec((tm, d), lambda i: (i, 0)),
                  pl.BlockSpec((1, 1, d), lambda i: (layer, 0, 0))],
        out_specs=pl.BlockSpec((tm, d), lambda i: (i, 0)),
        out_shape=jax.ShapeDtypeStruct((t, d), BF16),
        compiler_params=_cparams(("parallel",)),
        name="rms_cast",
    )(x, g.reshape(DEPTH, 1, d))


def _in_proj_kernel(x_ref, w_ref, o_ref):
    o_ref[...] = _bdot(x_ref[...], w_ref[0].astype(BF16))


def _in_proj(hn, w, layer, tm, tn):
    t, d = hn.shape
    n = w.shape[2]
    return pl.pallas_call(
        _in_proj_kernel,
        grid=(t // tm, n // tn),
        in_specs=[
            pl.BlockSpec((tm, d), lambda i, j: (i, 0)),
            pl.BlockSpec((1, d, tn), lambda i, j: (layer, 0, j)),
        ],
        out_specs=pl.BlockSpec((tm, tn), lambda i, j: (i, j)),
        out_shape=jax.ShapeDtypeStruct((t, n), F32),
        compiler_params=_cparams(("parallel", "arbitrary")),
        name="in_proj",
    )(hn, w)


def _upproj_kernel(cq_ref, ckv_ref, kr_ref, krot_ref, qn_ref, kvn_ref, wq_ref, wqr_ref,
                   wkv_ref, cos_ref, sin_ref, q_ref, kv_ref, kpe_ref):
    scale = (QK_NOPE + QK_ROPE) ** -0.5 * math.log2(math.e)
    cqn = _rms(cq_ref[...], qn_ref[0]).astype(BF16)
    a = _bdot(cqn, wq_ref[0].astype(BF16))
    r = _bdot(cqn, wqr_ref[0].astype(BF16))
    c = cos_ref[...]
    s = sin_ref[...]
    last_lane = lax.broadcasted_iota(jnp.int32, c.shape, 1) == LANE - 1
    for h in range(ATT_HEADS):
        lo = h * HEAD_PAD
        q_ref[:, lo:lo + LANE] = (a[:, lo:lo + LANE] * scale).astype(BF16)
        rope = a[:, lo + LANE:lo + HEAD_PAD] * c + r[:, h * LANE:(h + 1) * LANE] * s
        q_ref[:, lo + LANE:lo + HEAD_PAD] = jnp.where(last_lane, 1.0, rope * scale).astype(BF16)
    ckvn = _rms(ckv_ref[...], kvn_ref[0]).astype(BF16)
    kv_ref[...] = _bdot(ckvn, wkv_ref[0].astype(BF16)).astype(BF16)
    kpe_ref[...] = (kr_ref[...] * c + krot_ref[...] * s).astype(BF16)


def _upproj(proj, qn, kvn, wq, wqr, wkv, cos_t, sin_t, layer, tm):
    t = proj.shape[0]
    full = lambda a: pl.BlockSpec((1,) + a.shape[1:], lambda i: (layer, 0, 0))
    qn = qn.reshape(DEPTH, 1, Q_LORA)
    kvn = kvn.reshape(DEPTH, 1, KV_LORA)
    pos_blocks = L_TOK // tm
    return pl.pallas_call(
        _upproj_kernel,
        grid=(t // tm,),
        in_specs=[
            pl.BlockSpec((tm, Q_LORA), lambda i: (i, 0)),
            pl.BlockSpec((tm, KV_LORA), lambda i: (i, Q_LORA // KV_LORA)),
            pl.BlockSpec((tm, LANE), lambda i: (i, (Q_LORA + KV_LORA) // LANE)),
            pl.BlockSpec((tm, LANE), lambda i: (i, (Q_LORA + KV_LORA) // LANE + 1)),
            full(qn),
            full(kvn),
            full(wq),
            full(wqr),
            full(wkv),
            pl.BlockSpec((tm, LANE), lambda i: (i % pos_blocks, 0)),
            pl.BlockSpec((tm, LANE), lambda i: (i % pos_blocks, 0)),
        ],
        out_specs=[
            pl.BlockSpec((tm, ATT_HEADS * HEAD_PAD), lambda i: (i, 0)),
            pl.BlockSpec((tm, ATT_HEADS * (QK_NOPE + V_DIM)), lambda i: (i, 0)),
            pl.BlockSpec((tm, LANE), lambda i: (i, 0)),
        ],
        out_shape=[
            jax.ShapeDtypeStruct((t, ATT_HEADS * HEAD_PAD), BF16),
            jax.ShapeDtypeStruct((t, ATT_HEADS * (QK_NOPE + V_DIM)), BF16),
            jax.ShapeDtypeStruct((t, LANE), BF16),
        ],
        compiler_params=_cparams(("parallel",)),
        name="upproj",
    )(proj, proj, proj, proj, qn, kvn, wq, wqr, wkv, cos_t, sin_t)


ATT_HPS = 2
PAD_BIAS = -1e30


def _attn_kernel(q_ref, kv_ref, kpe_ref, o_ref, k_sc, v_sc):
    @pl.when(pl.program_id(2) == 0)
    def _():
        n_pad = L_PAD - L_TOK
        pad_lane = lax.broadcasted_iota(jnp.int32, (n_pad, LANE), 1)
        k_pad = jnp.where(pad_lane == LANE - 1, PAD_BIAS, 0.0).astype(BF16)
        row_lane = lax.broadcasted_iota(jnp.int32, (L_TOK, LANE), 1)
        ones_col = jnp.where(row_lane == 0, 1.0, 0.0).astype(BF16)
        for hh in range(ATT_HPS):
            lo = hh * HEAD_PAD
            k_sc[hh, 0:L_TOK, 0:LANE] = kv_ref[:, lo:lo + LANE]
            k_sc[hh, 0:L_TOK, LANE:HEAD_PAD] = kpe_ref[...]
            k_sc[hh, L_TOK:L_PAD, 0:LANE] = jnp.zeros((n_pad, LANE), BF16)
            k_sc[hh, L_TOK:L_PAD, LANE:HEAD_PAD] = k_pad
            v_sc[hh, 0:L_TOK, 0:V_DIM] = kv_ref[:, lo + LANE:lo + HEAD_PAD]
            v_sc[hh, 0:L_TOK, V_DIM:HEAD_PAD] = ones_col
            v_sc[hh, L_TOK:L_PAD, :] = jnp.zeros((n_pad, HEAD_PAD), BF16)

    for hh in range(ATT_HPS):
        q = q_ref[:, hh * HEAD_PAD:(hh + 1) * HEAD_PAD]
        s = lax.dot_general(q, k_sc[hh], (((1,), (1,)), ((), ())),
                            preferred_element_type=F32)
        m = jnp.max(s, axis=-1, keepdims=True)
        p = jnp.exp2(s - m).astype(BF16)
        o = _bdot(p, v_sc[hh])
        o_ref[:, hh * V_DIM:(hh + 1) * V_DIM] = o[:, :V_DIM] / o[:, V_DIM:V_DIM + 1]


def _attention(q, kv, kpe, tq):
    nq = L_TOK // tq
    hps = ATT_HPS
    return pl.pallas_call(
        _attn_kernel,
        grid=(BATCH, ATT_HEADS // hps, nq),
        in_specs=[
            pl.BlockSpec((tq, hps * HEAD_PAD), lambda b, h, i: (b * nq + i, h)),
            pl.BlockSpec((L_TOK, hps * HEAD_PAD), lambda b, h, i: (b, h)),
            pl.BlockSpec((L_TOK, LANE), lambda b, h, i: (b, 0)),
        ],
        out_specs=pl.BlockSpec((tq, hps * V_DIM), lambda b, h, i: (b * nq + i, h)),
        out_shape=jax.ShapeDtypeStruct((T_TOK, ATT_WIDTH), F32),
        scratch_shapes=[pltpu.VMEM((hps, L_PAD, HEAD_PAD), BF16),
                        pltpu.VMEM((hps, L_PAD, HEAD_PAD), BF16)],
        compiler_params=_cparams(("parallel", "parallel", "arbitrary")),
        name="attention",
    )(q, kv, kpe)


def _s5_param_kernel(lre_ref, lim_ref, ls_ref, bre_ref, bim_ref,
                     lbr_o, lbi_o, l2r_o, l2i_o, bbr_o, bbi_o, lbbr_o, lbbi_o):
    lre = lre_ref[...]
    lim = lim_ref[...]
    dt = jnp.exp(ls_ref[...])
    mag = jnp.exp(lre * dt)
    ang = lim * dt
    br = mag * jnp.cos(ang)
    bi = mag * jnp.sin(ang)
    nr = br - 1.0
    den = lre * lre + lim * lim
    cr = (nr * lre + bi * lim) / den
    ci = (bi * lre - nr * lim) / den
    b_r = bre_ref[...]
    b_i = bim_ref[...]
    bbr = cr * b_r - ci * b_i
    bbi = cr * b_i + ci * b_r
    lbr_o[...] = br
    lbi_o[...] = bi
    l2r_o[...] = br * br - bi * bi
    l2i_o[...] = 2.0 * br * bi
    bbr_o[...] = bbr
    bbi_o[...] = bbi
    lbbr_o[...] = br * bbr - bi * bbi
    lbbi_o[...] = br * bbi + bi * bbr


def _s5_params(lam_re, lam_im, log_step, b_re, b_im):
    n = DEPTH * 2 * SSM_GROUPS
    lre = lam_re.reshape(n, 1, SSM_STATE)
    lim = lam_im.reshape(n, 1, SSM_STATE)
    ls = log_step.reshape(n, 1, 1)
    btr = b_re.transpose(0, 1, 2, 4, 3).reshape(n, SSM_GROUP, SSM_STATE)
    bti = b_im.transpose(0, 1, 2, 4, 3).reshape(n, SSM_GROUP, SSM_STATE)
    small = jax.ShapeDtypeStruct((n, 1, SSM_STATE), F32)
    big = jax.ShapeDtypeStruct((n, SSM_GROUP, SSM_STATE), F32)
    return pl.pallas_call(
        _s5_param_kernel,
        out_shape=[small, small, small, small, big, big, big, big],
        compiler_params=pltpu.CompilerParams(vmem_limit_bytes=VMEM_LIMIT),
        name="s5_params",
    )(lre, lim, ls, btr, bti)


def _s5_layouts(lam_re, lam_im, log_step, b_re, b_im, c_re, c_im):
    lbr, lbi, l2r, l2i, bbr, bbi, lbbr, lbbi = _s5_params(lam_re, lam_im, log_step, b_re, b_im)
    eye = jnp.eye(SSM_BLK_GROUPS, dtype=F32)
    lead = (DEPTH, 2, SSM_BLKS, SSM_BLK_GROUPS)

    def in_block(x):
        x = x.reshape(lead + (SSM_GROUP, SSM_STATE))
        return jnp.einsum('ldbgcp,gh->ldbgchp', x, eye).reshape(
            DEPTH, 2, SSM_BLKS, SSM_BLK_CH, SSM_BLK_ST)

    def out_block(x):
        x = x.reshape(lead + (SSM_GROUP, SSM_STATE))
        return jnp.einsum('ldbgcp,gh->ldbgphc', x, eye).reshape(
            DEPTH, 2, SSM_BLKS, SSM_BLK_ST, SSM_BLK_CH)

    wb = jnp.concatenate([
        jnp.concatenate([in_block(bbr), in_block(bbi)], axis=-1),
        jnp.concatenate([in_block(lbbr), in_block(lbbi)], axis=-1)], axis=-2).astype(BF16)
    wc = jnp.concatenate([out_block(c_re), out_block(-c_im)], axis=-2).astype(BF16)

    def table(one, two):
        one = one.reshape(DEPTH, 2, SSM_BLKS, 1, SSM_BLK_ST)
        two = two.reshape(DEPTH, 2, SSM_BLKS, 1, SSM_BLK_ST)
        half = SUBLANE // 2
        fwd = jnp.concatenate([jnp.broadcast_to(one[:, 0:1], (DEPTH, 1, SSM_BLKS, half, SSM_BLK_ST)),
                               jnp.broadcast_to(two[:, 0:1], (DEPTH, 1, SSM_BLKS, half, SSM_BLK_ST))], axis=3)
        bwd = jnp.concatenate([jnp.broadcast_to(two[:, 1:2], (DEPTH, 1, SSM_BLKS, half, SSM_BLK_ST)),
                               jnp.broadcast_to(one[:, 1:2], (DEPTH, 1, SSM_BLKS, half, SSM_BLK_ST))], axis=3)
        return jnp.concatenate([fwd, bwd], axis=1)

    return wb, wc, table(lbr, l2r), table(lbi, l2i)


def _s5_scan_kernel(uf_ref, ub_ref, wb_ref, wc_ref, tre_ref, tim_ref, yf_ref, yb_ref,
                    tmaj, xs, carry):
    @pl.when(pl.program_id(1) == 0)
    def _():
        carry[...] = jnp.zeros(carry.shape, F32)

    n_tiles = S5_ROWS // SUBLANE
    half = SUBLANE // 2
    io_refs = (uf_ref, ub_ref, yf_ref, yb_ref)

    def time_major(blk, d):
        ch = slice(blk * SSM_BLK_CH, (blk + 1) * SSM_BLK_CH)
        for b in range(BATCH):
            tmaj[blk, d, pl.ds(b, S5_STEPS, stride=BATCH), :] = io_refs[d][b, :, ch]
        return tmaj[blk, d]

    def batch_major(y, blk, d):
        ch = slice(blk * SSM_BLK_CH, (blk + 1) * SSM_BLK_CH)
        tmaj[blk, 2 + d] = y
        for b in range(BATCH):
            io_refs[2 + d][b, :, ch] = tmaj[blk, 2 + d, pl.ds(b, S5_STEPS, stride=BATCH), :]

    def paired_lhs(u, take_upper):
        u3 = u.reshape(n_tiles, SUBLANE, SSM_BLK_CH)
        swapped = pltpu.roll(u3, half, axis=1)
        sub = lax.broadcasted_iota(jnp.int32, u3.shape, 1)
        keep = (sub >= half) if take_upper else (sub < half)
        nb = jnp.where(keep, swapped, 0.0).reshape(S5_ROWS, SSM_BLK_CH)
        return jnp.concatenate([u, nb], axis=1).astype(BF16)

    for blk in range(S5_NB):
        for d in range(2):
            xs[blk, d] = _bdot(paired_lhs(time_major(blk, d), d == 0), wb_ref[0, d, blk])

    lower = lax.broadcasted_iota(jnp.int32, (SUBLANE, 2 * LANE), 0) < half
    for blk in range(S5_NB):
        for hh in range(SSM_BLK_ST // (2 * LANE)):
            lo = hh * 2 * LANE
            re = slice(lo, lo + 2 * LANE)
            im = slice(SSM_BLK_ST + lo, SSM_BLK_ST + lo + 2 * LANE)
            afr, afi = tre_ref[0, 0, blk, :, re], tim_ref[0, 0, blk, :, re]
            abr, abi = tre_ref[0, 1, blk, :, re], tim_ref[0, 1, blk, :, re]
            hfr, hfi, hbr, hbi = (carry[blk, idx, :, re] for idx in range(4))
            for k in range(n_tiles):
                rf = slice(k * SUBLANE, (k + 1) * SUBLANE)
                pr = jnp.where(lower, pltpu.roll(hfr, half, axis=0), hfr)
                pi = jnp.where(lower, pltpu.roll(hfi, half, axis=0), hfi)
                hfr = xs[blk, 0, rf, re] + (afr * pr - afi * pi)
                hfi = xs[blk, 0, rf, im] + (afr * pi + afi * pr)
                xs[blk, 0, rf, re] = hfr
                xs[blk, 0, rf, im] = hfi
                kb = n_tiles - 1 - k
                rb = slice(kb * SUBLANE, (kb + 1) * SUBLANE)
                qr = jnp.where(lower, hbr, pltpu.roll(hbr, half, axis=0))
                qi = jnp.where(lower, hbi, pltpu.roll(hbi, half, axis=0))
                hbr = xs[blk, 1, rb, re] + (abr * qr - abi * qi)
                hbi = xs[blk, 1, rb, im] + (abr * qi + abi * qr)
                xs[blk, 1, rb, re] = hbr
                xs[blk, 1, rb, im] = hbi
            for idx, val in enumerate((hfr, hfi, hbr, hbi)):
                carry[blk, idx, :, re] = val

    for blk in range(S5_NB):
        for d in range(2):
            batch_major(_bdot(xs[blk, d].astype(BF16), wc_ref[0, d, blk]), blk, d)


def _s5_scan(proj, wb, wc, tre, tim, layer):
    last = S5_CHUNKS - 1
    width = S5_NB * SSM_BLK_CH
    u_blk0 = U_COL0 // width
    proj3 = proj.reshape(BATCH, L_TOK, PROJ_COLS)
    y_shape = jax.ShapeDtypeStruct((BATCH, L_TOK, SSM_WIDTH), F32)
    per_blk = lambda *tail: pl.BlockSpec((1, 2, S5_NB) + tail, lambda j, c: (layer, 0, j, 0, 0))
    yf, yb = pl.pallas_call(
        _s5_scan_kernel,
        grid=(SSM_BLKS // S5_NB, S5_CHUNKS),
        in_specs=[
            pl.BlockSpec((BATCH, S5_STEPS, width), lambda j, c: (0, c, u_blk0 + j)),
            pl.BlockSpec((BATCH, S5_STEPS, width), lambda j, c: (0, last - c, u_blk0 + j)),
            per_blk(2 * SSM_BLK_CH, 2 * SSM_BLK_ST),
            per_blk(2 * SSM_BLK_ST, SSM_BLK_CH),
            per_blk(SUBLANE, SSM_BLK_ST),
            per_blk(SUBLANE, SSM_BLK_ST),
        ],
        out_specs=[
            pl.BlockSpec((BATCH, S5_STEPS, width), lambda j, c: (0, c, j)),
            pl.BlockSpec((BATCH, S5_STEPS, width), lambda j, c: (0, last - c, j)),
        ],
        out_shape=[y_shape, y_shape],
        scratch_shapes=[
            pltpu.VMEM((S5_NB, 4, S5_ROWS, SSM_BLK_CH), F32),
            pltpu.VMEM((S5_NB, 2, S5_ROWS, 2 * SSM_BLK_ST), F32),
            pltpu.VMEM((S5_NB, 4, SUBLANE, SSM_BLK_ST), F32),
        ],
        compiler_params=_cparams(("parallel", "arbitrary")),
        name="s5_scan",
    )(proj3, proj3, wb, wc, tre, tim)
    return yf.reshape(T_TOK, SSM_WIDTH), yb.reshape(T_TOK, SSM_WIDTH)


def _gelu_tanh(x):
    return 0.5 * x * (1.0 + jnp.tanh(math.sqrt(2.0 / math.pi) * (x + 0.044715 * (x * x * x))))


def _glu_kernel(yf_ref, yb_ref, u_ref, d_ref, w_ref, gn_ref, o_ref):
    y = (yf_ref[...] + yb_ref[...]) + d_ref[0] * u_ref[...]
    g = _gelu_tanh(y)
    z = _bdot(g.astype(BF16), w_ref[0].astype(BF16))
    o_ref[...] = _rms(g * jax.nn.sigmoid(z), gn_ref[0]).astype(BF16)


def _glu(yf, yb, proj, d, w, gn, layer, tm):
    t = yf.shape[0]
    row = pl.BlockSpec((tm, SSM_WIDTH), lambda i: (i, 0))
    vec = pl.BlockSpec((1, 1, SSM_WIDTH), lambda i: (layer, 0, 0))
    return pl.pallas_call(
        _glu_kernel,
        grid=(t // tm,),
        in_specs=[row, row,
                  pl.BlockSpec((tm, SSM_WIDTH), lambda i: (i, U_COL0 // SSM_WIDTH)),
                  vec,
                  pl.BlockSpec((1, SSM_WIDTH, SSM_WIDTH), lambda i: (layer, 0, 0)),
                  vec],
        out_specs=row,
        out_shape=jax.ShapeDtypeStruct((t, SSM_WIDTH), BF16),
        compiler_params=_cparams(("parallel",)),
        name="s5_glu",
    )(yf, yb, proj, d.reshape(DEPTH, 1, SSM_WIDTH), w, gn.reshape(DEPTH, 1, SSM_WIDTH))


def _outproj_kernel(h_ref, a_ref, sn_ref, ga_ref, wa_ref, ws_ref, o_ref, an_ref):
    @pl.when(pl.program_id(1) == 0)
    def _():
        an_ref[...] = _rms(a_ref[...], ga_ref[0]).astype(BF16)

    o_ref[...] = (h_ref[...] + _bdot(an_ref[...], wa_ref[0].astype(BF16))
                  + _bdot(sn_ref[...], ws_ref[0].astype(BF16)))


def _outproj(h, att, ssm_n, ga, w, layer, tm, tn):
    t, d = h.shape
    return pl.pallas_call(
        _outproj_kernel,
        grid=(t // tm, d // tn),
        in_specs=[
            pl.BlockSpec((tm, tn), lambda i, j: (i, j)),
            pl.BlockSpec((tm, ATT_WIDTH), lambda i, j: (i, 0)),
            pl.BlockSpec((tm, SSM_WIDTH), lambda i, j: (i, 0)),
            pl.BlockSpec((1, 1, ATT_WIDTH), lambda i, j: (layer, 0, 0)),
            pl.BlockSpec((1, ATT_WIDTH, tn), lambda i, j: (layer, 0, j)),
            pl.BlockSpec((1, SSM_WIDTH, tn), lambda i, j: (layer, 1, j)),
        ],
        out_specs=pl.BlockSpec((tm, tn), lambda i, j: (i, j)),
        out_shape=jax.ShapeDtypeStruct((t, d), F32),
        scratch_shapes=[pltpu.VMEM((tm, ATT_WIDTH), BF16)],
        input_output_aliases={0: 0},
        compiler_params=_cparams(("parallel", "arbitrary")),
        name="out_proj",
    )(h, att, ssm_n, ga.reshape(DEPTH, 1, ATT_WIDTH), w, w)


def _ffn_kernel(x_ref, g_ref, wg_ref, wu_ref, wd_ref, gn_ref, o_ref, nxt_ref, hn_ref):
    j = pl.program_id(1)

    @pl.when(j == 0)
    def _():
        x = x_ref[...]
        hn_ref[...] = _rms(x, g_ref[0]).astype(BF16)
        o_ref[...] = x

    hn = hn_ref[...]
    a = _bdot(hn, wg_ref[0].astype(BF16))
    b = _bdot(hn, wu_ref[0].astype(BF16))
    hid = (a * jax.nn.sigmoid(a)) * b
    o_ref[...] += _bdot(hid.astype(BF16), wd_ref[0].astype(BF16))

    @pl.when(j == pl.num_programs(1) - 1)
    def _():
        nxt_ref[...] = _rms(o_ref[...], gn_ref[0]).astype(BF16)


def _ffn(h, g, gn, wg, wu, wd, layer, tm, tf):
    t, d = h.shape
    f = wg.shape[2]
    idx = layer // 2
    once = pl.Buffered(1)
    row = pl.BlockSpec((tm, d), lambda i, j: (i, 0), pipeline_mode=once)
    return pl.pallas_call(
        _ffn_kernel,
        grid=(t // tm, f // tf),
        in_specs=[
            row,
            pl.BlockSpec((1, 1, d), lambda i, j: (layer, 0, 0)),
            pl.BlockSpec((1, d, tf), lambda i, j: (idx, 0, j)),
            pl.BlockSpec((1, d, tf), lambda i, j: (idx, 0, j)),
            pl.BlockSpec((1, tf, d), lambda i, j: (idx, j, 0)),
            pl.BlockSpec((1, 1, d), lambda i, j: (layer + 1, 0, 0)),
        ],
        out_specs=[row, row],
        out_shape=[jax.ShapeDtypeStruct((t, d), F32), jax.ShapeDtypeStruct((t, d), BF16)],
        scratch_shapes=[pltpu.VMEM((tm, d), BF16)],
        input_output_aliases={0: 0},
        compiler_params=_cparams(("parallel", "arbitrary")),
        name="dense_ffn",
    )(h, g.reshape(DEPTH, 1, d), wg, wu, wd, gn.reshape(DEPTH, 1, d))


MOE_TM = 344
MOE_NT = (2 * T_TOK) // MOE_TM + N_EXPERTS
MOE_NP = MOE_NT * MOE_TM
DMA_FANOUT = 4


def _split_bf16(x):
    hi = x.astype(BF16)
    lo = (x - hi.astype(F32)).astype(BF16)
    return hi, lo


def _lane_pick(x, lane, k):
    return jnp.sum(jnp.where(lane == k, x, 0.0), axis=-1, keepdims=True)


ROW_SLABS = D_MODEL // LANE


def _row_slab(ref, c, rows):
    return ref.at[pl.ds(c, rows, stride=ROW_SLABS), :]


def _router_kernel(x_ref, g_ref, wr_ref, sel_ref, wts_ref, rows_ref):
    x = x_ref[...]
    for c in range(ROW_SLABS):
        _row_slab(rows_ref, c, x.shape[0])[...] = x[:, c * LANE:(c + 1) * LANE]
    hn = _rms(x, g_ref[0])
    xh, xl = _split_bf16(hn)
    wh, wl = _split_bf16(wr_ref[0])
    logits = _bdot(xh, wh) + (_bdot(xh, wl) + _bdot(xl, wh)) + _bdot(xl, wl)
    lane = lax.broadcasted_iota(jnp.int32, logits.shape, 1).astype(F32)
    neg = jnp.float32(-jnp.inf)
    logits = jnp.where(lane < N_EXPERTS, logits, neg)
    v1 = jnp.max(logits, axis=-1, keepdims=True)
    i1 = jnp.min(jnp.where(logits == v1, lane, float(LANE)), axis=-1, keepdims=True)
    rest = jnp.where(lane == i1, neg, logits)
    v2 = jnp.max(rest, axis=-1, keepdims=True)
    i2 = jnp.min(jnp.where(rest == v2, lane, float(LANE)), axis=-1, keepdims=True)
    e2 = jnp.exp(v2 - v1)
    w1 = 1.0 / (1.0 + e2)
    w2 = e2 / (1.0 + e2)
    sel_ref[...] = jnp.where((lane == i1) | (lane == i2), 1.0, 0.0)
    wts_ref[...] = jnp.where(lane == 0, w1, jnp.where(lane == 1, w2,
                             jnp.where(lane == 2, i1, jnp.where(lane == 3, i2, 0.0))))


def _router(h, g, wr_pad, layer, tm):
    t, d = h.shape
    idx = layer // 2
    row = pl.BlockSpec((tm, LANE), lambda i: (i, 0))
    tab = jax.ShapeDtypeStruct((t, LANE), F32)
    return pl.pallas_call(
        _router_kernel,
        grid=(t // tm,),
        in_specs=[
            pl.BlockSpec((tm, d), lambda i: (i, 0)),
            pl.BlockSpec((1, 1, d), lambda i: (layer, 0, 0)),
            pl.BlockSpec((1, d, LANE), lambda i: (idx, 0, 0)),
        ],
        out_specs=[row, row, pl.BlockSpec((tm * ROW_SLABS, LANE), lambda i: (i, 0))],
        out_shape=[tab, tab, jax.ShapeDtypeStruct((t * ROW_SLABS, LANE), F32)],
        compiler_params=_cparams(("parallel",)),
        name="moe_router",
    )(h, g.reshape(DEPTH, 1, d), wr_pad)


def _positions_kernel(sel_ref, wts_ref, pos_ref, meta_ref, cnt_ref, off_ref):
    p = pl.program_id(0)
    i = pl.program_id(1)
    tm = sel_ref.shape[0]
    lane = lax.broadcasted_iota(jnp.int32, (1, LANE), 1).astype(F32)
    sel = sel_ref[...]

    @pl.when((p == 0) & (i == 0))
    def _():
        cnt_ref[...] = jnp.zeros((1, LANE), F32)

    @pl.when(p == 0)
    def _():
        cnt_ref[...] += jnp.sum(sel, axis=0, keepdims=True)

    @pl.when((p == 1) & (i == 0))
    def _():
        cnt = cnt_ref[...]
        tiles = jnp.zeros((1, LANE), F32)
        for k in range(MOE_NT):
            tiles = tiles + jnp.where(cnt > float(k * MOE_TM), 1.0, 0.0)
        padded = tiles * float(MOE_TM)
        off = jnp.zeros((1, LANE), F32)
        for e in range(N_EXPERTS):
            off = off + jnp.where(lane > e, _lane_pick(padded, lane, e), 0.0)
        end = off + padded
        tile_start = lane * float(MOE_TM)
        owner = jnp.zeros((1, LANE), F32)
        for e in range(N_EXPERTS):
            owner = owner + jnp.where(tile_start >= _lane_pick(end, lane, e), 1.0, 0.0)
        owner = jnp.minimum(owner, float(N_EXPERTS - 1))
        used = jnp.sum(tiles, axis=-1, keepdims=True)
        off_ref[...] = off
        cnt_ref[...] = jnp.zeros((1, LANE), F32)
        row = lax.broadcasted_iota(jnp.int32, (SUBLANE, LANE), 0)
        meta = jnp.where(row == 0, owner, jnp.where(row == 1, used, jnp.where(row == 2, cnt, off)))
        meta_ref[...] = meta.astype(jnp.int32)

    @pl.when(p == 1)
    def _():
        r = lax.broadcasted_iota(jnp.int32, (tm, tm), 0)
        c = lax.broadcasted_iota(jnp.int32, (tm, tm), 1)
        earlier = jnp.where(c < r, 1.0, 0.0).astype(BF16)
        rank = _bdot(earlier, sel.astype(BF16)) + cnt_ref[...]
        slot = off_ref[...] + rank
        wts = wts_ref[...]
        lane_t = lax.broadcasted_iota(jnp.int32, (tm, LANE), 1).astype(F32)
        p1 = _lane_pick(slot, lane_t, _lane_pick(wts, lane_t, 2))
        p2 = _lane_pick(slot, lane_t, _lane_pick(wts, lane_t, 3))
        pos_ref[...] = jnp.where(lane_t == 0, p1, jnp.where(lane_t == 1, p2, 0.0)).astype(jnp.int32)
        cnt_ref[...] += jnp.sum(sel, axis=0, keepdims=True)


def _positions(sel, wts, tm):
    t = sel.shape[0]
    row = pl.BlockSpec((tm, LANE), lambda p, i: (i, 0))
    return pl.pallas_call(
        _positions_kernel,
        grid=(2, t // tm),
        in_specs=[row, row],
        out_specs=[
            pl.BlockSpec((tm, LANE), lambda p, i: (i * p, 0)),
            pl.BlockSpec((SUBLANE, LANE), lambda p, i: (0, 0)),
        ],
        out_shape=[jax.ShapeDtypeStruct((t, LANE), jnp.int32),
                   jax.ShapeDtypeStruct((SUBLANE, LANE), jnp.int32)],
        scratch_shapes=[pltpu.VMEM((1, LANE), F32), pltpu.VMEM((1, LANE), F32)],
        compiler_params=_cparams(("arbitrary", "arbitrary")),
        name="moe_positions",
    )(sel, wts)


def _experts_kernel(pos1_ref, pos2_ref, owner_ref, used_ref, rows_ref, g_ref, wg_ref, wu_ref, wd_ref,
                    o_ref, src_ref, xbuf, sems):
    i = pl.program_id(0)
    used = used_ref[0]
    cur = lax.rem(i, 2)

    def row_copy(tok, r, buf):
        return pltpu.make_async_copy(rows_ref.at[pl.ds(tok * ROW_SLABS, ROW_SLABS), :],
                                     xbuf.at[buf, pl.ds(r * ROW_SLABS, ROW_SLABS), :], sems.at[buf])

    def fetch_tile(tile, buf):
        def issue(q, c):
            for u in range(DMA_FANOUT):
                r = q * DMA_FANOUT + u
                row_copy(src_ref[tile * MOE_TM + r], r, buf).start(priority=u % 2)
            return c

        lax.fori_loop(0, MOE_TM // DMA_FANOUT, issue, 0)

    @pl.when(i == 0)
    def _():
        def clear(p, c):
            src_ref[p] = 0
            return c

        lax.fori_loop(0, MOE_NP, clear, 0, unroll=8)

        def fill(t, c):
            src_ref[pos1_ref[t]] = t
            src_ref[pos2_ref[t]] = t
            return c

        lax.fori_loop(0, T_TOK, fill, 0, unroll=4)
        fetch_tile(0, 0)

    @pl.when(i + 1 < used)
    def _():
        fetch_tile(i + 1, 1 - cur)

    live = i < used

    @pl.when(live)
    def _():
        def land(r, c):
            row_copy(0, 0, cur).wait()
            return c

        lax.fori_loop(0, MOE_TM, land, 0, unroll=4)
        x_ref = xbuf.at[cur]
        slabs = [_row_slab(x_ref, c, MOE_TM)[...] for c in range(ROW_SLABS)]
        ssq = slabs[0] * slabs[0]
        for c in range(1, ROW_SLABS):
            ssq = ssq + slabs[c] * slabs[c]
        inv = lax.rsqrt(jnp.sum(ssq, axis=-1, keepdims=True) / D_MODEL + EPS)
        g = g_ref[0]
        hn = jnp.concatenate([slabs[c] * inv * g[:, c * LANE:(c + 1) * LANE]
                              for c in range(ROW_SLABS)], axis=1).astype(BF16)
        a = _bdot(hn, wg_ref[0, 0])
        b = _bdot(hn, wu_ref[0, 0])
        hid = ((a * jax.nn.sigmoid(a)) * b).astype(BF16)
        res = _bdot(hid, wd_ref[0, 0])
        for c in range(ROW_SLABS):
            _row_slab(o_ref, c, MOE_TM)[...] = res[:, c * LANE:(c + 1) * LANE]

    @pl.when(jnp.logical_not(live))
    def _():
        o_ref[...] = jnp.zeros(o_ref.shape, F32)


def _experts(rows, pos1, pos2, owner, used, g, wg, wu, wd, layer):
    d = D_MODEL
    idx = layer // 2
    blk = MOE_TM * ROW_SLABS

    w_map = lambda i, p1, p2, o, u: (idx, o[jnp.minimum(i, u[0] - 1)], 0, 0)
    return pl.pallas_call(
        _experts_kernel,
        grid_spec=pltpu.PrefetchScalarGridSpec(
            num_scalar_prefetch=4,
            grid=(MOE_NT,),
            in_specs=[
                pl.BlockSpec(memory_space=pl.ANY),
                pl.BlockSpec((1, 1, d), lambda i, p1, p2, o, u: (layer, 0, 0)),
                pl.BlockSpec((1, 1, d, D_FF_EXPERT), w_map),
                pl.BlockSpec((1, 1, d, D_FF_EXPERT), w_map),
                pl.BlockSpec((1, 1, D_FF_EXPERT, d), w_map),
            ],
            out_specs=pl.BlockSpec((blk, LANE), lambda i, p1, p2, o, u: (i, 0)),
            scratch_shapes=[
                pltpu.SMEM((MOE_NP,), jnp.int32),
                pltpu.VMEM((2, blk, LANE), F32),
                pltpu.SemaphoreType.DMA((2,)),
            ],
        ),
        out_shape=jax.ShapeDtypeStruct((MOE_NP * ROW_SLABS, LANE), F32),
        compiler_params=_cparams(("arbitrary",)),
        name="moe_experts",
    )(pos1, pos2, owner, used, rows, g.reshape(DEPTH, 1, d), wg, wu, wd)


def _combine_kernel(pos1_ref, pos2_ref, h_ref, wts_ref, ys_ref, gn_ref, o_ref, nxt_ref,
                    y1, y2, sem1, sem2):
    tm = h_ref.shape[0]
    base = pl.program_id(0) * tm

    def row_copy(slot, r, buf, sem):
        return pltpu.make_async_copy(ys_ref.at[pl.ds(slot * ROW_SLABS, ROW_SLABS), :],
                                     buf.at[pl.ds(r * ROW_SLABS, ROW_SLABS), :], sem)

    def fetch(q, c):
        for u in range(DMA_FANOUT):
            r = q * DMA_FANOUT + u
            row_copy(pos1_ref[base + r], r, y1, sem1).start(priority=0)
            row_copy(pos2_ref[base + r], r, y2, sem2).start(priority=1)
        return c

    lax.fori_loop(0, tm // DMA_FANOUT, fetch, 0)

    def land(r, c):
        row_copy(0, 0, y1, sem1).wait()
        row_copy(0, 0, y2, sem2).wait()
        return c

    lax.fori_loop(0, tm, land, 0, unroll=4)
    wts = wts_ref[...]
    lane = lax.broadcasted_iota(jnp.int32, wts.shape, 1).astype(F32)
    w1 = _lane_pick(wts, lane, 0)
    w2 = _lane_pick(wts, lane, 1)
    ssq = jnp.zeros((tm, 1), F32)
    for c in range(ROW_SLABS):
        cols = slice(c * LANE, (c + 1) * LANE)
        hc = h_ref[:, cols] + (w1 * _row_slab(y1, c, tm)[...] + w2 * _row_slab(y2, c, tm)[...])
        o_ref[:, cols] = hc
        ssq = ssq + jnp.sum(hc * hc, axis=-1, keepdims=True)
    normed = o_ref[...] * lax.rsqrt(ssq / D_MODEL + EPS) * gn_ref[...]
    if nxt_ref is None:
        o_ref[...] = normed
    else:
        nxt_ref[...] = normed.astype(BF16)


def _combine_mid_kernel(pos1_ref, pos2_ref, h_ref, wts_ref, ys_ref, gn_ref, o_ref, nxt_ref,
                        y1, y2, sem1, sem2):
    _combine_kernel(pos1_ref, pos2_ref, h_ref, wts_ref, ys_ref, gn_ref, o_ref, nxt_ref,
                    y1, y2, sem1, sem2)


def _combine_last_kernel(pos1_ref, pos2_ref, h_ref, wts_ref, ys_ref, gn_ref, o_ref,
                         y1, y2, sem1, sem2):
    _combine_kernel(pos1_ref, pos2_ref, h_ref, wts_ref, ys_ref, gn_ref, o_ref, None,
                    y1, y2, sem1, sem2)


def _combine(h, wts, ys, pos1, pos2, gn, last, tm):
    t, d = h.shape
    row = pl.BlockSpec((tm, d), lambda i, a, b: (i, 0))
    h_out = jax.ShapeDtypeStruct((t, d), F32)
    return pl.pallas_call(
        _combine_last_kernel if last else _combine_mid_kernel,
        grid_spec=pltpu.PrefetchScalarGridSpec(
            num_scalar_prefetch=2,
            grid=(t // tm,),
            in_specs=[
                row,
                pl.BlockSpec((tm, LANE), lambda i, a, b: (i, 0)),
                pl.BlockSpec(memory_space=pl.ANY),
                pl.BlockSpec((1, d), lambda i, a, b: (0, 0)),
            ],
            out_specs=row if last else [row, row],
            scratch_shapes=[
                pltpu.VMEM((tm * ROW_SLABS, LANE), F32),
                pltpu.VMEM((tm * ROW_SLABS, LANE), F32),
                pltpu.SemaphoreType.DMA(()),
                pltpu.SemaphoreType.DMA(()),
            ],
        ),
        out_shape=h_out if last else [h_out, jax.ShapeDtypeStruct((t, d), BF16)],
        input_output_aliases={2: 0},
        compiler_params=_cparams(("arbitrary",)),
        name="moe_combine",
    )(pos1, pos2, h, wts, ys, gn.reshape(1, d))


def _moe(h, g, gn, wr_pad, wg, wu, wd, layer, last):
    sel, wts, rows = _router(h, g, wr_pad, layer, tm=688)
    pos, meta = _positions(sel, wts, tm=688)
    pos1 = pos[:, 0]
    pos2 = pos[:, 1]
    owner = meta[0, :MOE_NT]
    used = meta[1, :1]
    ys = _experts(rows, pos1, pos2, owner, used, g, wg, wu, wd, layer)
    return _combine(h, wts, ys, pos1, pos2, gn, last, tm=688)


def _rope_partner(w):
    half = QK_ROPE // 2
    return jnp.concatenate([-w[..., half:], w[..., :half]], axis=-1)


def _pad_cols(w, width):
    return jnp.pad(w, [(0, 0)] * (w.ndim - 1) + [(0, width - w.shape[-1])])


def _in_proj_layout(w_in):
    kr = w_in[..., Q_LORA + KV_LORA:Q_LORA + KV_LORA + QK_ROPE]
    return jnp.concatenate([
        w_in[..., :Q_LORA + KV_LORA],
        _pad_cols(kr, LANE),
        _pad_cols(_rope_partner(kr), LANE),
        w_in[..., Q_LORA + KV_LORA + QK_ROPE:]], axis=-1)


def _q_layouts(w_uq):
    w = w_uq.reshape(DEPTH, Q_LORA, ATT_HEADS, QK_NOPE + QK_ROPE)
    main = _pad_cols(w, HEAD_PAD).reshape(DEPTH, Q_LORA, ATT_HEADS * HEAD_PAD)
    rot = _pad_cols(_rope_partner(w[..., QK_NOPE:]), LANE).reshape(DEPTH, Q_LORA, ATT_HEADS * LANE)
    return main, rot


def _rope_tables():
    inv = ROPE_THETA ** (-jnp.arange(0, QK_ROPE, 2, dtype=F32) / QK_ROPE)
    ang = jnp.arange(L_TOK, dtype=F32)[:, None] * inv[None, :]
    cos = jnp.cos(ang)
    sin = jnp.sin(ang)
    cos_t = _pad_cols(jnp.concatenate([cos, cos], axis=-1), LANE)
    sin_t = _pad_cols(jnp.concatenate([sin, sin], axis=-1), LANE)
    return cos_t, sin_t


def kernel(x, meta_tokens, mix_norm, w_in, q_norm, w_uq, kv_norm, w_ukv, ssm_lambda_re, ssm_lambda_im, ssm_log_step, ssm_b_re, ssm_b_im, ssm_c_re, ssm_c_im, ssm_d, ssm_w_glu, attn_out_norm, ssm_out_norm, w_out, ffn_norm, dense_w_gate, dense_w_up, dense_w_down, moe_router, moe_w_gate, moe_w_up, moe_w_down, final_norm):
    meta = jnp.broadcast_to(meta_tokens[None].astype(x.dtype), (BATCH, N_META, D_MODEL))
    h = jnp.concatenate([meta, x], axis=1).reshape(T_TOK, D_MODEL)
    cos_t, sin_t = _rope_tables()
    w_in_l = _in_proj_layout(w_in)
    wq_main, wq_rot = _q_layouts(w_uq)
    wr_pad = _pad_cols(moe_router, LANE)
    moe_w_gate, moe_w_up, moe_w_down = (w.astype(BF16) for w in (moe_w_gate, moe_w_up, moe_w_down))
    wb, wc, tre, tim = _s5_layouts(ssm_lambda_re, ssm_lambda_im, ssm_log_step,
                                   ssm_b_re, ssm_b_im, ssm_c_re, ssm_c_im)
    hn = _rms_cast(h, mix_norm, 0, tm=688)
    for layer in range(DEPTH):
        proj = _in_proj(hn, w_in_l, layer, tm=2064, tn=512)
        q, kv, kpe = _upproj(proj, q_norm, kv_norm, wq_main, wq_rot, w_ukv, cos_t, sin_t, layer,
                             tm=688)
        att = _attention(q, kv, kpe, tq=688)
        yf, yb = _s5_scan(proj, wb, wc, tre, tim, layer)
        ssm_n = _glu(yf, yb, proj, ssm_d, ssm_w_glu, ssm_out_norm, layer, tm=688)
        h = _outproj(h, att, ssm_n, attn_out_norm, w_out, layer, tm=1376, tn=512)
        last = layer == DEPTH - 1
        if layer % 2 == 0:
            h, hn = _ffn(h, ffn_norm, mix_norm, dense_w_gate, dense_w_up, dense_w_down, layer,
                         tm=1376, tf=256)
        elif last:
            h = _moe(h, ffn_norm, final_norm, wr_pad, moe_w_gate, moe_w_up, moe_w_down, layer, last)
        else:
            h, hn = _moe(h, ffn_norm, mix_norm[layer + 1], wr_pad, moe_w_gate, moe_w_up, moe_w_down,
                         layer, last)
    return h.reshape(BATCH, L_TOK, D_MODEL)[:, N_META:]
```

```python
import functools
import math

import jax
import jax.numpy as jnp
from jax import lax
from jax.experimental import pallas as pl
from jax.experimental.pallas import tpu as pltpu

F32 = jnp.float32
BF16 = jnp.bfloat16

D_MODEL = 2048
BATCH = 4
SEQ = 2048
DEPTH = 4
N_META = 16
EPS = 1e-6
ATT_HEADS = 8
QK_NOPE = 128
QK_ROPE = 64
V_DIM = 128
Q_LORA = 512
KV_LORA = 256
ROPE_THETA = 10000.0
ATT_WIDTH = ATT_HEADS * V_DIM
SSM_GROUP = 16
SSM_WIDTH = D_MODEL - ATT_WIDTH
SSM_GROUPS = SSM_WIDTH // SSM_GROUP
SSM_STATE = 64
D_FF = 5632
N_EXPERTS = 8
D_FF_EXPERT = 1408

L_TOK = N_META + SEQ
T_TOK = L_TOK * BATCH
LANE = 128
SUBLANE = 8
HEAD_PAD = 256
L_PAD = 2176
PROJ_COLS = 2048
U_COL0 = 1024
VMEM_LIMIT = 58 * 1024 * 1024

SSM_BLK_GROUPS = 8
SSM_BLKS = SSM_GROUPS // SSM_BLK_GROUPS
SSM_BLK_CH = SSM_BLK_GROUPS * SSM_GROUP
SSM_BLK_ST = SSM_BLK_GROUPS * SSM_STATE
S5_CHUNKS = 6
S5_STEPS = L_TOK // S5_CHUNKS
S5_ROWS = S5_STEPS * BATCH
S5_NB = 2


def _cparams(sem):
    return pltpu.CompilerParams(dimension_semantics=sem, vmem_limit_bytes=VMEM_LIMIT)


def _rms(x, g):
    ms = jnp.mean(x * x, axis=-1, keepdims=True)
    return x * lax.rsqrt(ms + EPS) * g


def _bdot(a, b):
    return jnp.dot(a, b, preferred_element_type=F32)


def _rms_cast_kernel(x_ref, g_ref, o_ref):
    o_ref[...] = _rms(x_ref[...], g_ref[0]).astype(BF16)


def _rms_cast(x, g, layer, tm):
    t, d = x.shape
    return pl.pallas_call(
        _rms_cast_kernel,
        grid=(t // tm,),
        in_specs=[pl.BlockSpec((tm, d), lambda i: (i, 0)),
                  pl.BlockSpec((1, 1, d), lambda i: (layer, 0, 0))],
        out_specs=pl.BlockSpec((tm, d), lambda i: (i, 0)),
        out_shape=jax.ShapeDtypeStruct((t, d), BF16),
        compiler_params=_cparams(("parallel",)),
        name="rms_cast",
    )(x, g.reshape(DEPTH, 1, d))


def _in_proj_kernel(x_ref, w_ref, o_ref):
    o_ref[...] = _bdot(x_ref[...], w_ref[0].astype(BF16))


def _in_proj(hn, w, layer, tm, tn):
    t, d = hn.shape
    n = w.shape[2]
    return pl.pallas_call(
        _in_proj_kernel,
        grid=(t // tm, n // tn),
        in_specs=[
            pl.BlockSpec((tm, d), lambda i, j: (i, 0)),
            pl.BlockSpec((1, d, tn), lambda i, j: (layer, 0, j)),
        ],
        out_specs=pl.BlockSpec((tm, tn), lambda i, j: (i, j)),
        out_shape=jax.ShapeDtypeStruct((t, n), F32),
        compiler_params=_cparams(("parallel", "arbitrary")),
        name="in_proj",
    )(hn, w)


def _upproj_kernel(cq_ref, ckv_ref, kr_ref, krot_ref, qn_ref, kvn_ref, wq_ref, wqr_ref,
                   wkv_ref, cos_ref, sin_ref, q_ref, kv_ref, kpe_ref):
    scale = (QK_NOPE + QK_ROPE) ** -0.5 * math.log2(math.e)
    cqn = _rms(cq_ref[...], qn_ref[0]).astype(BF16)
    a = _bdot(cqn, wq_ref[0].astype(BF16))
    r = _bdot(cqn, wqr_ref[0].astype(BF16))
    c = cos_ref[...]
    s = sin_ref[...]
    last_lane = lax.broadcasted_iota(jnp.int32, c.shape, 1) == LANE - 1
    for h in range(ATT_HEADS):
        lo = h * HEAD_PAD
        q_ref[:, lo:lo + LANE] = (a[:, lo:lo + LANE] * scale).astype(BF16)
        rope = a[:, lo + LANE:lo + HEAD_PAD] * c + r[:, h * LANE:(h + 1) * LANE] * s
        q_ref[:, lo + LANE:lo + HEAD_PAD] = jnp.where(last_lane, 1.0, rope * scale).astype(BF16)
    ckvn = _rms(ckv_ref[...], kvn_ref[0]).astype(BF16)
    kv_ref[...] = _bdot(ckvn, wkv_ref[0].astype(BF16)).astype(BF16)
    kpe_ref[...] = (kr_ref[...] * c + krot_ref[...] * s).astype(BF16)


def _upproj(proj, qn, kvn, wq, wqr, wkv, cos_t, sin_t, layer, tm):
    t = proj.shape[0]
    full = lambda a: pl.BlockSpec((1,) + a.shape[1:], lambda i: (layer, 0, 0))
    qn = qn.reshape(DEPTH, 1, Q_LORA)
    kvn = kvn.reshape(DEPTH, 1, KV_LORA)
    pos_blocks = L_TOK // tm
    return pl.pallas_call(
        _upproj_kernel,
        grid=(t // tm,),
        in_specs=[
            pl.BlockSpec((tm, Q_LORA), lambda i: (i, 0)),
            pl.BlockSpec((tm, KV_LORA), lambda i: (i, Q_LORA // KV_LORA)),
            pl.BlockSpec((tm, LANE), lambda i: (i, (Q_LORA + KV_LORA) // LANE)),
            pl.BlockSpec((tm, LANE), lambda i: (i, (Q_LORA + KV_LORA) // LANE + 1)),
            full(qn),
            full(kvn),
            full(wq),
            full(wqr),
            full(wkv),
            pl.BlockSpec((tm, LANE), lambda i: (i % pos_blocks, 0)),
            pl.BlockSpec((tm, LANE), lambda i: (i % pos_blocks, 0)),
        ],
        out_specs=[
            pl.BlockSpec((tm, ATT_HEADS * HEAD_PAD), lambda i: (i, 0)),
            pl.BlockSpec((tm, ATT_HEADS * (QK_NOPE + V_DIM)), lambda i: (i, 0)),
            pl.BlockSpec((tm, LANE), lambda i: (i, 0)),
        ],
        out_shape=[
            jax.ShapeDtypeStruct((t, ATT_HEADS * HEAD_PAD), BF16),
            jax.ShapeDtypeStruct((t, ATT_HEADS * (QK_NOPE + V_DIM)), BF16),
            jax.ShapeDtypeStruct((t, LANE), BF16),
        ],
        compiler_params=_cparams(("parallel",)),
        name="upproj",
    )(proj, proj, proj, proj, qn, kvn, wq, wqr, wkv, cos_t, sin_t)


ATT_HPS = 2
PAD_BIAS = -1e30


def _attn_kernel(q_ref, kv_ref, kpe_ref, o_ref, k_sc, v_sc):
    @pl.when(pl.program_id(2) == 0)
    def _():
        n_pad = L_PAD - L_TOK
        pad_lane = lax.broadcasted_iota(jnp.int32, (n_pad, LANE), 1)
        k_pad = jnp.where(pad_lane == LANE - 1, PAD_BIAS, 0.0).astype(BF16)
        row_lane = lax.broadcasted_iota(jnp.int32, (L_TOK, LANE), 1)
        ones_col = jnp.where(row_lane == 0, 1.0, 0.0).astype(BF16)
        for hh in range(ATT_HPS):
            lo = hh * HEAD_PAD
            k_sc[hh, 0:L_TOK, 0:LANE] = kv_ref[:, lo:lo + LANE]
            k_sc[hh, 0:L_TOK, LANE:HEAD_PAD] = kpe_ref[...]
            k_sc[hh, L_TOK:L_PAD, 0:LANE] = jnp.zeros((n_pad, LANE), BF16)
            k_sc[hh, L_TOK:L_PAD, LANE:HEAD_PAD] = k_pad
            v_sc[hh, 0:L_TOK, 0:V_DIM] = kv_ref[:, lo + LANE:lo + HEAD_PAD]
            v_sc[hh, 0:L_TOK, V_DIM:HEAD_PAD] = ones_col
            v_sc[hh, L_TOK:L_PAD, :] = jnp.zeros((n_pad, HEAD_PAD), BF16)

    for hh in range(ATT_HPS):
        q = q_ref[:, hh * HEAD_PAD:(hh + 1) * HEAD_PAD]
        s = lax.dot_general(q, k_sc[hh], (((1,), (1,)), ((), ())),
                            preferred_element_type=F32)
        m = jnp.max(s, axis=-1, keepdims=True)
        p = jnp.exp2(s - m).astype(BF16)
        o = _bdot(p, v_sc[hh])
        o_ref[:, hh * V_DIM:(hh + 1) * V_DIM] = o[:, :V_DIM] / o[:, V_DIM:V_DIM + 1]


def _attention(q, kv, kpe, tq):
    nq = L_TOK // tq
    hps = ATT_HPS
    return pl.pallas_call(
        _attn_kernel,
        grid=(BATCH, ATT_HEADS // hps, nq),
        in_specs=[
            pl.BlockSpec((tq, hps * HEAD_PAD), lambda b, h, i: (b * nq + i, h)),
            pl.BlockSpec((L_TOK, hps * HEAD_PAD), lambda b, h, i: (b, h)),
            pl.BlockSpec((L_TOK, LANE), lambda b, h, i: (b, 0)),
        ],
        out_specs=pl.BlockSpec((tq, hps * V_DIM), lambda b, h, i: (b * nq + i, h)),
        out_shape=jax.ShapeDtypeStruct((T_TOK, ATT_WIDTH), F32),
        scratch_shapes=[pltpu.VMEM((hps, L_PAD, HEAD_PAD), BF16),
                        pltpu.VMEM((hps, L_PAD, HEAD_PAD), BF16)],
        compiler_params=_cparams(("parallel", "parallel", "arbitrary")),
        name="attention",
    )(q, kv, kpe)


def _s5_param_kernel(lre_ref, lim_ref, ls_ref, bre_ref, bim_ref,
                     lbr_o, lbi_o, l2r_o, l2i_o, bbr_o, bbi_o, lbbr_o, lbbi_o):
    lre = lre_ref[...]
    lim = lim_ref[...]
    dt = jnp.exp(ls_ref[...])
    mag = jnp.exp(lre * dt)
    ang = lim * dt
    br = mag * jnp.cos(ang)
    bi = mag * jnp.sin(ang)
    nr = br - 1.0
    den = lre * lre + lim * lim
    cr = (nr * lre + bi * lim) / den
    ci = (bi * lre - nr * lim) / den
    b_r = bre_ref[...]
    b_i = bim_ref[...]
    bbr = cr * b_r - ci * b_i
    bbi = cr * b_i + ci * b_r
    lbr_o[...] = br
    lbi_o[...] = bi
    l2r_o[...] = br * br - bi * bi
    l2i_o[...] = 2.0 * br * bi
    bbr_o[...] = bbr
    bbi_o[...] = bbi
    lbbr_o[...] = br * bbr - bi * bbi
    lbbi_o[...] = br * bbi + bi * bbr


def _s5_params(lam_re, lam_im, log_step, b_re, b_im):
    n = DEPTH * 2 * SSM_GROUPS
    lre = lam_re.reshape(n, 1, SSM_STATE)
    lim = lam_im.reshape(n, 1, SSM_STATE)
    ls = log_step.reshape(n, 1, 1)
    btr = b_re.transpose(0, 1, 2, 4, 3).reshape(n, SSM_GROUP, SSM_STATE)
    bti = b_im.transpose(0, 1, 2, 4, 3).reshape(n, SSM_GROUP, SSM_STATE)
    small = jax.ShapeDtypeStruct((n, 1, SSM_STATE), F32)
    big = jax.ShapeDtypeStruct((n, SSM_GROUP, SSM_STATE), F32)
    return pl.pallas_call(
        _s5_param_kernel,
        out_shape=[small, small, small, small, big, big, big, big],
        compiler_params=pltpu.CompilerParams(vmem_limit_bytes=VMEM_LIMIT),
        name="s5_params",
    )(lre, lim, ls, btr, bti)


def _s5_layouts(lam_re, lam_im, log_step, b_re, b_im, c_re, c_im):
    lbr, lbi, l2r, l2i, bbr, bbi, lbbr, lbbi = _s5_params(lam_re, lam_im, log_step, b_re, b_im)
    eye = jnp.eye(SSM_BLK_GROUPS, dtype=F32)
    lead = (DEPTH, 2, SSM_BLKS, SSM_BLK_GROUPS)

    def in_block(x):
        x = x.reshape(lead + (SSM_GROUP, SSM_STATE))
        return jnp.einsum('ldbgcp,gh->ldbgchp', x, eye).reshape(
            DEPTH, 2, SSM_BLKS, SSM_BLK_CH, SSM_BLK_ST)

    def out_block(x):
        x = x.reshape(lead + (SSM_GROUP, SSM_STATE))
        return jnp.einsum('ldbgcp,gh->ldbhcgp', x, eye).reshape(
            DEPTH, 2, SSM_BLKS, SSM_BLK_CH, SSM_BLK_ST)

    wb = jnp.concatenate([
        jnp.concatenate([in_block(bbr), in_block(bbi)], axis=-1),
        jnp.concatenate([in_block(lbbr), in_block(lbbi)], axis=-1)], axis=-2).astype(BF16)
    wc = jnp.concatenate([out_block(c_re), out_block(-c_im)], axis=-1).astype(BF16)

    def table(one, two):
        one = one.reshape(DEPTH, 2, SSM_BLKS, 1, SSM_BLK_ST)
        two = two.reshape(DEPTH, 2, SSM_BLKS, 1, SSM_BLK_ST)
        half = SUBLANE // 2
        fwd = jnp.concatenate([jnp.broadcast_to(one[:, 0:1], (DEPTH, 1, SSM_BLKS, half, SSM_BLK_ST)),
                               jnp.broadcast_to(two[:, 0:1], (DEPTH, 1, SSM_BLKS, half, SSM_BLK_ST))], axis=3)
        bwd = jnp.concatenate([jnp.broadcast_to(two[:, 1:2], (DEPTH, 1, SSM_BLKS, half, SSM_BLK_ST)),
                               jnp.broadcast_to(one[:, 1:2], (DEPTH, 1, SSM_BLKS, half, SSM_BLK_ST))], axis=3)
        return jnp.concatenate([fwd, bwd], axis=1)

    return wb, wc, table(lbr, l2r), table(lbi, l2i)


def _s5_scan_kernel(uf_ref, ub_ref, wb_ref, wc_ref, tre_ref, tim_ref, yf_ref, yb_ref,
                    tmaj, xs, carry):
    @pl.when(pl.program_id(1) == 0)
    def _():
        carry[...] = jnp.zeros(carry.shape, F32)

    n_tiles = S5_ROWS // SUBLANE
    half = SUBLANE // 2
    io_refs = (uf_ref, ub_ref, yf_ref, yb_ref)

    def time_major(blk, d):
        ch = slice(blk * SSM_BLK_CH, (blk + 1) * SSM_BLK_CH)
        for b in range(BATCH):
            tmaj[blk, d, pl.ds(b, S5_STEPS, stride=BATCH), :] = io_refs[d][b, :, ch]
        return tmaj[blk, d]

    def batch_major(y, blk, d):
        ch = slice(blk * SSM_BLK_CH, (blk + 1) * SSM_BLK_CH)
        tmaj[blk, 2 + d] = y
        for b in range(BATCH):
            io_refs[2 + d][b, :, ch] = tmaj[blk, 2 + d, pl.ds(b, S5_STEPS, stride=BATCH), :]

    def paired_lhs(u, take_upper):
        u3 = u.reshape(n_tiles, SUBLANE, SSM_BLK_CH)
        swapped = pltpu.roll(u3, half, axis=1)
        sub = lax.broadcasted_iota(jnp.int32, u3.shape, 1)
        keep = (sub >= half) if take_upper else (sub < half)
        nb = jnp.where(keep, swapped, 0.0).reshape(S5_ROWS, SSM_BLK_CH)
        return jnp.concatenate([u, nb], axis=1).astype(BF16)

    for blk in range(S5_NB):
        for d in range(2):
            xs[blk, d] = _bdot(paired_lhs(time_major(blk, d), d == 0), wb_ref[0, d, blk])

    lower = lax.broadcasted_iota(jnp.int32, (SUBLANE, 2 * LANE), 0) < half
    for blk in range(S5_NB):
        for hh in range(SSM_BLK_ST // (2 * LANE)):
            lo = hh * 2 * LANE
            re = slice(lo, lo + 2 * LANE)
            im = slice(SSM_BLK_ST + lo, SSM_BLK_ST + lo + 2 * LANE)
            afr, afi = tre_ref[0, 0, blk, :, re], tim_ref[0, 0, blk, :, re]
            abr, abi = tre_ref[0, 1, blk, :, re], tim_ref[0, 1, blk, :, re]
            hfr, hfi, hbr, hbi = (carry[blk, idx, :, re] for idx in range(4))
            for k in range(n_tiles):
                rf = slice(k * SUBLANE, (k + 1) * SUBLANE)
                pr = jnp.where(lower, pltpu.roll(hfr, half, axis=0), hfr)
                pi = jnp.where(lower, pltpu.roll(hfi, half, axis=0), hfi)
                hfr = xs[blk, 0, rf, re] + (afr * pr - afi * pi)
                hfi = xs[blk, 0, rf, im] + (afr * pi + afi * pr)
                xs[blk, 0, rf, re] = hfr
                xs[blk, 0, rf, im] = hfi
                kb = n_tiles - 1 - k
                rb = slice(kb * SUBLANE, (kb + 1) * SUBLANE)
                qr = jnp.where(lower, hbr, pltpu.roll(hbr, half, axis=0))
                qi = jnp.where(lower, hbi, pltpu.roll(hbi, half, axis=0))
                hbr = xs[blk, 1, rb, re] + (abr * qr - abi * qi)
                hbi = xs[blk, 1, rb, im] + (abr * qi + abi * qr)
                xs[blk, 1, rb, re] = hbr
                xs[blk, 1, rb, im] = hbi
            for idx, val in enumerate((hfr, hfi, hbr, hbi)):
                carry[blk, idx, :, re] = val

    for blk in range(S5_NB):
        for d in range(2):
            y = lax.dot_general(xs[blk, d].astype(BF16), wc_ref[0, d, blk],
                                (((1,), (1,)), ((), ())), preferred_element_type=F32)
            batch_major(y, blk, d)


def _s5_scan(proj, wb, wc, tre, tim, layer):
    last = S5_CHUNKS - 1
    width = S5_NB * SSM_BLK_CH
    u_blk0 = U_COL0 // width
    proj3 = proj.reshape(BATCH, L_TOK, PROJ_COLS)
    y_shape = jax.ShapeDtypeStruct((BATCH, L_TOK, SSM_WIDTH), F32)
    per_blk = lambda *tail: pl.BlockSpec((1, 2, S5_NB) + tail, lambda j, c: (layer, 0, j, 0, 0))
    yf, yb = pl.pallas_call(
        _s5_scan_kernel,
        grid=(SSM_BLKS // S5_NB, S5_CHUNKS),
        in_specs=[
            pl.BlockSpec((BATCH, S5_STEPS, width), lambda j, c: (0, c, u_blk0 + j)),
            pl.BlockSpec((BATCH, S5_STEPS, width), lambda j, c: (0, last - c, u_blk0 + j)),
            per_blk(2 * SSM_BLK_CH, 2 * SSM_BLK_ST),
            per_blk(SSM_BLK_CH, 2 * SSM_BLK_ST),
            per_blk(SUBLANE, SSM_BLK_ST),
            per_blk(SUBLANE, SSM_BLK_ST),
        ],
        out_specs=[
            pl.BlockSpec((BATCH, S5_STEPS, width), lambda j, c: (0, c, j)),
            pl.BlockSpec((BATCH, S5_STEPS, width), lambda j, c: (0, last - c, j)),
        ],
        out_shape=[y_shape, y_shape],
        scratch_shapes=[
            pltpu.VMEM((S5_NB, 4, S5_ROWS, SSM_BLK_CH), F32),
            pltpu.VMEM((S5_NB, 2, S5_ROWS, 2 * SSM_BLK_ST), F32),
            pltpu.VMEM((S5_NB, 4, SUBLANE, SSM_BLK_ST), F32),
        ],
        compiler_params=_cparams(("parallel", "arbitrary")),
        name="s5_scan",
    )(proj3, proj3, wb, wc, tre, tim)
    return yf.reshape(T_TOK, SSM_WIDTH), yb.reshape(T_TOK, SSM_WIDTH)


def _gelu_tanh(x):
    return 0.5 * x * (1.0 + jnp.tanh(math.sqrt(2.0 / math.pi) * (x + 0.044715 * (x * x * x))))


def _glu_kernel(yf_ref, yb_ref, u_ref, d_ref, w_ref, gn_ref, o_ref):
    y = (yf_ref[...] + yb_ref[...]) + d_ref[0] * u_ref[...]
    g = _gelu_tanh(y)
    z = _bdot(g.astype(BF16), w_ref[0].astype(BF16))
    o_ref[...] = _rms(g * jax.nn.sigmoid(z), gn_ref[0]).astype(BF16)


def _glu(yf, yb, proj, d, w, gn, layer, tm):
    t = yf.shape[0]
    row = pl.BlockSpec((tm, SSM_WIDTH), lambda i: (i, 0))
    vec = pl.BlockSpec((1, 1, SSM_WIDTH), lambda i: (layer, 0, 0))
    return pl.pallas_call(
        _glu_kernel,
        grid=(t // tm,),
        in_specs=[row, row,
                  pl.BlockSpec((tm, SSM_WIDTH), lambda i: (i, U_COL0 // SSM_WIDTH)),
                  vec,
                  pl.BlockSpec((1, SSM_WIDTH, SSM_WIDTH), lambda i: (layer, 0, 0)),
                  vec],
        out_specs=row,
        out_shape=jax.ShapeDtypeStruct((t, SSM_WIDTH), BF16),
        compiler_params=_cparams(("parallel",)),
        name="s5_glu",
    )(yf, yb, proj, d.reshape(DEPTH, 1, SSM_WIDTH), w, gn.reshape(DEPTH, 1, SSM_WIDTH))


def _outproj_kernel(h_ref, a_ref, sn_ref, ga_ref, wa_ref, ws_ref, o_ref, an_ref):
    @pl.when(pl.program_id(1) == 0)
    def _():
        an_ref[...] = _rms(a_ref[...], ga_ref[0]).astype(BF16)

    o_ref[...] = (h_ref[...] + _bdot(an_ref[...], wa_ref[0].astype(BF16))
                  + _bdot(sn_ref[...], ws_ref[0].astype(BF16)))


def _outproj(h, att, ssm_n, ga, w, layer, tm, tn):
    t, d = h.shape
    return pl.pallas_call(
        _outproj_kernel,
        grid=(t // tm, d // tn),
        in_specs=[
            pl.BlockSpec((tm, tn), lambda i, j: (i, j)),
            pl.BlockSpec((tm, ATT_WIDTH), lambda i, j: (i, 0)),
            pl.BlockSpec((tm, SSM_WIDTH), lambda i, j: (i, 0)),
            pl.BlockSpec((1, 1, ATT_WIDTH), lambda i, j: (layer, 0, 0)),
            pl.BlockSpec((1, ATT_WIDTH, tn), lambda i, j: (layer, 0, j)),
            pl.BlockSpec((1, SSM_WIDTH, tn), lambda i, j: (layer, 1, j)),
        ],
        out_specs=pl.BlockSpec((tm, tn), lambda i, j: (i, j)),
        out_shape=jax.ShapeDtypeStruct((t, d), F32),
        scratch_shapes=[pltpu.VMEM((tm, ATT_WIDTH), BF16)],
        input_output_aliases={0: 0},
        compiler_params=_cparams(("parallel", "arbitrary")),
        name="out_proj",
    )(h, att, ssm_n, ga.reshape(DEPTH, 1, ATT_WIDTH), w, w)


FFN_SUB = 256


def _ffn_kernel(x_ref, g_ref, wg_ref, wu_ref, wd_ref, gn_ref, o_ref, nxt_ref, hn_ref):
    j = pl.program_id(1)

    @pl.when(j == 0)
    def _():
        x = x_ref[...]
        hn_ref[...] = _rms(x, g_ref[0]).astype(BF16)
        o_ref[...] = x

    hn = hn_ref[...]
    tf = wg_ref.shape[2]
    part = None
    for lo in range(0, tf, FFN_SUB):
        cols = slice(lo, lo + FFN_SUB)
        a = _bdot(hn, wg_ref[0, :, cols].astype(BF16))
        b = _bdot(hn, wu_ref[0, :, cols].astype(BF16))
        hid = ((a * jax.nn.sigmoid(a)) * b).astype(BF16)
        down = _bdot(hid, wd_ref[0, cols, :].astype(BF16))
        part = down if part is None else part + down
    o_ref[...] += part

    @pl.when(j == pl.num_programs(1) - 1)
    def _():
        nxt_ref[...] = _rms(o_ref[...], gn_ref[0]).astype(BF16)


def _ffn(h, g, gn, wg, wu, wd, layer, tm, tf):
    t, d = h.shape
    f = wg.shape[2]
    idx = layer // 2
    once = pl.Buffered(1)
    row = pl.BlockSpec((tm, d), lambda i, j: (i, 0), pipeline_mode=once)
    return pl.pallas_call(
        _ffn_kernel,
        grid=(t // tm, f // tf),
        in_specs=[
            row,
            pl.BlockSpec((1, 1, d), lambda i, j: (layer, 0, 0)),
            pl.BlockSpec((1, d, tf), lambda i, j: (idx, 0, j)),
            pl.BlockSpec((1, d, tf), lambda i, j: (idx, 0, j)),
            pl.BlockSpec((1, tf, d), lambda i, j: (idx, j, 0)),
            pl.BlockSpec((1, 1, d), lambda i, j: (layer + 1, 0, 0)),
        ],
        out_specs=[row, row],
        out_shape=[jax.ShapeDtypeStruct((t, d), F32), jax.ShapeDtypeStruct((t, d), BF16)],
        scratch_shapes=[pltpu.VMEM((tm, d), BF16)],
        input_output_aliases={0: 0},
        compiler_params=_cparams(("parallel", "arbitrary")),
        name="dense_ffn",
    )(h, g.reshape(DEPTH, 1, d), wg, wu, wd, gn.reshape(DEPTH, 1, d))


MOE_TM = 344
MOE_NT = (2 * T_TOK) // MOE_TM + N_EXPERTS
MOE_NP = MOE_NT * MOE_TM
DMA_FANOUT = 4


def _split_bf16(x):
    hi = x.astype(BF16)
    lo = (x - hi.astype(F32)).astype(BF16)
    return hi, lo


def _lane_pick(x, lane, k):
    return jnp.sum(jnp.where(lane == k, x, 0.0), axis=-1, keepdims=True)


ROW_SLABS = D_MODEL // LANE


def _row_slab(ref, c, rows):
    return ref.at[pl.ds(c, rows, stride=ROW_SLABS), :]


def _router_kernel(x_ref, g_ref, wr_ref, sel_ref, wts_ref, rows_ref):
    x = x_ref[...]
    for c in range(ROW_SLABS):
        _row_slab(rows_ref, c, x.shape[0])[...] = x[:, c * LANE:(c + 1) * LANE]
    hn = _rms(x, g_ref[0])
    xh, xl = _split_bf16(hn)
    wh, wl = _split_bf16(wr_ref[0])
    logits = _bdot(xh, wh) + (_bdot(xh, wl) + _bdot(xl, wh)) + _bdot(xl, wl)
    lane = lax.broadcasted_iota(jnp.int32, logits.shape, 1).astype(F32)
    neg = jnp.float32(-jnp.inf)
    logits = jnp.where(lane < N_EXPERTS, logits, neg)
    v1 = jnp.max(logits, axis=-1, keepdims=True)
    i1 = jnp.min(jnp.where(logits == v1, lane, float(LANE)), axis=-1, keepdims=True)
    rest = jnp.where(lane == i1, neg, logits)
    v2 = jnp.max(rest, axis=-1, keepdims=True)
    i2 = jnp.min(jnp.where(rest == v2, lane, float(LANE)), axis=-1, keepdims=True)
    e2 = jnp.exp(v2 - v1)
    w1 = 1.0 / (1.0 + e2)
    w2 = e2 / (1.0 + e2)
    sel_ref[...] = jnp.where((lane == i1) | (lane == i2), 1.0, 0.0)
    wts_ref[...] = jnp.where(lane == 0, w1, jnp.where(lane == 1, w2,
                             jnp.where(lane == 2, i1, jnp.where(lane == 3, i2, 0.0))))


def _router(h, g, wr_pad, layer, tm):
    t, d = h.shape
    idx = layer // 2
    row = pl.BlockSpec((tm, LANE), lambda i: (i, 0))
    tab = jax.ShapeDtypeStruct((t, LANE), F32)
    return pl.pallas_call(
        _router_kernel,
        grid=(t // tm,),
        in_specs=[
            pl.BlockSpec((tm, d), lambda i: (i, 0)),
            pl.BlockSpec((1, 1, d), lambda i: (layer, 0, 0)),
            pl.BlockSpec((1, d, LANE), lambda i: (idx, 0, 0)),
        ],
        out_specs=[row, row, pl.BlockSpec((tm * ROW_SLABS, LANE), lambda i: (i, 0))],
        out_shape=[tab, tab, jax.ShapeDtypeStruct((t * ROW_SLABS, LANE), F32)],
        compiler_params=_cparams(("parallel",)),
        name="moe_router",
    )(h, g.reshape(DEPTH, 1, d), wr_pad)


def _positions_kernel(sel_ref, wts_ref, pos_ref, meta_ref, cnt_ref, off_ref):
    p = pl.program_id(0)
    i = pl.program_id(1)
    tm = sel_ref.shape[0]
    lane = lax.broadcasted_iota(jnp.int32, (1, LANE), 1).astype(F32)
    sel = sel_ref[...]

    @pl.when((p == 0) & (i == 0))
    def _():
        cnt_ref[...] = jnp.zeros((1, LANE), F32)

    @pl.when(p == 0)
    def _():
        cnt_ref[...] += jnp.sum(sel, axis=0, keepdims=True)

    @pl.when((p == 1) & (i == 0))
    def _():
        cnt = cnt_ref[...]
        tiles = jnp.zeros((1, LANE), F32)
        for k in range(MOE_NT):
            tiles = tiles + jnp.where(cnt > float(k * MOE_TM), 1.0, 0.0)
        padded = tiles * float(MOE_TM)
        off = jnp.zeros((1, LANE), F32)
        for e in range(N_EXPERTS):
            off = off + jnp.where(lane > e, _lane_pick(padded, lane, e), 0.0)
        end = off + padded
        tile_start = lane * float(MOE_TM)
        owner = jnp.zeros((1, LANE), F32)
        for e in range(N_EXPERTS):
            owner = owner + jnp.where(tile_start >= _lane_pick(end, lane, e), 1.0, 0.0)
        owner = jnp.minimum(owner, float(N_EXPERTS - 1))
        used = jnp.sum(tiles, axis=-1, keepdims=True)
        off_ref[...] = off
        cnt_ref[...] = jnp.zeros((1, LANE), F32)
        row = lax.broadcasted_iota(jnp.int32, (SUBLANE, LANE), 0)
        meta = jnp.where(row == 0, owner, jnp.where(row == 1, used, jnp.where(row == 2, cnt, off)))
        meta_ref[...] = meta.astype(jnp.int32)

    @pl.when(p == 1)
    def _():
        r = lax.broadcasted_iota(jnp.int32, (tm, tm), 0)
        c = lax.broadcasted_iota(jnp.int32, (tm, tm), 1)
        earlier = jnp.where(c < r, 1.0, 0.0).astype(BF16)
        rank = _bdot(earlier, sel.astype(BF16)) + cnt_ref[...]
        slot = off_ref[...] + rank
        wts = wts_ref[...]
        lane_t = lax.broadcasted_iota(jnp.int32, (tm, LANE), 1).astype(F32)
        p1 = _lane_pick(slot, lane_t, _lane_pick(wts, lane_t, 2))
        p2 = _lane_pick(slot, lane_t, _lane_pick(wts, lane_t, 3))
        pos_ref[...] = jnp.where(lane_t == 0, p1, jnp.where(lane_t == 1, p2, 0.0)).astype(jnp.int32)
        cnt_ref[...] += jnp.sum(sel, axis=0, keepdims=True)


def _positions(sel, wts, tm):
    t = sel.shape[0]
    row = pl.BlockSpec((tm, LANE), lambda p, i: (i, 0))
    return pl.pallas_call(
        _positions_kernel,
        grid=(2, t // tm),
        in_specs=[row, row],
        out_specs=[
            pl.BlockSpec((tm, LANE), lambda p, i: (i * p, 0)),
            pl.BlockSpec((SUBLANE, LANE), lambda p, i: (0, 0)),
        ],
        out_shape=[jax.ShapeDtypeStruct((t, LANE), jnp.int32),
                   jax.ShapeDtypeStruct((SUBLANE, LANE), jnp.int32)],
        scratch_shapes=[pltpu.VMEM((1, LANE), F32), pltpu.VMEM((1, LANE), F32)],
        compiler_params=_cparams(("arbitrary", "arbitrary")),
        name="moe_positions",
    )(sel, wts)


def _experts_kernel(pos1_ref, pos2_ref, owner_ref, used_ref, rows_ref, g_ref, wg_ref, wu_ref, wd_ref,
                    o_ref, src_ref, xbuf, sems):
    i = pl.program_id(0)
    used = used_ref[0]
    cur = lax.rem(i, 2)

    def row_copy(tok, r, buf):
        return pltpu.make_async_copy(rows_ref.at[pl.ds(tok * ROW_SLABS, ROW_SLABS), :],
                                     xbuf.at[buf, pl.ds(r * ROW_SLABS, ROW_SLABS), :], sems.at[buf])

    def fetch_tile(tile, buf):
        def issue(q, c):
            for u in range(DMA_FANOUT):
                r = q * DMA_FANOUT + u
                row_copy(src_ref[tile * MOE_TM + r], r, buf).start(priority=u % 2)
            return c

        lax.fori_loop(0, MOE_TM // DMA_FANOUT, issue, 0)

    @pl.when(i == 0)
    def _():
        def clear(p, c):
            src_ref[p] = 0
            return c

        lax.fori_loop(0, MOE_NP, clear, 0, unroll=8)

        def fill(t, c):
            src_ref[pos1_ref[t]] = t
            src_ref[pos2_ref[t]] = t
            return c

        lax.fori_loop(0, T_TOK, fill, 0, unroll=4)
        fetch_tile(0, 0)

    @pl.when(i + 1 < used)
    def _():
        fetch_tile(i + 1, 1 - cur)

    live = i < used

    @pl.when(live)
    def _():
        def land(r, c):
            row_copy(0, 0, cur).wait()
            return c

        lax.fori_loop(0, MOE_TM, land, 0, unroll=4)
        x_ref = xbuf.at[cur]
        slabs = [_row_slab(x_ref, c, MOE_TM)[...] for c in range(ROW_SLABS)]
        ssq = slabs[0] * slabs[0]
        for c in range(1, ROW_SLABS):
            ssq = ssq + slabs[c] * slabs[c]
        inv = lax.rsqrt(jnp.sum(ssq, axis=-1, keepdims=True) / D_MODEL + EPS)
        g = g_ref[0]
        hn = jnp.concatenate([slabs[c] * inv * g[:, c * LANE:(c + 1) * LANE]
                              for c in range(ROW_SLABS)], axis=1).astype(BF16)
        a = _bdot(hn, wg_ref[0, 0])
        b = _bdot(hn, wu_ref[0, 0])
        hid = ((a * jax.nn.sigmoid(a)) * b).astype(BF16)
        res = _bdot(hid, wd_ref[0, 0])
        for c in range(ROW_SLABS):
            _row_slab(o_ref, c, MOE_TM)[...] = res[:, c * LANE:(c + 1) * LANE]

    @pl.when(jnp.logical_not(live))
    def _():
        o_ref[...] = jnp.zeros(o_ref.shape, F32)


def _experts(rows, pos1, pos2, owner, used, g, wg, wu, wd, layer):
    d = D_MODEL
    idx = layer // 2
    blk = MOE_TM * ROW_SLABS

    w_map = lambda i, p1, p2, o, u: (idx, o[jnp.minimum(i, u[0] - 1)], 0, 0)
    return pl.pallas_call(
        _experts_kernel,
        grid_spec=pltpu.PrefetchScalarGridSpec(
            num_scalar_prefetch=4,
            grid=(MOE_NT,),
            in_specs=[
                pl.BlockSpec(memory_space=pl.ANY),
                pl.BlockSpec((1, 1, d), lambda i, p1, p2, o, u: (layer, 0, 0)),
                pl.BlockSpec((1, 1, d, D_FF_EXPERT), w_map),
                pl.BlockSpec((1, 1, d, D_FF_EXPERT), w_map),
                pl.BlockSpec((1, 1, D_FF_EXPERT, d), w_map),
            ],
            out_specs=pl.BlockSpec((blk, LANE), lambda i, p1, p2, o, u: (i, 0)),
            scratch_shapes=[
                pltpu.SMEM((MOE_NP,), jnp.int32),
                pltpu.VMEM((2, blk, LANE), F32),
                pltpu.SemaphoreType.DMA((2,)),
            ],
        ),
        out_shape=jax.ShapeDtypeStruct((MOE_NP * ROW_SLABS, LANE), F32),
        compiler_params=_cparams(("arbitrary",)),
        name="moe_experts",
    )(pos1, pos2, owner, used, rows, g.reshape(DEPTH, 1, d), wg, wu, wd)


def _combine_kernel(pos1_ref, pos2_ref, h_ref, wts_ref, ys_ref, gn_ref, o_ref, nxt_ref,
                    y1, y2, sem1, sem2):
    tm = h_ref.shape[0]
    base = pl.program_id(0) * tm

    def row_copy(slot, r, buf, sem):
        return pltpu.make_async_copy(ys_ref.at[pl.ds(slot * ROW_SLABS, ROW_SLABS), :],
                                     buf.at[pl.ds(r * ROW_SLABS, ROW_SLABS), :], sem)

    def fetch(q, c):
        for u in range(DMA_FANOUT):
            r = q * DMA_FANOUT + u
            row_copy(pos1_ref[base + r], r, y1, sem1).start(priority=0)
            row_copy(pos2_ref[base + r], r, y2, sem2).start(priority=1)
        return c

    lax.fori_loop(0, tm // DMA_FANOUT, fetch, 0)

    def land(r, c):
        row_copy(0, 0, y1, sem1).wait()
        row_copy(0, 0, y2, sem2).wait()
        return c

    lax.fori_loop(0, tm, land, 0, unroll=4)
    wts = wts_ref[...]
    lane = lax.broadcasted_iota(jnp.int32, wts.shape, 1).astype(F32)
    w1 = _lane_pick(wts, lane, 0)
    w2 = _lane_pick(wts, lane, 1)
    ssq = jnp.zeros((tm, 1), F32)
    for c in range(ROW_SLABS):
        cols = slice(c * LANE, (c + 1) * LANE)
        hc = h_ref[:, cols] + (w1 * _row_slab(y1, c, tm)[...] + w2 * _row_slab(y2, c, tm)[...])
        o_ref[:, cols] = hc
        ssq = ssq + jnp.sum(hc * hc, axis=-1, keepdims=True)
    normed = o_ref[...] * lax.rsqrt(ssq / D_MODEL + EPS) * gn_ref[...]
    if nxt_ref is None:
        o_ref[...] = normed
    else:
        nxt_ref[...] = normed.astype(BF16)


def _combine_mid_kernel(pos1_ref, pos2_ref, h_ref, wts_ref, ys_ref, gn_ref, o_ref, nxt_ref,
                        y1, y2, sem1, sem2):
    _combine_kernel(pos1_ref, pos2_ref, h_ref, wts_ref, ys_ref, gn_ref, o_ref, nxt_ref,
                    y1, y2, sem1, sem2)


def _combine_last_kernel(pos1_ref, pos2_ref, h_ref, wts_ref, ys_ref, gn_ref, o_ref,
                         y1, y2, sem1, sem2):
    _combine_kernel(pos1_ref, pos2_ref, h_ref, wts_ref, ys_ref, gn_ref, o_ref, None,
                    y1, y2, sem1, sem2)


def _combine(h, wts, ys, pos1, pos2, gn, last, tm):
    t, d = h.shape
    row = pl.BlockSpec((tm, d), lambda i, a, b: (i, 0))
    h_out = jax.ShapeDtypeStruct((t, d), F32)
    return pl.pallas_call(
        _combine_last_kernel if last else _combine_mid_kernel,
        grid_spec=pltpu.PrefetchScalarGridSpec(
            num_scalar_prefetch=2,
            grid=(t // tm,),
            in_specs=[
                row,
                pl.BlockSpec((tm, LANE), lambda i, a, b: (i, 0)),
                pl.BlockSpec(memory_space=pl.ANY),
                pl.BlockSpec((1, d), lambda i, a, b: (0, 0)),
            ],
            out_specs=row if last else [row, row],
            scratch_shapes=[
                pltpu.VMEM((tm * ROW_SLABS, LANE), F32),
                pltpu.VMEM((tm * ROW_SLABS, LANE), F32),
                pltpu.SemaphoreType.DMA(()),
                pltpu.SemaphoreType.DMA(()),
            ],
        ),
        out_shape=h_out if last else [h_out, jax.ShapeDtypeStruct((t, d), BF16)],
        input_output_aliases={2: 0},
        compiler_params=_cparams(("arbitrary",)),
        name="moe_combine",
    )(pos1, pos2, h, wts, ys, gn.reshape(1, d))


def _moe(h, g, gn, wr_pad, wg, wu, wd, layer, last):
    sel, wts, rows = _router(h, g, wr_pad, layer, tm=688)
    pos, meta = _positions(sel, wts, tm=688)
    pos1 = pos[:, 0]
    pos2 = pos[:, 1]
    owner = meta[0, :MOE_NT]
    used = meta[1, :1]
    ys = _experts(rows, pos1, pos2, owner, used, g, wg, wu, wd, layer)
    return _combine(h, wts, ys, pos1, pos2, gn, last, tm=688)


def _rope_partner(w):
    half = QK_ROPE // 2
    return jnp.concatenate([-w[..., half:], w[..., :half]], axis=-1)


def _pad_cols(w, width):
    return jnp.pad(w, [(0, 0)] * (w.ndim - 1) + [(0, width - w.shape[-1])])


def _in_proj_layout(w_in):
    kr = w_in[..., Q_LORA + KV_LORA:Q_LORA + KV_LORA + QK_ROPE]
    return jnp.concatenate([
        w_in[..., :Q_LORA + KV_LORA],
        _pad_cols(kr, LANE),
        _pad_cols(_rope_partner(kr), LANE),
        w_in[..., Q_LORA + KV_LORA + QK_ROPE:]], axis=-1)


def _q_layouts(w_uq):
    w = w_uq.reshape(DEPTH, Q_LORA, ATT_HEADS, QK_NOPE + QK_ROPE)
    main = _pad_cols(w, HEAD_PAD).reshape(DEPTH, Q_LORA, ATT_HEADS * HEAD_PAD)
    rot = _pad_cols(_rope_partner(w[..., QK_NOPE:]), LANE).reshape(DEPTH, Q_LORA, ATT_HEADS * LANE)
    return main, rot


def _rope_tables():
    inv = ROPE_THETA ** (-jnp.arange(0, QK_ROPE, 2, dtype=F32) / QK_ROPE)
    ang = jnp.arange(L_TOK, dtype=F32)[:, None] * inv[None, :]
    cos = jnp.cos(ang)
    sin = jnp.sin(ang)
    cos_t = _pad_cols(jnp.concatenate([cos, cos], axis=-1), LANE)
    sin_t = _pad_cols(jnp.concatenate([sin, sin], axis=-1), LANE)
    return cos_t, sin_t


def kernel(x, meta_tokens, mix_norm, w_in, q_norm, w_uq, kv_norm, w_ukv, ssm_lambda_re, ssm_lambda_im, ssm_log_step, ssm_b_re, ssm_b_im, ssm_c_re, ssm_c_im, ssm_d, ssm_w_glu, attn_out_norm, ssm_out_norm, w_out, ffn_norm, dense_w_gate, dense_w_up, dense_w_down, moe_router, moe_w_gate, moe_w_up, moe_w_down, final_norm):
    meta = jnp.broadcast_to(meta_tokens[None].astype(x.dtype), (BATCH, N_META, D_MODEL))
    h = jnp.concatenate([meta, x], axis=1).reshape(T_TOK, D_MODEL)
    cos_t, sin_t = _rope_tables()
    w_in_l = _in_proj_layout(w_in)
    wq_main, wq_rot = _q_layouts(w_uq)
    wr_pad = _pad_cols(moe_router, LANE)
    moe_w_gate, moe_w_up, moe_w_down = (w.astype(BF16) for w in (moe_w_gate, moe_w_up, moe_w_down))
    wb, wc, tre, tim = _s5_layouts(ssm_lambda_re, ssm_lambda_im, ssm_log_step,
                                   ssm_b_re, ssm_b_im, ssm_c_re, ssm_c_im)
    hn = _rms_cast(h, mix_norm, 0, tm=688)
    for layer in range(DEPTH):
        proj = _in_proj(hn, w_in_l, layer, tm=2064, tn=512)
        q, kv, kpe = _upproj(proj, q_norm, kv_norm, wq_main, wq_rot, w_ukv, cos_t, sin_t, layer,
                             tm=688)
        att = _attention(q, kv, kpe, tq=688)
        yf, yb = _s5_scan(proj, wb, wc, tre, tim, layer)
        ssm_n = _glu(yf, yb, proj, ssm_d, ssm_w_glu, ssm_out_norm, layer, tm=688)
        h = _outproj(h, att, ssm_n, attn_out_norm, w_out, layer, tm=1376, tn=512)
        last = layer == DEPTH - 1
        if layer % 2 == 0:
            h, hn = _ffn(h, ffn_norm, mix_norm, dense_w_gate, dense_w_up, dense_w_down, layer,
                         tm=1032, tf=512)
        elif last:
            h = _moe(h, ffn_norm, final_norm, wr_pad, moe_w_gate, moe_w_up, moe_w_down, layer, last)
        else:
            h, hn = _moe(h, ffn_norm, mix_norm[layer + 1], wr_pad, moe_w_gate, moe_w_up, moe_w_down,
                         layer, last)
    return h.reshape(BATCH, L_TOK, D_MODEL)[:, N_META:]
```

```python
import functools
import math

import jax
import jax.numpy as jnp
from jax import lax
from jax.experimental import pallas as pl
from jax.experimental.pallas import tpu as pltpu

F32 = jnp.float32
BF16 = jnp.bfloat16

D_MODEL = 2048
BATCH = 4
SEQ = 2048
DEPTH = 4
N_META = 16
EPS = 1e-6
ATT_HEADS = 8
QK_NOPE = 128
QK_ROPE = 64
V_DIM = 128
Q_LORA = 512
KV_LORA = 256
ROPE_THETA = 10000.0
ATT_WIDTH = ATT_HEADS * V_DIM
SSM_GROUP = 16
SSM_WIDTH = D_MODEL - ATT_WIDTH
SSM_GROUPS = SSM_WIDTH // SSM_GROUP
SSM_STATE = 64
D_FF = 5632
N_EXPERTS = 8
D_FF_EXPERT = 1408

L_TOK = N_META + SEQ
T_TOK = L_TOK * BATCH
LANE = 128
SUBLANE = 8
HEAD_PAD = 256
L_PAD = 2176
PROJ_COLS = 2048
U_COL0 = 1024
VMEM_LIMIT = 58 * 1024 * 1024

SSM_BLK_GROUPS = 8
SSM_BLKS = SSM_GROUPS // SSM_BLK_GROUPS
SSM_BLK_CH = SSM_BLK_GROUPS * SSM_GROUP
SSM_BLK_ST = SSM_BLK_GROUPS * SSM_STATE
S5_CHUNKS = 6
S5_STEPS = L_TOK // S5_CHUNKS
S5_ROWS = S5_STEPS * BATCH
S5_NB = 2
S5_SCAN_LANES = 256


def _cparams(sem):
    return pltpu.CompilerParams(dimension_semantics=sem, vmem_limit_bytes=VMEM_LIMIT)


def _rms(x, g):
    ms = jnp.mean(x * x, axis=-1, keepdims=True)
    return x * lax.rsqrt(ms + EPS) * g


def _bdot(a, b):
    return jnp.dot(a, b, preferred_element_type=F32)


def _rms_cast_kernel(x_ref, g_ref, o_ref):
    o_ref[...] = _rms(x_ref[...], g_ref[0]).astype(BF16)


def _rms_cast(x, g, layer, tm):
    t, d = x.shape
    return pl.pallas_call(
        _rms_cast_kernel,
        grid=(t // tm,),
        in_specs=[pl.BlockSpec((tm, d), lambda i: (i, 0)),
                  pl.BlockSpec((1, 1, d), lambda i: (layer, 0, 0))],
        out_specs=pl.BlockSpec((tm, d), lambda i: (i, 0)),
        out_shape=jax.ShapeDtypeStruct((t, d), BF16),
        compiler_params=_cparams(("parallel",)),
        name="rms_cast",
    )(x, g.reshape(DEPTH, 1, d))


def _in_proj_kernel(x_ref, w_ref, o_ref):
    o_ref[...] = _bdot(x_ref[...], w_ref[0].astype(BF16))


def _in_proj(hn, w, layer, tm, tn):
    t, d = hn.shape
    n = w.shape[2]
    return pl.pallas_call(
        _in_proj_kernel,
        grid=(t // tm, n // tn),
        in_specs=[
            pl.BlockSpec((tm, d), lambda i, j: (i, 0)),
            pl.BlockSpec((1, d, tn), lambda i, j: (layer, 0, j)),
        ],
        out_specs=pl.BlockSpec((tm, tn), lambda i, j: (i, j)),
        out_shape=jax.ShapeDtypeStruct((t, n), F32),
        compiler_params=_cparams(("parallel", "arbitrary")),
        name="in_proj",
    )(hn, w)


def _upproj_kernel(cq_ref, ckv_ref, kr_ref, krot_ref, qn_ref, kvn_ref, wq_ref, wqr_ref,
                   wkv_ref, cos_ref, sin_ref, q_ref, kv_ref, kpe_ref):
    scale = (QK_NOPE + QK_ROPE) ** -0.5 * math.log2(math.e)
    cqn = _rms(cq_ref[...], qn_ref[0]).astype(BF16)
    a = _bdot(cqn, wq_ref[0].astype(BF16))
    r = _bdot(cqn, wqr_ref[0].astype(BF16))
    c = cos_ref[...]
    s = sin_ref[...]
    last_lane = lax.broadcasted_iota(jnp.int32, c.shape, 1) == LANE - 1
    for h in range(ATT_HEADS):
        lo = h * HEAD_PAD
        q_ref[:, lo:lo + LANE] = (a[:, lo:lo + LANE] * scale).astype(BF16)
        rope = a[:, lo + LANE:lo + HEAD_PAD] * c + r[:, h * LANE:(h + 1) * LANE] * s
        q_ref[:, lo + LANE:lo + HEAD_PAD] = jnp.where(last_lane, 1.0, rope * scale).astype(BF16)
    ckvn = _rms(ckv_ref[...], kvn_ref[0]).astype(BF16)
    kv_ref[...] = _bdot(ckvn, wkv_ref[0].astype(BF16)).astype(BF16)
    kpe_ref[...] = (kr_ref[...] * c + krot_ref[...] * s).astype(BF16)


def _upproj(proj, qn, kvn, wq, wqr, wkv, cos_t, sin_t, layer, tm):
    t = proj.shape[0]
    full = lambda a: pl.BlockSpec((1,) + a.shape[1:], lambda i: (layer, 0, 0))
    qn = qn.reshape(DEPTH, 1, Q_LORA)
    kvn = kvn.reshape(DEPTH, 1, KV_LORA)
    pos_blocks = L_TOK // tm
    return pl.pallas_call(
        _upproj_kernel,
        grid=(t // tm,),
        in_specs=[
            pl.BlockSpec((tm, Q_LORA), lambda i: (i, 0)),
            pl.BlockSpec((tm, KV_LORA), lambda i: (i, Q_LORA // KV_LORA)),
            pl.BlockSpec((tm, LANE), lambda i: (i, (Q_LORA + KV_LORA) // LANE)),
            pl.BlockSpec((tm, LANE), lambda i: (i, (Q_LORA + KV_LORA) // LANE + 1)),
            full(qn),
            full(kvn),
            full(wq),
            full(wqr),
            full(wkv),
            pl.BlockSpec((tm, LANE), lambda i: (i % pos_blocks, 0)),
            pl.BlockSpec((tm, LANE), lambda i: (i % pos_blocks, 0)),
        ],
        out_specs=[
            pl.BlockSpec((tm, ATT_HEADS * HEAD_PAD), lambda i: (i, 0)),
            pl.BlockSpec((tm, ATT_HEADS * (QK_NOPE + V_DIM)), lambda i: (i, 0)),
            pl.BlockSpec((tm, LANE), lambda i: (i, 0)),
        ],
        out_shape=[
            jax.ShapeDtypeStruct((t, ATT_HEADS * HEAD_PAD), BF16),
            jax.ShapeDtypeStruct((t, ATT_HEADS * (QK_NOPE + V_DIM)), BF16),
            jax.ShapeDtypeStruct((t, LANE), BF16),
        ],
        compiler_params=_cparams(("parallel",)),
        name="upproj",
    )(proj, proj, proj, proj, qn, kvn, wq, wqr, wkv, cos_t, sin_t)


ATT_HPS = 2
PAD_BIAS = -1e30


def _attn_kernel(q_ref, kv_ref, kpe_ref, o_ref, k_sc, v_sc):
    @pl.when(pl.program_id(2) == 0)
    def _():
        n_pad = L_PAD - L_TOK
        pad_lane = lax.broadcasted_iota(jnp.int32, (n_pad, LANE), 1)
        k_pad = jnp.where(pad_lane == LANE - 1, PAD_BIAS, 0.0).astype(BF16)
        row_lane = lax.broadcasted_iota(jnp.int32, (L_TOK, LANE), 1)
        ones_col = jnp.where(row_lane == 0, 1.0, 0.0).astype(BF16)
        for hh in range(ATT_HPS):
            lo = hh * HEAD_PAD
            k_sc[hh, 0:L_TOK, 0:LANE] = kv_ref[:, lo:lo + LANE]
            k_sc[hh, 0:L_TOK, LANE:HEAD_PAD] = kpe_ref[...]
            k_sc[hh, L_TOK:L_PAD, 0:LANE] = jnp.zeros((n_pad, LANE), BF16)
            k_sc[hh, L_TOK:L_PAD, LANE:HEAD_PAD] = k_pad
            v_sc[hh, 0:L_TOK, 0:V_DIM] = kv_ref[:, lo + LANE:lo + HEAD_PAD]
            v_sc[hh, 0:L_TOK, V_DIM:HEAD_PAD] = ones_col
            v_sc[hh, L_TOK:L_PAD, :] = jnp.zeros((n_pad, HEAD_PAD), BF16)

    for hh in range(ATT_HPS):
        q = q_ref[:, hh * HEAD_PAD:(hh + 1) * HEAD_PAD]
        s = lax.dot_general(q, k_sc[hh], (((1,), (1,)), ((), ())),
                            preferred_element_type=F32)
        m = jnp.max(s, axis=-1, keepdims=True)
        p = jnp.exp2(s - m).astype(BF16)
        o = _bdot(p, v_sc[hh])
        o_ref[:, hh * V_DIM:(hh + 1) * V_DIM] = o[:, :V_DIM] / o[:, V_DIM:V_DIM + 1]


def _attention(q, kv, kpe, tq):
    nq = L_TOK // tq
    hps = ATT_HPS
    return pl.pallas_call(
        _attn_kernel,
        grid=(BATCH, ATT_HEADS // hps, nq),
        in_specs=[
            pl.BlockSpec((tq, hps * HEAD_PAD), lambda b, h, i: (b * nq + i, h)),
            pl.BlockSpec((L_TOK, hps * HEAD_PAD), lambda b, h, i: (b, h)),
            pl.BlockSpec((L_TOK, LANE), lambda b, h, i: (b, 0)),
        ],
        out_specs=pl.BlockSpec((tq, hps * V_DIM), lambda b, h, i: (b * nq + i, h)),
        out_shape=jax.ShapeDtypeStruct((T_TOK, ATT_WIDTH), F32),
        scratch_shapes=[pltpu.VMEM((hps, L_PAD, HEAD_PAD), BF16),
                        pltpu.VMEM((hps, L_PAD, HEAD_PAD), BF16)],
        compiler_params=_cparams(("parallel", "parallel", "arbitrary")),
        name="attention",
    )(q, kv, kpe)


def _s5_param_kernel(lre_ref, lim_ref, ls_ref, bre_ref, bim_ref,
                     lbr_o, lbi_o, l2r_o, l2i_o, bbr_o, bbi_o, lbbr_o, lbbi_o):
    lre = lre_ref[...]
    lim = lim_ref[...]
    dt = jnp.exp(ls_ref[...])
    mag = jnp.exp(lre * dt)
    ang = lim * dt
    br = mag * jnp.cos(ang)
    bi = mag * jnp.sin(ang)
    nr = br - 1.0
    den = lre * lre + lim * lim
    cr = (nr * lre + bi * lim) / den
    ci = (bi * lre - nr * lim) / den
    b_r = bre_ref[...]
    b_i = bim_ref[...]
    bbr = cr * b_r - ci * b_i
    bbi = cr * b_i + ci * b_r
    lbr_o[...] = br
    lbi_o[...] = bi
    l2r_o[...] = br * br - bi * bi
    l2i_o[...] = 2.0 * br * bi
    bbr_o[...] = bbr
    bbi_o[...] = bbi
    lbbr_o[...] = br * bbr - bi * bbi
    lbbi_o[...] = br * bbi + bi * bbr


def _s5_params(lam_re, lam_im, log_step, b_re, b_im):
    n = DEPTH * 2 * SSM_GROUPS
    lre = lam_re.reshape(n, 1, SSM_STATE)
    lim = lam_im.reshape(n, 1, SSM_STATE)
    ls = log_step.reshape(n, 1, 1)
    btr = b_re.transpose(0, 1, 2, 4, 3).reshape(n, SSM_GROUP, SSM_STATE)
    bti = b_im.transpose(0, 1, 2, 4, 3).reshape(n, SSM_GROUP, SSM_STATE)
    small = jax.ShapeDtypeStruct((n, 1, SSM_STATE), F32)
    big = jax.ShapeDtypeStruct((n, SSM_GROUP, SSM_STATE), F32)
    return pl.pallas_call(
        _s5_param_kernel,
        out_shape=[small, small, small, small, big, big, big, big],
        compiler_params=pltpu.CompilerParams(vmem_limit_bytes=VMEM_LIMIT),
        name="s5_params",
    )(lre, lim, ls, btr, bti)


def _s5_layouts(lam_re, lam_im, log_step, b_re, b_im, c_re, c_im):
    lbr, lbi, l2r, l2i, bbr, bbi, lbbr, lbbi = _s5_params(lam_re, lam_im, log_step, b_re, b_im)
    same = jnp.repeat(jnp.eye(SSM_BLK_GROUPS, dtype=F32), SSM_STATE, axis=1)
    lead = (DEPTH, 2, SSM_BLKS, SSM_BLK_GROUPS)
    blocked = (DEPTH, 2, SSM_BLKS, SSM_BLK_CH, SSM_BLK_ST)

    def in_block(x):
        x = x.reshape(lead + (SSM_GROUP, SSM_STATE))
        wide = jnp.concatenate([x] * SSM_BLK_GROUPS, axis=-1)
        return (wide * same[:, None, :]).reshape(blocked)

    def out_block(x):
        x = x.reshape(lead + (SSM_GROUP, SSM_STATE))
        wide = jnp.concatenate([x[..., g, :, :] for g in range(SSM_BLK_GROUPS)], axis=-1)
        return (wide[..., None, :, :] * same[:, None, :]).reshape(blocked)

    wb = jnp.concatenate([
        jnp.concatenate([in_block(bbr), in_block(bbi)], axis=-1),
        jnp.concatenate([in_block(lbbr), in_block(lbbi)], axis=-1)], axis=-2).astype(BF16)
    wc = jnp.concatenate([out_block(c_re), out_block(-c_im)], axis=-1).astype(BF16)

    def table(one, two):
        one = one.reshape(DEPTH, 2, SSM_BLKS, 1, SSM_BLK_ST)
        two = two.reshape(DEPTH, 2, SSM_BLKS, 1, SSM_BLK_ST)
        half = SUBLANE // 2
        fwd = jnp.concatenate([jnp.broadcast_to(one[:, 0:1], (DEPTH, 1, SSM_BLKS, half, SSM_BLK_ST)),
                               jnp.broadcast_to(two[:, 0:1], (DEPTH, 1, SSM_BLKS, half, SSM_BLK_ST))], axis=3)
        bwd = jnp.concatenate([jnp.broadcast_to(two[:, 1:2], (DEPTH, 1, SSM_BLKS, half, SSM_BLK_ST)),
                               jnp.broadcast_to(one[:, 1:2], (DEPTH, 1, SSM_BLKS, half, SSM_BLK_ST))], axis=3)
        return jnp.concatenate([fwd, bwd], axis=1)

    return wb, wc, table(lbr, l2r), table(lbi, l2i)


def _s5_scan_kernel(uf_ref, ub_ref, wb_ref, wc_ref, tre_ref, tim_ref, yf_ref, yb_ref,
                    tmaj, xs, carry):
    @pl.when(pl.program_id(1) == 0)
    def _():
        carry[...] = jnp.zeros(carry.shape, F32)

    n_tiles = S5_ROWS // SUBLANE
    half = SUBLANE // 2
    io_refs = (uf_ref, ub_ref, yf_ref, yb_ref)

    def time_major(blk, d):
        ch = slice(blk * SSM_BLK_CH, (blk + 1) * SSM_BLK_CH)
        for b in range(BATCH):
            tmaj[blk, d, pl.ds(b, S5_STEPS, stride=BATCH), :] = io_refs[d][b, :, ch]
        return tmaj[blk, d]

    def batch_major(y, blk, d):
        ch = slice(blk * SSM_BLK_CH, (blk + 1) * SSM_BLK_CH)
        tmaj[blk, 2 + d] = y
        for b in range(BATCH):
            io_refs[2 + d][b, :, ch] = tmaj[blk, 2 + d, pl.ds(b, S5_STEPS, stride=BATCH), :]

    def paired_lhs(u, take_upper):
        u3 = u.reshape(n_tiles, SUBLANE, SSM_BLK_CH)
        swapped = pltpu.roll(u3, half, axis=1)
        sub = lax.broadcasted_iota(jnp.int32, u3.shape, 1)
        keep = (sub >= half) if take_upper else (sub < half)
        nb = jnp.where(keep, swapped, 0.0).reshape(S5_ROWS, SSM_BLK_CH)
        return jnp.concatenate([u, nb], axis=1).astype(BF16)

    for blk in range(S5_NB):
        for d in range(2):
            xs[blk, d] = _bdot(paired_lhs(time_major(blk, d), d == 0), wb_ref[0, d, blk])

    lower = lax.broadcasted_iota(jnp.int32, (SUBLANE, S5_SCAN_LANES), 0) < half
    for blk in range(S5_NB):
        for hh in range(SSM_BLK_ST // S5_SCAN_LANES):
            lo = hh * S5_SCAN_LANES
            re = slice(lo, lo + S5_SCAN_LANES)
            im = slice(SSM_BLK_ST + lo, SSM_BLK_ST + lo + S5_SCAN_LANES)
            afr, afi = tre_ref[0, 0, blk, :, re], tim_ref[0, 0, blk, :, re]
            abr, abi = tre_ref[0, 1, blk, :, re], tim_ref[0, 1, blk, :, re]
            hfr, hfi, hbr, hbi = (carry[blk, idx, :, re] for idx in range(4))
            for k in range(n_tiles):
                rf = slice(k * SUBLANE, (k + 1) * SUBLANE)
                pr = jnp.where(lower, pltpu.roll(hfr, half, axis=0), hfr)
                pi = jnp.where(lower, pltpu.roll(hfi, half, axis=0), hfi)
                hfr = xs[blk, 0, rf, re] + (afr * pr - afi * pi)
                hfi = xs[blk, 0, rf, im] + (afr * pi + afi * pr)
                xs[blk, 0, rf, re] = hfr
                xs[blk, 0, rf, im] = hfi
                kb = n_tiles - 1 - k
                rb = slice(kb * SUBLANE, (kb + 1) * SUBLANE)
                qr = jnp.where(lower, hbr, pltpu.roll(hbr, half, axis=0))
                qi = jnp.where(lower, hbi, pltpu.roll(hbi, half, axis=0))
                hbr = xs[blk, 1, rb, re] + (abr * qr - abi * qi)
                hbi = xs[blk, 1, rb, im] + (abr * qi + abi * qr)
                xs[blk, 1, rb, re] = hbr
                xs[blk, 1, rb, im] = hbi
            for idx, val in enumerate((hfr, hfi, hbr, hbi)):
                carry[blk, idx, :, re] = val

    for blk in range(S5_NB):
        for d in range(2):
            y = lax.dot_general(xs[blk, d].astype(BF16), wc_ref[0, d, blk],
                                (((1,), (1,)), ((), ())), preferred_element_type=F32)
            batch_major(y, blk, d)


def _s5_scan(proj, wb, wc, tre, tim, layer):
    last = S5_CHUNKS - 1
    width = S5_NB * SSM_BLK_CH
    u_blk0 = U_COL0 // width
    proj3 = proj.reshape(BATCH, L_TOK, PROJ_COLS)
    y_shape = jax.ShapeDtypeStruct((BATCH, L_TOK, SSM_WIDTH), F32)
    per_blk = lambda *tail: pl.BlockSpec((1, 2, S5_NB) + tail, lambda j, c: (layer, 0, j, 0, 0))
    yf, yb = pl.pallas_call(
        _s5_scan_kernel,
        grid=(SSM_BLKS // S5_NB, S5_CHUNKS),
        in_specs=[
            pl.BlockSpec((BATCH, S5_STEPS, width), lambda j, c: (0, c, u_blk0 + j)),
            pl.BlockSpec((BATCH, S5_STEPS, width), lambda j, c: (0, last - c, u_blk0 + j)),
            per_blk(2 * SSM_BLK_CH, 2 * SSM_BLK_ST),
            per_blk(SSM_BLK_CH, 2 * SSM_BLK_ST),
            per_blk(SUBLANE, SSM_BLK_ST),
            per_blk(SUBLANE, SSM_BLK_ST),
        ],
        out_specs=[
            pl.BlockSpec((BATCH, S5_STEPS, width), lambda j, c: (0, c, j)),
            pl.BlockSpec((BATCH, S5_STEPS, width), lambda j, c: (0, last - c, j)),
        ],
        out_shape=[y_shape, y_shape],
        scratch_shapes=[
            pltpu.VMEM((S5_NB, 4, S5_ROWS, SSM_BLK_CH), F32),
            pltpu.VMEM((S5_NB, 2, S5_ROWS, 2 * SSM_BLK_ST), F32),
            pltpu.VMEM((S5_NB, 4, SUBLANE, SSM_BLK_ST), F32),
        ],
        compiler_params=_cparams(("parallel", "arbitrary")),
        name="s5_scan",
    )(proj3, proj3, wb, wc, tre, tim)
    return yf.reshape(T_TOK, SSM_WIDTH), yb.reshape(T_TOK, SSM_WIDTH)


def _gelu_tanh(x):
    return 0.5 * x * (1.0 + jnp.tanh(math.sqrt(2.0 / math.pi) * (x + 0.044715 * (x * x * x))))


def _glu_kernel(yf_ref, yb_ref, u_ref, d_ref, w_ref, gn_ref, o_ref):
    y = (yf_ref[...] + yb_ref[...]) + d_ref[0] * u_ref[...]
    g = _gelu_tanh(y)
    z = _bdot(g.astype(BF16), w_ref[0].astype(BF16))
    o_ref[...] = _rms(g * jax.nn.sigmoid(z), gn_ref[0]).astype(BF16)


def _glu(yf, yb, proj, d, w, gn, layer, tm):
    t = yf.shape[0]
    row = pl.BlockSpec((tm, SSM_WIDTH), lambda i: (i, 0))
    vec = pl.BlockSpec((1, 1, SSM_WIDTH), lambda i: (layer, 0, 0))
    return pl.pallas_call(
        _glu_kernel,
        grid=(t // tm,),
        in_specs=[row, row,
                  pl.BlockSpec((tm, SSM_WIDTH), lambda i: (i, U_COL0 // SSM_WIDTH)),
                  vec,
                  pl.BlockSpec((1, SSM_WIDTH, SSM_WIDTH), lambda i: (layer, 0, 0)),
                  vec],
        out_specs=row,
        out_shape=jax.ShapeDtypeStruct((t, SSM_WIDTH), BF16),
        compiler_params=_cparams(("parallel",)),
        name="s5_glu",
    )(yf, yb, proj, d.reshape(DEPTH, 1, SSM_WIDTH), w, gn.reshape(DEPTH, 1, SSM_WIDTH))


def _outproj_kernel(h_ref, a_ref, sn_ref, ga_ref, wa_ref, ws_ref, o_ref, an_ref):
    @pl.when(pl.program_id(1) == 0)
    def _():
        an_ref[...] = _rms(a_ref[...], ga_ref[0]).astype(BF16)

    o_ref[...] = (h_ref[...] + _bdot(an_ref[...], wa_ref[0].astype(BF16))
                  + _bdot(sn_ref[...], ws_ref[0].astype(BF16)))


def _outproj(h, att, ssm_n, ga, w, layer, tm, tn):
    t, d = h.shape
    return pl.pallas_call(
        _outproj_kernel,
        grid=(t // tm, d // tn),
        in_specs=[
            pl.BlockSpec((tm, tn), lambda i, j: (i, j)),
            pl.BlockSpec((tm, ATT_WIDTH), lambda i, j: (i, 0)),
            pl.BlockSpec((tm, SSM_WIDTH), lambda i, j: (i, 0)),
            pl.BlockSpec((1, 1, ATT_WIDTH), lambda i, j: (layer, 0, 0)),
            pl.BlockSpec((1, ATT_WIDTH, tn), lambda i, j: (layer, 0, j)),
            pl.BlockSpec((1, SSM_WIDTH, tn), lambda i, j: (layer, 1, j)),
        ],
        out_specs=pl.BlockSpec((tm, tn), lambda i, j: (i, j)),
        out_shape=jax.ShapeDtypeStruct((t, d), F32),
        scratch_shapes=[pltpu.VMEM((tm, ATT_WIDTH), BF16)],
        input_output_aliases={0: 0},
        compiler_params=_cparams(("parallel", "arbitrary")),
        name="out_proj",
    )(h, att, ssm_n, ga.reshape(DEPTH, 1, ATT_WIDTH), w, w)


FFN_SUB = 256


def _ffn_kernel(x_ref, g_ref, wg_ref, wu_ref, wd_ref, gn_ref, o_ref, nxt_ref, hn_ref):
    j = pl.program_id(1)

    @pl.when(j == 0)
    def _():
        x = x_ref[...]
        hn_ref[...] = _rms(x, g_ref[0]).astype(BF16)
        o_ref[...] = x

    hn = hn_ref[...]
    tf = wg_ref.shape[2]
    part = None
    for lo in range(0, tf, FFN_SUB):
        cols = slice(lo, lo + FFN_SUB)
        a = _bdot(hn, wg_ref[0, :, cols].astype(BF16))
        b = _bdot(hn, wu_ref[0, :, cols].astype(BF16))
        hid = ((a * jax.nn.sigmoid(a)) * b).astype(BF16)
        down = _bdot(hid, wd_ref[0, cols, :].astype(BF16))
        part = down if part is None else part + down
    o_ref[...] += part

    @pl.when(j == pl.num_programs(1) - 1)
    def _():
        nxt_ref[...] = _rms(o_ref[...], gn_ref[0]).astype(BF16)


def _ffn(h, g, gn, wg, wu, wd, layer, tm, tf):
    t, d = h.shape
    f = wg.shape[2]
    idx = layer // 2
    once = pl.Buffered(1)
    row = pl.BlockSpec((tm, d), lambda i, j: (i, 0), pipeline_mode=once)
    return pl.pallas_call(
        _ffn_kernel,
        grid=(t // tm, f // tf),
        in_specs=[
            row,
            pl.BlockSpec((1, 1, d), lambda i, j: (layer, 0, 0)),
            pl.BlockSpec((1, d, tf), lambda i, j: (idx, 0, j)),
            pl.BlockSpec((1, d, tf), lambda i, j: (idx, 0, j)),
            pl.BlockSpec((1, tf, d), lambda i, j: (idx, j, 0)),
            pl.BlockSpec((1, 1, d), lambda i, j: (layer + 1, 0, 0)),
        ],
        out_specs=[row, row],
        out_shape=[jax.ShapeDtypeStruct((t, d), F32), jax.ShapeDtypeStruct((t, d), BF16)],
        scratch_shapes=[pltpu.VMEM((tm, d), BF16)],
        input_output_aliases={0: 0},
        compiler_params=_cparams(("parallel", "arbitrary")),
        name="dense_ffn",
    )(h, g.reshape(DEPTH, 1, d), wg, wu, wd, gn.reshape(DEPTH, 1, d))


MOE_TM = 344
MOE_NT = (2 * T_TOK) // MOE_TM + N_EXPERTS
MOE_NP = MOE_NT * MOE_TM
DMA_FANOUT = 4


def _split_bf16(x):
    hi = x.astype(BF16)
    lo = (x - hi.astype(F32)).astype(BF16)
    return hi, lo


def _lane_pick(x, lane, k):
    return jnp.sum(jnp.where(lane == k, x, 0.0), axis=-1, keepdims=True)


ROW_SLABS = D_MODEL // LANE


def _row_slab(ref, c, rows):
    return ref.at[pl.ds(c, rows, stride=ROW_SLABS), :]


def _router_kernel(x_ref, g_ref, wr_ref, sel_ref, wts_ref, rows_ref):
    x = x_ref[...]
    for c in range(ROW_SLABS):
        _row_slab(rows_ref, c, x.shape[0])[...] = x[:, c * LANE:(c + 1) * LANE]
    hn = _rms(x, g_ref[0])
    xh, xl = _split_bf16(hn)
    wh, wl = _split_bf16(wr_ref[0])
    logits = _bdot(xh, wh) + (_bdot(xh, wl) + _bdot(xl, wh)) + _bdot(xl, wl)
    lane = lax.broadcasted_iota(jnp.int32, logits.shape, 1).astype(F32)
    neg = jnp.float32(-jnp.inf)
    logits = jnp.where(lane < N_EXPERTS, logits, neg)
    v1 = jnp.max(logits, axis=-1, keepdims=True)
    i1 = jnp.min(jnp.where(logits == v1, lane, float(LANE)), axis=-1, keepdims=True)
    rest = jnp.where(lane == i1, neg, logits)
    v2 = jnp.max(rest, axis=-1, keepdims=True)
    i2 = jnp.min(jnp.where(rest == v2, lane, float(LANE)), axis=-1, keepdims=True)
    e2 = jnp.exp(v2 - v1)
    w1 = 1.0 / (1.0 + e2)
    w2 = e2 / (1.0 + e2)
    sel_ref[...] = jnp.where((lane == i1) | (lane == i2), 1.0, 0.0)
    wts_ref[...] = jnp.where(lane == 0, w1, jnp.where(lane == 1, w2,
                             jnp.where(lane == 2, i1, jnp.where(lane == 3, i2, 0.0))))


def _router(h, g, wr_pad, layer, tm):
    t, d = h.shape
    idx = layer // 2
    row = pl.BlockSpec((tm, LANE), lambda i: (i, 0))
    tab = jax.ShapeDtypeStruct((t, LANE), F32)
    return pl.pallas_call(
        _router_kernel,
        grid=(t // tm,),
        in_specs=[
            pl.BlockSpec((tm, d), lambda i: (i, 0)),
            pl.BlockSpec((1, 1, d), lambda i: (layer, 0, 0)),
            pl.BlockSpec((1, d, LANE), lambda i: (idx, 0, 0)),
        ],
        out_specs=[row, row, pl.BlockSpec((tm * ROW_SLABS, LANE), lambda i: (i, 0))],
        out_shape=[tab, tab, jax.ShapeDtypeStruct((t * ROW_SLABS, LANE), F32)],
        compiler_params=_cparams(("parallel",)),
        name="moe_router",
    )(h, g.reshape(DEPTH, 1, d), wr_pad)


def _positions_kernel(sel_ref, wts_ref, pos_ref, meta_ref, cnt_ref, off_ref):
    p = pl.program_id(0)
    i = pl.program_id(1)
    tm = sel_ref.shape[0]
    lane = lax.broadcasted_iota(jnp.int32, (1, LANE), 1).astype(F32)
    sel = sel_ref[...]

    @pl.when((p == 0) & (i == 0))
    def _():
        cnt_ref[...] = jnp.zeros((1, LANE), F32)

    @pl.when(p == 0)
    def _():
        cnt_ref[...] += jnp.sum(sel, axis=0, keepdims=True)

    @pl.when((p == 1) & (i == 0))
    def _():
        cnt = cnt_ref[...]
        tiles = jnp.zeros((1, LANE), F32)
        for k in range(MOE_NT):
            tiles = tiles + jnp.where(cnt > float(k * MOE_TM), 1.0, 0.0)
        padded = tiles * float(MOE_TM)
        off = jnp.zeros((1, LANE), F32)
        for e in range(N_EXPERTS):
            off = off + jnp.where(lane > e, _lane_pick(padded, lane, e), 0.0)
        end = off + padded
        tile_start = lane * float(MOE_TM)
        owner = jnp.zeros((1, LANE), F32)
        for e in range(N_EXPERTS):
            owner = owner + jnp.where(tile_start >= _lane_pick(end, lane, e), 1.0, 0.0)
        owner = jnp.minimum(owner, float(N_EXPERTS - 1))
        used = jnp.sum(tiles, axis=-1, keepdims=True)
        off_ref[...] = off
        cnt_ref[...] = jnp.zeros((1, LANE), F32)
        row = lax.broadcasted_iota(jnp.int32, (SUBLANE, LANE), 0)
        meta = jnp.where(row == 0, owner, jnp.where(row == 1, used, jnp.where(row == 2, cnt, off)))
        meta_ref[...] = meta.astype(jnp.int32)

    @pl.when(p == 1)
    def _():
        r = lax.broadcasted_iota(jnp.int32, (tm, tm), 0)
        c = lax.broadcasted_iota(jnp.int32, (tm, tm), 1)
        earlier = jnp.where(c < r, 1.0, 0.0).astype(BF16)
        rank = _bdot(earlier, sel.astype(BF16)) + cnt_ref[...]
        slot = off_ref[...] + rank
        wts = wts_ref[...]
        lane_t = lax.broadcasted_iota(jnp.int32, (tm, LANE), 1).astype(F32)
        p1 = _lane_pick(slot, lane_t, _lane_pick(wts, lane_t, 2))
        p2 = _lane_pick(slot, lane_t, _lane_pick(wts, lane_t, 3))
        pos_ref[...] = jnp.where(lane_t == 0, p1, jnp.where(lane_t == 1, p2, 0.0)).astype(jnp.int32)
        cnt_ref[...] += jnp.sum(sel, axis=0, keepdims=True)


def _positions(sel, wts, tm):
    t = sel.shape[0]
    row = pl.BlockSpec((tm, LANE), lambda p, i: (i, 0))
    return pl.pallas_call(
        _positions_kernel,
        grid=(2, t // tm),
        in_specs=[row, row],
        out_specs=[
            pl.BlockSpec((tm, LANE), lambda p, i: (i * p, 0)),
            pl.BlockSpec((SUBLANE, LANE), lambda p, i: (0, 0)),
        ],
        out_shape=[jax.ShapeDtypeStruct((t, LANE), jnp.int32),
                   jax.ShapeDtypeStruct((SUBLANE, LANE), jnp.int32)],
        scratch_shapes=[pltpu.VMEM((1, LANE), F32), pltpu.VMEM((1, LANE), F32)],
        compiler_params=_cparams(("arbitrary", "arbitrary")),
        name="moe_positions",
    )(sel, wts)


def _experts_kernel(pos1_ref, pos2_ref, owner_ref, used_ref, rows_ref, g_ref, wg_ref, wu_ref, wd_ref,
                    o_ref, src_ref, xbuf, sems):
    i = pl.program_id(0)
    used = used_ref[0]
    cur = lax.rem(i, 2)

    def row_copy(tok, r, buf):
        return pltpu.make_async_copy(rows_ref.at[pl.ds(tok * ROW_SLABS, ROW_SLABS), :],
                                     xbuf.at[buf, pl.ds(r * ROW_SLABS, ROW_SLABS), :], sems.at[buf])

    def fetch_tile(tile, buf):
        def issue(q, c):
            for u in range(DMA_FANOUT):
                r = q * DMA_FANOUT + u
                row_copy(src_ref[tile * MOE_TM + r], r, buf).start(priority=u % 2)
            return c

        lax.fori_loop(0, MOE_TM // DMA_FANOUT, issue, 0)

    @pl.when(i == 0)
    def _():
        def clear(p, c):
            src_ref[p] = 0
            return c

        lax.fori_loop(0, MOE_NP, clear, 0, unroll=8)

        def fill(t, c):
            src_ref[pos1_ref[t]] = t
            src_ref[pos2_ref[t]] = t
            return c

        lax.fori_loop(0, T_TOK, fill, 0, unroll=4)
        fetch_tile(0, 0)

    @pl.when(i + 1 < used)
    def _():
        fetch_tile(i + 1, 1 - cur)

    live = i < used

    @pl.when(live)
    def _():
        def land(r, c):
            row_copy(0, 0, cur).wait()
            return c

        lax.fori_loop(0, MOE_TM, land, 0, unroll=4)
        x_ref = xbuf.at[cur]
        slabs = [_row_slab(x_ref, c, MOE_TM)[...] for c in range(ROW_SLABS)]
        ssq = slabs[0] * slabs[0]
        for c in range(1, ROW_SLABS):
            ssq = ssq + slabs[c] * slabs[c]
        inv = lax.rsqrt(jnp.sum(ssq, axis=-1, keepdims=True) / D_MODEL + EPS)
        g = g_ref[0]
        hn = jnp.concatenate([slabs[c] * inv * g[:, c * LANE:(c + 1) * LANE]
                              for c in range(ROW_SLABS)], axis=1).astype(BF16)
        a = _bdot(hn, wg_ref[0, 0])
        b = _bdot(hn, wu_ref[0, 0])
        hid = ((a * jax.nn.sigmoid(a)) * b).astype(BF16)
        res = _bdot(hid, wd_ref[0, 0])
        for c in range(ROW_SLABS):
            _row_slab(o_ref, c, MOE_TM)[...] = res[:, c * LANE:(c + 1) * LANE]

    @pl.when(jnp.logical_not(live))
    def _():
        o_ref[...] = jnp.zeros(o_ref.shape, F32)


def _experts(rows, pos1, pos2, owner, used, g, wg, wu, wd, layer):
    d = D_MODEL
    idx = layer // 2
    blk = MOE_TM * ROW_SLABS

    w_map = lambda i, p1, p2, o, u: (idx, o[jnp.minimum(i, u[0] - 1)], 0, 0)
    return pl.pallas_call(
        _experts_kernel,
        grid_spec=pltpu.PrefetchScalarGridSpec(
            num_scalar_prefetch=4,
            grid=(MOE_NT,),
            in_specs=[
                pl.BlockSpec(memory_space=pl.ANY),
                pl.BlockSpec((1, 1, d), lambda i, p1, p2, o, u: (layer, 0, 0)),
                pl.BlockSpec((1, 1, d, D_FF_EXPERT), w_map),
                pl.BlockSpec((1, 1, d, D_FF_EXPERT), w_map),
                pl.BlockSpec((1, 1, D_FF_EXPERT, d), w_map),
            ],
            out_specs=pl.BlockSpec((blk, LANE), lambda i, p1, p2, o, u: (i, 0)),
            scratch_shapes=[
                pltpu.SMEM((MOE_NP,), jnp.int32),
                pltpu.VMEM((2, blk, LANE), F32),
                pltpu.SemaphoreType.DMA((2,)),
            ],
        ),
        out_shape=jax.ShapeDtypeStruct((MOE_NP * ROW_SLABS, LANE), F32),
        compiler_params=_cparams(("arbitrary",)),
        name="moe_experts",
    )(pos1, pos2, owner, used, rows, g.reshape(DEPTH, 1, d), wg, wu, wd)


def _combine_kernel(pos1_ref, pos2_ref, h_ref, wts_ref, ys_ref, gn_ref, o_ref, nxt_ref,
                    y1, y2, sem1, sem2):
    tm = h_ref.shape[0]
    base = pl.program_id(0) * tm

    def row_copy(slot, r, buf, sem):
        return pltpu.make_async_copy(ys_ref.at[pl.ds(slot * ROW_SLABS, ROW_SLABS), :],
                                     buf.at[pl.ds(r * ROW_SLABS, ROW_SLABS), :], sem)

    def fetch(q, c):
        for u in range(DMA_FANOUT):
            r = q * DMA_FANOUT + u
            row_copy(pos1_ref[base + r], r, y1, sem1).start(priority=0)
            row_copy(pos2_ref[base + r], r, y2, sem2).start(priority=1)
        return c

    lax.fori_loop(0, tm // DMA_FANOUT, fetch, 0)

    def land(r, c):
        row_copy(0, 0, y1, sem1).wait()
        row_copy(0, 0, y2, sem2).wait()
        return c

    lax.fori_loop(0, tm, land, 0, unroll=4)
    wts = wts_ref[...]
    lane = lax.broadcasted_iota(jnp.int32, wts.shape, 1).astype(F32)
    w1 = _lane_pick(wts, lane, 0)
    w2 = _lane_pick(wts, lane, 1)
    ssq = jnp.zeros((tm, 1), F32)
    for c in range(ROW_SLABS):
        cols = slice(c * LANE, (c + 1) * LANE)
        hc = h_ref[:, cols] + (w1 * _row_slab(y1, c, tm)[...] + w2 * _row_slab(y2, c, tm)[...])
        o_ref[:, cols] = hc
        ssq = ssq + jnp.sum(hc * hc, axis=-1, keepdims=True)
    normed = o_ref[...] * lax.rsqrt(ssq / D_MODEL + EPS) * gn_ref[...]
    if nxt_ref is None:
        o_ref[...] = normed
    else:
        nxt_ref[...] = normed.astype(BF16)


def _combine_mid_kernel(pos1_ref, pos2_ref, h_ref, wts_ref, ys_ref, gn_ref, o_ref, nxt_ref,
                        y1, y2, sem1, sem2):
    _combine_kernel(pos1_ref, pos2_ref, h_ref, wts_ref, ys_ref, gn_ref, o_ref, nxt_ref,
                    y1, y2, sem1, sem2)


def _combine_last_kernel(pos1_ref, pos2_ref, h_ref, wts_ref, ys_ref, gn_ref, o_ref,
                         y1, y2, sem1, sem2):
    _combine_kernel(pos1_ref, pos2_ref, h_ref, wts_ref, ys_ref, gn_ref, o_ref, None,
                    y1, y2, sem1, sem2)


def _combine(h, wts, ys, pos1, pos2, gn, last, tm):
    t, d = h.shape
    row = pl.BlockSpec((tm, d), lambda i, a, b: (i, 0))
    h_out = jax.ShapeDtypeStruct((t, d), F32)
    return pl.pallas_call(
        _combine_last_kernel if last else _combine_mid_kernel,
        grid_spec=pltpu.PrefetchScalarGridSpec(
            num_scalar_prefetch=2,
            grid=(t // tm,),
            in_specs=[
                row,
                pl.BlockSpec((tm, LANE), lambda i, a, b: (i, 0)),
                pl.BlockSpec(memory_space=pl.ANY),
                pl.BlockSpec((1, d), lambda i, a, b: (0, 0)),
            ],
            out_specs=row if last else [row, row],
            scratch_shapes=[
                pltpu.VMEM((tm * ROW_SLABS, LANE), F32),
                pltpu.VMEM((tm * ROW_SLABS, LANE), F32),
                pltpu.SemaphoreType.DMA(()),
                pltpu.SemaphoreType.DMA(()),
            ],
        ),
        out_shape=h_out if last else [h_out, jax.ShapeDtypeStruct((t, d), BF16)],
        input_output_aliases={2: 0},
        compiler_params=_cparams(("arbitrary",)),
        name="moe_combine",
    )(pos1, pos2, h, wts, ys, gn.reshape(1, d))


def _moe(h, g, gn, wr_pad, wg, wu, wd, layer, last):
    sel, wts, rows = _router(h, g, wr_pad, layer, tm=688)
    pos, meta = _positions(sel, wts, tm=688)
    pos1 = pos[:, 0]
    pos2 = pos[:, 1]
    owner = meta[0, :MOE_NT]
    used = meta[1, :1]
    ys = _experts(rows, pos1, pos2, owner, used, g, wg, wu, wd, layer)
    return _combine(h, wts, ys, pos1, pos2, gn, last, tm=688)


def _rope_partner(w):
    half = QK_ROPE // 2
    return jnp.concatenate([-w[..., half:], w[..., :half]], axis=-1)


def _pad_cols(w, width):
    return jnp.pad(w, [(0, 0)] * (w.ndim - 1) + [(0, width - w.shape[-1])])


def _in_proj_layout(w_in):
    kr = w_in[..., Q_LORA + KV_LORA:Q_LORA + KV_LORA + QK_ROPE]
    return jnp.concatenate([
        w_in[..., :Q_LORA + KV_LORA],
        _pad_cols(kr, LANE),
        _pad_cols(_rope_partner(kr), LANE),
        w_in[..., Q_LORA + KV_LORA + QK_ROPE:]], axis=-1)


def _q_layouts(w_uq):
    w = w_uq.reshape(DEPTH, Q_LORA, ATT_HEADS, QK_NOPE + QK_ROPE)
    main = _pad_cols(w, HEAD_PAD).reshape(DEPTH, Q_LORA, ATT_HEADS * HEAD_PAD)
    rot = _pad_cols(_rope_partner(w[..., QK_NOPE:]), LANE).reshape(DEPTH, Q_LORA, ATT_HEADS * LANE)
    return main, rot


def _rope_tables():
    inv = ROPE_THETA ** (-jnp.arange(0, QK_ROPE, 2, dtype=F32) / QK_ROPE)
    ang = jnp.arange(L_TOK, dtype=F32)[:, None] * inv[None, :]
    cos = jnp.cos(ang)
    sin = jnp.sin(ang)
    cos_t = _pad_cols(jnp.concatenate([cos, cos], axis=-1), LANE)
    sin_t = _pad_cols(jnp.concatenate([sin, sin], axis=-1), LANE)
    return cos_t, sin_t


def kernel(x, meta_tokens, mix_norm, w_in, q_norm, w_uq, kv_norm, w_ukv, ssm_lambda_re, ssm_lambda_im, ssm_log_step, ssm_b_re, ssm_b_im, ssm_c_re, ssm_c_im, ssm_d, ssm_w_glu, attn_out_norm, ssm_out_norm, w_out, ffn_norm, dense_w_gate, dense_w_up, dense_w_down, moe_router, moe_w_gate, moe_w_up, moe_w_down, final_norm):
    meta = jnp.broadcast_to(meta_tokens[None].astype(x.dtype), (BATCH, N_META, D_MODEL))
    h = jnp.concatenate([meta, x], axis=1).reshape(T_TOK, D_MODEL)
    cos_t, sin_t = _rope_tables()
    w_in_l = _in_proj_layout(w_in)
    wq_main, wq_rot = _q_layouts(w_uq)
    wr_pad = _pad_cols(moe_router, LANE)
    moe_w_gate, moe_w_up, moe_w_down = (w.astype(BF16) for w in (moe_w_gate, moe_w_up, moe_w_down))
    wb, wc, tre, tim = _s5_layouts(ssm_lambda_re, ssm_lambda_im, ssm_log_step,
                                   ssm_b_re, ssm_b_im, ssm_c_re, ssm_c_im)
    hn = _rms_cast(h, mix_norm, 0, tm=688)
    for layer in range(DEPTH):
        proj = _in_proj(hn, w_in_l, layer, tm=2064, tn=512)
        q, kv, kpe = _upproj(proj, q_norm, kv_norm, wq_main, wq_rot, w_ukv, cos_t, sin_t, layer,
                             tm=688)
        att = _attention(q, kv, kpe, tq=688)
        yf, yb = _s5_scan(proj, wb, wc, tre, tim, layer)
        ssm_n = _glu(yf, yb, proj, ssm_d, ssm_w_glu, ssm_out_norm, layer, tm=688)
        h = _outproj(h, att, ssm_n, attn_out_norm, w_out, layer, tm=1376, tn=512)
        last = layer == DEPTH - 1
        if layer % 2 == 0:
            h, hn = _ffn(h, ffn_norm, mix_norm, dense_w_gate, dense_w_up, dense_w_down, layer,
                         tm=1032, tf=512)
        elif last:
            h = _moe(h, ffn_norm, final_norm, wr_pad, moe_w_gate, moe_w_up, moe_w_down, layer, last)
        else:
            h, hn = _moe(h, ffn_norm, mix_norm[layer + 1], wr_pad, moe_w_gate, moe_w_up, moe_w_down,
                         layer, last)
    return h.reshape(BATCH, L_TOK, D_MODEL)[:, N_META:]
```

```python
import functools
import math

import jax
import jax.numpy as jnp
from jax import lax
from jax.experimental import pallas as pl
from jax.experimental.pallas import tpu as pltpu

F32 = jnp.float32
BF16 = jnp.bfloat16

D_MODEL = 2048
BATCH = 4
SEQ = 2048
DEPTH = 4
N_META = 16
EPS = 1e-6
ATT_HEADS = 8
QK_NOPE = 128
QK_ROPE = 64
V_DIM = 128
Q_LORA = 512
KV_LORA = 256
ROPE_THETA = 10000.0
ATT_WIDTH = ATT_HEADS * V_DIM
SSM_GROUP = 16
SSM_WIDTH = D_MODEL - ATT_WIDTH
SSM_GROUPS = SSM_WIDTH // SSM_GROUP
SSM_STATE = 64
D_FF = 5632
N_EXPERTS = 8
D_FF_EXPERT = 1408

L_TOK = N_META + SEQ
T_TOK = L_TOK * BATCH
LANE = 128
SUBLANE = 8
HEAD_PAD = 256
L_PAD = 2176
PROJ_COLS = 2048
U_COL0 = 1024
VMEM_LIMIT = 58 * 1024 * 1024

SSM_BLK_GROUPS = 8
SSM_BLKS = SSM_GROUPS // SSM_BLK_GROUPS
SSM_BLK_CH = SSM_BLK_GROUPS * SSM_GROUP
SSM_BLK_ST = SSM_BLK_GROUPS * SSM_STATE
S5_CHUNKS = 6
S5_STEPS = L_TOK // S5_CHUNKS
S5_ROWS = S5_STEPS * BATCH
S5_NB = 2
S5_SCAN_LANES = 256


def _cparams(sem):
    return pltpu.CompilerParams(dimension_semantics=sem, vmem_limit_bytes=VMEM_LIMIT)


def _rms(x, g):
    ms = jnp.mean(x * x, axis=-1, keepdims=True)
    return x * lax.rsqrt(ms + EPS) * g


def _bdot(a, b):
    return jnp.dot(a, b, preferred_element_type=F32)


def _rms_cast_kernel(x_ref, g_ref, o_ref):
    o_ref[...] = _rms(x_ref[...], g_ref[0]).astype(BF16)


def _rms_cast(x, g, layer, tm):
    t, d = x.shape
    return pl.pallas_call(
        _rms_cast_kernel,
        grid=(t // tm,),
        in_specs=[pl.BlockSpec((tm, d), lambda i: (i, 0)),
                  pl.BlockSpec((1, 1, d), lambda i: (layer, 0, 0))],
        out_specs=pl.BlockSpec((tm, d), lambda i: (i, 0)),
        out_shape=jax.ShapeDtypeStruct((t, d), BF16),
        compiler_params=_cparams(("parallel",)),
        name="rms_cast",
    )(x, g.reshape(DEPTH, 1, d))


def _in_proj_kernel(x_ref, w_ref, o_ref):
    o_ref[...] = _bdot(x_ref[...], w_ref[0].astype(BF16))


def _in_proj(hn, w, layer, tm, tn):
    t, d = hn.shape
    n = w.shape[2]
    return pl.pallas_call(
        _in_proj_kernel,
        grid=(t // tm, n // tn),
        in_specs=[
            pl.BlockSpec((tm, d), lambda i, j: (i, 0)),
            pl.BlockSpec((1, d, tn), lambda i, j: (layer, 0, j)),
        ],
        out_specs=pl.BlockSpec((tm, tn), lambda i, j: (i, j)),
        out_shape=jax.ShapeDtypeStruct((t, n), F32),
        compiler_params=_cparams(("parallel", "arbitrary")),
        name="in_proj",
    )(hn, w)


def _upproj_kernel(cq_ref, ckv_ref, kr_ref, krot_ref, qn_ref, kvn_ref, wq_ref, wqr_ref,
                   wkv_ref, cos_ref, sin_ref, q_ref, kv_ref, kpe_ref):
    scale = (QK_NOPE + QK_ROPE) ** -0.5 * math.log2(math.e)
    cqn = _rms(cq_ref[...], qn_ref[0]).astype(BF16)
    a = _bdot(cqn, wq_ref[0].astype(BF16))
    r = _bdot(cqn, wqr_ref[0].astype(BF16))
    c = cos_ref[...]
    s = sin_ref[...]
    last_lane = lax.broadcasted_iota(jnp.int32, c.shape, 1) == LANE - 1
    for h in range(ATT_HEADS):
        lo = h * HEAD_PAD
        q_ref[:, lo:lo + LANE] = (a[:, lo:lo + LANE] * scale).astype(BF16)
        rope = a[:, lo + LANE:lo + HEAD_PAD] * c + r[:, h * LANE:(h + 1) * LANE] * s
        q_ref[:, lo + LANE:lo + HEAD_PAD] = jnp.where(last_lane, 1.0, rope * scale).astype(BF16)
    ckvn = _rms(ckv_ref[...], kvn_ref[0]).astype(BF16)
    kv_ref[...] = _bdot(ckvn, wkv_ref[0].astype(BF16)).astype(BF16)
    kpe_ref[...] = (kr_ref[...] * c + krot_ref[...] * s).astype(BF16)


def _upproj(proj, qn, kvn, wq, wqr, wkv, cos_t, sin_t, layer, tm):
    t = proj.shape[0]
    full = lambda a: pl.BlockSpec((1,) + a.shape[1:], lambda i: (layer, 0, 0))
    qn = qn.reshape(DEPTH, 1, Q_LORA)
    kvn = kvn.reshape(DEPTH, 1, KV_LORA)
    pos_blocks = L_TOK // tm
    return pl.pallas_call(
        _upproj_kernel,
        grid=(t // tm,),
        in_specs=[
            pl.BlockSpec((tm, Q_LORA), lambda i: (i, 0)),
            pl.BlockSpec((tm, KV_LORA), lambda i: (i, Q_LORA // KV_LORA)),
            pl.BlockSpec((tm, LANE), lambda i: (i, (Q_LORA + KV_LORA) // LANE)),
            pl.BlockSpec((tm, LANE), lambda i: (i, (Q_LORA + KV_LORA) // LANE + 1)),
            full(qn),
            full(kvn),
            full(wq),
            full(wqr),
            full(wkv),
            pl.BlockSpec((tm, LANE), lambda i: (i % pos_blocks, 0)),
            pl.BlockSpec((tm, LANE), lambda i: (i % pos_blocks, 0)),
        ],
        out_specs=[
            pl.BlockSpec((tm, ATT_HEADS * HEAD_PAD), lambda i: (i, 0)),
            pl.BlockSpec((tm, ATT_HEADS * (QK_NOPE + V_DIM)), lambda i: (i, 0)),
            pl.BlockSpec((tm, LANE), lambda i: (i, 0)),
        ],
        out_shape=[
            jax.ShapeDtypeStruct((t, ATT_HEADS * HEAD_PAD), BF16),
            jax.ShapeDtypeStruct((t, ATT_HEADS * (QK_NOPE + V_DIM)), BF16),
            jax.ShapeDtypeStruct((t, LANE), BF16),
        ],
        compiler_params=_cparams(("parallel",)),
        name="upproj",
    )(proj, proj, proj, proj, qn, kvn, wq, wqr, wkv, cos_t, sin_t)


ATT_HPS = 2
PAD_BIAS = -1e30


def _attn_kernel(q_ref, kv_ref, kpe_ref, o_ref, k_sc, v_sc):
    @pl.when(pl.program_id(2) == 0)
    def _():
        n_pad = L_PAD - L_TOK
        pad_lane = lax.broadcasted_iota(jnp.int32, (n_pad, LANE), 1)
        k_pad = jnp.where(pad_lane == LANE - 1, PAD_BIAS, 0.0).astype(BF16)
        row_lane = lax.broadcasted_iota(jnp.int32, (L_TOK, LANE), 1)
        ones_col = jnp.where(row_lane == 0, 1.0, 0.0).astype(BF16)
        for hh in range(ATT_HPS):
            lo = hh * HEAD_PAD
            k_sc[hh, 0:L_TOK, 0:LANE] = kv_ref[:, lo:lo + LANE]
            k_sc[hh, 0:L_TOK, LANE:HEAD_PAD] = kpe_ref[...]
            k_sc[hh, L_TOK:L_PAD, 0:LANE] = jnp.zeros((n_pad, LANE), BF16)
            k_sc[hh, L_TOK:L_PAD, LANE:HEAD_PAD] = k_pad
            v_sc[hh, 0:L_TOK, 0:V_DIM] = kv_ref[:, lo + LANE:lo + HEAD_PAD]
            v_sc[hh, 0:L_TOK, V_DIM:HEAD_PAD] = ones_col
            v_sc[hh, L_TOK:L_PAD, :] = jnp.zeros((n_pad, HEAD_PAD), BF16)

    for hh in range(ATT_HPS):
        q = q_ref[:, hh * HEAD_PAD:(hh + 1) * HEAD_PAD]
        s = lax.dot_general(q, k_sc[hh], (((1,), (1,)), ((), ())),
                            preferred_element_type=F32)
        m = jnp.max(s, axis=-1, keepdims=True)
        p = jnp.exp2(s - m).astype(BF16)
        o = _bdot(p, v_sc[hh])
        o_ref[:, hh * V_DIM:(hh + 1) * V_DIM] = o[:, :V_DIM] / o[:, V_DIM:V_DIM + 1]


def _attention(q, kv, kpe, tq):
    nq = L_TOK // tq
    hps = ATT_HPS
    return pl.pallas_call(
        _attn_kernel,
        grid=(BATCH, ATT_HEADS // hps, nq),
        in_specs=[
            pl.BlockSpec((tq, hps * HEAD_PAD), lambda b, h, i: (b * nq + i, h)),
            pl.BlockSpec((L_TOK, hps * HEAD_PAD), lambda b, h, i: (b, h)),
            pl.BlockSpec((L_TOK, LANE), lambda b, h, i: (b, 0)),
        ],
        out_specs=pl.BlockSpec((tq, hps * V_DIM), lambda b, h, i: (b * nq + i, h)),
        out_shape=jax.ShapeDtypeStruct((T_TOK, ATT_WIDTH), F32),
        scratch_shapes=[pltpu.VMEM((hps, L_PAD, HEAD_PAD), BF16),
                        pltpu.VMEM((hps, L_PAD, HEAD_PAD), BF16)],
        compiler_params=_cparams(("parallel", "parallel", "arbitrary")),
        name="attention",
    )(q, kv, kpe)


def _s5_param_kernel(lre_ref, lim_ref, ls_ref, bre_ref, bim_ref,
                     lbr_o, lbi_o, l2r_o, l2i_o, bbr_o, bbi_o, lbbr_o, lbbi_o):
    lre = lre_ref[...]
    lim = lim_ref[...]
    dt = jnp.exp(ls_ref[...])
    mag = jnp.exp(lre * dt)
    ang = lim * dt
    br = mag * jnp.cos(ang)
    bi = mag * jnp.sin(ang)
    nr = br - 1.0
    den = lre * lre + lim * lim
    cr = (nr * lre + bi * lim) / den
    ci = (bi * lre - nr * lim) / den
    b_r = bre_ref[...]
    b_i = bim_ref[...]
    bbr = cr * b_r - ci * b_i
    bbi = cr * b_i + ci * b_r
    lbr_o[...] = br
    lbi_o[...] = bi
    l2r_o[...] = br * br - bi * bi
    l2i_o[...] = 2.0 * br * bi
    bbr_o[...] = bbr
    bbi_o[...] = bbi
    lbbr_o[...] = br * bbr - bi * bbi
    lbbi_o[...] = br * bbi + bi * bbr


def _s5_params(lam_re, lam_im, log_step, b_re, b_im):
    n = DEPTH * 2 * SSM_GROUPS
    lre = lam_re.reshape(n, 1, SSM_STATE)
    lim = lam_im.reshape(n, 1, SSM_STATE)
    ls = log_step.reshape(n, 1, 1)
    btr = b_re.transpose(0, 1, 2, 4, 3).reshape(n, SSM_GROUP, SSM_STATE)
    bti = b_im.transpose(0, 1, 2, 4, 3).reshape(n, SSM_GROUP, SSM_STATE)
    small = jax.ShapeDtypeStruct((n, 1, SSM_STATE), F32)
    big = jax.ShapeDtypeStruct((n, SSM_GROUP, SSM_STATE), F32)
    return pl.pallas_call(
        _s5_param_kernel,
        out_shape=[small, small, small, small, big, big, big, big],
        compiler_params=pltpu.CompilerParams(vmem_limit_bytes=VMEM_LIMIT),
        name="s5_params",
    )(lre, lim, ls, btr, bti)


def _s5_layouts(lam_re, lam_im, log_step, b_re, b_im, c_re, c_im):
    lbr, lbi, l2r, l2i, bbr, bbi, lbbr, lbbi = _s5_params(lam_re, lam_im, log_step, b_re, b_im)
    eye = jnp.eye(SSM_BLK_GROUPS, dtype=F32)
    lead = (DEPTH, 2, SSM_BLKS, SSM_BLK_GROUPS)
    blocked = (DEPTH, 2, SSM_BLKS, SSM_BLK_CH, SSM_BLK_ST)

    def in_block(x):
        x = x.reshape(lead + (SSM_GROUP, SSM_STATE))
        return jnp.einsum('ldbgcp,gh->ldbgchp', x, eye).reshape(blocked)

    def out_block(x):
        x = x.reshape(lead + (SSM_GROUP, SSM_STATE))
        return jnp.einsum('ldbgcp,gh->ldbhcgp', x, eye).reshape(blocked)

    wb = jnp.concatenate([
        jnp.concatenate([in_block(bbr), in_block(bbi)], axis=-1),
        jnp.concatenate([in_block(lbbr), in_block(lbbi)], axis=-1)], axis=-2).astype(BF16)
    wc = jnp.concatenate([out_block(c_re), out_block(-c_im)], axis=-1).astype(BF16)

    def table(one, two):
        one = one.reshape(DEPTH, 2, SSM_BLKS, 1, SSM_BLK_ST)
        two = two.reshape(DEPTH, 2, SSM_BLKS, 1, SSM_BLK_ST)
        half = SUBLANE // 2
        fwd = jnp.concatenate([jnp.broadcast_to(one[:, 0:1], (DEPTH, 1, SSM_BLKS, half, SSM_BLK_ST)),
                               jnp.broadcast_to(two[:, 0:1], (DEPTH, 1, SSM_BLKS, half, SSM_BLK_ST))], axis=3)
        bwd = jnp.concatenate([jnp.broadcast_to(two[:, 1:2], (DEPTH, 1, SSM_BLKS, half, SSM_BLK_ST)),
                               jnp.broadcast_to(one[:, 1:2], (DEPTH, 1, SSM_BLKS, half, SSM_BLK_ST))], axis=3)
        return jnp.concatenate([fwd, bwd], axis=1)

    return wb, wc, table(lbr, l2r), table(lbi, l2i)


def _s5_scan_kernel(uf_ref, ub_ref, wb_ref, wc_ref, tre_ref, tim_ref, yf_ref, yb_ref,
                    tmaj, xs, carry):
    @pl.when(pl.program_id(1) == 0)
    def _():
        carry[...] = jnp.zeros(carry.shape, F32)

    n_tiles = S5_ROWS // SUBLANE
    half = SUBLANE // 2
    io_refs = (uf_ref, ub_ref, yf_ref, yb_ref)

    def time_major(blk, d):
        ch = slice(blk * SSM_BLK_CH, (blk + 1) * SSM_BLK_CH)
        for b in range(BATCH):
            tmaj[blk, d, pl.ds(b, S5_STEPS, stride=BATCH), :] = io_refs[d][b, :, ch]
        return tmaj[blk, d]

    def batch_major(y, blk, d):
        ch = slice(blk * SSM_BLK_CH, (blk + 1) * SSM_BLK_CH)
        tmaj[blk, 2 + d] = y
        for b in range(BATCH):
            io_refs[2 + d][b, :, ch] = tmaj[blk, 2 + d, pl.ds(b, S5_STEPS, stride=BATCH), :]

    def paired_lhs(u, take_upper):
        u3 = u.reshape(n_tiles, SUBLANE, SSM_BLK_CH)
        swapped = pltpu.roll(u3, half, axis=1)
        sub = lax.broadcasted_iota(jnp.int32, u3.shape, 1)
        keep = (sub >= half) if take_upper else (sub < half)
        nb = jnp.where(keep, swapped, 0.0).reshape(S5_ROWS, SSM_BLK_CH)
        return jnp.concatenate([u, nb], axis=1).astype(BF16)

    for blk in range(S5_NB):
        for d in range(2):
            xs[blk, d] = _bdot(paired_lhs(time_major(blk, d), d == 0), wb_ref[0, d, blk])

    lower = lax.broadcasted_iota(jnp.int32, (SUBLANE, S5_SCAN_LANES), 0) < half
    for blk in range(S5_NB):
        for hh in range(SSM_BLK_ST // S5_SCAN_LANES):
            lo = hh * S5_SCAN_LANES
            re = slice(lo, lo + S5_SCAN_LANES)
            im = slice(SSM_BLK_ST + lo, SSM_BLK_ST + lo + S5_SCAN_LANES)
            afr, afi = tre_ref[0, 0, blk, :, re], tim_ref[0, 0, blk, :, re]
            abr, abi = tre_ref[0, 1, blk, :, re], tim_ref[0, 1, blk, :, re]
            hfr, hfi, hbr, hbi = (carry[blk, idx, :, re] for idx in range(4))
            for k in range(n_tiles):
                rf = slice(k * SUBLANE, (k + 1) * SUBLANE)
                pr = jnp.where(lower, pltpu.roll(hfr, half, axis=0), hfr)
                pi = jnp.where(lower, pltpu.roll(hfi, half, axis=0), hfi)
                hfr = xs[blk, 0, rf, re] + (afr * pr - afi * pi)
                hfi = xs[blk, 0, rf, im] + (afr * pi + afi * pr)
                xs[blk, 0, rf, re] = hfr
                xs[blk, 0, rf, im] = hfi
                kb = n_tiles - 1 - k
                rb = slice(kb * SUBLANE, (kb + 1) * SUBLANE)
                qr = jnp.where(lower, hbr, pltpu.roll(hbr, half, axis=0))
                qi = jnp.where(lower, hbi, pltpu.roll(hbi, half, axis=0))
                hbr = xs[blk, 1, rb, re] + (abr * qr - abi * qi)
                hbi = xs[blk, 1, rb, im] + (abr * qi + abi * qr)
                xs[blk, 1, rb, re] = hbr
                xs[blk, 1, rb, im] = hbi
            for idx, val in enumerate((hfr, hfi, hbr, hbi)):
                carry[blk, idx, :, re] = val

    for blk in range(S5_NB):
        for d in range(2):
            y = lax.dot_general(xs[blk, d].astype(BF16), wc_ref[0, d, blk],
                                (((1,), (1,)), ((), ())), preferred_element_type=F32)
            batch_major(y, blk, d)


def _s5_scan(proj, wb, wc, tre, tim, layer):
    last = S5_CHUNKS - 1
    width = S5_NB * SSM_BLK_CH
    u_blk0 = U_COL0 // width
    proj3 = proj.reshape(BATCH, L_TOK, PROJ_COLS)
    y_shape = jax.ShapeDtypeStruct((BATCH, L_TOK, SSM_WIDTH), F32)
    per_blk = lambda *tail: pl.BlockSpec((1, 2, S5_NB) + tail, lambda j, c: (layer, 0, j, 0, 0))
    yf, yb = pl.pallas_call(
        _s5_scan_kernel,
        grid=(SSM_BLKS // S5_NB, S5_CHUNKS),
        in_specs=[
            pl.BlockSpec((BATCH, S5_STEPS, width), lambda j, c: (0, c, u_blk0 + j)),
            pl.BlockSpec((BATCH, S5_STEPS, width), lambda j, c: (0, last - c, u_blk0 + j)),
            per_blk(2 * SSM_BLK_CH, 2 * SSM_BLK_ST),
            per_blk(SSM_BLK_CH, 2 * SSM_BLK_ST),
            per_blk(SUBLANE, SSM_BLK_ST),
            per_blk(SUBLANE, SSM_BLK_ST),
        ],
        out_specs=[
            pl.BlockSpec((BATCH, S5_STEPS, width), lambda j, c: (0, c, j)),
            pl.BlockSpec((BATCH, S5_STEPS, width), lambda j, c: (0, last - c, j)),
        ],
        out_shape=[y_shape, y_shape],
        scratch_shapes=[
            pltpu.VMEM((S5_NB, 4, S5_ROWS, SSM_BLK_CH), F32),
            pltpu.VMEM((S5_NB, 2, S5_ROWS, 2 * SSM_BLK_ST), F32),
            pltpu.VMEM((S5_NB, 4, SUBLANE, SSM_BLK_ST), F32),
        ],
        compiler_params=_cparams(("parallel", "arbitrary")),
        name="s5_scan",
    )(proj3, proj3, wb, wc, tre, tim)
    return yf.reshape(T_TOK, SSM_WIDTH), yb.reshape(T_TOK, SSM_WIDTH)


def _gelu_tanh(x):
    return 0.5 * x * (1.0 + jnp.tanh(math.sqrt(2.0 / math.pi) * (x + 0.044715 * (x * x * x))))


def _glu_kernel(yf_ref, yb_ref, u_ref, d_ref, w_ref, gn_ref, o_ref):
    y = (yf_ref[...] + yb_ref[...]) + d_ref[0] * u_ref[...]
    g = _gelu_tanh(y)
    z = _bdot(g.astype(BF16), w_ref[0].astype(BF16))
    o_ref[...] = _rms(g * jax.nn.sigmoid(z), gn_ref[0]).astype(BF16)


def _glu(yf, yb, proj, d, w, gn, layer, tm):
    t = yf.shape[0]
    row = pl.BlockSpec((tm, SSM_WIDTH), lambda i: (i, 0))
    vec = pl.BlockSpec((1, 1, SSM_WIDTH), lambda i: (layer, 0, 0))
    return pl.pallas_call(
        _glu_kernel,
        grid=(t // tm,),
        in_specs=[row, row,
                  pl.BlockSpec((tm, SSM_WIDTH), lambda i: (i, U_COL0 // SSM_WIDTH)),
                  vec,
                  pl.BlockSpec((1, SSM_WIDTH, SSM_WIDTH), lambda i: (layer, 0, 0)),
                  vec],
        out_specs=row,
        out_shape=jax.ShapeDtypeStruct((t, SSM_WIDTH), BF16),
        compiler_params=_cparams(("parallel",)),
        name="s5_glu",
    )(yf, yb, proj, d.reshape(DEPTH, 1, SSM_WIDTH), w, gn.reshape(DEPTH, 1, SSM_WIDTH))


def _outproj_kernel(h_ref, a_ref, sn_ref, ga_ref, wa_ref, ws_ref, o_ref, an_ref):
    @pl.when(pl.program_id(1) == 0)
    def _():
        an_ref[...] = _rms(a_ref[...], ga_ref[0]).astype(BF16)

    o_ref[...] = (h_ref[...] + _bdot(an_ref[...], wa_ref[0].astype(BF16))
                  + _bdot(sn_ref[...], ws_ref[0].astype(BF16)))


def _outproj(h, att, ssm_n, ga, w, layer, tm, tn):
    t, d = h.shape
    return pl.pallas_call(
        _outproj_kernel,
        grid=(t // tm, d // tn),
        in_specs=[
            pl.BlockSpec((tm, tn), lambda i, j: (i, j)),
            pl.BlockSpec((tm, ATT_WIDTH), lambda i, j: (i, 0)),
            pl.BlockSpec((tm, SSM_WIDTH), lambda i, j: (i, 0)),
            pl.BlockSpec((1, 1, ATT_WIDTH), lambda i, j: (layer, 0, 0)),
            pl.BlockSpec((1, ATT_WIDTH, tn), lambda i, j: (layer, 0, j)),
            pl.BlockSpec((1, SSM_WIDTH, tn), lambda i, j: (layer, 1, j)),
        ],
        out_specs=pl.BlockSpec((tm, tn), lambda i, j: (i, j)),
        out_shape=jax.ShapeDtypeStruct((t, d), F32),
        scratch_shapes=[pltpu.VMEM((tm, ATT_WIDTH), BF16)],
        input_output_aliases={0: 0},
        compiler_params=_cparams(("parallel", "arbitrary")),
        name="out_proj",
    )(h, att, ssm_n, ga.reshape(DEPTH, 1, ATT_WIDTH), w, w)


FFN_SUB = 256


def _ffn_kernel(x_ref, g_ref, wg_ref, wu_ref, wd_ref, gn_ref, o_ref, nxt_ref, hn_ref):
    j = pl.program_id(1)

    @pl.when(j == 0)
    def _():
        x = x_ref[...]
        hn_ref[...] = _rms(x, g_ref[0]).astype(BF16)
        o_ref[...] = x

    hn = hn_ref[...]
    tf = wg_ref.shape[2]
    part = None
    for lo in range(0, tf, FFN_SUB):
        cols = slice(lo, lo + FFN_SUB)
        a = _bdot(hn, wg_ref[0, :, cols].astype(BF16))
        b = _bdot(hn, wu_ref[0, :, cols].astype(BF16))
        hid = ((a * jax.nn.sigmoid(a)) * b).astype(BF16)
        down = _bdot(hid, wd_ref[0, cols, :].astype(BF16))
        part = down if part is None else part + down
    o_ref[...] += part

    @pl.when(j == pl.num_programs(1) - 1)
    def _():
        nxt_ref[...] = _rms(o_ref[...], gn_ref[0]).astype(BF16)


def _ffn(h, g, gn, wg, wu, wd, layer, tm, tf):
    t, d = h.shape
    f = wg.shape[2]
    idx = layer // 2
    once = pl.Buffered(1)
    row = pl.BlockSpec((tm, d), lambda i, j: (i, 0), pipeline_mode=once)
    return pl.pallas_call(
        _ffn_kernel,
        grid=(t // tm, f // tf),
        in_specs=[
            row,
            pl.BlockSpec((1, 1, d), lambda i, j: (layer, 0, 0)),
            pl.BlockSpec((1, d, tf), lambda i, j: (idx, 0, j)),
            pl.BlockSpec((1, d, tf), lambda i, j: (idx, 0, j)),
            pl.BlockSpec((1, tf, d), lambda i, j: (idx, j, 0)),
            pl.BlockSpec((1, 1, d), lambda i, j: (layer + 1, 0, 0)),
        ],
        out_specs=[row, row],
        out_shape=[jax.ShapeDtypeStruct((t, d), F32), jax.ShapeDtypeStruct((t, d), BF16)],
        scratch_shapes=[pltpu.VMEM((tm, d), BF16)],
        input_output_aliases={0: 0},
        compiler_params=_cparams(("parallel", "arbitrary")),
        name="dense_ffn",
    )(h, g.reshape(DEPTH, 1, d), wg, wu, wd, gn.reshape(DEPTH, 1, d))


MOE_TM = 344
MOE_NT = (2 * T_TOK) // MOE_TM + N_EXPERTS
MOE_NP = MOE_NT * MOE_TM
DMA_FANOUT = 4


def _split_bf16(x):
    hi = x.astype(BF16)
    lo = (x - hi.astype(F32)).astype(BF16)
    return hi, lo


def _lane_pick(x, lane, k):
    return jnp.sum(jnp.where(lane == k, x, 0.0), axis=-1, keepdims=True)


ROW_SLABS = D_MODEL // LANE


def _row_slab(ref, c, rows):
    return ref.at[pl.ds(c, rows, stride=ROW_SLABS), :]


def _router_kernel(x_ref, g_ref, wr_ref, sel_ref, wts_ref, rows_ref):
    x = x_ref[...]
    for c in range(ROW_SLABS):
        _row_slab(rows_ref, c, x.shape[0])[...] = x[:, c * LANE:(c + 1) * LANE]
    hn = _rms(x, g_ref[0])
    xh, xl = _split_bf16(hn)
    wh, wl = _split_bf16(wr_ref[0])
    logits = _bdot(xh, wh) + (_bdot(xh, wl) + _bdot(xl, wh)) + _bdot(xl, wl)
    lane = lax.broadcasted_iota(jnp.int32, logits.shape, 1).astype(F32)
    neg = jnp.float32(-jnp.inf)
    logits = jnp.where(lane < N_EXPERTS, logits, neg)
    v1 = jnp.max(logits, axis=-1, keepdims=True)
    i1 = jnp.min(jnp.where(logits == v1, lane, float(LANE)), axis=-1, keepdims=True)
    rest = jnp.where(lane == i1, neg, logits)
    v2 = jnp.max(rest, axis=-1, keepdims=True)
    i2 = jnp.min(jnp.where(rest == v2, lane, float(LANE)), axis=-1, keepdims=True)
    e2 = jnp.exp(v2 - v1)
    w1 = 1.0 / (1.0 + e2)
    w2 = e2 / (1.0 + e2)
    sel_ref[...] = jnp.where((lane == i1) | (lane == i2), 1.0, 0.0)
    wts_ref[...] = jnp.where(lane == 0, w1, jnp.where(lane == 1, w2,
                             jnp.where(lane == 2, i1, jnp.where(lane == 3, i2, 0.0))))


def _router(h, g, wr_pad, layer, tm):
    t, d = h.shape
    idx = layer // 2
    row = pl.BlockSpec((tm, LANE), lambda i: (i, 0))
    tab = jax.ShapeDtypeStruct((t, LANE), F32)
    return pl.pallas_call(
        _router_kernel,
        grid=(t // tm,),
        in_specs=[
            pl.BlockSpec((tm, d), lambda i: (i, 0)),
            pl.BlockSpec((1, 1, d), lambda i: (layer, 0, 0)),
            pl.BlockSpec((1, d, LANE), lambda i: (idx, 0, 0)),
        ],
        out_specs=[row, row, pl.BlockSpec((tm * ROW_SLABS, LANE), lambda i: (i, 0))],
        out_shape=[tab, tab, jax.ShapeDtypeStruct((t * ROW_SLABS, LANE), F32)],
        compiler_params=_cparams(("parallel",)),
        name="moe_router",
    )(h, g.reshape(DEPTH, 1, d), wr_pad)


def _positions_kernel(sel_ref, wts_ref, pos_ref, meta_ref, cnt_ref, off_ref):
    p = pl.program_id(0)
    i = pl.program_id(1)
    tm = sel_ref.shape[0]
    lane = lax.broadcasted_iota(jnp.int32, (1, LANE), 1).astype(F32)
    sel = sel_ref[...]

    @pl.when((p == 0) & (i == 0))
    def _():
        cnt_ref[...] = jnp.zeros((1, LANE), F32)

    @pl.when(p == 0)
    def _():
        cnt_ref[...] += jnp.sum(sel, axis=0, keepdims=True)

    @pl.when((p == 1) & (i == 0))
    def _():
        cnt = cnt_ref[...]
        tiles = jnp.zeros((1, LANE), F32)
        for k in range(MOE_NT):
            tiles = tiles + jnp.where(cnt > float(k * MOE_TM), 1.0, 0.0)
        padded = tiles * float(MOE_TM)
        off = jnp.zeros((1, LANE), F32)
        for e in range(N_EXPERTS):
            off = off + jnp.where(lane > e, _lane_pick(padded, lane, e), 0.0)
        end = off + padded
        tile_start = lane * float(MOE_TM)
        owner = jnp.zeros((1, LANE), F32)
        for e in range(N_EXPERTS):
            owner = owner + jnp.where(tile_start >= _lane_pick(end, lane, e), 1.0, 0.0)
        owner = jnp.minimum(owner, float(N_EXPERTS - 1))
        used = jnp.sum(tiles, axis=-1, keepdims=True)
        off_ref[...] = off
        cnt_ref[...] = jnp.zeros((1, LANE), F32)
        row = lax.broadcasted_iota(jnp.int32, (SUBLANE, LANE), 0)
        meta = jnp.where(row == 0, owner, jnp.where(row == 1, used, jnp.where(row == 2, cnt, off)))
        meta_ref[...] = meta.astype(jnp.int32)

    @pl.when(p == 1)
    def _():
        r = lax.broadcasted_iota(jnp.int32, (tm, tm), 0)
        c = lax.broadcasted_iota(jnp.int32, (tm, tm), 1)
        earlier = jnp.where(c < r, 1.0, 0.0).astype(BF16)
        rank = _bdot(earlier, sel.astype(BF16)) + cnt_ref[...]
        slot = off_ref[...] + rank
        wts = wts_ref[...]
        lane_t = lax.broadcasted_iota(jnp.int32, (tm, LANE), 1).astype(F32)
        p1 = _lane_pick(slot, lane_t, _lane_pick(wts, lane_t, 2))
        p2 = _lane_pick(slot, lane_t, _lane_pick(wts, lane_t, 3))
        pos_ref[...] = jnp.where(lane_t == 0, p1, jnp.where(lane_t == 1, p2, 0.0)).astype(jnp.int32)
        cnt_ref[...] += jnp.sum(sel, axis=0, keepdims=True)


def _positions(sel, wts, tm):
    t = sel.shape[0]
    row = pl.BlockSpec((tm, LANE), lambda p, i: (i, 0))
    return pl.pallas_call(
        _positions_kernel,
        grid=(2, t // tm),
        in_specs=[row, row],
        out_specs=[
            pl.BlockSpec((tm, LANE), lambda p, i: (i * p, 0)),
            pl.BlockSpec((SUBLANE, LANE), lambda p, i: (0, 0)),
        ],
        out_shape=[jax.ShapeDtypeStruct((t, LANE), jnp.int32),
                   jax.ShapeDtypeStruct((SUBLANE, LANE), jnp.int32)],
        scratch_shapes=[pltpu.VMEM((1, LANE), F32), pltpu.VMEM((1, LANE), F32)],
        compiler_params=_cparams(("arbitrary", "arbitrary")),
        name="moe_positions",
    )(sel, wts)


def _experts_kernel(pos1_ref, pos2_ref, owner_ref, used_ref, rows_ref, g_ref, wg_ref, wu_ref, wd_ref,
                    o_ref, src_ref, xbuf, sems):
    i = pl.program_id(0)
    used = used_ref[0]
    cur = lax.rem(i, 2)

    def row_copy(tok, r, buf):
        return pltpu.make_async_copy(rows_ref.at[pl.ds(tok * ROW_SLABS, ROW_SLABS), :],
                                     xbuf.at[buf, pl.ds(r * ROW_SLABS, ROW_SLABS), :], sems.at[buf])

    def fetch_tile(tile, buf):
        def issue(q, c):
            for u in range(DMA_FANOUT):
                r = q * DMA_FANOUT + u
                row_copy(src_ref[tile * MOE_TM + r], r, buf).start(priority=u % 2)
            return c

        lax.fori_loop(0, MOE_TM // DMA_FANOUT, issue, 0)

    @pl.when(i == 0)
    def _():
        def clear(p, c):
            src_ref[p] = 0
            return c

        lax.fori_loop(0, MOE_NP, clear, 0, unroll=8)

        def fill(t, c):
            src_ref[pos1_ref[t]] = t
            src_ref[pos2_ref[t]] = t
            return c

        lax.fori_loop(0, T_TOK, fill, 0, unroll=4)
        fetch_tile(0, 0)

    @pl.when(i + 1 < used)
    def _():
        fetch_tile(i + 1, 1 - cur)

    live = i < used

    @pl.when(live)
    def _():
        pltpu.make_async_copy(rows_ref.at[pl.ds(0, MOE_TM * ROW_SLABS), :], xbuf.at[cur],
                              sems.at[cur]).wait()
        x_ref = xbuf.at[cur]
        slabs = [_row_slab(x_ref, c, MOE_TM)[...] for c in range(ROW_SLABS)]
        ssq = slabs[0] * slabs[0]
        for c in range(1, ROW_SLABS):
            ssq = ssq + slabs[c] * slabs[c]
        inv = lax.rsqrt(jnp.sum(ssq, axis=-1, keepdims=True) / D_MODEL + EPS)
        g = g_ref[0]
        hn = jnp.concatenate([slabs[c] * inv * g[:, c * LANE:(c + 1) * LANE]
                              for c in range(ROW_SLABS)], axis=1).astype(BF16)
        a = _bdot(hn, wg_ref[0, 0])
        b = _bdot(hn, wu_ref[0, 0])
        hid = ((a * jax.nn.sigmoid(a)) * b).astype(BF16)
        res = _bdot(hid, wd_ref[0, 0])
        for c in range(ROW_SLABS):
            _row_slab(o_ref, c, MOE_TM)[...] = res[:, c * LANE:(c + 1) * LANE]

    @pl.when(jnp.logical_not(live))
    def _():
        o_ref[...] = jnp.zeros(o_ref.shape, F32)


def _experts(rows, pos1, pos2, owner, used, g, wg, wu, wd, layer):
    d = D_MODEL
    idx = layer // 2
    blk = MOE_TM * ROW_SLABS

    w_map = lambda i, p1, p2, o, u: (idx, o[jnp.minimum(i, u[0] - 1)], 0, 0)
    return pl.pallas_call(
        _experts_kernel,
        grid_spec=pltpu.PrefetchScalarGridSpec(
            num_scalar_prefetch=4,
            grid=(MOE_NT,),
            in_specs=[
                pl.BlockSpec(memory_space=pl.ANY),
                pl.BlockSpec((1, 1, d), lambda i, p1, p2, o, u: (layer, 0, 0)),
                pl.BlockSpec((1, 1, d, D_FF_EXPERT), w_map),
                pl.BlockSpec((1, 1, d, D_FF_EXPERT), w_map),
                pl.BlockSpec((1, 1, D_FF_EXPERT, d), w_map),
            ],
            out_specs=pl.BlockSpec((blk, LANE), lambda i, p1, p2, o, u: (i, 0)),
            scratch_shapes=[
                pltpu.SMEM((MOE_NP,), jnp.int32),
                pltpu.VMEM((2, blk, LANE), F32),
                pltpu.SemaphoreType.DMA((2,)),
            ],
        ),
        out_shape=jax.ShapeDtypeStruct((MOE_NP * ROW_SLABS, LANE), F32),
        compiler_params=_cparams(("arbitrary",)),
        name="moe_experts",
    )(pos1, pos2, owner, used, rows, g.reshape(DEPTH, 1, d), wg, wu, wd)


def _combine_kernel(pos1_ref, pos2_ref, h_ref, wts_ref, ys_ref, gn_ref, o_ref, nxt_ref,
                    y1, y2, sem1, sem2):
    tm = h_ref.shape[0]
    base = pl.program_id(0) * tm

    def row_copy(slot, r, buf, sem):
        return pltpu.make_async_copy(ys_ref.at[pl.ds(slot * ROW_SLABS, ROW_SLABS), :],
                                     buf.at[pl.ds(r * ROW_SLABS, ROW_SLABS), :], sem)

    def fetch(q, c):
        for u in range(DMA_FANOUT):
            r = q * DMA_FANOUT + u
            row_copy(pos1_ref[base + r], r, y1, sem1).start(priority=0)
            row_copy(pos2_ref[base + r], r, y2, sem2).start(priority=1)
        return c

    lax.fori_loop(0, tm // DMA_FANOUT, fetch, 0)

    pltpu.make_async_copy(ys_ref.at[pl.ds(0, tm * ROW_SLABS), :], y1, sem1).wait()
    pltpu.make_async_copy(ys_ref.at[pl.ds(0, tm * ROW_SLABS), :], y2, sem2).wait()
    wts = wts_ref[...]
    lane = lax.broadcasted_iota(jnp.int32, wts.shape, 1).astype(F32)
    w1 = _lane_pick(wts, lane, 0)
    w2 = _lane_pick(wts, lane, 1)
    ssq = jnp.zeros((tm, 1), F32)
    for c in range(ROW_SLABS):
        cols = slice(c * LANE, (c + 1) * LANE)
        hc = h_ref[:, cols] + (w1 * _row_slab(y1, c, tm)[...] + w2 * _row_slab(y2, c, tm)[...])
        o_ref[:, cols] = hc
        ssq = ssq + jnp.sum(hc * hc, axis=-1, keepdims=True)
    normed = o_ref[...] * lax.rsqrt(ssq / D_MODEL + EPS) * gn_ref[...]
    if nxt_ref is None:
        o_ref[...] = normed
    else:
        nxt_ref[...] = normed.astype(BF16)


def _combine_mid_kernel(pos1_ref, pos2_ref, h_ref, wts_ref, ys_ref, gn_ref, o_ref, nxt_ref,
                        y1, y2, sem1, sem2):
    _combine_kernel(pos1_ref, pos2_ref, h_ref, wts_ref, ys_ref, gn_ref, o_ref, nxt_ref,
                    y1, y2, sem1, sem2)


def _combine_last_kernel(pos1_ref, pos2_ref, h_ref, wts_ref, ys_ref, gn_ref, o_ref,
                         y1, y2, sem1, sem2):
    _combine_kernel(pos1_ref, pos2_ref, h_ref, wts_ref, ys_ref, gn_ref, o_ref, None,
                    y1, y2, sem1, sem2)


def _combine(h, wts, ys, pos1, pos2, gn, last, tm):
    t, d = h.shape
    row = pl.BlockSpec((tm, d), lambda i, a, b: (i, 0))
    h_out = jax.ShapeDtypeStruct((t, d), F32)
    return pl.pallas_call(
        _combine_last_kernel if last else _combine_mid_kernel,
        grid_spec=pltpu.PrefetchScalarGridSpec(
            num_scalar_prefetch=2,
            grid=(t // tm,),
            in_specs=[
                row,
                pl.BlockSpec((tm, LANE), lambda i, a, b: (i, 0)),
                pl.BlockSpec(memory_space=pl.ANY),
                pl.BlockSpec((1, d), lambda i, a, b: (0, 0)),
            ],
            out_specs=row if last else [row, row],
            scratch_shapes=[
                pltpu.VMEM((tm * ROW_SLABS, LANE), F32),
                pltpu.VMEM((tm * ROW_SLABS, LANE), F32),
                pltpu.SemaphoreType.DMA(()),
                pltpu.SemaphoreType.DMA(()),
            ],
        ),
        out_shape=h_out if last else [h_out, jax.ShapeDtypeStruct((t, d), BF16)],
        input_output_aliases={2: 0},
        compiler_params=_cparams(("arbitrary",)),
        name="moe_combine",
    )(pos1, pos2, h, wts, ys, gn.reshape(1, d))


def _moe(h, g, gn, wr_pad, wg, wu, wd, layer, last):
    sel, wts, rows = _router(h, g, wr_pad, layer, tm=688)
    pos, meta = _positions(sel, wts, tm=688)
    pos1 = pos[:, 0]
    pos2 = pos[:, 1]
    owner = meta[0, :MOE_NT]
    used = meta[1, :1]
    ys = _experts(rows, pos1, pos2, owner, used, g, wg, wu, wd, layer)
    return _combine(h, wts, ys, pos1, pos2, gn, last, tm=688)


def _rope_partner(w):
    half = QK_ROPE // 2
    return jnp.concatenate([-w[..., half:], w[..., :half]], axis=-1)


def _pad_cols(w, width):
    return jnp.pad(w, [(0, 0)] * (w.ndim - 1) + [(0, width - w.shape[-1])])


def _in_proj_layout(w_in):
    kr = w_in[..., Q_LORA + KV_LORA:Q_LORA + KV_LORA + QK_ROPE]
    return jnp.concatenate([
        w_in[..., :Q_LORA + KV_LORA],
        _pad_cols(kr, LANE),
        _pad_cols(_rope_partner(kr), LANE),
        w_in[..., Q_LORA + KV_LORA + QK_ROPE:]], axis=-1)


def _q_layouts(w_uq):
    w = w_uq.reshape(DEPTH, Q_LORA, ATT_HEADS, QK_NOPE + QK_ROPE)
    main = _pad_cols(w, HEAD_PAD).reshape(DEPTH, Q_LORA, ATT_HEADS * HEAD_PAD)
    rot = _pad_cols(_rope_partner(w[..., QK_NOPE:]), LANE).reshape(DEPTH, Q_LORA, ATT_HEADS * LANE)
    return main, rot


def _rope_tables():
    inv = ROPE_THETA ** (-jnp.arange(0, QK_ROPE, 2, dtype=F32) / QK_ROPE)
    ang = jnp.arange(L_TOK, dtype=F32)[:, None] * inv[None, :]
    cos = jnp.cos(ang)
    sin = jnp.sin(ang)
    cos_t = _pad_cols(jnp.concatenate([cos, cos], axis=-1), LANE)
    sin_t = _pad_cols(jnp.concatenate([sin, sin], axis=-1), LANE)
    return cos_t, sin_t


def kernel(x, meta_tokens, mix_norm, w_in, q_norm, w_uq, kv_norm, w_ukv, ssm_lambda_re, ssm_lambda_im, ssm_log_step, ssm_b_re, ssm_b_im, ssm_c_re, ssm_c_im, ssm_d, ssm_w_glu, attn_out_norm, ssm_out_norm, w_out, ffn_norm, dense_w_gate, dense_w_up, dense_w_down, moe_router, moe_w_gate, moe_w_up, moe_w_down, final_norm):
    meta = jnp.broadcast_to(meta_tokens[None].astype(x.dtype), (BATCH, N_META, D_MODEL))
    h = jnp.concatenate([meta, x], axis=1).reshape(T_TOK, D_MODEL)
    cos_t, sin_t = _rope_tables()
    w_in_l = _in_proj_layout(w_in)
    wq_main, wq_rot = _q_layouts(w_uq)
    wr_pad = _pad_cols(moe_router, LANE)
    moe_w_gate, moe_w_up, moe_w_down = (w.astype(BF16) for w in (moe_w_gate, moe_w_up, moe_w_down))
    wb, wc, tre, tim = _s5_layouts(ssm_lambda_re, ssm_lambda_im, ssm_log_step,
                                   ssm_b_re, ssm_b_im, ssm_c_re, ssm_c_im)
    hn = _rms_cast(h, mix_norm, 0, tm=688)
    for layer in range(DEPTH):
        proj = _in_proj(hn, w_in_l, layer, tm=2064, tn=512)
        q, kv, kpe = _upproj(proj, q_norm, kv_norm, wq_main, wq_rot, w_ukv, cos_t, sin_t, layer,
                             tm=688)
        att = _attention(q, kv, kpe, tq=688)
        yf, yb = _s5_scan(proj, wb, wc, tre, tim, layer)
        ssm_n = _glu(yf, yb, proj, ssm_d, ssm_w_glu, ssm_out_norm, layer, tm=688)
        h = _outproj(h, att, ssm_n, attn_out_norm, w_out, layer, tm=1376, tn=512)
        last = layer == DEPTH - 1
        if layer % 2 == 0:
            h, hn = _ffn(h, ffn_norm, mix_norm, dense_w_gate, dense_w_up, dense_w_down, layer,
                         tm=1032, tf=512)
        elif last:
            h = _moe(h, ffn_norm, final_norm, wr_pad, moe_w_gate, moe_w_up, moe_w_down, layer, last)
        else:
            h, hn = _moe(h, ffn_norm, mix_norm[layer + 1], wr_pad, moe_w_gate, moe_w_up, moe_w_down,
                         layer, last)
    return h.reshape(BATCH, L_TOK, D_MODEL)[:, N_META:]
```

```python
import functools
import math

import jax
import jax.numpy as jnp
from jax import lax
from jax.experimental import pallas as pl
from jax.experimental.pallas import tpu as pltpu

F32 = jnp.float32
BF16 = jnp.bfloat16

D_MODEL = 2048
BATCH = 4
SEQ = 2048
DEPTH = 4
N_META = 16
EPS = 1e-6
ATT_HEADS = 8
QK_NOPE = 128
QK_ROPE = 64
V_DIM = 128
Q_LORA = 512
KV_LORA = 256
ROPE_THETA = 10000.0
ATT_WIDTH = ATT_HEADS * V_DIM
SSM_GROUP = 16
SSM_WIDTH = D_MODEL - ATT_WIDTH
SSM_GROUPS = SSM_WIDTH // SSM_GROUP
SSM_STATE = 64
D_FF = 5632
N_EXPERTS = 8
D_FF_EXPERT = 1408

L_TOK = N_META + SEQ
T_TOK = L_TOK * BATCH
LANE = 128
SUBLANE = 8
HEAD_PAD = 256
L_PAD = 2176
PROJ_COLS = 2048
U_COL0 = 1024
VMEM_LIMIT = 58 * 1024 * 1024

SSM_BLK_GROUPS = 8
SSM_BLKS = SSM_GROUPS // SSM_BLK_GROUPS
SSM_BLK_CH = SSM_BLK_GROUPS * SSM_GROUP
SSM_BLK_ST = SSM_BLK_GROUPS * SSM_STATE
S5_CHUNKS = 6
S5_STEPS = L_TOK // S5_CHUNKS
S5_ROWS = S5_STEPS * BATCH
S5_NB = 2
S5_SCAN_LANES = 256


def _cparams(sem):
    return pltpu.CompilerParams(dimension_semantics=sem, vmem_limit_bytes=VMEM_LIMIT)


def _rms(x, g):
    ms = jnp.mean(x * x, axis=-1, keepdims=True)
    return x * lax.rsqrt(ms + EPS) * g


def _bdot(a, b):
    return jnp.dot(a, b, preferred_element_type=F32)


def _rms_cast_kernel(x_ref, g_ref, o_ref):
    o_ref[...] = _rms(x_ref[...], g_ref[0]).astype(BF16)


def _rms_cast(x, g, layer, tm):
    t, d = x.shape
    return pl.pallas_call(
        _rms_cast_kernel,
        grid=(t // tm,),
        in_specs=[pl.BlockSpec((tm, d), lambda i: (i, 0)),
                  pl.BlockSpec((1, 1, d), lambda i: (layer, 0, 0))],
        out_specs=pl.BlockSpec((tm, d), lambda i: (i, 0)),
        out_shape=jax.ShapeDtypeStruct((t, d), BF16),
        compiler_params=_cparams(("parallel",)),
        name="rms_cast",
    )(x, g.reshape(DEPTH, 1, d))


def _in_proj_kernel(x_ref, w_ref, o_ref):
    o_ref[...] = _bdot(x_ref[...], w_ref[0].astype(BF16))


def _in_proj(hn, w, layer, tm, tn):
    t, d = hn.shape
    n = w.shape[2]
    return pl.pallas_call(
        _in_proj_kernel,
        grid=(t // tm, n // tn),
        in_specs=[
            pl.BlockSpec((tm, d), lambda i, j: (i, 0)),
            pl.BlockSpec((1, d, tn), lambda i, j: (layer, 0, j)),
        ],
        out_specs=pl.BlockSpec((tm, tn), lambda i, j: (i, j)),
        out_shape=jax.ShapeDtypeStruct((t, n), F32),
        compiler_params=_cparams(("parallel", "arbitrary")),
        name="in_proj",
    )(hn, w)


def _upproj_kernel(cq_ref, ckv_ref, kr_ref, krot_ref, qn_ref, kvn_ref, wq_ref, wqr_ref,
                   wkv_ref, cos_ref, sin_ref, q_ref, kv_ref, kpe_ref):
    scale = (QK_NOPE + QK_ROPE) ** -0.5 * math.log2(math.e)
    cqn = _rms(cq_ref[...], qn_ref[0]).astype(BF16)
    a = _bdot(cqn, wq_ref[0].astype(BF16))
    r = _bdot(cqn, wqr_ref[0].astype(BF16))
    c = cos_ref[...]
    s = sin_ref[...]
    last_lane = lax.broadcasted_iota(jnp.int32, c.shape, 1) == LANE - 1
    for h in range(ATT_HEADS):
        lo = h * HEAD_PAD
        q_ref[:, lo:lo + LANE] = (a[:, lo:lo + LANE] * scale).astype(BF16)
        rope = a[:, lo + LANE:lo + HEAD_PAD] * c + r[:, h * LANE:(h + 1) * LANE] * s
        q_ref[:, lo + LANE:lo + HEAD_PAD] = jnp.where(last_lane, 1.0, rope * scale).astype(BF16)
    ckvn = _rms(ckv_ref[...], kvn_ref[0]).astype(BF16)
    kv_ref[...] = _bdot(ckvn, wkv_ref[0].astype(BF16)).astype(BF16)
    kpe_ref[...] = (kr_ref[...] * c + krot_ref[...] * s).astype(BF16)


def _upproj(proj, qn, kvn, wq, wqr, wkv, cos_t, sin_t, layer, tm):
    t = proj.shape[0]
    full = lambda a: pl.BlockSpec((1,) + a.shape[1:], lambda i: (layer, 0, 0))
    qn = qn.reshape(DEPTH, 1, Q_LORA)
    kvn = kvn.reshape(DEPTH, 1, KV_LORA)
    pos_blocks = L_TOK // tm
    return pl.pallas_call(
        _upproj_kernel,
        grid=(t // tm,),
        in_specs=[
            pl.BlockSpec((tm, Q_LORA), lambda i: (i, 0)),
            pl.BlockSpec((tm, KV_LORA), lambda i: (i, Q_LORA // KV_LORA)),
            pl.BlockSpec((tm, LANE), lambda i: (i, (Q_LORA + KV_LORA) // LANE)),
            pl.BlockSpec((tm, LANE), lambda i: (i, (Q_LORA + KV_LORA) // LANE + 1)),
            full(qn),
            full(kvn),
            full(wq),
            full(wqr),
            full(wkv),
            pl.BlockSpec((tm, LANE), lambda i: (i % pos_blocks, 0)),
            pl.BlockSpec((tm, LANE), lambda i: (i % pos_blocks, 0)),
        ],
        out_specs=[
            pl.BlockSpec((tm, ATT_HEADS * HEAD_PAD), lambda i: (i, 0)),
            pl.BlockSpec((tm, ATT_HEADS * (QK_NOPE + V_DIM)), lambda i: (i, 0)),
            pl.BlockSpec((tm, LANE), lambda i: (i, 0)),
        ],
        out_shape=[
            jax.ShapeDtypeStruct((t, ATT_HEADS * HEAD_PAD), BF16),
            jax.ShapeDtypeStruct((t, ATT_HEADS * (QK_NOPE + V_DIM)), BF16),
            jax.ShapeDtypeStruct((t, LANE), BF16),
        ],
        compiler_params=_cparams(("parallel",)),
        name="upproj",
    )(proj, proj, proj, proj, qn, kvn, wq, wqr, wkv, cos_t, sin_t)


ATT_HPS = 4
PAD_BIAS = -1e30


def _attn_kernel(q_ref, kv_ref, kpe_ref, o_ref, k_sc, v_sc):
    @pl.when(pl.program_id(2) == 0)
    def _():
        n_pad = L_PAD - L_TOK
        pad_lane = lax.broadcasted_iota(jnp.int32, (n_pad, LANE), 1)
        k_pad = jnp.where(pad_lane == LANE - 1, PAD_BIAS, 0.0).astype(BF16)
        row_lane = lax.broadcasted_iota(jnp.int32, (L_TOK, LANE), 1)
        ones_col = jnp.where(row_lane == 0, 1.0, 0.0).astype(BF16)
        for hh in range(ATT_HPS):
            lo = hh * HEAD_PAD
            k_sc[hh, 0:L_TOK, 0:LANE] = kv_ref[:, lo:lo + LANE]
            k_sc[hh, 0:L_TOK, LANE:HEAD_PAD] = kpe_ref[...]
            k_sc[hh, L_TOK:L_PAD, 0:LANE] = jnp.zeros((n_pad, LANE), BF16)
            k_sc[hh, L_TOK:L_PAD, LANE:HEAD_PAD] = k_pad
            v_sc[hh, 0:L_TOK, 0:V_DIM] = kv_ref[:, lo + LANE:lo + HEAD_PAD]
            v_sc[hh, 0:L_TOK, V_DIM:HEAD_PAD] = ones_col
            v_sc[hh, L_TOK:L_PAD, :] = jnp.zeros((n_pad, HEAD_PAD), BF16)

    for hh in range(ATT_HPS):
        q = q_ref[:, hh * HEAD_PAD:(hh + 1) * HEAD_PAD]
        s = lax.dot_general(q, k_sc[hh], (((1,), (1,)), ((), ())),
                            preferred_element_type=F32)
        m = jnp.max(s, axis=-1, keepdims=True)
        p = jnp.exp2(s - m).astype(BF16)
        o = _bdot(p, v_sc[hh])
        o_ref[:, hh * V_DIM:(hh + 1) * V_DIM] = o[:, :V_DIM] / o[:, V_DIM:V_DIM + 1]


def _attention(q, kv, kpe, tq):
    nq = L_TOK // tq
    hps = ATT_HPS
    return pl.pallas_call(
        _attn_kernel,
        grid=(BATCH, ATT_HEADS // hps, nq),
        in_specs=[
            pl.BlockSpec((tq, hps * HEAD_PAD), lambda b, h, i: (b * nq + i, h)),
            pl.BlockSpec((L_TOK, hps * HEAD_PAD), lambda b, h, i: (b, h)),
            pl.BlockSpec((L_TOK, LANE), lambda b, h, i: (b, 0)),
        ],
        out_specs=pl.BlockSpec((tq, hps * V_DIM), lambda b, h, i: (b * nq + i, h)),
        out_shape=jax.ShapeDtypeStruct((T_TOK, ATT_WIDTH), F32),
        scratch_shapes=[pltpu.VMEM((hps, L_PAD, HEAD_PAD), BF16),
                        pltpu.VMEM((hps, L_PAD, HEAD_PAD), BF16)],
        compiler_params=_cparams(("parallel", "parallel", "arbitrary")),
        name="attention",
    )(q, kv, kpe)


def _s5_param_kernel(lre_ref, lim_ref, ls_ref, bre_ref, bim_ref,
                     lbr_o, lbi_o, l2r_o, l2i_o, bbr_o, bbi_o, lbbr_o, lbbi_o):
    lre = lre_ref[...]
    lim = lim_ref[...]
    dt = jnp.exp(ls_ref[...])
    mag = jnp.exp(lre * dt)
    ang = lim * dt
    br = mag * jnp.cos(ang)
    bi = mag * jnp.sin(ang)
    nr = br - 1.0
    den = lre * lre + lim * lim
    cr = (nr * lre + bi * lim) / den
    ci = (bi * lre - nr * lim) / den
    b_r = bre_ref[...]
    b_i = bim_ref[...]
    bbr = cr * b_r - ci * b_i
    bbi = cr * b_i + ci * b_r
    lbr_o[...] = br
    lbi_o[...] = bi
    l2r_o[...] = br * br - bi * bi
    l2i_o[...] = 2.0 * br * bi
    bbr_o[...] = bbr
    bbi_o[...] = bbi
    lbbr_o[...] = br * bbr - bi * bbi
    lbbi_o[...] = br * bbi + bi * bbr


def _s5_params(lam_re, lam_im, log_step, b_re, b_im):
    n = DEPTH * 2 * SSM_GROUPS
    lre = lam_re.reshape(n, 1, SSM_STATE)
    lim = lam_im.reshape(n, 1, SSM_STATE)
    ls = log_step.reshape(n, 1, 1)
    btr = b_re.transpose(0, 1, 2, 4, 3).reshape(n, SSM_GROUP, SSM_STATE)
    bti = b_im.transpose(0, 1, 2, 4, 3).reshape(n, SSM_GROUP, SSM_STATE)
    small = jax.ShapeDtypeStruct((n, 1, SSM_STATE), F32)
    big = jax.ShapeDtypeStruct((n, SSM_GROUP, SSM_STATE), F32)
    return pl.pallas_call(
        _s5_param_kernel,
        out_shape=[small, small, small, small, big, big, big, big],
        compiler_params=pltpu.CompilerParams(vmem_limit_bytes=VMEM_LIMIT),
        name="s5_params",
    )(lre, lim, ls, btr, bti)


def _s5_layouts(lam_re, lam_im, log_step, b_re, b_im, c_re, c_im):
    lbr, lbi, l2r, l2i, bbr, bbi, lbbr, lbbi = _s5_params(lam_re, lam_im, log_step, b_re, b_im)
    eye = jnp.eye(SSM_BLK_GROUPS, dtype=F32)
    lead = (DEPTH, 2, SSM_BLKS, SSM_BLK_GROUPS)
    blocked = (DEPTH, 2, SSM_BLKS, SSM_BLK_CH, SSM_BLK_ST)

    def in_block(x):
        x = x.reshape(lead + (SSM_GROUP, SSM_STATE))
        return jnp.einsum('ldbgcp,gh->ldbgchp', x, eye).reshape(blocked)

    def out_block(x):
        x = x.reshape(lead + (SSM_GROUP, SSM_STATE))
        return jnp.einsum('ldbgcp,gh->ldbhcgp', x, eye).reshape(blocked)

    wb = jnp.concatenate([
        jnp.concatenate([in_block(bbr), in_block(bbi)], axis=-1),
        jnp.concatenate([in_block(lbbr), in_block(lbbi)], axis=-1)], axis=-2).astype(BF16)
    wc = jnp.concatenate([out_block(c_re), out_block(-c_im)], axis=-1).astype(BF16)

    def table(one, two):
        one = one.reshape(DEPTH, 2, SSM_BLKS, 1, SSM_BLK_ST)
        two = two.reshape(DEPTH, 2, SSM_BLKS, 1, SSM_BLK_ST)
        half = SUBLANE // 2
        fwd = jnp.concatenate([jnp.broadcast_to(one[:, 0:1], (DEPTH, 1, SSM_BLKS, half, SSM_BLK_ST)),
                               jnp.broadcast_to(two[:, 0:1], (DEPTH, 1, SSM_BLKS, half, SSM_BLK_ST))], axis=3)
        bwd = jnp.concatenate([jnp.broadcast_to(two[:, 1:2], (DEPTH, 1, SSM_BLKS, half, SSM_BLK_ST)),
                               jnp.broadcast_to(one[:, 1:2], (DEPTH, 1, SSM_BLKS, half, SSM_BLK_ST))], axis=3)
        return jnp.concatenate([fwd, bwd], axis=1)

    return wb, wc, table(lbr, l2r), table(lbi, l2i)


def _s5_scan_kernel(uf_ref, ub_ref, wb_ref, wc_ref, tre_ref, tim_ref, yf_ref, yb_ref,
                    tmaj, xs, carry):
    @pl.when(pl.program_id(1) == 0)
    def _():
        carry[...] = jnp.zeros(carry.shape, F32)

    n_tiles = S5_ROWS // SUBLANE
    half = SUBLANE // 2
    io_refs = (uf_ref, ub_ref, yf_ref, yb_ref)

    def time_major(blk, d):
        ch = slice(blk * SSM_BLK_CH, (blk + 1) * SSM_BLK_CH)
        for b in range(BATCH):
            tmaj[blk, d, pl.ds(b, S5_STEPS, stride=BATCH), :] = io_refs[d][b, :, ch]
        return tmaj[blk, d]

    def batch_major(y, blk, d):
        ch = slice(blk * SSM_BLK_CH, (blk + 1) * SSM_BLK_CH)
        tmaj[blk, 2 + d] = y
        for b in range(BATCH):
            io_refs[2 + d][b, :, ch] = tmaj[blk, 2 + d, pl.ds(b, S5_STEPS, stride=BATCH), :]

    def paired_lhs(u, take_upper):
        u3 = u.reshape(n_tiles, SUBLANE, SSM_BLK_CH)
        swapped = pltpu.roll(u3, half, axis=1)
        sub = lax.broadcasted_iota(jnp.int32, u3.shape, 1)
        keep = (sub >= half) if take_upper else (sub < half)
        nb = jnp.where(keep, swapped, 0.0).reshape(S5_ROWS, SSM_BLK_CH)
        return jnp.concatenate([u, nb], axis=1).astype(BF16)

    for blk in range(S5_NB):
        for d in range(2):
            xs[blk, d] = _bdot(paired_lhs(time_major(blk, d), d == 0), wb_ref[0, d, blk])

    lower = lax.broadcasted_iota(jnp.int32, (SUBLANE, S5_SCAN_LANES), 0) < half
    for blk in range(S5_NB):
        for hh in range(SSM_BLK_ST // S5_SCAN_LANES):
            lo = hh * S5_SCAN_LANES
            re = slice(lo, lo + S5_SCAN_LANES)
            im = slice(SSM_BLK_ST + lo, SSM_BLK_ST + lo + S5_SCAN_LANES)
            afr, afi = tre_ref[0, 0, blk, :, re], tim_ref[0, 0, blk, :, re]
            abr, abi = tre_ref[0, 1, blk, :, re], tim_ref[0, 1, blk, :, re]
            hfr, hfi, hbr, hbi = (carry[blk, idx, :, re] for idx in range(4))
            for k in range(n_tiles):
                rf = slice(k * SUBLANE, (k + 1) * SUBLANE)
                pr = jnp.where(lower, pltpu.roll(hfr, half, axis=0), hfr)
                pi = jnp.where(lower, pltpu.roll(hfi, half, axis=0), hfi)
                hfr = xs[blk, 0, rf, re] + (afr * pr - afi * pi)
                hfi = xs[blk, 0, rf, im] + (afr * pi + afi * pr)
                xs[blk, 0, rf, re] = hfr
                xs[blk, 0, rf, im] = hfi
                kb = n_tiles - 1 - k
                rb = slice(kb * SUBLANE, (kb + 1) * SUBLANE)
                qr = jnp.where(lower, hbr, pltpu.roll(hbr, half, axis=0))
                qi = jnp.where(lower, hbi, pltpu.roll(hbi, half, axis=0))
                hbr = xs[blk, 1, rb, re] + (abr * qr - abi * qi)
                hbi = xs[blk, 1, rb, im] + (abr * qi + abi * qr)
                xs[blk, 1, rb, re] = hbr
                xs[blk, 1, rb, im] = hbi
            for idx, val in enumerate((hfr, hfi, hbr, hbi)):
                carry[blk, idx, :, re] = val

    for blk in range(S5_NB):
        for d in range(2):
            y = lax.dot_general(xs[blk, d].astype(BF16), wc_ref[0, d, blk],
                                (((1,), (1,)), ((), ())), preferred_element_type=F32)
            batch_major(y, blk, d)


def _s5_scan(proj, wb, wc, tre, tim, layer):
    last = S5_CHUNKS - 1
    width = S5_NB * SSM_BLK_CH
    u_blk0 = U_COL0 // width
    proj3 = proj.reshape(BATCH, L_TOK, PROJ_COLS)
    y_shape = jax.ShapeDtypeStruct((BATCH, L_TOK, SSM_WIDTH), F32)
    per_blk = lambda *tail: pl.BlockSpec((1, 2, S5_NB) + tail, lambda j, c: (layer, 0, j, 0, 0))
    yf, yb = pl.pallas_call(
        _s5_scan_kernel,
        grid=(SSM_BLKS // S5_NB, S5_CHUNKS),
        in_specs=[
            pl.BlockSpec((BATCH, S5_STEPS, width), lambda j, c: (0, c, u_blk0 + j)),
            pl.BlockSpec((BATCH, S5_STEPS, width), lambda j, c: (0, last - c, u_blk0 + j)),
            per_blk(2 * SSM_BLK_CH, 2 * SSM_BLK_ST),
            per_blk(SSM_BLK_CH, 2 * SSM_BLK_ST),
            per_blk(SUBLANE, SSM_BLK_ST),
            per_blk(SUBLANE, SSM_BLK_ST),
        ],
        out_specs=[
            pl.BlockSpec((BATCH, S5_STEPS, width), lambda j, c: (0, c, j)),
            pl.BlockSpec((BATCH, S5_STEPS, width), lambda j, c: (0, last - c, j)),
        ],
        out_shape=[y_shape, y_shape],
        scratch_shapes=[
            pltpu.VMEM((S5_NB, 4, S5_ROWS, SSM_BLK_CH), F32),
            pltpu.VMEM((S5_NB, 2, S5_ROWS, 2 * SSM_BLK_ST), F32),
            pltpu.VMEM((S5_NB, 4, SUBLANE, SSM_BLK_ST), F32),
        ],
        compiler_params=_cparams(("parallel", "arbitrary")),
        name="s5_scan",
    )(proj3, proj3, wb, wc, tre, tim)
    return yf.reshape(T_TOK, SSM_WIDTH), yb.reshape(T_TOK, SSM_WIDTH)


def _gelu_tanh(x):
    return 0.5 * x * (1.0 + jnp.tanh(math.sqrt(2.0 / math.pi) * (x + 0.044715 * (x * x * x))))


def _glu_kernel(yf_ref, yb_ref, u_ref, d_ref, w_ref, gn_ref, o_ref):
    y = (yf_ref[...] + yb_ref[...]) + d_ref[0] * u_ref[...]
    g = _gelu_tanh(y)
    z = _bdot(g.astype(BF16), w_ref[0].astype(BF16))
    o_ref[...] = _rms(g * jax.nn.sigmoid(z), gn_ref[0]).astype(BF16)


def _glu(yf, yb, proj, d, w, gn, layer, tm):
    t = yf.shape[0]
    row = pl.BlockSpec((tm, SSM_WIDTH), lambda i: (i, 0))
    vec = pl.BlockSpec((1, 1, SSM_WIDTH), lambda i: (layer, 0, 0))
    return pl.pallas_call(
        _glu_kernel,
        grid=(t // tm,),
        in_specs=[row, row,
                  pl.BlockSpec((tm, SSM_WIDTH), lambda i: (i, U_COL0 // SSM_WIDTH)),
                  vec,
                  pl.BlockSpec((1, SSM_WIDTH, SSM_WIDTH), lambda i: (layer, 0, 0)),
                  vec],
        out_specs=row,
        out_shape=jax.ShapeDtypeStruct((t, SSM_WIDTH), BF16),
        compiler_params=_cparams(("parallel",)),
        name="s5_glu",
    )(yf, yb, proj, d.reshape(DEPTH, 1, SSM_WIDTH), w, gn.reshape(DEPTH, 1, SSM_WIDTH))


def _outproj_kernel(h_ref, a_ref, sn_ref, ga_ref, wa_ref, ws_ref, o_ref, an_ref):
    @pl.when(pl.program_id(1) == 0)
    def _():
        an_ref[...] = _rms(a_ref[...], ga_ref[0]).astype(BF16)

    o_ref[...] = (h_ref[...] + _bdot(an_ref[...], wa_ref[0].astype(BF16))
                  + _bdot(sn_ref[...], ws_ref[0].astype(BF16)))


def _outproj(h, att, ssm_n, ga, w, layer, tm, tn):
    t, d = h.shape
    return pl.pallas_call(
        _outproj_kernel,
        grid=(t // tm, d // tn),
        in_specs=[
            pl.BlockSpec((tm, tn), lambda i, j: (i, j)),
            pl.BlockSpec((tm, ATT_WIDTH), lambda i, j: (i, 0)),
            pl.BlockSpec((tm, SSM_WIDTH), lambda i, j: (i, 0)),
            pl.BlockSpec((1, 1, ATT_WIDTH), lambda i, j: (layer, 0, 0)),
            pl.BlockSpec((1, ATT_WIDTH, tn), lambda i, j: (layer, 0, j)),
            pl.BlockSpec((1, SSM_WIDTH, tn), lambda i, j: (layer, 1, j)),
        ],
        out_specs=pl.BlockSpec((tm, tn), lambda i, j: (i, j)),
        out_shape=jax.ShapeDtypeStruct((t, d), F32),
        scratch_shapes=[pltpu.VMEM((tm, ATT_WIDTH), BF16)],
        input_output_aliases={0: 0},
        compiler_params=_cparams(("parallel", "arbitrary")),
        name="out_proj",
    )(h, att, ssm_n, ga.reshape(DEPTH, 1, ATT_WIDTH), w, w)


FFN_SUB = 256


def _ffn_kernel(x_ref, g_ref, wg_ref, wu_ref, wd_ref, gn_ref, o_ref, nxt_ref, hn_ref):
    j = pl.program_id(1)

    @pl.when(j == 0)
    def _():
        x = x_ref[...]
        hn_ref[...] = _rms(x, g_ref[0]).astype(BF16)
        o_ref[...] = x

    hn = hn_ref[...]
    tf = wg_ref.shape[2]
    part = None
    for lo in range(0, tf, FFN_SUB):
        cols = slice(lo, lo + FFN_SUB)
        a = _bdot(hn, wg_ref[0, :, cols].astype(BF16))
        b = _bdot(hn, wu_ref[0, :, cols].astype(BF16))
        hid = ((a * jax.nn.sigmoid(a)) * b).astype(BF16)
        down = _bdot(hid, wd_ref[0, cols, :].astype(BF16))
        part = down if part is None else part + down
    o_ref[...] += part

    @pl.when(j == pl.num_programs(1) - 1)
    def _():
        nxt_ref[...] = _rms(o_ref[...], gn_ref[0]).astype(BF16)


def _ffn(h, g, gn, wg, wu, wd, layer, tm, tf):
    t, d = h.shape
    f = wg.shape[2]
    idx = layer // 2
    once = pl.Buffered(1)
    row = pl.BlockSpec((tm, d), lambda i, j: (i, 0), pipeline_mode=once)
    return pl.pallas_call(
        _ffn_kernel,
        grid=(t // tm, f // tf),
        in_specs=[
            row,
            pl.BlockSpec((1, 1, d), lambda i, j: (layer, 0, 0)),
            pl.BlockSpec((1, d, tf), lambda i, j: (idx, 0, j)),
            pl.BlockSpec((1, d, tf), lambda i, j: (idx, 0, j)),
            pl.BlockSpec((1, tf, d), lambda i, j: (idx, j, 0)),
            pl.BlockSpec((1, 1, d), lambda i, j: (layer + 1, 0, 0)),
        ],
        out_specs=[row, row],
        out_shape=[jax.ShapeDtypeStruct((t, d), F32), jax.ShapeDtypeStruct((t, d), BF16)],
        scratch_shapes=[pltpu.VMEM((tm, d), BF16)],
        input_output_aliases={0: 0},
        compiler_params=_cparams(("parallel", "arbitrary")),
        name="dense_ffn",
    )(h, g.reshape(DEPTH, 1, d), wg, wu, wd, gn.reshape(DEPTH, 1, d))


MOE_TM = 344
MOE_NT = (2 * T_TOK) // MOE_TM + N_EXPERTS
MOE_NP = MOE_NT * MOE_TM
DMA_FANOUT = 4


def _split_bf16(x):
    hi = x.astype(BF16)
    lo = (x - hi.astype(F32)).astype(BF16)
    return hi, lo


def _lane_pick(x, lane, k):
    return jnp.sum(jnp.where(lane == k, x, 0.0), axis=-1, keepdims=True)


ROW_SLABS = D_MODEL // LANE


def _row_slab(ref, c, rows):
    return ref.at[pl.ds(c, rows, stride=ROW_SLABS), :]


def _router_kernel(x_ref, g_ref, wr_ref, sel_ref, wts_ref, rows_ref):
    x = x_ref[...]
    for c in range(ROW_SLABS):
        _row_slab(rows_ref, c, x.shape[0])[...] = x[:, c * LANE:(c + 1) * LANE]
    hn = _rms(x, g_ref[0])
    xh, xl = _split_bf16(hn)
    wh, wl = _split_bf16(wr_ref[0])
    logits = _bdot(xh, wh) + (_bdot(xh, wl) + _bdot(xl, wh)) + _bdot(xl, wl)
    lane = lax.broadcasted_iota(jnp.int32, logits.shape, 1).astype(F32)
    neg = jnp.float32(-jnp.inf)
    logits = jnp.where(lane < N_EXPERTS, logits, neg)
    v1 = jnp.max(logits, axis=-1, keepdims=True)
    i1 = jnp.min(jnp.where(logits == v1, lane, float(LANE)), axis=-1, keepdims=True)
    rest = jnp.where(lane == i1, neg, logits)
    v2 = jnp.max(rest, axis=-1, keepdims=True)
    i2 = jnp.min(jnp.where(rest == v2, lane, float(LANE)), axis=-1, keepdims=True)
    e2 = jnp.exp(v2 - v1)
    w1 = 1.0 / (1.0 + e2)
    w2 = e2 / (1.0 + e2)
    sel_ref[...] = jnp.where((lane == i1) | (lane == i2), 1.0, 0.0)
    wts_ref[...] = jnp.where(lane == 0, w1, jnp.where(lane == 1, w2,
                             jnp.where(lane == 2, i1, jnp.where(lane == 3, i2, 0.0))))


def _router(h, g, wr_pad, layer, tm):
    t, d = h.shape
    idx = layer // 2
    row = pl.BlockSpec((tm, LANE), lambda i: (i, 0))
    tab = jax.ShapeDtypeStruct((t, LANE), F32)
    return pl.pallas_call(
        _router_kernel,
        grid=(t // tm,),
        in_specs=[
            pl.BlockSpec((tm, d), lambda i: (i, 0)),
            pl.BlockSpec((1, 1, d), lambda i: (layer, 0, 0)),
            pl.BlockSpec((1, d, LANE), lambda i: (idx, 0, 0)),
        ],
        out_specs=[row, row, pl.BlockSpec((tm * ROW_SLABS, LANE), lambda i: (i, 0))],
        out_shape=[tab, tab, jax.ShapeDtypeStruct((t * ROW_SLABS, LANE), F32)],
        compiler_params=_cparams(("parallel",)),
        name="moe_router",
    )(h, g.reshape(DEPTH, 1, d), wr_pad)


def _positions_kernel(sel_ref, wts_ref, pos_ref, meta_ref, cnt_ref, off_ref):
    p = pl.program_id(0)
    i = pl.program_id(1)
    tm = sel_ref.shape[0]
    lane = lax.broadcasted_iota(jnp.int32, (1, LANE), 1).astype(F32)
    sel = sel_ref[...]

    @pl.when((p == 0) & (i == 0))
    def _():
        cnt_ref[...] = jnp.zeros((1, LANE), F32)

    @pl.when(p == 0)
    def _():
        cnt_ref[...] += jnp.sum(sel, axis=0, keepdims=True)

    @pl.when((p == 1) & (i == 0))
    def _():
        cnt = cnt_ref[...]
        tiles = jnp.zeros((1, LANE), F32)
        for k in range(MOE_NT):
            tiles = tiles + jnp.where(cnt > float(k * MOE_TM), 1.0, 0.0)
        padded = tiles * float(MOE_TM)
        off = jnp.zeros((1, LANE), F32)
        for e in range(N_EXPERTS):
            off = off + jnp.where(lane > e, _lane_pick(padded, lane, e), 0.0)
        end = off + padded
        tile_start = lane * float(MOE_TM)
        owner = jnp.zeros((1, LANE), F32)
        for e in range(N_EXPERTS):
            owner = owner + jnp.where(tile_start >= _lane_pick(end, lane, e), 1.0, 0.0)
        owner = jnp.minimum(owner, float(N_EXPERTS - 1))
        used = jnp.sum(tiles, axis=-1, keepdims=True)
        off_ref[...] = off
        cnt_ref[...] = jnp.zeros((1, LANE), F32)
        row = lax.broadcasted_iota(jnp.int32, (SUBLANE, LANE), 0)
        meta = jnp.where(row == 0, owner, jnp.where(row == 1, used, jnp.where(row == 2, cnt, off)))
        meta_ref[...] = meta.astype(jnp.int32)

    @pl.when(p == 1)
    def _():
        r = lax.broadcasted_iota(jnp.int32, (tm, tm), 0)
        c = lax.broadcasted_iota(jnp.int32, (tm, tm), 1)
        earlier = jnp.where(c < r, 1.0, 0.0).astype(BF16)
        rank = _bdot(earlier, sel.astype(BF16)) + cnt_ref[...]
        slot = off_ref[...] + rank
        wts = wts_ref[...]
        lane_t = lax.broadcasted_iota(jnp.int32, (tm, LANE), 1).astype(F32)
        p1 = _lane_pick(slot, lane_t, _lane_pick(wts, lane_t, 2))
        p2 = _lane_pick(slot, lane_t, _lane_pick(wts, lane_t, 3))
        pos_ref[...] = jnp.where(lane_t == 0, p1, jnp.where(lane_t == 1, p2, 0.0)).astype(jnp.int32)
        cnt_ref[...] += jnp.sum(sel, axis=0, keepdims=True)


def _positions(sel, wts, tm):
    t = sel.shape[0]
    row = pl.BlockSpec((tm, LANE), lambda p, i: (i, 0))
    return pl.pallas_call(
        _positions_kernel,
        grid=(2, t // tm),
        in_specs=[row, row],
        out_specs=[
            pl.BlockSpec((tm, LANE), lambda p, i: (i * p, 0)),
            pl.BlockSpec((SUBLANE, LANE), lambda p, i: (0, 0)),
        ],
        out_shape=[jax.ShapeDtypeStruct((t, LANE), jnp.int32),
                   jax.ShapeDtypeStruct((SUBLANE, LANE), jnp.int32)],
        scratch_shapes=[pltpu.VMEM((1, LANE), F32), pltpu.VMEM((1, LANE), F32)],
        compiler_params=_cparams(("arbitrary", "arbitrary")),
        name="moe_positions",
    )(sel, wts)


def _experts_kernel(pos1_ref, pos2_ref, owner_ref, used_ref, rows_ref, g_ref, wg_ref, wu_ref, wd_ref,
                    o_ref, src_ref, xbuf, sems):
    i = pl.program_id(0)
    used = used_ref[0]
    cur = lax.rem(i, 2)

    def row_copy(tok, r, buf):
        return pltpu.make_async_copy(rows_ref.at[pl.ds(tok * ROW_SLABS, ROW_SLABS), :],
                                     xbuf.at[buf, pl.ds(r * ROW_SLABS, ROW_SLABS), :], sems.at[buf])

    def fetch_tile(tile, buf):
        def issue(q, c):
            for u in range(DMA_FANOUT):
                r = q * DMA_FANOUT + u
                row_copy(src_ref[tile * MOE_TM + r], r, buf).start(priority=u % 2)
            return c

        lax.fori_loop(0, MOE_TM // DMA_FANOUT, issue, 0)

    @pl.when(i == 0)
    def _():
        def clear(p, c):
            src_ref[p] = 0
            return c

        lax.fori_loop(0, MOE_NP, clear, 0, unroll=8)

        def fill(t, c):
            src_ref[pos1_ref[t]] = t
            src_ref[pos2_ref[t]] = t
            return c

        lax.fori_loop(0, T_TOK, fill, 0, unroll=4)
        fetch_tile(0, 0)

    @pl.when(i + 1 < used)
    def _():
        fetch_tile(i + 1, 1 - cur)

    live = i < used

    @pl.when(live)
    def _():
        pltpu.make_async_copy(rows_ref.at[pl.ds(0, MOE_TM * ROW_SLABS), :], xbuf.at[cur],
                              sems.at[cur]).wait()
        x_ref = xbuf.at[cur]
        slabs = [_row_slab(x_ref, c, MOE_TM)[...] for c in range(ROW_SLABS)]
        ssq = slabs[0] * slabs[0]
        for c in range(1, ROW_SLABS):
            ssq = ssq + slabs[c] * slabs[c]
        inv = lax.rsqrt(jnp.sum(ssq, axis=-1, keepdims=True) / D_MODEL + EPS)
        g = g_ref[0]
        hn = jnp.concatenate([slabs[c] * inv * g[:, c * LANE:(c + 1) * LANE]
                              for c in range(ROW_SLABS)], axis=1).astype(BF16)
        a = _bdot(hn, wg_ref[0, 0])
        b = _bdot(hn, wu_ref[0, 0])
        hid = ((a * jax.nn.sigmoid(a)) * b).astype(BF16)
        res = _bdot(hid, wd_ref[0, 0])
        for c in range(ROW_SLABS):
            _row_slab(o_ref, c, MOE_TM)[...] = res[:, c * LANE:(c + 1) * LANE]

    @pl.when(jnp.logical_not(live))
    def _():
        o_ref[...] = jnp.zeros(o_ref.shape, F32)


def _experts(rows, pos1, pos2, owner, used, g, wg, wu, wd, layer):
    d = D_MODEL
    idx = layer // 2
    blk = MOE_TM * ROW_SLABS

    w_map = lambda i, p1, p2, o, u: (idx, o[jnp.minimum(i, u[0] - 1)], 0, 0)
    return pl.pallas_call(
        _experts_kernel,
        grid_spec=pltpu.PrefetchScalarGridSpec(
            num_scalar_prefetch=4,
            grid=(MOE_NT,),
            in_specs=[
                pl.BlockSpec(memory_space=pl.ANY),
                pl.BlockSpec((1, 1, d), lambda i, p1, p2, o, u: (layer, 0, 0)),
                pl.BlockSpec((1, 1, d, D_FF_EXPERT), w_map),
                pl.BlockSpec((1, 1, d, D_FF_EXPERT), w_map),
                pl.BlockSpec((1, 1, D_FF_EXPERT, d), w_map),
            ],
            out_specs=pl.BlockSpec((blk, LANE), lambda i, p1, p2, o, u: (i, 0)),
            scratch_shapes=[
                pltpu.SMEM((MOE_NP,), jnp.int32),
                pltpu.VMEM((2, blk, LANE), F32),
                pltpu.SemaphoreType.DMA((2,)),
            ],
        ),
        out_shape=jax.ShapeDtypeStruct((MOE_NP * ROW_SLABS, LANE), F32),
        compiler_params=_cparams(("arbitrary",)),
        name="moe_experts",
    )(pos1, pos2, owner, used, rows, g.reshape(DEPTH, 1, d), wg, wu, wd)


def _combine_kernel(pos1_ref, pos2_ref, h_ref, wts_ref, ys_ref, gn_ref, o_ref, nxt_ref,
                    y1, y2, sem1, sem2):
    tm = h_ref.shape[0]
    base = pl.program_id(0) * tm

    def row_copy(slot, r, buf, sem):
        return pltpu.make_async_copy(ys_ref.at[pl.ds(slot * ROW_SLABS, ROW_SLABS), :],
                                     buf.at[pl.ds(r * ROW_SLABS, ROW_SLABS), :], sem)

    def fetch(q, c):
        for u in range(DMA_FANOUT):
            r = q * DMA_FANOUT + u
            row_copy(pos1_ref[base + r], r, y1, sem1).start(priority=0)
            row_copy(pos2_ref[base + r], r, y2, sem2).start(priority=1)
        return c

    lax.fori_loop(0, tm // DMA_FANOUT, fetch, 0)

    pltpu.make_async_copy(ys_ref.at[pl.ds(0, tm * ROW_SLABS), :], y1, sem1).wait()
    pltpu.make_async_copy(ys_ref.at[pl.ds(0, tm * ROW_SLABS), :], y2, sem2).wait()
    wts = wts_ref[...]
    lane = lax.broadcasted_iota(jnp.int32, wts.shape, 1).astype(F32)
    w1 = _lane_pick(wts, lane, 0)
    w2 = _lane_pick(wts, lane, 1)
    ssq = jnp.zeros((tm, 1), F32)
    for c in range(ROW_SLABS):
        cols = slice(c * LANE, (c + 1) * LANE)
        hc = h_ref[:, cols] + (w1 * _row_slab(y1, c, tm)[...] + w2 * _row_slab(y2, c, tm)[...])
        o_ref[:, cols] = hc
        ssq = ssq + jnp.sum(hc * hc, axis=-1, keepdims=True)
    normed = o_ref[...] * lax.rsqrt(ssq / D_MODEL + EPS) * gn_ref[...]
    if nxt_ref is None:
        o_ref[...] = normed
    else:
        nxt_ref[...] = normed.astype(BF16)


def _combine_mid_kernel(pos1_ref, pos2_ref, h_ref, wts_ref, ys_ref, gn_ref, o_ref, nxt_ref,
                        y1, y2, sem1, sem2):
    _combine_kernel(pos1_ref, pos2_ref, h_ref, wts_ref, ys_ref, gn_ref, o_ref, nxt_ref,
                    y1, y2, sem1, sem2)


def _combine_last_kernel(pos1_ref, pos2_ref, h_ref, wts_ref, ys_ref, gn_ref, o_ref,
                         y1, y2, sem1, sem2):
    _combine_kernel(pos1_ref, pos2_ref, h_ref, wts_ref, ys_ref, gn_ref, o_ref, None,
                    y1, y2, sem1, sem2)


def _combine(h, wts, ys, pos1, pos2, gn, last, tm):
    t, d = h.shape
    row = pl.BlockSpec((tm, d), lambda i, a, b: (i, 0))
    h_out = jax.ShapeDtypeStruct((t, d), F32)
    return pl.pallas_call(
        _combine_last_kernel if last else _combine_mid_kernel,
        grid_spec=pltpu.PrefetchScalarGridSpec(
            num_scalar_prefetch=2,
            grid=(t // tm,),
            in_specs=[
                row,
                pl.BlockSpec((tm, LANE), lambda i, a, b: (i, 0)),
                pl.BlockSpec(memory_space=pl.ANY),
                pl.BlockSpec((1, d), lambda i, a, b: (0, 0)),
            ],
            out_specs=row if last else [row, row],
            scratch_shapes=[
                pltpu.VMEM((tm * ROW_SLABS, LANE), F32),
                pltpu.VMEM((tm * ROW_SLABS, LANE), F32),
                pltpu.SemaphoreType.DMA(()),
                pltpu.SemaphoreType.DMA(()),
            ],
        ),
        out_shape=h_out if last else [h_out, jax.ShapeDtypeStruct((t, d), BF16)],
        input_output_aliases={2: 0},
        compiler_params=_cparams(("arbitrary",)),
        name="moe_combine",
    )(pos1, pos2, h, wts, ys, gn.reshape(1, d))


def _moe(h, g, gn, wr_pad, wg, wu, wd, layer, last):
    sel, wts, rows = _router(h, g, wr_pad, layer, tm=688)
    pos, meta = _positions(sel, wts, tm=688)
    pos1 = pos[:, 0]
    pos2 = pos[:, 1]
    owner = meta[0, :MOE_NT]
    used = meta[1, :1]
    ys = _experts(rows, pos1, pos2, owner, used, g, wg, wu, wd, layer)
    return _combine(h, wts, ys, pos1, pos2, gn, last, tm=688)


def _rope_partner(w):
    half = QK_ROPE // 2
    return jnp.concatenate([-w[..., half:], w[..., :half]], axis=-1)


def _pad_cols(w, width):
    return jnp.pad(w, [(0, 0)] * (w.ndim - 1) + [(0, width - w.shape[-1])])


def _in_proj_layout(w_in):
    kr = w_in[..., Q_LORA + KV_LORA:Q_LORA + KV_LORA + QK_ROPE]
    return jnp.concatenate([
        w_in[..., :Q_LORA + KV_LORA],
        _pad_cols(kr, LANE),
        _pad_cols(_rope_partner(kr), LANE),
        w_in[..., Q_LORA + KV_LORA + QK_ROPE:]], axis=-1)


def _q_layouts(w_uq):
    w = w_uq.reshape(DEPTH, Q_LORA, ATT_HEADS, QK_NOPE + QK_ROPE)
    main = _pad_cols(w, HEAD_PAD).reshape(DEPTH, Q_LORA, ATT_HEADS * HEAD_PAD)
    rot = _pad_cols(_rope_partner(w[..., QK_NOPE:]), LANE).reshape(DEPTH, Q_LORA, ATT_HEADS * LANE)
    return main, rot


def _rope_tables():
    inv = ROPE_THETA ** (-jnp.arange(0, QK_ROPE, 2, dtype=F32) / QK_ROPE)
    ang = jnp.arange(L_TOK, dtype=F32)[:, None] * inv[None, :]
    cos = jnp.cos(ang)
    sin = jnp.sin(ang)
    cos_t = _pad_cols(jnp.concatenate([cos, cos], axis=-1), LANE)
    sin_t = _pad_cols(jnp.concatenate([sin, sin], axis=-1), LANE)
    return cos_t, sin_t


def kernel(x, meta_tokens, mix_norm, w_in, q_norm, w_uq, kv_norm, w_ukv, ssm_lambda_re, ssm_lambda_im, ssm_log_step, ssm_b_re, ssm_b_im, ssm_c_re, ssm_c_im, ssm_d, ssm_w_glu, attn_out_norm, ssm_out_norm, w_out, ffn_norm, dense_w_gate, dense_w_up, dense_w_down, moe_router, moe_w_gate, moe_w_up, moe_w_down, final_norm):
    meta = jnp.broadcast_to(meta_tokens[None].astype(x.dtype), (BATCH, N_META, D_MODEL))
    h = jnp.concatenate([meta, x], axis=1).reshape(T_TOK, D_MODEL)
    cos_t, sin_t = _rope_tables()
    w_in_l = _in_proj_layout(w_in)
    wq_main, wq_rot = _q_layouts(w_uq)
    wr_pad = _pad_cols(moe_router, LANE)
    moe_w_gate, moe_w_up, moe_w_down = (w.astype(BF16) for w in (moe_w_gate, moe_w_up, moe_w_down))
    wb, wc, tre, tim = _s5_layouts(ssm_lambda_re, ssm_lambda_im, ssm_log_step,
                                   ssm_b_re, ssm_b_im, ssm_c_re, ssm_c_im)
    hn = _rms_cast(h, mix_norm, 0, tm=688)
    for layer in range(DEPTH):
        proj = _in_proj(hn, w_in_l, layer, tm=2064, tn=512)
        q, kv, kpe = _upproj(proj, q_norm, kv_norm, wq_main, wq_rot, w_ukv, cos_t, sin_t, layer,
                             tm=688)
        att = _attention(q, kv, kpe, tq=688)
        yf, yb = _s5_scan(proj, wb, wc, tre, tim, layer)
        ssm_n = _glu(yf, yb, proj, ssm_d, ssm_w_glu, ssm_out_norm, layer, tm=688)
        h = _outproj(h, att, ssm_n, attn_out_norm, w_out, layer, tm=1376, tn=512)
        last = layer == DEPTH - 1
        if layer % 2 == 0:
            h, hn = _ffn(h, ffn_norm, mix_norm, dense_w_gate, dense_w_up, dense_w_down, layer,
                         tm=1032, tf=512)
        elif last:
            h = _moe(h, ffn_norm, final_norm, wr_pad, moe_w_gate, moe_w_up, moe_w_down, layer, last)
        else:
            h, hn = _moe(h, ffn_norm, mix_norm[layer + 1], wr_pad, moe_w_gate, moe_w_up, moe_w_down,
                         layer, last)
    return h.reshape(BATCH, L_TOK, D_MODEL)[:, N_META:]
```

```python
import functools
import math

import jax
import jax.numpy as jnp
from jax import lax
from jax.experimental import pallas as pl
from jax.experimental.pallas import tpu as pltpu

F32 = jnp.float32
BF16 = jnp.bfloat16

D_MODEL = 2048
BATCH = 4
SEQ = 2048
DEPTH = 4
N_META = 16
EPS = 1e-6
ATT_HEADS = 8
QK_NOPE = 128
QK_ROPE = 64
V_DIM = 128
Q_LORA = 512
KV_LORA = 256
ROPE_THETA = 10000.0
ATT_WIDTH = ATT_HEADS * V_DIM
SSM_GROUP = 16
SSM_WIDTH = D_MODEL - ATT_WIDTH
SSM_GROUPS = SSM_WIDTH // SSM_GROUP
SSM_STATE = 64
D_FF = 5632
N_EXPERTS = 8
D_FF_EXPERT = 1408

L_TOK = N_META + SEQ
T_TOK = L_TOK * BATCH
LANE = 128
SUBLANE = 8
HEAD_PAD = 256
L_PAD = 2176
PROJ_COLS = 2048
U_COL0 = 1024
VMEM_LIMIT = 58 * 1024 * 1024

SSM_BLK_GROUPS = 8
SSM_BLKS = SSM_GROUPS // SSM_BLK_GROUPS
SSM_BLK_CH = SSM_BLK_GROUPS * SSM_GROUP
SSM_BLK_ST = SSM_BLK_GROUPS * SSM_STATE
S5_CHUNKS = 6
S5_STEPS = L_TOK // S5_CHUNKS
S5_ROWS = S5_STEPS * BATCH
S5_NB = 2
S5_SCAN_LANES = 256


def _cparams(sem):
    return pltpu.CompilerParams(dimension_semantics=sem, vmem_limit_bytes=VMEM_LIMIT)


def _rms(x, g):
    ms = jnp.mean(x * x, axis=-1, keepdims=True)
    return x * lax.rsqrt(ms + EPS) * g


def _bdot(a, b):
    return jnp.dot(a, b, preferred_element_type=F32)


def _rms_cast_kernel(x_ref, g_ref, o_ref):
    o_ref[...] = _rms(x_ref[...], g_ref[0]).astype(BF16)


def _rms_cast(x, g, layer, tm):
    t, d = x.shape
    return pl.pallas_call(
        _rms_cast_kernel,
        grid=(t // tm,),
        in_specs=[pl.BlockSpec((tm, d), lambda i: (i, 0)),
                  pl.BlockSpec((1, 1, d), lambda i: (layer, 0, 0))],
        out_specs=pl.BlockSpec((tm, d), lambda i: (i, 0)),
        out_shape=jax.ShapeDtypeStruct((t, d), BF16),
        compiler_params=_cparams(("parallel",)),
        name="rms_cast",
    )(x, g.reshape(DEPTH, 1, d))


def _in_proj_kernel(x_ref, w_ref, o_ref):
    o_ref[...] = _bdot(x_ref[...], w_ref[0].astype(BF16))


def _in_proj(hn, w, layer, tm, tn):
    t, d = hn.shape
    n = w.shape[2]
    return pl.pallas_call(
        _in_proj_kernel,
        grid=(t // tm, n // tn),
        in_specs=[
            pl.BlockSpec((tm, d), lambda i, j: (i, 0)),
            pl.BlockSpec((1, d, tn), lambda i, j: (layer, 0, j)),
        ],
        out_specs=pl.BlockSpec((tm, tn), lambda i, j: (i, j)),
        out_shape=jax.ShapeDtypeStruct((t, n), F32),
        compiler_params=_cparams(("parallel", "arbitrary")),
        name="in_proj",
    )(hn, w)


def _upproj_kernel(cq_ref, ckv_ref, kr_ref, krot_ref, qn_ref, kvn_ref, wq_ref, wqr_ref,
                   wkv_ref, cos_ref, sin_ref, q_ref, kv_ref, kpe_ref):
    scale = (QK_NOPE + QK_ROPE) ** -0.5 * math.log2(math.e)
    cqn = _rms(cq_ref[...], qn_ref[0]).astype(BF16)
    a = _bdot(cqn, wq_ref[0].astype(BF16))
    r = _bdot(cqn, wqr_ref[0].astype(BF16))
    c = cos_ref[...]
    s = sin_ref[...]
    last_lane = lax.broadcasted_iota(jnp.int32, c.shape, 1) == LANE - 1
    for h in range(ATT_HEADS):
        lo = h * HEAD_PAD
        q_ref[:, lo:lo + LANE] = (a[:, lo:lo + LANE] * scale).astype(BF16)
        rope = a[:, lo + LANE:lo + HEAD_PAD] * c + r[:, h * LANE:(h + 1) * LANE] * s
        q_ref[:, lo + LANE:lo + HEAD_PAD] = jnp.where(last_lane, 1.0, rope * scale).astype(BF16)
    ckvn = _rms(ckv_ref[...], kvn_ref[0]).astype(BF16)
    kv_ref[...] = _bdot(ckvn, wkv_ref[0].astype(BF16)).astype(BF16)
    kpe_ref[...] = (kr_ref[...] * c + krot_ref[...] * s).astype(BF16)


def _upproj(proj, qn, kvn, wq, wqr, wkv, cos_t, sin_t, layer, tm):
    t = proj.shape[0]
    full = lambda a: pl.BlockSpec((1,) + a.shape[1:], lambda i: (layer, 0, 0))
    qn = qn.reshape(DEPTH, 1, Q_LORA)
    kvn = kvn.reshape(DEPTH, 1, KV_LORA)
    pos_blocks = L_TOK // tm
    return pl.pallas_call(
        _upproj_kernel,
        grid=(t // tm,),
        in_specs=[
            pl.BlockSpec((tm, Q_LORA), lambda i: (i, 0)),
            pl.BlockSpec((tm, KV_LORA), lambda i: (i, Q_LORA // KV_LORA)),
            pl.BlockSpec((tm, LANE), lambda i: (i, (Q_LORA + KV_LORA) // LANE)),
            pl.BlockSpec((tm, LANE), lambda i: (i, (Q_LORA + KV_LORA) // LANE + 1)),
            full(qn),
            full(kvn),
            full(wq),
            full(wqr),
            full(wkv),
            pl.BlockSpec((tm, LANE), lambda i: (i % pos_blocks, 0)),
            pl.BlockSpec((tm, LANE), lambda i: (i % pos_blocks, 0)),
        ],
        out_specs=[
            pl.BlockSpec((tm, ATT_HEADS * HEAD_PAD), lambda i: (i, 0)),
            pl.BlockSpec((tm, ATT_HEADS * (QK_NOPE + V_DIM)), lambda i: (i, 0)),
            pl.BlockSpec((tm, LANE), lambda i: (i, 0)),
        ],
        out_shape=[
            jax.ShapeDtypeStruct((t, ATT_HEADS * HEAD_PAD), BF16),
            jax.ShapeDtypeStruct((t, ATT_HEADS * (QK_NOPE + V_DIM)), BF16),
            jax.ShapeDtypeStruct((t, LANE), BF16),
        ],
        compiler_params=_cparams(("parallel",)),
        name="upproj",
    )(proj, proj, proj, proj, qn, kvn, wq, wqr, wkv, cos_t, sin_t)


ATT_HPS = 4
PAD_BIAS = -1e30


def _attn_kernel(q_ref, kv_ref, kpe_ref, o_ref, k_sc, v_sc):
    @pl.when(pl.program_id(2) == 0)
    def _():
        n_pad = L_PAD - L_TOK
        pad_lane = lax.broadcasted_iota(jnp.int32, (n_pad, LANE), 1)
        k_pad = jnp.where(pad_lane == LANE - 1, PAD_BIAS, 0.0).astype(BF16)
        row_lane = lax.broadcasted_iota(jnp.int32, (L_TOK, LANE), 1)
        ones_col = jnp.where(row_lane == 0, 1.0, 0.0).astype(BF16)
        for hh in range(ATT_HPS):
            lo = hh * HEAD_PAD
            k_sc[hh, 0:L_TOK, 0:LANE] = kv_ref[:, lo:lo + LANE]
            k_sc[hh, 0:L_TOK, LANE:HEAD_PAD] = kpe_ref[...]
            k_sc[hh, L_TOK:L_PAD, 0:LANE] = jnp.zeros((n_pad, LANE), BF16)
            k_sc[hh, L_TOK:L_PAD, LANE:HEAD_PAD] = k_pad
            v_sc[hh, 0:L_TOK, 0:V_DIM] = kv_ref[:, lo + LANE:lo + HEAD_PAD]
            v_sc[hh, 0:L_TOK, V_DIM:HEAD_PAD] = ones_col
            v_sc[hh, L_TOK:L_PAD, :] = jnp.zeros((n_pad, HEAD_PAD), BF16)

    for hh in range(ATT_HPS):
        q = q_ref[:, hh * HEAD_PAD:(hh + 1) * HEAD_PAD]
        s = lax.dot_general(q, k_sc[hh], (((1,), (1,)), ((), ())),
                            preferred_element_type=F32)
        m = jnp.max(s, axis=-1, keepdims=True)
        p = jnp.exp2(s - m).astype(BF16)
        o = _bdot(p, v_sc[hh])
        o_ref[:, hh * V_DIM:(hh + 1) * V_DIM] = o[:, :V_DIM] / o[:, V_DIM:V_DIM + 1]


def _attention(q, kv, kpe, tq):
    nq = L_TOK // tq
    hps = ATT_HPS
    return pl.pallas_call(
        _attn_kernel,
        grid=(BATCH, ATT_HEADS // hps, nq),
        in_specs=[
            pl.BlockSpec((tq, hps * HEAD_PAD), lambda b, h, i: (b * nq + i, h)),
            pl.BlockSpec((L_TOK, hps * HEAD_PAD), lambda b, h, i: (b, h)),
            pl.BlockSpec((L_TOK, LANE), lambda b, h, i: (b, 0)),
        ],
        out_specs=pl.BlockSpec((tq, hps * V_DIM), lambda b, h, i: (b * nq + i, h)),
        out_shape=jax.ShapeDtypeStruct((T_TOK, ATT_WIDTH), F32),
        scratch_shapes=[pltpu.VMEM((hps, L_PAD, HEAD_PAD), BF16),
                        pltpu.VMEM((hps, L_PAD, HEAD_PAD), BF16)],
        compiler_params=_cparams(("parallel", "parallel", "arbitrary")),
        name="attention",
    )(q, kv, kpe)


def _s5_param_kernel(lre_ref, lim_ref, ls_ref, bre_ref, bim_ref,
                     lbr_o, lbi_o, l2r_o, l2i_o, bbr_o, bbi_o, lbbr_o, lbbi_o):
    lre = lre_ref[...]
    lim = lim_ref[...]
    dt = jnp.exp(ls_ref[...])
    mag = jnp.exp(lre * dt)
    ang = lim * dt
    br = mag * jnp.cos(ang)
    bi = mag * jnp.sin(ang)
    nr = br - 1.0
    den = lre * lre + lim * lim
    cr = (nr * lre + bi * lim) / den
    ci = (bi * lre - nr * lim) / den
    b_r = bre_ref[...]
    b_i = bim_ref[...]
    bbr = cr * b_r - ci * b_i
    bbi = cr * b_i + ci * b_r
    lbr_o[...] = br
    lbi_o[...] = bi
    l2r_o[...] = br * br - bi * bi
    l2i_o[...] = 2.0 * br * bi
    bbr_o[...] = bbr
    bbi_o[...] = bbi
    lbbr_o[...] = br * bbr - bi * bbi
    lbbi_o[...] = br * bbi + bi * bbr


def _s5_params(lam_re, lam_im, log_step, b_re, b_im):
    n = DEPTH * 2 * SSM_GROUPS
    lre = lam_re.reshape(n, 1, SSM_STATE)
    lim = lam_im.reshape(n, 1, SSM_STATE)
    ls = log_step.reshape(n, 1, 1)
    btr = b_re.transpose(0, 1, 2, 4, 3).reshape(n, SSM_GROUP, SSM_STATE)
    bti = b_im.transpose(0, 1, 2, 4, 3).reshape(n, SSM_GROUP, SSM_STATE)
    small = jax.ShapeDtypeStruct((n, 1, SSM_STATE), F32)
    big = jax.ShapeDtypeStruct((n, SSM_GROUP, SSM_STATE), F32)
    return pl.pallas_call(
        _s5_param_kernel,
        out_shape=[small, small, small, small, big, big, big, big],
        compiler_params=pltpu.CompilerParams(vmem_limit_bytes=VMEM_LIMIT),
        name="s5_params",
    )(lre, lim, ls, btr, bti)


def _s5_layouts(lam_re, lam_im, log_step, b_re, b_im, c_re, c_im):
    lbr, lbi, l2r, l2i, bbr, bbi, lbbr, lbbi = _s5_params(lam_re, lam_im, log_step, b_re, b_im)
    eye = jnp.eye(SSM_BLK_GROUPS, dtype=F32)
    lead = (DEPTH, 2, SSM_BLKS, SSM_BLK_GROUPS)
    blocked = (DEPTH, 2, SSM_BLKS, SSM_BLK_CH, SSM_BLK_ST)

    def in_block(x):
        x = x.reshape(lead + (SSM_GROUP, SSM_STATE))
        return jnp.einsum('ldbgcp,gh->ldbgchp', x, eye).reshape(blocked)

    def out_block(x):
        x = x.reshape(lead + (SSM_GROUP, SSM_STATE))
        return jnp.einsum('ldbgcp,gh->ldbhcgp', x, eye).reshape(blocked)

    wb = jnp.concatenate([
        jnp.concatenate([in_block(bbr), in_block(bbi)], axis=-1),
        jnp.concatenate([in_block(lbbr), in_block(lbbi)], axis=-1)], axis=-2).astype(BF16)
    wc = jnp.concatenate([out_block(c_re), out_block(-c_im)], axis=-1).astype(BF16)

    def table(one, two):
        one = one.reshape(DEPTH, 2, SSM_BLKS, 1, SSM_BLK_ST)
        two = two.reshape(DEPTH, 2, SSM_BLKS, 1, SSM_BLK_ST)
        half = SUBLANE // 2
        fwd = jnp.concatenate([jnp.broadcast_to(one[:, 0:1], (DEPTH, 1, SSM_BLKS, half, SSM_BLK_ST)),
                               jnp.broadcast_to(two[:, 0:1], (DEPTH, 1, SSM_BLKS, half, SSM_BLK_ST))], axis=3)
        bwd = jnp.concatenate([jnp.broadcast_to(two[:, 1:2], (DEPTH, 1, SSM_BLKS, half, SSM_BLK_ST)),
                               jnp.broadcast_to(one[:, 1:2], (DEPTH, 1, SSM_BLKS, half, SSM_BLK_ST))], axis=3)
        return jnp.concatenate([fwd, bwd], axis=1)

    return wb, wc, table(lbr, l2r), table(lbi, l2i)


def _s5_scan_kernel(uf_ref, ub_ref, wb_ref, wc_ref, tre_ref, tim_ref, yf_ref, yb_ref,
                    tmaj, xs, carry):
    @pl.when(pl.program_id(1) == 0)
    def _():
        carry[...] = jnp.zeros(carry.shape, F32)

    n_tiles = S5_ROWS // SUBLANE
    half = SUBLANE // 2
    io_refs = (uf_ref, ub_ref, yf_ref, yb_ref)

    def time_major(blk, d):
        ch = slice(blk * SSM_BLK_CH, (blk + 1) * SSM_BLK_CH)
        for b in range(BATCH):
            tmaj[blk, d, pl.ds(b, S5_STEPS, stride=BATCH), :] = io_refs[d][b, :, ch]
        return tmaj[blk, d]

    def batch_major(y, blk, d):
        ch = slice(blk * SSM_BLK_CH, (blk + 1) * SSM_BLK_CH)
        tmaj[blk, 2 + d] = y
        for b in range(BATCH):
            io_refs[2 + d][b, :, ch] = tmaj[blk, 2 + d, pl.ds(b, S5_STEPS, stride=BATCH), :]

    def paired_lhs(u, take_upper):
        u3 = u.reshape(n_tiles, SUBLANE, SSM_BLK_CH)
        swapped = pltpu.roll(u3, half, axis=1)
        sub = lax.broadcasted_iota(jnp.int32, u3.shape, 1)
        keep = (sub >= half) if take_upper else (sub < half)
        nb = jnp.where(keep, swapped, 0.0).reshape(S5_ROWS, SSM_BLK_CH)
        return jnp.concatenate([u, nb], axis=1).astype(BF16)

    for blk in range(S5_NB):
        for d in range(2):
            xs[blk, d] = _bdot(paired_lhs(time_major(blk, d), d == 0), wb_ref[0, d, blk])

    lower = lax.broadcasted_iota(jnp.int32, (SUBLANE, S5_SCAN_LANES), 0) < half
    for blk in range(S5_NB):
        for hh in range(SSM_BLK_ST // S5_SCAN_LANES):
            lo = hh * S5_SCAN_LANES
            re = slice(lo, lo + S5_SCAN_LANES)
            im = slice(SSM_BLK_ST + lo, SSM_BLK_ST + lo + S5_SCAN_LANES)
            afr, afi = tre_ref[0, 0, blk, :, re], tim_ref[0, 0, blk, :, re]
            abr, abi = tre_ref[0, 1, blk, :, re], tim_ref[0, 1, blk, :, re]
            hfr, hfi, hbr, hbi = (carry[blk, idx, :, re] for idx in range(4))
            for k in range(n_tiles):
                rf = slice(k * SUBLANE, (k + 1) * SUBLANE)
                pr = jnp.where(lower, pltpu.roll(hfr, half, axis=0), hfr)
                pi = jnp.where(lower, pltpu.roll(hfi, half, axis=0), hfi)
                hfr = xs[blk, 0, rf, re] + (afr * pr - afi * pi)
                hfi = xs[blk, 0, rf, im] + (afr * pi + afi * pr)
                xs[blk, 0, rf, re] = hfr
                xs[blk, 0, rf, im] = hfi
                kb = n_tiles - 1 - k
                rb = slice(kb * SUBLANE, (kb + 1) * SUBLANE)
                qr = jnp.where(lower, hbr, pltpu.roll(hbr, half, axis=0))
                qi = jnp.where(lower, hbi, pltpu.roll(hbi, half, axis=0))
                hbr = xs[blk, 1, rb, re] + (abr * qr - abi * qi)
                hbi = xs[blk, 1, rb, im] + (abr * qi + abi * qr)
                xs[blk, 1, rb, re] = hbr
                xs[blk, 1, rb, im] = hbi
            for idx, val in enumerate((hfr, hfi, hbr, hbi)):
                carry[blk, idx, :, re] = val

    for blk in range(S5_NB):
        for d in range(2):
            y = lax.dot_general(xs[blk, d].astype(BF16), wc_ref[0, d, blk],
                                (((1,), (1,)), ((), ())), preferred_element_type=F32)
            batch_major(y, blk, d)


def _s5_scan(proj, wb, wc, tre, tim, layer):
    last = S5_CHUNKS - 1
    width = S5_NB * SSM_BLK_CH
    u_blk0 = U_COL0 // width
    proj3 = proj.reshape(BATCH, L_TOK, PROJ_COLS)
    y_shape = jax.ShapeDtypeStruct((BATCH, L_TOK, SSM_WIDTH), F32)
    per_blk = lambda *tail: pl.BlockSpec((1, 2, S5_NB) + tail, lambda j, c: (layer, 0, j, 0, 0))
    yf, yb = pl.pallas_call(
        _s5_scan_kernel,
        grid=(SSM_BLKS // S5_NB, S5_CHUNKS),
        in_specs=[
            pl.BlockSpec((BATCH, S5_STEPS, width), lambda j, c: (0, c, u_blk0 + j)),
            pl.BlockSpec((BATCH, S5_STEPS, width), lambda j, c: (0, last - c, u_blk0 + j)),
            per_blk(2 * SSM_BLK_CH, 2 * SSM_BLK_ST),
            per_blk(SSM_BLK_CH, 2 * SSM_BLK_ST),
            per_blk(SUBLANE, SSM_BLK_ST),
            per_blk(SUBLANE, SSM_BLK_ST),
        ],
        out_specs=[
            pl.BlockSpec((BATCH, S5_STEPS, width), lambda j, c: (0, c, j)),
            pl.BlockSpec((BATCH, S5_STEPS, width), lambda j, c: (0, last - c, j)),
        ],
        out_shape=[y_shape, y_shape],
        scratch_shapes=[
            pltpu.VMEM((S5_NB, 4, S5_ROWS, SSM_BLK_CH), F32),
            pltpu.VMEM((S5_NB, 2, S5_ROWS, 2 * SSM_BLK_ST), F32),
            pltpu.VMEM((S5_NB, 4, SUBLANE, SSM_BLK_ST), F32),
        ],
        compiler_params=_cparams(("parallel", "arbitrary")),
        name="s5_scan",
    )(proj3, proj3, wb, wc, tre, tim)
    return yf.reshape(T_TOK, SSM_WIDTH), yb.reshape(T_TOK, SSM_WIDTH)


def _gelu_tanh(x):
    return 0.5 * x * (1.0 + jnp.tanh(math.sqrt(2.0 / math.pi) * (x + 0.044715 * (x * x * x))))


def _glu_kernel(yf_ref, yb_ref, u_ref, d_ref, w_ref, gn_ref, o_ref):
    y = (yf_ref[...] + yb_ref[...]) + d_ref[0] * u_ref[...]
    g = _gelu_tanh(y)
    z = _bdot(g.astype(BF16), w_ref[0].astype(BF16))
    o_ref[...] = _rms(g * jax.nn.sigmoid(z), gn_ref[0]).astype(BF16)


def _glu(yf, yb, proj, d, w, gn, layer, tm):
    t = yf.shape[0]
    row = pl.BlockSpec((tm, SSM_WIDTH), lambda i: (i, 0))
    vec = pl.BlockSpec((1, 1, SSM_WIDTH), lambda i: (layer, 0, 0))
    return pl.pallas_call(
        _glu_kernel,
        grid=(t // tm,),
        in_specs=[row, row,
                  pl.BlockSpec((tm, SSM_WIDTH), lambda i: (i, U_COL0 // SSM_WIDTH)),
                  vec,
                  pl.BlockSpec((1, SSM_WIDTH, SSM_WIDTH), lambda i: (layer, 0, 0)),
                  vec],
        out_specs=row,
        out_shape=jax.ShapeDtypeStruct((t, SSM_WIDTH), BF16),
        compiler_params=_cparams(("parallel",)),
        name="s5_glu",
    )(yf, yb, proj, d.reshape(DEPTH, 1, SSM_WIDTH), w, gn.reshape(DEPTH, 1, SSM_WIDTH))


def _outproj_kernel(h_ref, a_ref, sn_ref, ga_ref, wa_ref, ws_ref, o_ref, an_ref):
    @pl.when(pl.program_id(1) == 0)
    def _():
        an_ref[...] = _rms(a_ref[...], ga_ref[0]).astype(BF16)

    o_ref[...] = (h_ref[...] + _bdot(an_ref[...], wa_ref[0].astype(BF16))
                  + _bdot(sn_ref[...], ws_ref[0].astype(BF16)))


def _outproj(h, att, ssm_n, ga, w, layer, tm, tn):
    t, d = h.shape
    return pl.pallas_call(
        _outproj_kernel,
        grid=(t // tm, d // tn),
        in_specs=[
            pl.BlockSpec((tm, tn), lambda i, j: (i, j)),
            pl.BlockSpec((tm, ATT_WIDTH), lambda i, j: (i, 0)),
            pl.BlockSpec((tm, SSM_WIDTH), lambda i, j: (i, 0)),
            pl.BlockSpec((1, 1, ATT_WIDTH), lambda i, j: (layer, 0, 0)),
            pl.BlockSpec((1, ATT_WIDTH, tn), lambda i, j: (layer, 0, j)),
            pl.BlockSpec((1, SSM_WIDTH, tn), lambda i, j: (layer, 1, j)),
        ],
        out_specs=pl.BlockSpec((tm, tn), lambda i, j: (i, j)),
        out_shape=jax.ShapeDtypeStruct((t, d), F32),
        scratch_shapes=[pltpu.VMEM((tm, ATT_WIDTH), BF16)],
        input_output_aliases={0: 0},
        compiler_params=_cparams(("parallel", "arbitrary")),
        name="out_proj",
    )(h, att, ssm_n, ga.reshape(DEPTH, 1, ATT_WIDTH), w, w)


FFN_SUB = 256


def _ffn_kernel(x_ref, g_ref, wg_ref, wu_ref, wd_ref, gn_ref, o_ref, nxt_ref, hn_ref):
    j = pl.program_id(1)

    @pl.when(j == 0)
    def _():
        x = x_ref[...]
        hn_ref[...] = _rms(x, g_ref[0]).astype(BF16)
        o_ref[...] = x

    hn = hn_ref[...]
    tf = wg_ref.shape[2]
    part = None
    for lo in range(0, tf, FFN_SUB):
        cols = slice(lo, lo + FFN_SUB)
        a = _bdot(hn, wg_ref[0, :, cols].astype(BF16))
        b = _bdot(hn, wu_ref[0, :, cols].astype(BF16))
        hid = ((a * jax.nn.sigmoid(a)) * b).astype(BF16)
        down = _bdot(hid, wd_ref[0, cols, :].astype(BF16))
        part = down if part is None else part + down
    o_ref[...] += part

    @pl.when(j == pl.num_programs(1) - 1)
    def _():
        nxt_ref[...] = _rms(o_ref[...], gn_ref[0]).astype(BF16)


def _ffn(h, g, gn, wg, wu, wd, layer, tm, tf):
    t, d = h.shape
    f = wg.shape[2]
    idx = layer // 2
    once = pl.Buffered(1)
    row = pl.BlockSpec((tm, d), lambda i, j: (i, 0), pipeline_mode=once)
    return pl.pallas_call(
        _ffn_kernel,
        grid=(t // tm, f // tf),
        in_specs=[
            row,
            pl.BlockSpec((1, 1, d), lambda i, j: (layer, 0, 0)),
            pl.BlockSpec((1, d, tf), lambda i, j: (idx, 0, j)),
            pl.BlockSpec((1, d, tf), lambda i, j: (idx, 0, j)),
            pl.BlockSpec((1, tf, d), lambda i, j: (idx, j, 0)),
            pl.BlockSpec((1, 1, d), lambda i, j: (layer + 1, 0, 0)),
        ],
        out_specs=[row, row],
        out_shape=[jax.ShapeDtypeStruct((t, d), F32), jax.ShapeDtypeStruct((t, d), BF16)],
        scratch_shapes=[pltpu.VMEM((tm, d), BF16)],
        input_output_aliases={0: 0},
        compiler_params=_cparams(("parallel", "arbitrary")),
        name="dense_ffn",
    )(h, g.reshape(DEPTH, 1, d), wg, wu, wd, gn.reshape(DEPTH, 1, d))


MOE_TM = 344
MOE_NT = (2 * T_TOK) // MOE_TM + N_EXPERTS
MOE_NP = MOE_NT * MOE_TM
DMA_FANOUT = 4


def _split_bf16(x):
    hi = x.astype(BF16)
    lo = (x - hi.astype(F32)).astype(BF16)
    return hi, lo


def _lane_pick(x, lane, k):
    return jnp.sum(jnp.where(lane == k, x, 0.0), axis=-1, keepdims=True)


ROW_SLABS = D_MODEL // LANE


def _row_slab(ref, c, rows):
    return ref.at[pl.ds(c, rows, stride=ROW_SLABS), :]


def _router_kernel(x_ref, g_ref, wr_ref, sel_ref, wts_ref, rows_ref):
    x = x_ref[...]
    for c in range(ROW_SLABS):
        _row_slab(rows_ref, c, x.shape[0])[...] = x[:, c * LANE:(c + 1) * LANE]
    hn = _rms(x, g_ref[0])
    xh, xl = _split_bf16(hn)
    wh, wl = _split_bf16(wr_ref[0])
    logits = _bdot(xh, wh) + (_bdot(xh, wl) + _bdot(xl, wh)) + _bdot(xl, wl)
    lane = lax.broadcasted_iota(jnp.int32, logits.shape, 1).astype(F32)
    neg = jnp.float32(-jnp.inf)
    logits = jnp.where(lane < N_EXPERTS, logits, neg)
    v1 = jnp.max(logits, axis=-1, keepdims=True)
    i1 = jnp.min(jnp.where(logits == v1, lane, float(LANE)), axis=-1, keepdims=True)
    rest = jnp.where(lane == i1, neg, logits)
    v2 = jnp.max(rest, axis=-1, keepdims=True)
    i2 = jnp.min(jnp.where(rest == v2, lane, float(LANE)), axis=-1, keepdims=True)
    e2 = jnp.exp(v2 - v1)
    w1 = 1.0 / (1.0 + e2)
    w2 = e2 / (1.0 + e2)
    sel_ref[...] = jnp.where((lane == i1) | (lane == i2), 1.0, 0.0)
    wts_ref[...] = jnp.where(lane == 0, w1, jnp.where(lane == 1, w2,
                             jnp.where(lane == 2, i1, jnp.where(lane == 3, i2, 0.0))))


def _router(h, g, wr_pad, layer, tm):
    t, d = h.shape
    idx = layer // 2
    row = pl.BlockSpec((tm, LANE), lambda i: (i, 0))
    tab = jax.ShapeDtypeStruct((t, LANE), F32)
    return pl.pallas_call(
        _router_kernel,
        grid=(t // tm,),
        in_specs=[
            pl.BlockSpec((tm, d), lambda i: (i, 0)),
            pl.BlockSpec((1, 1, d), lambda i: (layer, 0, 0)),
            pl.BlockSpec((1, d, LANE), lambda i: (idx, 0, 0)),
        ],
        out_specs=[row, row, pl.BlockSpec((tm * ROW_SLABS, LANE), lambda i: (i, 0))],
        out_shape=[tab, tab, jax.ShapeDtypeStruct((t * ROW_SLABS, LANE), F32)],
        compiler_params=_cparams(("parallel",)),
        name="moe_router",
    )(h, g.reshape(DEPTH, 1, d), wr_pad)


def _positions_kernel(sel_ref, wts_ref, pos_ref, meta_ref, cnt_ref, off_ref):
    p = pl.program_id(0)
    i = pl.program_id(1)
    tm = sel_ref.shape[0]
    lane = lax.broadcasted_iota(jnp.int32, (1, LANE), 1).astype(F32)
    sel = sel_ref[...]

    @pl.when((p == 0) & (i == 0))
    def _():
        cnt_ref[...] = jnp.zeros((1, LANE), F32)

    @pl.when(p == 0)
    def _():
        cnt_ref[...] += jnp.sum(sel, axis=0, keepdims=True)

    @pl.when((p == 1) & (i == 0))
    def _():
        cnt = cnt_ref[...]
        tiles = jnp.zeros((1, LANE), F32)
        for k in range(MOE_NT):
            tiles = tiles + jnp.where(cnt > float(k * MOE_TM), 1.0, 0.0)
        padded = tiles * float(MOE_TM)
        off = jnp.zeros((1, LANE), F32)
        for e in range(N_EXPERTS):
            off = off + jnp.where(lane > e, _lane_pick(padded, lane, e), 0.0)
        end = off + padded
        tile_start = lane * float(MOE_TM)
        owner = jnp.zeros((1, LANE), F32)
        for e in range(N_EXPERTS):
            owner = owner + jnp.where(tile_start >= _lane_pick(end, lane, e), 1.0, 0.0)
        owner = jnp.minimum(owner, float(N_EXPERTS - 1))
        used = jnp.sum(tiles, axis=-1, keepdims=True)
        off_ref[...] = off
        cnt_ref[...] = jnp.zeros((1, LANE), F32)
        row = lax.broadcasted_iota(jnp.int32, (SUBLANE, LANE), 0)
        meta = jnp.where(row == 0, owner, jnp.where(row == 1, used, jnp.where(row == 2, cnt, off)))
        meta_ref[...] = meta.astype(jnp.int32)

    @pl.when(p == 1)
    def _():
        r = lax.broadcasted_iota(jnp.int32, (tm, tm), 0)
        c = lax.broadcasted_iota(jnp.int32, (tm, tm), 1)
        earlier = jnp.where(c < r, 1.0, 0.0).astype(BF16)
        rank = _bdot(earlier, sel.astype(BF16)) + cnt_ref[...]
        slot = off_ref[...] + rank
        wts = wts_ref[...]
        lane_t = lax.broadcasted_iota(jnp.int32, (tm, LANE), 1).astype(F32)
        p1 = _lane_pick(slot, lane_t, _lane_pick(wts, lane_t, 2))
        p2 = _lane_pick(slot, lane_t, _lane_pick(wts, lane_t, 3))
        pos_ref[...] = jnp.where(lane_t == 0, p1, jnp.where(lane_t == 1, p2, 0.0)).astype(jnp.int32)
        cnt_ref[...] += jnp.sum(sel, axis=0, keepdims=True)


def _positions(sel, wts, tm):
    t = sel.shape[0]
    row = pl.BlockSpec((tm, LANE), lambda p, i: (i, 0))
    return pl.pallas_call(
        _positions_kernel,
        grid=(2, t // tm),
        in_specs=[row, row],
        out_specs=[
            pl.BlockSpec((tm, LANE), lambda p, i: (i * p, 0)),
            pl.BlockSpec((SUBLANE, LANE), lambda p, i: (0, 0)),
        ],
        out_shape=[jax.ShapeDtypeStruct((t, LANE), jnp.int32),
                   jax.ShapeDtypeStruct((SUBLANE, LANE), jnp.int32)],
        scratch_shapes=[pltpu.VMEM((1, LANE), F32), pltpu.VMEM((1, LANE), F32)],
        compiler_params=_cparams(("arbitrary", "arbitrary")),
        name="moe_positions",
    )(sel, wts)


def _experts_kernel(pos1_ref, pos2_ref, owner_ref, used_ref, rows_ref, g_ref, wg_ref, wu_ref, wd_ref,
                    o_ref, src_ref, xbuf, sems):
    i = pl.program_id(0)
    used = used_ref[0]
    cur = lax.rem(i, 2)

    def row_copy(tok, r, buf):
        return pltpu.make_async_copy(rows_ref.at[pl.ds(tok * ROW_SLABS, ROW_SLABS), :],
                                     xbuf.at[buf, pl.ds(r * ROW_SLABS, ROW_SLABS), :], sems.at[buf])

    def fetch_tile(tile, buf):
        def issue(q, c):
            for u in range(DMA_FANOUT):
                r = q * DMA_FANOUT + u
                row_copy(src_ref[tile * MOE_TM + r], r, buf).start(priority=u % 2)
            return c

        lax.fori_loop(0, MOE_TM // DMA_FANOUT, issue, 0)

    @pl.when(i == 0)
    def _():
        def clear(p, c):
            src_ref[p] = 0
            return c

        lax.fori_loop(0, MOE_NP, clear, 0, unroll=8)

        def fill(t, c):
            src_ref[pos1_ref[t]] = t
            src_ref[pos2_ref[t]] = t
            return c

        lax.fori_loop(0, T_TOK, fill, 0, unroll=4)
        fetch_tile(0, 0)

    @pl.when(i + 1 < used)
    def _():
        fetch_tile(i + 1, 1 - cur)

    live = i < used

    @pl.when(live)
    def _():
        pltpu.make_async_copy(rows_ref.at[pl.ds(0, MOE_TM * ROW_SLABS), :], xbuf.at[cur],
                              sems.at[cur]).wait()
        x_ref = xbuf.at[cur]
        slabs = [_row_slab(x_ref, c, MOE_TM)[...] for c in range(ROW_SLABS)]
        ssq = slabs[0] * slabs[0]
        for c in range(1, ROW_SLABS):
            ssq = ssq + slabs[c] * slabs[c]
        inv = lax.rsqrt(jnp.sum(ssq, axis=-1, keepdims=True) / D_MODEL + EPS)
        g = g_ref[0]
        hn = jnp.concatenate([slabs[c] * inv * g[:, c * LANE:(c + 1) * LANE]
                              for c in range(ROW_SLABS)], axis=1).astype(BF16)
        a = _bdot(hn, wg_ref[0, 0])
        b = _bdot(hn, wu_ref[0, 0])
        hid = ((a * jax.nn.sigmoid(a)) * b).astype(BF16)
        res = _bdot(hid, wd_ref[0, 0])
        for c in range(ROW_SLABS):
            _row_slab(o_ref, c, MOE_TM)[...] = res[:, c * LANE:(c + 1) * LANE]

    @pl.when(jnp.logical_not(live))
    def _():
        o_ref[...] = jnp.zeros(o_ref.shape, F32)


def _experts(rows, pos1, pos2, owner, used, g, wg, wu, wd, layer):
    d = D_MODEL
    idx = layer // 2
    blk = MOE_TM * ROW_SLABS

    w_map = lambda i, p1, p2, o, u: (idx, o[jnp.minimum(i, u[0] - 1)], 0, 0)
    return pl.pallas_call(
        _experts_kernel,
        grid_spec=pltpu.PrefetchScalarGridSpec(
            num_scalar_prefetch=4,
            grid=(MOE_NT,),
            in_specs=[
                pl.BlockSpec(memory_space=pl.ANY),
                pl.BlockSpec((1, 1, d), lambda i, p1, p2, o, u: (layer, 0, 0)),
                pl.BlockSpec((1, 1, d, D_FF_EXPERT), w_map),
                pl.BlockSpec((1, 1, d, D_FF_EXPERT), w_map),
                pl.BlockSpec((1, 1, D_FF_EXPERT, d), w_map),
            ],
            out_specs=pl.BlockSpec((blk, LANE), lambda i, p1, p2, o, u: (i, 0)),
            scratch_shapes=[
                pltpu.SMEM((MOE_NP,), jnp.int32),
                pltpu.VMEM((2, blk, LANE), F32),
                pltpu.SemaphoreType.DMA((2,)),
            ],
        ),
        out_shape=jax.ShapeDtypeStruct((MOE_NP * ROW_SLABS, LANE), F32),
        compiler_params=_cparams(("arbitrary",)),
        name="moe_experts",
    )(pos1, pos2, owner, used, rows, g.reshape(DEPTH, 1, d), wg, wu, wd)


def _combine_kernel(pos1_ref, pos2_ref, h_ref, wts_ref, ys_ref, gn_ref, o_ref, nxt_ref,
                    y1, y2, sem1, sem2):
    tm = h_ref.shape[0]
    base = pl.program_id(0) * tm

    def row_copy(slot, r, buf, sem):
        return pltpu.make_async_copy(ys_ref.at[pl.ds(slot * ROW_SLABS, ROW_SLABS), :],
                                     buf.at[pl.ds(r * ROW_SLABS, ROW_SLABS), :], sem)

    def fetch(q, c):
        for u in range(DMA_FANOUT):
            r = q * DMA_FANOUT + u
            row_copy(pos1_ref[base + r], r, y1, sem1).start(priority=0)
            row_copy(pos2_ref[base + r], r, y2, sem2).start(priority=1)
        return c

    lax.fori_loop(0, tm // DMA_FANOUT, fetch, 0)

    pltpu.make_async_copy(ys_ref.at[pl.ds(0, tm * ROW_SLABS), :], y1, sem1).wait()
    pltpu.make_async_copy(ys_ref.at[pl.ds(0, tm * ROW_SLABS), :], y2, sem2).wait()
    wts = wts_ref[...]
    lane = lax.broadcasted_iota(jnp.int32, wts.shape, 1).astype(F32)
    w1 = _lane_pick(wts, lane, 0)
    w2 = _lane_pick(wts, lane, 1)
    ssq = jnp.zeros((tm, 1), F32)
    for c in range(ROW_SLABS):
        cols = slice(c * LANE, (c + 1) * LANE)
        hc = h_ref[:, cols] + (w1 * _row_slab(y1, c, tm)[...] + w2 * _row_slab(y2, c, tm)[...])
        o_ref[:, cols] = hc
        ssq = ssq + jnp.sum(hc * hc, axis=-1, keepdims=True)
    normed = o_ref[...] * lax.rsqrt(ssq / D_MODEL + EPS) * gn_ref[...]
    if nxt_ref is None:
        o_ref[...] = normed
    else:
        nxt_ref[...] = normed.astype(BF16)


def _combine_mid_kernel(pos1_ref, pos2_ref, h_ref, wts_ref, ys_ref, gn_ref, o_ref, nxt_ref,
                        y1, y2, sem1, sem2):
    _combine_kernel(pos1_ref, pos2_ref, h_ref, wts_ref, ys_ref, gn_ref, o_ref, nxt_ref,
                    y1, y2, sem1, sem2)


def _combine_last_kernel(pos1_ref, pos2_ref, h_ref, wts_ref, ys_ref, gn_ref, o_ref,
                         y1, y2, sem1, sem2):
    _combine_kernel(pos1_ref, pos2_ref, h_ref, wts_ref, ys_ref, gn_ref, o_ref, None,
                    y1, y2, sem1, sem2)


def _combine(h, wts, ys, pos1, pos2, gn, last, tm):
    t, d = h.shape
    row = pl.BlockSpec((tm, d), lambda i, a, b: (i, 0))
    h_out = jax.ShapeDtypeStruct((t, d), F32)
    return pl.pallas_call(
        _combine_last_kernel if last else _combine_mid_kernel,
        grid_spec=pltpu.PrefetchScalarGridSpec(
            num_scalar_prefetch=2,
            grid=(t // tm,),
            in_specs=[
                row,
                pl.BlockSpec((tm, LANE), lambda i, a, b: (i, 0)),
                pl.BlockSpec(memory_space=pl.ANY),
                pl.BlockSpec((1, d), lambda i, a, b: (0, 0)),
            ],
            out_specs=row if last else [row, row],
            scratch_shapes=[
                pltpu.VMEM((tm * ROW_SLABS, LANE), F32),
                pltpu.VMEM((tm * ROW_SLABS, LANE), F32),
                pltpu.SemaphoreType.DMA(()),
                pltpu.SemaphoreType.DMA(()),
            ],
        ),
        out_shape=h_out if last else [h_out, jax.ShapeDtypeStruct((t, d), BF16)],
        input_output_aliases={2: 0},
        compiler_params=_cparams(("arbitrary",)),
        name="moe_combine",
    )(pos1, pos2, h, wts, ys, gn.reshape(1, d))


def _moe(h, g, gn, wr_pad, wg, wu, wd, layer, last):
    sel, wts, rows = _router(h, g, wr_pad, layer, tm=688)
    pos, meta = _positions(sel, wts, tm=688)
    pos1 = pos[:, 0]
    pos2 = pos[:, 1]
    owner = meta[0, :MOE_NT]
    used = meta[1, :1]
    ys = _experts(rows, pos1, pos2, owner, used, g, wg, wu, wd, layer)
    return _combine(h, wts, ys, pos1, pos2, gn, last, tm=688)


def _rope_partner(w):
    half = QK_ROPE // 2
    return jnp.concatenate([-w[..., half:], w[..., :half]], axis=-1)


def _pad_cols(w, width):
    return jnp.pad(w, [(0, 0)] * (w.ndim - 1) + [(0, width - w.shape[-1])])


def _in_proj_layout(w_in):
    kr = w_in[..., Q_LORA + KV_LORA:Q_LORA + KV_LORA + QK_ROPE]
    return jnp.concatenate([
        w_in[..., :Q_LORA + KV_LORA],
        _pad_cols(kr, LANE),
        _pad_cols(_rope_partner(kr), LANE),
        w_in[..., Q_LORA + KV_LORA + QK_ROPE:]], axis=-1)


def _q_layouts(w_uq):
    w = w_uq.reshape(DEPTH, Q_LORA, ATT_HEADS, QK_NOPE + QK_ROPE)
    main = _pad_cols(w, HEAD_PAD).reshape(DEPTH, Q_LORA, ATT_HEADS * HEAD_PAD)
    rot = _pad_cols(_rope_partner(w[..., QK_NOPE:]), LANE).reshape(DEPTH, Q_LORA, ATT_HEADS * LANE)
    return main, rot


def _rope_tables():
    inv = ROPE_THETA ** (-jnp.arange(0, QK_ROPE, 2, dtype=F32) / QK_ROPE)
    ang = jnp.arange(L_TOK, dtype=F32)[:, None] * inv[None, :]
    cos = jnp.cos(ang)
    sin = jnp.sin(ang)
    cos_t = _pad_cols(jnp.concatenate([cos, cos], axis=-1), LANE)
    sin_t = _pad_cols(jnp.concatenate([sin, sin], axis=-1), LANE)
    return cos_t, sin_t


def kernel(x, meta_tokens, mix_norm, w_in, q_norm, w_uq, kv_norm, w_ukv, ssm_lambda_re, ssm_lambda_im, ssm_log_step, ssm_b_re, ssm_b_im, ssm_c_re, ssm_c_im, ssm_d, ssm_w_glu, attn_out_norm, ssm_out_norm, w_out, ffn_norm, dense_w_gate, dense_w_up, dense_w_down, moe_router, moe_w_gate, moe_w_up, moe_w_down, final_norm):
    meta = jnp.broadcast_to(meta_tokens[None].astype(x.dtype), (BATCH, N_META, D_MODEL))
    h = jnp.concatenate([meta, x], axis=1).reshape(T_TOK, D_MODEL)
    cos_t, sin_t = _rope_tables()
    w_in_l = _in_proj_layout(w_in)
    wq_main, wq_rot = _q_layouts(w_uq)
    wr_pad = _pad_cols(moe_router, LANE)
    moe_w_gate, moe_w_up, moe_w_down = (w.astype(BF16) for w in (moe_w_gate, moe_w_up, moe_w_down))
    wb, wc, tre, tim = _s5_layouts(ssm_lambda_re, ssm_lambda_im, ssm_log_step,
                                   ssm_b_re, ssm_b_im, ssm_c_re, ssm_c_im)
    hn = _rms_cast(h, mix_norm, 0, tm=688)
    for layer in range(DEPTH):
        proj = _in_proj(hn, w_in_l, layer, tm=2064, tn=512)
        q, kv, kpe = _upproj(proj, q_norm, kv_norm, wq_main, wq_rot, w_ukv, cos_t, sin_t, layer,
                             tm=688)
        att = _attention(q, kv, kpe, tq=688)
        yf, yb = _s5_scan(proj, wb, wc, tre, tim, layer)
        ssm_n = _glu(yf, yb, proj, ssm_d, ssm_w_glu, ssm_out_norm, layer, tm=688)
        h = _outproj(h, att, ssm_n, attn_out_norm, w_out, layer, tm=2064, tn=512)
        last = layer == DEPTH - 1
        if layer % 2 == 0:
            h, hn = _ffn(h, ffn_norm, mix_norm, dense_w_gate, dense_w_up, dense_w_down, layer,
                         tm=1032, tf=512)
        elif last:
            h = _moe(h, ffn_norm, final_norm, wr_pad, moe_w_gate, moe_w_up, moe_w_down, layer, last)
        else:
            h, hn = _moe(h, ffn_norm, mix_norm[layer + 1], wr_pad, moe_w_gate, moe_w_up, moe_w_down,
                         layer, last)
    return h.reshape(BATCH, L_TOK, D_MODEL)[:, N_META:]
```

```python
import functools
import math

import jax
import jax.numpy as jnp
from jax import lax
from jax.experimental import pallas as pl
from jax.experimental.pallas import tpu as pltpu

F32 = jnp.float32
BF16 = jnp.bfloat16

D_MODEL = 2048
BATCH = 4
SEQ = 2048
DEPTH = 4
N_META = 16
EPS = 1e-6
ATT_HEADS = 8
QK_NOPE = 128
QK_ROPE = 64
V_DIM = 128
Q_LORA = 512
KV_LORA = 256
ROPE_THETA = 10000.0
ATT_WIDTH = ATT_HEADS * V_DIM
SSM_GROUP = 16
SSM_WIDTH = D_MODEL - ATT_WIDTH
SSM_GROUPS = SSM_WIDTH // SSM_GROUP
SSM_STATE = 64
D_FF = 5632
N_EXPERTS = 8
D_FF_EXPERT = 1408

L_TOK = N_META + SEQ
T_TOK = L_TOK * BATCH
LANE = 128
SUBLANE = 8
HEAD_PAD = 256
L_PAD = 2176
PROJ_COLS = 2048
U_COL0 = 1024
VMEM_LIMIT = 58 * 1024 * 1024

SSM_BLK_GROUPS = 8
SSM_BLKS = SSM_GROUPS // SSM_BLK_GROUPS
SSM_BLK_CH = SSM_BLK_GROUPS * SSM_GROUP
SSM_BLK_ST = SSM_BLK_GROUPS * SSM_STATE
S5_CHUNKS = 6
S5_STEPS = L_TOK // S5_CHUNKS
S5_ROWS = S5_STEPS * BATCH
S5_NB = 2
S5_SCAN_LANES = 256


def _cparams(sem):
    return pltpu.CompilerParams(dimension_semantics=sem, vmem_limit_bytes=VMEM_LIMIT)


def _rms(x, g):
    ms = jnp.mean(x * x, axis=-1, keepdims=True)
    return x * lax.rsqrt(ms + EPS) * g


def _bdot(a, b):
    return jnp.dot(a, b, preferred_element_type=F32)


def _rms_cast_kernel(x_ref, g_ref, o_ref):
    o_ref[...] = _rms(x_ref[...], g_ref[0]).astype(BF16)


def _rms_cast(x, g, layer, tm):
    t, d = x.shape
    return pl.pallas_call(
        _rms_cast_kernel,
        grid=(t // tm,),
        in_specs=[pl.BlockSpec((tm, d), lambda i: (i, 0)),
                  pl.BlockSpec((1, 1, d), lambda i: (layer, 0, 0))],
        out_specs=pl.BlockSpec((tm, d), lambda i: (i, 0)),
        out_shape=jax.ShapeDtypeStruct((t, d), BF16),
        compiler_params=_cparams(("parallel",)),
        name="rms_cast",
    )(x, g.reshape(DEPTH, 1, d))


def _in_proj_kernel(x_ref, w_ref, o_ref):
    o_ref[...] = _bdot(x_ref[...], w_ref[0].astype(BF16))


def _in_proj(hn, w, layer, tm, tn):
    t, d = hn.shape
    n = w.shape[2]
    return pl.pallas_call(
        _in_proj_kernel,
        grid=(t // tm, n // tn),
        in_specs=[
            pl.BlockSpec((tm, d), lambda i, j: (i, 0)),
            pl.BlockSpec((1, d, tn), lambda i, j: (layer, 0, j)),
        ],
        out_specs=pl.BlockSpec((tm, tn), lambda i, j: (i, j)),
        out_shape=jax.ShapeDtypeStruct((t, n), F32),
        compiler_params=_cparams(("parallel", "arbitrary")),
        name="in_proj",
    )(hn, w)


def _upproj_kernel(cq_ref, ckv_ref, kr_ref, krot_ref, qn_ref, kvn_ref, wq_ref, wqr_ref,
                   wkv_ref, cos_ref, sin_ref, q_ref, kv_ref, kpe_ref):
    scale = (QK_NOPE + QK_ROPE) ** -0.5 * math.log2(math.e)
    cqn = _rms(cq_ref[...], qn_ref[0]).astype(BF16)
    a = _bdot(cqn, wq_ref[0].astype(BF16))
    r = _bdot(cqn, wqr_ref[0].astype(BF16))
    c = cos_ref[...]
    s = sin_ref[...]
    last_lane = lax.broadcasted_iota(jnp.int32, c.shape, 1) == LANE - 1
    for h in range(ATT_HEADS):
        lo = h * HEAD_PAD
        q_ref[:, lo:lo + LANE] = (a[:, lo:lo + LANE] * scale).astype(BF16)
        rope = a[:, lo + LANE:lo + HEAD_PAD] * c + r[:, h * LANE:(h + 1) * LANE] * s
        q_ref[:, lo + LANE:lo + HEAD_PAD] = jnp.where(last_lane, 1.0, rope * scale).astype(BF16)
    ckvn = _rms(ckv_ref[...], kvn_ref[0]).astype(BF16)
    kv_ref[...] = _bdot(ckvn, wkv_ref[0].astype(BF16)).astype(BF16)
    kpe_ref[...] = (kr_ref[...] * c + krot_ref[...] * s).astype(BF16)


def _upproj(proj, qn, kvn, wq, wqr, wkv, cos_t, sin_t, layer, tm):
    t = proj.shape[0]
    full = lambda a: pl.BlockSpec((1,) + a.shape[1:], lambda i: (layer, 0, 0))
    qn = qn.reshape(DEPTH, 1, Q_LORA)
    kvn = kvn.reshape(DEPTH, 1, KV_LORA)
    pos_blocks = L_TOK // tm
    return pl.pallas_call(
        _upproj_kernel,
        grid=(t // tm,),
        in_specs=[
            pl.BlockSpec((tm, Q_LORA), lambda i: (i, 0)),
            pl.BlockSpec((tm, KV_LORA), lambda i: (i, Q_LORA // KV_LORA)),
            pl.BlockSpec((tm, LANE), lambda i: (i, (Q_LORA + KV_LORA) // LANE)),
            pl.BlockSpec((tm, LANE), lambda i: (i, (Q_LORA + KV_LORA) // LANE + 1)),
            full(qn),
            full(kvn),
            full(wq),
            full(wqr),
            full(wkv),
            pl.BlockSpec((tm, LANE), lambda i: (i % pos_blocks, 0)),
            pl.BlockSpec((tm, LANE), lambda i: (i % pos_blocks, 0)),
        ],
        out_specs=[
            pl.BlockSpec((tm, ATT_HEADS * HEAD_PAD), lambda i: (i, 0)),
            pl.BlockSpec((tm, ATT_HEADS * (QK_NOPE + V_DIM)), lambda i: (i, 0)),
            pl.BlockSpec((tm, LANE), lambda i: (i, 0)),
        ],
        out_shape=[
            jax.ShapeDtypeStruct((t, ATT_HEADS * HEAD_PAD), BF16),
            jax.ShapeDtypeStruct((t, ATT_HEADS * (QK_NOPE + V_DIM)), BF16),
            jax.ShapeDtypeStruct((t, LANE), BF16),
        ],
        compiler_params=_cparams(("parallel",)),
        name="upproj",
    )(proj, proj, proj, proj, qn, kvn, wq, wqr, wkv, cos_t, sin_t)


ATT_HPS = 4
PAD_BIAS = -1e30


def _attn_kernel(q_ref, kv_ref, kpe_ref, o_ref, k_sc, v_sc):
    @pl.when(pl.program_id(2) == 0)
    def _():
        n_pad = L_PAD - L_TOK
        pad_lane = lax.broadcasted_iota(jnp.int32, (n_pad, LANE), 1)
        k_pad = jnp.where(pad_lane == LANE - 1, PAD_BIAS, 0.0).astype(BF16)
        row_lane = lax.broadcasted_iota(jnp.int32, (L_TOK, LANE), 1)
        ones_col = jnp.where(row_lane == 0, 1.0, 0.0).astype(BF16)
        for hh in range(ATT_HPS):
            lo = hh * HEAD_PAD
            k_sc[hh, 0:L_TOK, 0:LANE] = kv_ref[:, lo:lo + LANE]
            k_sc[hh, 0:L_TOK, LANE:HEAD_PAD] = kpe_ref[...]
            k_sc[hh, L_TOK:L_PAD, 0:LANE] = jnp.zeros((n_pad, LANE), BF16)
            k_sc[hh, L_TOK:L_PAD, LANE:HEAD_PAD] = k_pad
            v_sc[hh, 0:L_TOK, 0:V_DIM] = kv_ref[:, lo + LANE:lo + HEAD_PAD]
            v_sc[hh, 0:L_TOK, V_DIM:HEAD_PAD] = ones_col
            v_sc[hh, L_TOK:L_PAD, :] = jnp.zeros((n_pad, HEAD_PAD), BF16)

    for hh in range(ATT_HPS):
        q = q_ref[:, hh * HEAD_PAD:(hh + 1) * HEAD_PAD]
        s = lax.dot_general(q, k_sc[hh], (((1,), (1,)), ((), ())),
                            preferred_element_type=F32)
        m = jnp.max(s, axis=-1, keepdims=True)
        p = jnp.exp2(s - m).astype(BF16)
        o = _bdot(p, v_sc[hh])
        o_ref[:, hh * V_DIM:(hh + 1) * V_DIM] = o[:, :V_DIM] / o[:, V_DIM:V_DIM + 1]


def _attention(q, kv, kpe, tq):
    nq = L_TOK // tq
    hps = ATT_HPS
    return pl.pallas_call(
        _attn_kernel,
        grid=(BATCH, ATT_HEADS // hps, nq),
        in_specs=[
            pl.BlockSpec((tq, hps * HEAD_PAD), lambda b, h, i: (b * nq + i, h)),
            pl.BlockSpec((L_TOK, hps * HEAD_PAD), lambda b, h, i: (b, h)),
            pl.BlockSpec((L_TOK, LANE), lambda b, h, i: (b, 0)),
        ],
        out_specs=pl.BlockSpec((tq, hps * V_DIM), lambda b, h, i: (b * nq + i, h)),
        out_shape=jax.ShapeDtypeStruct((T_TOK, ATT_WIDTH), F32),
        scratch_shapes=[pltpu.VMEM((hps, L_PAD, HEAD_PAD), BF16),
                        pltpu.VMEM((hps, L_PAD, HEAD_PAD), BF16)],
        compiler_params=_cparams(("parallel", "parallel", "arbitrary")),
        name="attention",
    )(q, kv, kpe)


def _s5_param_kernel(lre_ref, lim_ref, ls_ref, bre_ref, bim_ref,
                     lbr_o, lbi_o, l2r_o, l2i_o, bbr_o, bbi_o, lbbr_o, lbbi_o):
    lre = lre_ref[...]
    lim = lim_ref[...]
    dt = jnp.exp(ls_ref[...])
    mag = jnp.exp(lre * dt)
    ang = lim * dt
    br = mag * jnp.cos(ang)
    bi = mag * jnp.sin(ang)
    nr = br - 1.0
    den = lre * lre + lim * lim
    cr = (nr * lre + bi * lim) / den
    ci = (bi * lre - nr * lim) / den
    b_r = bre_ref[...]
    b_i = bim_ref[...]
    bbr = cr * b_r - ci * b_i
    bbi = cr * b_i + ci * b_r
    lbr_o[...] = br
    lbi_o[...] = bi
    l2r_o[...] = br * br - bi * bi
    l2i_o[...] = 2.0 * br * bi
    bbr_o[...] = bbr
    bbi_o[...] = bbi
    lbbr_o[...] = br * bbr - bi * bbi
    lbbi_o[...] = br * bbi + bi * bbr


def _s5_params(lam_re, lam_im, log_step, b_re, b_im):
    n = DEPTH * 2 * SSM_GROUPS
    lre = lam_re.reshape(n, 1, SSM_STATE)
    lim = lam_im.reshape(n, 1, SSM_STATE)
    ls = log_step.reshape(n, 1, 1)
    btr = b_re.transpose(0, 1, 2, 4, 3).reshape(n, SSM_GROUP, SSM_STATE)
    bti = b_im.transpose(0, 1, 2, 4, 3).reshape(n, SSM_GROUP, SSM_STATE)
    small = jax.ShapeDtypeStruct((n, 1, SSM_STATE), F32)
    big = jax.ShapeDtypeStruct((n, SSM_GROUP, SSM_STATE), F32)
    return pl.pallas_call(
        _s5_param_kernel,
        out_shape=[small, small, small, small, big, big, big, big],
        compiler_params=pltpu.CompilerParams(vmem_limit_bytes=VMEM_LIMIT),
        name="s5_params",
    )(lre, lim, ls, btr, bti)


def _s5_layouts(lam_re, lam_im, log_step, b_re, b_im, c_re, c_im):
    lbr, lbi, l2r, l2i, bbr, bbi, lbbr, lbbi = _s5_params(lam_re, lam_im, log_step, b_re, b_im)
    eye = jnp.eye(SSM_BLK_GROUPS, dtype=F32)
    lead = (DEPTH, 2, SSM_BLKS, SSM_BLK_GROUPS)
    blocked = (DEPTH, 2, SSM_BLKS, SSM_BLK_CH, SSM_BLK_ST)

    def in_block(x):
        x = x.reshape(lead + (SSM_GROUP, SSM_STATE))
        return jnp.einsum('ldbgcp,gh->ldbgchp', x, eye).reshape(blocked)

    def out_block(x):
        x = x.reshape(lead + (SSM_GROUP, SSM_STATE))
        return jnp.einsum('ldbgcp,gh->ldbhcgp', x, eye).reshape(blocked)

    wb = jnp.concatenate([
        jnp.concatenate([in_block(bbr), in_block(bbi)], axis=-1),
        jnp.concatenate([in_block(lbbr), in_block(lbbi)], axis=-1)], axis=-2).astype(BF16)
    wc = jnp.concatenate([out_block(c_re), out_block(-c_im)], axis=-1).astype(BF16)

    def table(one, two):
        one = one.reshape(DEPTH, 2, SSM_BLKS, 1, SSM_BLK_ST)
        two = two.reshape(DEPTH, 2, SSM_BLKS, 1, SSM_BLK_ST)
        half = SUBLANE // 2
        fwd = jnp.concatenate([jnp.broadcast_to(one[:, 0:1], (DEPTH, 1, SSM_BLKS, half, SSM_BLK_ST)),
                               jnp.broadcast_to(two[:, 0:1], (DEPTH, 1, SSM_BLKS, half, SSM_BLK_ST))], axis=3)
        bwd = jnp.concatenate([jnp.broadcast_to(two[:, 1:2], (DEPTH, 1, SSM_BLKS, half, SSM_BLK_ST)),
                               jnp.broadcast_to(one[:, 1:2], (DEPTH, 1, SSM_BLKS, half, SSM_BLK_ST))], axis=3)
        return jnp.concatenate([fwd, bwd], axis=1)

    return wb, wc, table(lbr, l2r), table(lbi, l2i)


def _s5_scan_kernel(uf_ref, ub_ref, wb_ref, wc_ref, tre_ref, tim_ref, yf_ref, yb_ref,
                    tmaj, xs, carry):
    @pl.when(pl.program_id(1) == 0)
    def _():
        carry[...] = jnp.zeros(carry.shape, F32)

    n_tiles = S5_ROWS // SUBLANE
    half = SUBLANE // 2
    io_refs = (uf_ref, ub_ref, yf_ref, yb_ref)

    def time_major(blk, d):
        ch = slice(blk * SSM_BLK_CH, (blk + 1) * SSM_BLK_CH)
        for b in range(BATCH):
            tmaj[blk, d, pl.ds(b, S5_STEPS, stride=BATCH), :] = io_refs[d][b, :, ch]
        return tmaj[blk, d]

    def batch_major(y, blk, d):
        ch = slice(blk * SSM_BLK_CH, (blk + 1) * SSM_BLK_CH)
        tmaj[blk, 2 + d] = y
        for b in range(BATCH):
            io_refs[2 + d][b, :, ch] = tmaj[blk, 2 + d, pl.ds(b, S5_STEPS, stride=BATCH), :]

    def paired_lhs(u, take_upper):
        u3 = u.reshape(n_tiles, SUBLANE, SSM_BLK_CH)
        swapped = pltpu.roll(u3, half, axis=1)
        sub = lax.broadcasted_iota(jnp.int32, u3.shape, 1)
        keep = (sub >= half) if take_upper else (sub < half)
        nb = jnp.where(keep, swapped, 0.0).reshape(S5_ROWS, SSM_BLK_CH)
        return jnp.concatenate([u, nb], axis=1).astype(BF16)

    for blk in range(S5_NB):
        for d in range(2):
            xs[blk, d] = _bdot(paired_lhs(time_major(blk, d), d == 0), wb_ref[0, d, blk])

    lower = lax.broadcasted_iota(jnp.int32, (SUBLANE, S5_SCAN_LANES), 0) < half
    for blk in range(S5_NB):
        for hh in range(SSM_BLK_ST // S5_SCAN_LANES):
            lo = hh * S5_SCAN_LANES
            re = slice(lo, lo + S5_SCAN_LANES)
            im = slice(SSM_BLK_ST + lo, SSM_BLK_ST + lo + S5_SCAN_LANES)
            afr, afi = tre_ref[0, 0, blk, :, re], tim_ref[0, 0, blk, :, re]
            abr, abi = tre_ref[0, 1, blk, :, re], tim_ref[0, 1, blk, :, re]
            hfr, hfi, hbr, hbi = (carry[blk, idx, :, re] for idx in range(4))
            for k in range(n_tiles):
                rf = slice(k * SUBLANE, (k + 1) * SUBLANE)
                pr = jnp.where(lower, pltpu.roll(hfr, half, axis=0), hfr)
                pi = jnp.where(lower, pltpu.roll(hfi, half, axis=0), hfi)
                hfr = xs[blk, 0, rf, re] + (afr * pr - afi * pi)
                hfi = xs[blk, 0, rf, im] + (afr * pi + afi * pr)
                xs[blk, 0, rf, re] = hfr
                xs[blk, 0, rf, im] = hfi
                kb = n_tiles - 1 - k
                rb = slice(kb * SUBLANE, (kb + 1) * SUBLANE)
                qr = jnp.where(lower, hbr, pltpu.roll(hbr, half, axis=0))
                qi = jnp.where(lower, hbi, pltpu.roll(hbi, half, axis=0))
                hbr = xs[blk, 1, rb, re] + (abr * qr - abi * qi)
                hbi = xs[blk, 1, rb, im] + (abr * qi + abi * qr)
                xs[blk, 1, rb, re] = hbr
                xs[blk, 1, rb, im] = hbi
            for idx, val in enumerate((hfr, hfi, hbr, hbi)):
                carry[blk, idx, :, re] = val

    for blk in range(S5_NB):
        for d in range(2):
            y = lax.dot_general(xs[blk, d].astype(BF16), wc_ref[0, d, blk],
                                (((1,), (1,)), ((), ())), preferred_element_type=F32)
            batch_major(y, blk, d)


def _s5_scan(proj, wb, wc, tre, tim, layer):
    last = S5_CHUNKS - 1
    width = S5_NB * SSM_BLK_CH
    u_blk0 = U_COL0 // width
    proj3 = proj.reshape(BATCH, L_TOK, PROJ_COLS)
    y_shape = jax.ShapeDtypeStruct((BATCH, L_TOK, SSM_WIDTH), F32)
    per_blk = lambda *tail: pl.BlockSpec((1, 2, S5_NB) + tail, lambda j, c: (layer, 0, j, 0, 0))
    yf, yb = pl.pallas_call(
        _s5_scan_kernel,
        grid=(SSM_BLKS // S5_NB, S5_CHUNKS),
        in_specs=[
            pl.BlockSpec((BATCH, S5_STEPS, width), lambda j, c: (0, c, u_blk0 + j)),
            pl.BlockSpec((BATCH, S5_STEPS, width), lambda j, c: (0, last - c, u_blk0 + j)),
            per_blk(2 * SSM_BLK_CH, 2 * SSM_BLK_ST),
            per_blk(SSM_BLK_CH, 2 * SSM_BLK_ST),
            per_blk(SUBLANE, SSM_BLK_ST),
            per_blk(SUBLANE, SSM_BLK_ST),
        ],
        out_specs=[
            pl.BlockSpec((BATCH, S5_STEPS, width), lambda j, c: (0, c, j)),
            pl.BlockSpec((BATCH, S5_STEPS, width), lambda j, c: (0, last - c, j)),
        ],
        out_shape=[y_shape, y_shape],
        scratch_shapes=[
            pltpu.VMEM((S5_NB, 4, S5_ROWS, SSM_BLK_CH), F32),
            pltpu.VMEM((S5_NB, 2, S5_ROWS, 2 * SSM_BLK_ST), F32),
            pltpu.VMEM((S5_NB, 4, SUBLANE, SSM_BLK_ST), F32),
        ],
        compiler_params=_cparams(("parallel", "arbitrary")),
        name="s5_scan",
    )(proj3, proj3, wb, wc, tre, tim)
    return yf.reshape(T_TOK, SSM_WIDTH), yb.reshape(T_TOK, SSM_WIDTH)


def _gelu_tanh(x):
    return 0.5 * x * (1.0 + jnp.tanh(math.sqrt(2.0 / math.pi) * (x + 0.044715 * (x * x * x))))


def _glu_kernel(yf_ref, yb_ref, u_ref, d_ref, w_ref, gn_ref, o_ref):
    y = (yf_ref[...] + yb_ref[...]) + d_ref[0] * u_ref[...]
    g = _gelu_tanh(y)
    z = _bdot(g.astype(BF16), w_ref[0].astype(BF16))
    o_ref[...] = _rms(g * jax.nn.sigmoid(z), gn_ref[0]).astype(BF16)


def _glu(yf, yb, proj, d, w, gn, layer, tm):
    t = yf.shape[0]
    row = pl.BlockSpec((tm, SSM_WIDTH), lambda i: (i, 0))
    vec = pl.BlockSpec((1, 1, SSM_WIDTH), lambda i: (layer, 0, 0))
    return pl.pallas_call(
        _glu_kernel,
        grid=(t // tm,),
        in_specs=[row, row,
                  pl.BlockSpec((tm, SSM_WIDTH), lambda i: (i, U_COL0 // SSM_WIDTH)),
                  vec,
                  pl.BlockSpec((1, SSM_WIDTH, SSM_WIDTH), lambda i: (layer, 0, 0)),
                  vec],
        out_specs=row,
        out_shape=jax.ShapeDtypeStruct((t, SSM_WIDTH), BF16),
        compiler_params=_cparams(("parallel",)),
        name="s5_glu",
    )(yf, yb, proj, d.reshape(DEPTH, 1, SSM_WIDTH), w, gn.reshape(DEPTH, 1, SSM_WIDTH))


def _outproj_kernel(h_ref, a_ref, sn_ref, ga_ref, wa_ref, ws_ref, o_ref, an_ref):
    @pl.when(pl.program_id(1) == 0)
    def _():
        an_ref[...] = _rms(a_ref[...], ga_ref[0]).astype(BF16)

    o_ref[...] = (h_ref[...] + _bdot(an_ref[...], wa_ref[0].astype(BF16))
                  + _bdot(sn_ref[...], ws_ref[0].astype(BF16)))


def _outproj(h, att, ssm_n, ga, w, layer, tm, tn):
    t, d = h.shape
    return pl.pallas_call(
        _outproj_kernel,
        grid=(t // tm, d // tn),
        in_specs=[
            pl.BlockSpec((tm, tn), lambda i, j: (i, j)),
            pl.BlockSpec((tm, ATT_WIDTH), lambda i, j: (i, 0)),
            pl.BlockSpec((tm, SSM_WIDTH), lambda i, j: (i, 0)),
            pl.BlockSpec((1, 1, ATT_WIDTH), lambda i, j: (layer, 0, 0)),
            pl.BlockSpec((1, ATT_WIDTH, tn), lambda i, j: (layer, 0, j)),
            pl.BlockSpec((1, SSM_WIDTH, tn), lambda i, j: (layer, 1, j)),
        ],
        out_specs=pl.BlockSpec((tm, tn), lambda i, j: (i, j)),
        out_shape=jax.ShapeDtypeStruct((t, d), F32),
        scratch_shapes=[pltpu.VMEM((tm, ATT_WIDTH), BF16)],
        input_output_aliases={0: 0},
        compiler_params=_cparams(("parallel", "arbitrary")),
        name="out_proj",
    )(h, att, ssm_n, ga.reshape(DEPTH, 1, ATT_WIDTH), w, w)


FFN_SUB = 256


def _ffn_kernel(x_ref, g_ref, wg_ref, wu_ref, wd_ref, gn_ref, o_ref, nxt_ref, hn_ref):
    j = pl.program_id(1)

    @pl.when(j == 0)
    def _():
        x = x_ref[...]
        hn_ref[...] = _rms(x, g_ref[0]).astype(BF16)
        o_ref[...] = x

    hn = hn_ref[...]
    tf = wg_ref.shape[2]
    part = None
    for lo in range(0, tf, FFN_SUB):
        cols = slice(lo, lo + FFN_SUB)
        a = _bdot(hn, wg_ref[0, :, cols].astype(BF16))
        b = _bdot(hn, wu_ref[0, :, cols].astype(BF16))
        hid = ((a * jax.nn.sigmoid(a)) * b).astype(BF16)
        down = _bdot(hid, wd_ref[0, cols, :].astype(BF16))
        part = down if part is None else part + down
    o_ref[...] += part

    @pl.when(j == pl.num_programs(1) - 1)
    def _():
        nxt_ref[...] = _rms(o_ref[...], gn_ref[0]).astype(BF16)


def _ffn(h, g, gn, wg, wu, wd, layer, tm, tf):
    t, d = h.shape
    f = wg.shape[2]
    idx = layer // 2
    once = pl.Buffered(1)
    row = pl.BlockSpec((tm, d), lambda i, j: (i, 0), pipeline_mode=once)
    return pl.pallas_call(
        _ffn_kernel,
        grid=(t // tm, f // tf),
        in_specs=[
            row,
            pl.BlockSpec((1, 1, d), lambda i, j: (layer, 0, 0)),
            pl.BlockSpec((1, d, tf), lambda i, j: (idx, 0, j)),
            pl.BlockSpec((1, d, tf), lambda i, j: (idx, 0, j)),
            pl.BlockSpec((1, tf, d), lambda i, j: (idx, j, 0)),
            pl.BlockSpec((1, 1, d), lambda i, j: (layer + 1, 0, 0)),
        ],
        out_specs=[row, row],
        out_shape=[jax.ShapeDtypeStruct((t, d), F32), jax.ShapeDtypeStruct((t, d), BF16)],
        scratch_shapes=[pltpu.VMEM((tm, d), BF16)],
        input_output_aliases={0: 0},
        compiler_params=_cparams(("parallel", "arbitrary")),
        name="dense_ffn",
    )(h, g.reshape(DEPTH, 1, d), wg, wu, wd, gn.reshape(DEPTH, 1, d))


MOE_TM = 344
MOE_NT = (2 * T_TOK) // MOE_TM + N_EXPERTS
MOE_NP = MOE_NT * MOE_TM
DMA_FANOUT = 4


def _split_bf16(x):
    hi = x.astype(BF16)
    lo = (x - hi.astype(F32)).astype(BF16)
    return hi, lo


def _lane_pick(x, lane, k):
    return jnp.sum(jnp.where(lane == k, x, 0.0), axis=-1, keepdims=True)


ROW_SLABS = D_MODEL // LANE


def _row_slab(ref, c, rows):
    return ref.at[pl.ds(c, rows, stride=ROW_SLABS), :]


def _router_kernel(x_ref, g_ref, wr_ref, sel_ref, wts_ref, rows_ref):
    x = x_ref[...]
    for c in range(ROW_SLABS):
        _row_slab(rows_ref, c, x.shape[0])[...] = x[:, c * LANE:(c + 1) * LANE]
    hn = _rms(x, g_ref[0])
    xh, xl = _split_bf16(hn)
    wh, wl = _split_bf16(wr_ref[0])
    logits = _bdot(xh, wh) + (_bdot(xh, wl) + _bdot(xl, wh)) + _bdot(xl, wl)
    lane = lax.broadcasted_iota(jnp.int32, logits.shape, 1).astype(F32)
    neg = jnp.float32(-jnp.inf)
    logits = jnp.where(lane < N_EXPERTS, logits, neg)
    v1 = jnp.max(logits, axis=-1, keepdims=True)
    i1 = jnp.min(jnp.where(logits == v1, lane, float(LANE)), axis=-1, keepdims=True)
    rest = jnp.where(lane == i1, neg, logits)
    v2 = jnp.max(rest, axis=-1, keepdims=True)
    i2 = jnp.min(jnp.where(rest == v2, lane, float(LANE)), axis=-1, keepdims=True)
    e2 = jnp.exp(v2 - v1)
    w1 = 1.0 / (1.0 + e2)
    w2 = e2 / (1.0 + e2)
    sel_ref[...] = jnp.where((lane == i1) | (lane == i2), 1.0, 0.0)
    wts_ref[...] = jnp.where(lane == 0, w1, jnp.where(lane == 1, w2,
                             jnp.where(lane == 2, i1, jnp.where(lane == 3, i2, 0.0))))


def _router(h, g, wr_pad, layer, tm):
    t, d = h.shape
    idx = layer // 2
    row = pl.BlockSpec((tm, LANE), lambda i: (i, 0))
    tab = jax.ShapeDtypeStruct((t, LANE), F32)
    return pl.pallas_call(
        _router_kernel,
        grid=(t // tm,),
        in_specs=[
            pl.BlockSpec((tm, d), lambda i: (i, 0)),
            pl.BlockSpec((1, 1, d), lambda i: (layer, 0, 0)),
            pl.BlockSpec((1, d, LANE), lambda i: (idx, 0, 0)),
        ],
        out_specs=[row, row, pl.BlockSpec((tm * ROW_SLABS, LANE), lambda i: (i, 0))],
        out_shape=[tab, tab, jax.ShapeDtypeStruct((t * ROW_SLABS, LANE), F32)],
        compiler_params=_cparams(("parallel",)),
        name="moe_router",
    )(h, g.reshape(DEPTH, 1, d), wr_pad)


def _positions_kernel(sel_ref, wts_ref, pos_ref, meta_ref, cnt_ref, off_ref):
    p = pl.program_id(0)
    i = pl.program_id(1)
    tm = sel_ref.shape[0]
    lane = lax.broadcasted_iota(jnp.int32, (1, LANE), 1).astype(F32)
    sel = sel_ref[...]

    @pl.when((p == 0) & (i == 0))
    def _():
        cnt_ref[...] = jnp.zeros((1, LANE), F32)

    @pl.when(p == 0)
    def _():
        cnt_ref[...] += jnp.sum(sel, axis=0, keepdims=True)

    @pl.when((p == 1) & (i == 0))
    def _():
        cnt = cnt_ref[...]
        tiles = jnp.zeros((1, LANE), F32)
        for k in range(MOE_NT):
            tiles = tiles + jnp.where(cnt > float(k * MOE_TM), 1.0, 0.0)
        padded = tiles * float(MOE_TM)
        off = jnp.zeros((1, LANE), F32)
        for e in range(N_EXPERTS):
            off = off + jnp.where(lane > e, _lane_pick(padded, lane, e), 0.0)
        end = off + padded
        tile_start = lane * float(MOE_TM)
        owner = jnp.zeros((1, LANE), F32)
        for e in range(N_EXPERTS):
            owner = owner + jnp.where(tile_start >= _lane_pick(end, lane, e), 1.0, 0.0)
        owner = jnp.minimum(owner, float(N_EXPERTS - 1))
        used = jnp.sum(tiles, axis=-1, keepdims=True)
        off_ref[...] = off
        cnt_ref[...] = jnp.zeros((1, LANE), F32)
        row = lax.broadcasted_iota(jnp.int32, (SUBLANE, LANE), 0)
        meta = jnp.where(row == 0, owner, jnp.where(row == 1, used, jnp.where(row == 2, cnt, off)))
        meta_ref[...] = meta.astype(jnp.int32)

    @pl.when(p == 1)
    def _():
        r = lax.broadcasted_iota(jnp.int32, (tm, tm), 0)
        c = lax.broadcasted_iota(jnp.int32, (tm, tm), 1)
        earlier = jnp.where(c < r, 1.0, 0.0).astype(BF16)
        rank = _bdot(earlier, sel.astype(BF16)) + cnt_ref[...]
        slot = off_ref[...] + rank
        wts = wts_ref[...]
        lane_t = lax.broadcasted_iota(jnp.int32, (tm, LANE), 1).astype(F32)
        p1 = _lane_pick(slot, lane_t, _lane_pick(wts, lane_t, 2))
        p2 = _lane_pick(slot, lane_t, _lane_pick(wts, lane_t, 3))
        pos_ref[...] = jnp.where(lane_t == 0, p1, jnp.where(lane_t == 1, p2, 0.0)).astype(jnp.int32)
        cnt_ref[...] += jnp.sum(sel, axis=0, keepdims=True)


def _positions(sel, wts, tm):
    t = sel.shape[0]
    row = pl.BlockSpec((tm, LANE), lambda p, i: (i, 0))
    return pl.pallas_call(
        _positions_kernel,
        grid=(2, t // tm),
        in_specs=[row, row],
        out_specs=[
            pl.BlockSpec((tm, LANE), lambda p, i: (i * p, 0)),
            pl.BlockSpec((SUBLANE, LANE), lambda p, i: (0, 0)),
        ],
        out_shape=[jax.ShapeDtypeStruct((t, LANE), jnp.int32),
                   jax.ShapeDtypeStruct((SUBLANE, LANE), jnp.int32)],
        scratch_shapes=[pltpu.VMEM((1, LANE), F32), pltpu.VMEM((1, LANE), F32)],
        compiler_params=_cparams(("arbitrary", "arbitrary")),
        name="moe_positions",
    )(sel, wts)


def _experts_kernel(pos1_ref, pos2_ref, owner_ref, used_ref, rows_ref, g_ref, wg_ref, wu_ref, wd_ref,
                    o_ref, src_ref, xbuf, sems):
    i = pl.program_id(0)
    used = used_ref[0]
    cur = lax.rem(i, 2)

    def row_copy(tok, r, buf):
        return pltpu.make_async_copy(rows_ref.at[pl.ds(tok * ROW_SLABS, ROW_SLABS), :],
                                     xbuf.at[buf, pl.ds(r * ROW_SLABS, ROW_SLABS), :], sems.at[buf])

    def fetch_tile(tile, buf):
        def issue(q, c):
            for u in range(DMA_FANOUT):
                r = q * DMA_FANOUT + u
                row_copy(src_ref[tile * MOE_TM + r], r, buf).start(priority=u % 2)
            return c

        lax.fori_loop(0, MOE_TM // DMA_FANOUT, issue, 0)

    @pl.when(i == 0)
    def _():
        def clear(p, c):
            src_ref[p] = 0
            return c

        lax.fori_loop(0, MOE_NP, clear, 0, unroll=8)

        def fill(t, c):
            src_ref[pos1_ref[t]] = t
            src_ref[pos2_ref[t]] = t
            return c

        lax.fori_loop(0, T_TOK, fill, 0, unroll=4)
        fetch_tile(0, 0)

    @pl.when(i + 1 < used)
    def _():
        fetch_tile(i + 1, 1 - cur)

    live = i < used

    @pl.when(live)
    def _():
        pltpu.make_async_copy(rows_ref.at[pl.ds(0, MOE_TM * ROW_SLABS), :], xbuf.at[cur],
                              sems.at[cur]).wait()
        x_ref = xbuf.at[cur]
        slabs = [_row_slab(x_ref, c, MOE_TM)[...] for c in range(ROW_SLABS)]
        ssq = slabs[0] * slabs[0]
        for c in range(1, ROW_SLABS):
            ssq = ssq + slabs[c] * slabs[c]
        inv = lax.rsqrt(jnp.sum(ssq, axis=-1, keepdims=True) / D_MODEL + EPS)
        g = g_ref[0]
        hn = jnp.concatenate([slabs[c] * inv * g[:, c * LANE:(c + 1) * LANE]
                              for c in range(ROW_SLABS)], axis=1).astype(BF16)
        a = _bdot(hn, wg_ref[0, 0])
        b = _bdot(hn, wu_ref[0, 0])
        hid = ((a * jax.nn.sigmoid(a)) * b).astype(BF16)
        res = _bdot(hid, wd_ref[0, 0])
        for c in range(ROW_SLABS):
            _row_slab(o_ref, c, MOE_TM)[...] = res[:, c * LANE:(c + 1) * LANE]

    @pl.when(jnp.logical_not(live))
    def _():
        o_ref[...] = jnp.zeros(o_ref.shape, F32)


def _experts(rows, pos1, pos2, owner, used, g, wg, wu, wd, layer):
    d = D_MODEL
    idx = layer // 2
    blk = MOE_TM * ROW_SLABS

    w_map = lambda i, p1, p2, o, u: (idx, o[jnp.minimum(i, u[0] - 1)], 0, 0)
    return pl.pallas_call(
        _experts_kernel,
        grid_spec=pltpu.PrefetchScalarGridSpec(
            num_scalar_prefetch=4,
            grid=(MOE_NT,),
            in_specs=[
                pl.BlockSpec(memory_space=pl.ANY),
                pl.BlockSpec((1, 1, d), lambda i, p1, p2, o, u: (layer, 0, 0)),
                pl.BlockSpec((1, 1, d, D_FF_EXPERT), w_map),
                pl.BlockSpec((1, 1, d, D_FF_EXPERT), w_map),
                pl.BlockSpec((1, 1, D_FF_EXPERT, d), w_map),
            ],
            out_specs=pl.BlockSpec((blk, LANE), lambda i, p1, p2, o, u: (i, 0)),
            scratch_shapes=[
                pltpu.SMEM((MOE_NP,), jnp.int32),
                pltpu.VMEM((2, blk, LANE), F32),
                pltpu.SemaphoreType.DMA((2,)),
            ],
        ),
        out_shape=jax.ShapeDtypeStruct((MOE_NP * ROW_SLABS, LANE), F32),
        compiler_params=_cparams(("arbitrary",)),
        name="moe_experts",
    )(pos1, pos2, owner, used, rows, g.reshape(DEPTH, 1, d), wg, wu, wd)


def _combine_kernel(pos1_ref, pos2_ref, h_ref, wts_ref, ys_ref, gn_ref, o_ref, nxt_ref,
                    y1, y2, sem1, sem2):
    tm = h_ref.shape[0]
    i = pl.program_id(0)
    cur = lax.rem(i, 2)
    y1f, y2f = y1, y2

    def row_copy(slot, r, buf, sem):
        return pltpu.make_async_copy(ys_ref.at[pl.ds(slot * ROW_SLABS, ROW_SLABS), :],
                                     buf.at[pl.ds(r * ROW_SLABS, ROW_SLABS), :], sem)

    def fetch_tile(tile, half):
        def fetch(q, c):
            for u in range(DMA_FANOUT):
                r = q * DMA_FANOUT + u
                row_copy(pos1_ref[tile * tm + r], r, y1f.at[half], sem1.at[half]).start(priority=0)
                row_copy(pos2_ref[tile * tm + r], r, y2f.at[half], sem2.at[half]).start(priority=1)
            return c

        lax.fori_loop(0, tm // DMA_FANOUT, fetch, 0)

    @pl.when(i == 0)
    def _():
        fetch_tile(0, 0)

    @pl.when(i + 1 < pl.num_programs(0))
    def _():
        fetch_tile(i + 1, 1 - cur)

    y1 = y1f.at[cur]
    y2 = y2f.at[cur]
    pltpu.make_async_copy(ys_ref.at[pl.ds(0, tm * ROW_SLABS), :], y1, sem1.at[cur]).wait()
    pltpu.make_async_copy(ys_ref.at[pl.ds(0, tm * ROW_SLABS), :], y2, sem2.at[cur]).wait()
    wts = wts_ref[...]
    lane = lax.broadcasted_iota(jnp.int32, wts.shape, 1).astype(F32)
    w1 = _lane_pick(wts, lane, 0)
    w2 = _lane_pick(wts, lane, 1)
    ssq = jnp.zeros((tm, 1), F32)
    for c in range(ROW_SLABS):
        cols = slice(c * LANE, (c + 1) * LANE)
        hc = h_ref[:, cols] + (w1 * _row_slab(y1, c, tm)[...] + w2 * _row_slab(y2, c, tm)[...])
        o_ref[:, cols] = hc
        ssq = ssq + jnp.sum(hc * hc, axis=-1, keepdims=True)
    normed = o_ref[...] * lax.rsqrt(ssq / D_MODEL + EPS) * gn_ref[...]
    if nxt_ref is None:
        o_ref[...] = normed
    else:
        nxt_ref[...] = normed.astype(BF16)


def _combine_mid_kernel(pos1_ref, pos2_ref, h_ref, wts_ref, ys_ref, gn_ref, o_ref, nxt_ref,
                        y1, y2, sem1, sem2):
    _combine_kernel(pos1_ref, pos2_ref, h_ref, wts_ref, ys_ref, gn_ref, o_ref, nxt_ref,
                    y1, y2, sem1, sem2)


def _combine_last_kernel(pos1_ref, pos2_ref, h_ref, wts_ref, ys_ref, gn_ref, o_ref,
                         y1, y2, sem1, sem2):
    _combine_kernel(pos1_ref, pos2_ref, h_ref, wts_ref, ys_ref, gn_ref, o_ref, None,
                    y1, y2, sem1, sem2)


def _combine(h, wts, ys, pos1, pos2, gn, last, tm):
    t, d = h.shape
    row = pl.BlockSpec((tm, d), lambda i, a, b: (i, 0))
    h_out = jax.ShapeDtypeStruct((t, d), F32)
    return pl.pallas_call(
        _combine_last_kernel if last else _combine_mid_kernel,
        grid_spec=pltpu.PrefetchScalarGridSpec(
            num_scalar_prefetch=2,
            grid=(t // tm,),
            in_specs=[
                row,
                pl.BlockSpec((tm, LANE), lambda i, a, b: (i, 0)),
                pl.BlockSpec(memory_space=pl.ANY),
                pl.BlockSpec((1, d), lambda i, a, b: (0, 0)),
            ],
            out_specs=row if last else [row, row],
            scratch_shapes=[
                pltpu.VMEM((2, tm * ROW_SLABS, LANE), F32),
                pltpu.VMEM((2, tm * ROW_SLABS, LANE), F32),
                pltpu.SemaphoreType.DMA((2,)),
                pltpu.SemaphoreType.DMA((2,)),
            ],
        ),
        out_shape=h_out if last else [h_out, jax.ShapeDtypeStruct((t, d), BF16)],
        input_output_aliases={2: 0},
        compiler_params=_cparams(("arbitrary",)),
        name="moe_combine",
    )(pos1, pos2, h, wts, ys, gn.reshape(1, d))


def _moe(h, g, gn, wr_pad, wg, wu, wd, layer, last):
    sel, wts, rows = _router(h, g, wr_pad, layer, tm=688)
    pos, meta = _positions(sel, wts, tm=688)
    pos1 = pos[:, 0]
    pos2 = pos[:, 1]
    owner = meta[0, :MOE_NT]
    used = meta[1, :1]
    ys = _experts(rows, pos1, pos2, owner, used, g, wg, wu, wd, layer)
    return _combine(h, wts, ys, pos1, pos2, gn, last, tm=688)


def _rope_partner(w):
    half = QK_ROPE // 2
    return jnp.concatenate([-w[..., half:], w[..., :half]], axis=-1)


def _pad_cols(w, width):
    return jnp.pad(w, [(0, 0)] * (w.ndim - 1) + [(0, width - w.shape[-1])])


def _in_proj_layout(w_in):
    kr = w_in[..., Q_LORA + KV_LORA:Q_LORA + KV_LORA + QK_ROPE]
    return jnp.concatenate([
        w_in[..., :Q_LORA + KV_LORA],
        _pad_cols(kr, LANE),
        _pad_cols(_rope_partner(kr), LANE),
        w_in[..., Q_LORA + KV_LORA + QK_ROPE:]], axis=-1)


def _q_layouts(w_uq):
    w = w_uq.reshape(DEPTH, Q_LORA, ATT_HEADS, QK_NOPE + QK_ROPE)
    main = _pad_cols(w, HEAD_PAD).reshape(DEPTH, Q_LORA, ATT_HEADS * HEAD_PAD)
    rot = _pad_cols(_rope_partner(w[..., QK_NOPE:]), LANE).reshape(DEPTH, Q_LORA, ATT_HEADS * LANE)
    return main, rot


def _rope_tables():
    inv = ROPE_THETA ** (-jnp.arange(0, QK_ROPE, 2, dtype=F32) / QK_ROPE)
    ang = jnp.arange(L_TOK, dtype=F32)[:, None] * inv[None, :]
    cos = jnp.cos(ang)
    sin = jnp.sin(ang)
    cos_t = _pad_cols(jnp.concatenate([cos, cos], axis=-1), LANE)
    sin_t = _pad_cols(jnp.concatenate([sin, sin], axis=-1), LANE)
    return cos_t, sin_t


def kernel(x, meta_tokens, mix_norm, w_in, q_norm, w_uq, kv_norm, w_ukv, ssm_lambda_re, ssm_lambda_im, ssm_log_step, ssm_b_re, ssm_b_im, ssm_c_re, ssm_c_im, ssm_d, ssm_w_glu, attn_out_norm, ssm_out_norm, w_out, ffn_norm, dense_w_gate, dense_w_up, dense_w_down, moe_router, moe_w_gate, moe_w_up, moe_w_down, final_norm):
    meta = jnp.broadcast_to(meta_tokens[None].astype(x.dtype), (BATCH, N_META, D_MODEL))
    h = jnp.concatenate([meta, x], axis=1).reshape(T_TOK, D_MODEL)
    cos_t, sin_t = _rope_tables()
    w_in_l = _in_proj_layout(w_in)
    wq_main, wq_rot = _q_layouts(w_uq)
    wr_pad = _pad_cols(moe_router, LANE)
    moe_w_gate, moe_w_up, moe_w_down = (w.astype(BF16) for w in (moe_w_gate, moe_w_up, moe_w_down))
    wb, wc, tre, tim = _s5_layouts(ssm_lambda_re, ssm_lambda_im, ssm_log_step,
                                   ssm_b_re, ssm_b_im, ssm_c_re, ssm_c_im)
    hn = _rms_cast(h, mix_norm, 0, tm=688)
    for layer in range(DEPTH):
        proj = _in_proj(hn, w_in_l, layer, tm=2064, tn=512)
        q, kv, kpe = _upproj(proj, q_norm, kv_norm, wq_main, wq_rot, w_ukv, cos_t, sin_t, layer,
                             tm=688)
        att = _attention(q, kv, kpe, tq=688)
        yf, yb = _s5_scan(proj, wb, wc, tre, tim, layer)
        ssm_n = _glu(yf, yb, proj, ssm_d, ssm_w_glu, ssm_out_norm, layer, tm=688)
        h = _outproj(h, att, ssm_n, attn_out_norm, w_out, layer, tm=2064, tn=512)
        last = layer == DEPTH - 1
        if layer % 2 == 0:
            h, hn = _ffn(h, ffn_norm, mix_norm, dense_w_gate, dense_w_up, dense_w_down, layer,
                         tm=1032, tf=512)
        elif last:
            h = _moe(h, ffn_norm, final_norm, wr_pad, moe_w_gate, moe_w_up, moe_w_down, layer, last)
        else:
            h, hn = _moe(h, ffn_norm, mix_norm[layer + 1], wr_pad, moe_w_gate, moe_w_up, moe_w_down,
                         layer, last)
    return h.reshape(BATCH, L_TOK, D_MODEL)[:, N_META:]
```
